```python
import math
import jax, jax.numpy as jnp
from jax import lax
import numpy as np

D_MODEL = 1024
BATCH = 8
SEQ = 4096
DEPTH = 4

HGRN_HEADS = 4
HGRN_DK = 128
HGRN_DV = 128
HGRN_KW = HGRN_HEADS * HGRN_DK
HGRN_VW = HGRN_HEADS * HGRN_DV
HGRN_CHUNK = 32
MLA_HEADS = 4
MLA_NOPE = 64
MLA_ROPE = 32
MLA_V = 64
MLA_Q_LORA = 192
MLA_KV_LORA = 128
ROPE_THETA = 10000.0
DSA_HEADS = 4
DSA_HEAD_DIM = 64
DSA_WIDTH = DSA_HEADS * DSA_HEAD_DIM
IDX_HEADS = 8
IDX_DIM = 64
TOPK_MAX = 256
NUM_BUCKETS = 32
MAX_DISTANCE = 128
MAX_POS_OFFSET = 1024
D_FF = 4 * D_MODEL
QB = 128
EPS = 1e-6

D_MIX = HGRN_VW + MLA_HEADS * MLA_V + DSA_WIDTH
IN_SIZES = (HGRN_KW, HGRN_KW, HGRN_VW, HGRN_VW,
            MLA_Q_LORA, MLA_KV_LORA + MLA_ROPE,
            DSA_WIDTH, DSA_WIDTH, DSA_WIDTH,
            IDX_HEADS * IDX_DIM, IDX_DIM, IDX_HEADS)
N_IN = sum(IN_SIZES)
SPLIT_POINTS = tuple(int(v) for v in np.cumsum(IN_SIZES)[:-1])

kernel_name = 'hybrid_hgrn2_mla_dsa'


def rms_norm(x, w):
    xf = x.astype(jnp.float32)
    y = xf * lax.rsqrt(jnp.mean(xf * xf, axis=-1, keepdims=True) + EPS)
    return (y * w.astype(jnp.float32)).astype(x.dtype)


def layer_norm(x, w, b):
    xf = x.astype(jnp.float32)
    mu = jnp.mean(xf, axis=-1, keepdims=True)
    var = jnp.mean(jnp.square(xf - mu), axis=-1, keepdims=True)
    y = (xf - mu) * lax.rsqrt(var + EPS)
    return (y * w.astype(jnp.float32) + b.astype(jnp.float32)).astype(x.dtype)


def apply_rope(x, cos, sin):
    half = x.shape[-1] // 2
    x1 = x[..., :half].astype(jnp.float32)
    x2 = x[..., half:].astype(jnp.float32)
    return jnp.concatenate([x1 * cos - x2 * sin, x2 * cos + x1 * sin], axis=-1).astype(x.dtype)


def t5_bucket(dist):
    n = jnp.maximum(dist, 0)
    max_exact = NUM_BUCKETS // 2
    nf = jnp.maximum(n, 1).astype(jnp.float32)
    large = max_exact + (jnp.log(nf / max_exact) / math.log(MAX_DISTANCE / max_exact)
                         * (NUM_BUCKETS - max_exact)).astype(jnp.int32)
    large = jnp.minimum(large, NUM_BUCKETS - 1)
    return jnp.where(n < max_exact, n, large)


def hgrn2_mixer(q, f_pre, i, g, lb, norm_w):
    B, S, _ = q.shape
    N = S // HGRN_CHUNK
    C = HGRN_CHUNK
    fp = f_pre.astype(jnp.float32)
    lb = lb.astype(jnp.float32)
    log_f = jnp.logaddexp(jnp.log(lb), jnp.log1p(-lb) + jax.nn.log_sigmoid(fp))
    k = (1.0 - lb) * jax.nn.sigmoid(-fp)

    def chunked(a, d):
        return a.astype(jnp.float32).reshape(B, N, C, HGRN_HEADS, d).transpose(1, 0, 3, 2, 4)

    qc = chunked(q, HGRN_DK) * (HGRN_DK ** -0.5)
    kc = chunked(k, HGRN_DK)
    vc = chunked(i, HGRN_DV)
    G = jnp.cumsum(chunked(log_f, HGRN_DK), axis=3)
    q_dec = qc * jnp.exp(G)
    k_inv = kc * jnp.exp(-G)
    causal = jnp.tril(jnp.ones((C, C), dtype=bool))
    A = jnp.where(causal, jnp.einsum('nbhtk,nbhsk->nbhts', q_dec, k_inv), 0.0)
    o_intra = jnp.einsum('nbhts,nbhsv->nbhtv', A, vc)
    G_last = G[:, :, :, -1:, :]
    k_state = kc * jnp.exp(G_last - G)
    chunk_decay = jnp.exp(G_last[:, :, :, 0, :])

    def step(state, xs):
        q_n, k_n, v_n, d_n = xs
        o_n = jnp.einsum('bhtk,bhkv->bhtv', q_n, state)
        state = d_n[..., None] * state + jnp.einsum('bhtk,bhtv->bhkv', k_n, v_n)
        return state, o_n

    s0 = jnp.zeros((B, HGRN_HEADS, HGRN_DK, HGRN_DV), jnp.float32)
    _, o_inter = lax.scan(step, s0, (q_dec, k_state, vc, chunk_decay))
    o = (o_intra + o_inter).transpose(1, 0, 3, 2, 4).reshape(B, S, HGRN_HEADS, HGRN_DV)
    gate = g.astype(jnp.float32).reshape(B, S, HGRN_HEADS, HGRN_DV)
    o = rms_norm(o, norm_w) * jax.nn.silu(gate)
    return o.reshape(B, S, HGRN_VW).astype(q.dtype)


def dense_causal_attention(q, k, v, scale):
    B, S, H, Dk = q.shape
    Dv = v.shape[-1]
    nb = S // QB
    qb = q.reshape(B, nb, QB, H, Dk).swapaxes(0, 1)
    key_pos = jnp.arange(S)

    def one_block(args):
        q_b, start = args
        t = start + jnp.arange(QB)
        logits = jnp.einsum('bqhd,bshd->bhqs', q_b, k).astype(jnp.float32) * scale
        mask = key_pos[None, :] <= t[:, None]
        logits = jnp.where(mask, logits, -jnp.inf)
        p = jax.nn.softmax(logits, axis=-1)
        return jnp.einsum('bhqs,bshd->bqhd', p.astype(v.dtype), v)

    out = lax.map(one_block, (qb, jnp.arange(nb) * QB))
    return out.swapaxes(0, 1).reshape(B, S, H * Dv)


def mla_mixer(q_a, kv_a, cos, sin, q_norm_w, w_qb, kv_norm_w, w_kvb):
    B, S, _ = q_a.shape
    q = (rms_norm(q_a, q_norm_w) @ w_qb).reshape(B, S, MLA_HEADS, MLA_NOPE + MLA_ROPE)
    q_nope, q_pe = q[..., :MLA_NOPE], q[..., MLA_NOPE:]
    c_kv, k_pe = kv_a[..., :MLA_KV_LORA], kv_a[..., MLA_KV_LORA:]
    kv = (rms_norm(c_kv, kv_norm_w) @ w_kvb).reshape(B, S, MLA_HEADS, MLA_NOPE + MLA_V)
    k_nope, v = kv[..., :MLA_NOPE], kv[..., MLA_NOPE:]
    q_pe = apply_rope(q_pe, cos[:, :, None, :], sin[:, :, None, :])
    k_pe = apply_rope(k_pe, cos, sin)[:, :, None, :]
    qf = jnp.concatenate([q_nope, q_pe], axis=-1)
    kf = jnp.concatenate([k_nope, jnp.broadcast_to(k_pe, (B, S, MLA_HEADS, MLA_ROPE))], axis=-1)
    return dense_causal_attention(qf, kf, v, (MLA_NOPE + MLA_ROPE) ** -0.5)


def dsa_mixer(q, k, v, qi, ki, wi, positions, rel_table):
    B, S, H, D = q.shape
    nb = S // QB
    n_sel = min(TOPK_MAX, S // 4)
    key_pos = jnp.arange(S)
    gather = jax.vmap(lambda a, idx: a[idx])
    kif = ki.astype(jnp.float32)

    def blockify(a):
        return a.reshape((B, nb, QB) + a.shape[2:]).swapaxes(0, 1)

    def one_block(args):
        q_b, qi_b, w_b, pos_b, start = args
        t = start + jnp.arange(QB)
        idx_logits = jnp.einsum('bqhd,bsd->bqhs', qi_b.astype(jnp.float32), kif) * (IDX_DIM ** -0.5)
        score = jnp.einsum('bqhs,bqh->bqs', jax.nn.relu(idx_logits), w_b.astype(jnp.float32))
        causal = key_pos[None, :] <= t[:, None]
        score = jnp.where(causal[None], score, -jnp.inf)
        top_val, top_idx = lax.top_k(score, n_sel)
        valid = jnp.isfinite(top_val)
        k_sel = gather(k, top_idx)
        v_sel = gather(v, top_idx)
        pos_sel = gather(positions, top_idx)
        bias = rel_table.astype(jnp.float32)[t5_bucket(pos_b[:, :, None] - pos_sel)]
        logits = (jnp.einsum('bqhd,bqkhd->bqhk', q_b, k_sel).astype(jnp.float32) * (D ** -0.5)
                  + bias.transpose(0, 1, 3, 2))
        logits = jnp.where(valid[:, :, None, :], logits, -jnp.inf)
        p = jax.nn.softmax(logits, axis=-1)
        return jnp.einsum('bqhk,bqkhd->bqhd', p.astype(v.dtype), v_sel)

    out = lax.map(one_block, (blockify(q), blockify(qi), blockify(wi), blockify(positions),
                              jnp.arange(nb) * QB))
    return out.swapaxes(0, 1).reshape(B, S, H * D)


def setup_inputs(seed: int = 0) -> dict:
    key = jax.random.key(seed)
    ks = jax.random.split(key, 20)
    f32 = jnp.float32

    def nrm(k, shape, scale):
        return jax.random.normal(k, shape, f32) * scale

    def gain(k, shape):
        return 1.0 + 0.02 * jax.random.normal(k, shape, f32)

    x = jax.random.normal(ks[0], (BATCH, SEQ, D_MODEL), f32)
    offsets = jax.random.randint(ks[1], (BATCH, 1), 0, MAX_POS_OFFSET, dtype=jnp.int32)
    positions = (offsets + jnp.arange(SEQ, dtype=jnp.int32)[None, :]).astype(jnp.int32)
    return {
        'x': x,
        'positions': positions,
        'attn_norm_w': gain(ks[2], (DEPTH, D_MODEL)),
        'w_in': nrm(ks[3], (DEPTH, D_MODEL, N_IN), D_MODEL ** -0.5),
        'hgrn_lb_logits': nrm(ks[4], (DEPTH, HGRN_KW), 0.1),
        'hgrn_norm_w': gain(ks[5], (DEPTH, HGRN_DV)),
        'mla_q_norm_w': gain(ks[6], (DEPTH, MLA_Q_LORA)),
        'mla_w_qb': nrm(ks[7], (DEPTH, MLA_Q_LORA, MLA_HEADS * (MLA_NOPE + MLA_ROPE)), MLA_Q_LORA ** -0.5),
        'mla_kv_norm_w': gain(ks[8], (DEPTH, MLA_KV_LORA)),
        'mla_w_kvb': nrm(ks[9], (DEPTH, MLA_KV_LORA, MLA_HEADS * (MLA_NOPE + MLA_V)), MLA_KV_LORA ** -0.5),
        'idx_k_norm_w': gain(ks[10], (DEPTH, IDX_DIM)),
        'idx_k_norm_b': nrm(ks[11], (DEPTH, IDX_DIM), 0.02),
        'rel_bias_table': nrm(ks[12], (NUM_BUCKETS, DSA_HEADS), 0.3),
        'w_out': nrm(ks[13], (DEPTH, D_MIX, D_MODEL), D_MIX ** -0.5),
        'mlp_norm_w': gain(ks[14], (DEPTH, D_MODEL)),
        'w_mlp_in': nrm(ks[15], (DEPTH, D_MODEL, D_FF), D_MODEL ** -0.5),
        'w_mlp_out': nrm(ks[16], (DEPTH, D_FF, D_MODEL), D_FF ** -0.5),
        'final_norm_w': gain(ks[17], (D_MODEL,)),
    }


def reference(x, positions, attn_norm_w, w_in, hgrn_lb_logits, hgrn_norm_w, mla_q_norm_w, mla_w_qb,
              mla_kv_norm_w, mla_w_kvb, idx_k_norm_w, idx_k_norm_b, rel_bias_table, w_out,
              mlp_norm_w, w_mlp_in, w_mlp_out, final_norm_w):
    B, S, _ = x.shape
    inv_freq = 1.0 / (ROPE_THETA ** (jnp.arange(0, MLA_ROPE, 2, dtype=jnp.float32) / MLA_ROPE))
    ang = positions.astype(jnp.float32)[..., None] * inv_freq
    cos, sin = jnp.cos(ang), jnp.sin(ang)
    lb_all = jnp.cumsum(jax.nn.softmax(hgrn_lb_logits.astype(jnp.float32), axis=0), axis=0)
    lb_all = lb_all - lb_all[0:1]
    h = x
    for l in range(DEPTH):
        u = rms_norm(h, attn_norm_w[l])
        proj = u @ w_in[l]
        (hq, hf, hi, hg, mqa, mkva, dq, dk, dv, iq, ik, iw) = jnp.split(proj, SPLIT_POINTS, axis=-1)
        y_hgrn = hgrn2_mixer(hq, hf, hi, hg, lb_all[l], hgrn_norm_w[l])
        y_mla = mla_mixer(mqa, mkva, cos, sin, mla_q_norm_w[l], mla_w_qb[l], mla_kv_norm_w[l], mla_w_kvb[l])
        y_dsa = dsa_mixer(dq.reshape(B, S, DSA_HEADS, DSA_HEAD_DIM),
                          dk.reshape(B, S, DSA_HEADS, DSA_HEAD_DIM),
                          dv.reshape(B, S, DSA_HEADS, DSA_HEAD_DIM),
                          iq.reshape(B, S, IDX_HEADS, IDX_DIM),
                          layer_norm(ik, idx_k_norm_w[l], idx_k_norm_b[l]),
                          iw * (IDX_HEADS ** -0.5),
                          positions, rel_bias_table)
        mixed = jnp.concatenate([y_hgrn, y_mla.astype(h.dtype), y_dsa.astype(h.dtype)], axis=-1)
        h = h + mixed @ w_out[l]
        u = rms_norm(h, mlp_norm_w[l])
        h = h + jnp.square(jax.nn.relu(u @ w_mlp_in[l])) @ w_mlp_out[l]
    return rms_norm(h, final_norm_w)
```

```python
import functools
import math

import jax
import jax.numpy as jnp
from jax import lax
from jax.experimental import pallas as pl
from jax.experimental.pallas import tpu as pltpu

D_MODEL = 1024
HGRN_HEADS = 4
HGRN_DK = 128
HGRN_DV = 128
HGRN_W = HGRN_HEADS * HGRN_DK
HGRN_CHUNK = 32
MLA_HEADS = 4
MLA_NOPE = 64
MLA_ROPE = 32
MLA_V = 64
MLA_Q_LORA = 192
MLA_KV_LORA = 128
ROPE_THETA = 10000.0
DSA_HEADS = 4
DSA_HEAD_DIM = 64
DSA_WIDTH = DSA_HEADS * DSA_HEAD_DIM
IDX_HEADS = 8
IDX_DIM = 64
TOPK_MAX = 256
NUM_BUCKETS = 32
MAX_EXACT = NUM_BUCKETS // 2
MAX_DISTANCE = 128
D_FF = 4 * D_MODEL
EPS = 1e-6

LANES = 128
HEAD_PAD = 128
NEG = -1e30
VMEM_LIMIT = 52 * 1024 * 1024

C_H = 0
C_M = C_H + 4 * HGRN_W
W_M = 640
C_D = C_M + W_M
W_D = 3 * DSA_WIDTH
C_Q = C_D + W_D
W_Q = IDX_HEADS * IDX_DIM
C_K = C_Q + W_Q
W_K = 256
N_PAD = C_K + W_K

BF = jnp.bfloat16
F32 = jnp.float32


def _dot(a, b):
    return jnp.dot(a, b, preferred_element_type=F32)


def _dot_nt(a, b):
    return lax.dot_general(a, b, (((1,), (1,)), ((), ())), preferred_element_type=F32)


def _rms(x, w, n=None):
    n = x.shape[-1] if n is None else n
    ms = jnp.sum(x * x, axis=-1, keepdims=True) * (1.0 / n)
    return x * lax.rsqrt(ms + EPS) * w


def _cparams(sem):
    return pltpu.CompilerParams(dimension_semantics=sem, vmem_limit_bytes=VMEM_LIMIT)


def _const_spec(shape):
    nd = len(shape)
    return pl.BlockSpec(shape, lambda *_: (0,) * nd, pipeline_mode=pl.Buffered(1))


def _inproj_kernel(x_ref, nw_ref, w_ref, oh_ref, om_ref, od_ref, oq_ref, ok_ref):
    u = _rms(x_ref[...], nw_ref[...]).astype(BF)
    oh_ref[...] = _dot(u, w_ref[:, C_H:C_M])
    om_ref[...] = _dot(u, w_ref[:, C_M:C_D])
    od_ref[...] = _dot(u, w_ref[:, C_D:C_Q]).astype(BF)
    oq_ref[...] = _dot(u, w_ref[:, C_Q:C_K]).astype(BF)
    ok_ref[...] = _dot(u, w_ref[:, C_K:N_PAD])


def _inproj(h, nw, w, tm=512):
    T = h.shape[0]
    row = lambda i: (i, 0)
    return pl.pallas_call(
        _inproj_kernel,
        grid=(T // tm,),
        in_specs=[pl.BlockSpec((tm, D_MODEL), row), _const_spec((1, D_MODEL)),
                  _const_spec((D_MODEL, N_PAD))],
        out_specs=[pl.BlockSpec((tm, 4 * HGRN_W), row), pl.BlockSpec((tm, W_M), row),
                   pl.BlockSpec((tm, W_D), row), pl.BlockSpec((tm, W_Q), row),
                   pl.BlockSpec((tm, W_K), row)],
        out_shape=[jax.ShapeDtypeStruct((T, 4 * HGRN_W), F32), jax.ShapeDtypeStruct((T, W_M), F32),
                   jax.ShapeDtypeStruct((T, W_D), BF), jax.ShapeDtypeStruct((T, W_Q), BF),
                   jax.ShapeDtypeStruct((T, W_K), F32)],
        compiler_params=_cparams(("parallel",)),
        name="inproj",
    )(h, nw, w)


def _prep_kernel(gm_ref, gk_ref, cos_ref, sin_ref, qnw_ref, wq_ref, wqr_ref, kvnw_ref, wk_ref,
                 wv_ref, lnw_ref, lnb_ref, q_ref, k_ref, v_ref, ki_ref):
    cs = cos_ref[...]
    sn = sin_ref[...]
    cs4 = jnp.concatenate([cs] * MLA_HEADS, axis=1)
    sn4 = jnp.concatenate([sn] * MLA_HEADS, axis=1)
    qn = _rms(gm_ref[:, 0:256], qnw_ref[...], n=MLA_Q_LORA).astype(BF)
    scale = (MLA_NOPE + MLA_ROPE) ** -0.5
    q = (_dot(qn, wq_ref[...]) * cs4 + _dot(qn, wqr_ref[...]) * sn4) * scale
    q_ref[...] = q.astype(BF)
    cn = _rms(gm_ref[:, 256:384], kvnw_ref[...]).astype(BF)
    kp = gm_ref[:, 384:512] * cs + gm_ref[:, 512:640] * sn
    k = _dot(cn, wk_ref[...]) + jnp.concatenate([kp] * MLA_HEADS, axis=1)
    k_ref[...] = k.astype(BF)
    v_ref[...] = _dot(cn, wv_ref[...]).astype(BF)
    x = gk_ref[:, 0:LANES]
    first = lax.broadcasted_iota(jnp.int32, x.shape, 1) < IDX_DIM
    mu = jnp.sum(jnp.where(first, x, 0.0), axis=-1, keepdims=True) * (1.0 / IDX_DIM)
    xc = x - mu
    var = jnp.sum(jnp.where(first, xc * xc, 0.0), axis=-1, keepdims=True) * (1.0 / IDX_DIM)
    ki_ref[...] = (xc * lax.rsqrt(var + EPS) * lnw_ref[...] + lnb_ref[...]).astype(BF)


def _prep(gm, gk, cos_t, sin_t, qnw, wq, wqr, kvnw, wk, wv, lnw, lnb, tm=512):
    T = gm.shape[0]
    row = lambda i: (i, 0)
    hp = MLA_HEADS * HEAD_PAD
    return pl.pallas_call(
        _prep_kernel,
        grid=(T // tm,),
        in_specs=[pl.BlockSpec((tm, W_M), row), pl.BlockSpec((tm, W_K), row),
                  pl.BlockSpec((tm, LANES), row), pl.BlockSpec((tm, LANES), row),
                  _const_spec((1, 256)), _const_spec((256, hp)), _const_spec((256, hp)),
                  _const_spec((1, MLA_KV_LORA)), _const_spec((MLA_KV_LORA, hp)),
                  _const_spec((MLA_KV_LORA, hp)), _const_spec((1, LANES)), _const_spec((1, LANES))],
        out_specs=[pl.BlockSpec((tm, hp), row), pl.BlockSpec((tm, hp), row),
                   pl.BlockSpec((tm, hp), row), pl.BlockSpec((tm, LANES), row)],
        out_shape=[jax.ShapeDtypeStruct((T, hp), BF), jax.ShapeDtypeStruct((T, hp), BF),
                   jax.ShapeDtypeStruct((T, hp), BF), jax.ShapeDtypeStruct((T, LANES), BF)],
        compiler_params=_cparams(("parallel",)),
        name="prep",
    )(gm, gk, cos_t, sin_t, qnw, wq, wqr, kvnw, wk, wv, lnw, lnb)


def _split3(x):
    a = x.astype(BF)
    r = x - a.astype(F32)
    b = r.astype(BF)
    c = (r - b.astype(F32)).astype(BF)
    return a, b, c


def _hgrn_kernel(q_ref, f_ref, i_ref, g_ref, lb_ref, nw_ref, o_ref, st_ref, os_ref, *, tb):
    C = HGRN_CHUNK
    nc = tb // C

    @pl.when(pl.program_id(1) == 0)
    def _():
        st_ref[...] = jnp.zeros_like(st_ref)

    fp = f_ref[...]
    log_lb = lb_ref[0:1, :]
    log1m_lb = lb_ref[1:2, :]
    one_m_lb = lb_ref[2:3, :]
    ls = jnp.minimum(fp, 0.0) - jnp.log1p(jnp.exp(-jnp.abs(fp)))
    b = log1m_lb + ls
    log_f = jnp.maximum(log_lb, b) + jnp.log1p(jnp.exp(-jnp.abs(log_lb - b)))
    k = one_m_lb * (1.0 / (1.0 + jnp.exp(fp)))

    r = lax.broadcasted_iota(jnp.int32, (tb, tb), 0)
    c = lax.broadcasted_iota(jnp.int32, (tb, tb), 1)
    same = (r // C) == (c // C)
    tri = jnp.where(same & (c <= r), 1.0, 0.0).astype(BF)
    blk = jnp.where(same, 1.0, 0.0).astype(BF)
    a0, a1, a2 = _split3(log_f)
    G = _dot(tri, a0) + _dot(tri, a1) + _dot(tri, a2)
    G_last = _dot(blk, a0) + _dot(blk, a1) + _dot(blk, a2)

    eg = jnp.exp(G)
    q_dec = (q_ref[...] * (HGRN_DK ** -0.5) * eg).astype(BF)
    k_inv = (k * jnp.exp(-G)).astype(BF)
    k_state = k * jnp.exp(G_last - G)
    decay = jnp.exp(G_last)
    v = i_ref[...]
    vb = v.astype(BF)

    rr = lax.broadcasted_iota(jnp.int32, (C, C), 0)
    cc = lax.broadcasted_iota(jnp.int32, (C, C), 1)
    causal = cc <= rr

    for h in range(HGRN_HEADS):
        hs = slice(h * HGRN_DK, (h + 1) * HGRN_DK)
        st = st_ref[h]
        for n in range(nc):
            rs = slice(n * C, (n + 1) * C)
            qd = q_dec[rs, hs]
            A = jnp.where(causal, _dot_nt(qd, k_inv[rs, hs]), 0.0)
            o = _dot(A.astype(BF), vb[rs, hs]) + _dot_nt(qd, st.astype(BF))
            os_ref[rs, hs] = o
            vt = v[rs, hs].T.astype(BF)
            st = decay[n * C:n * C + 1, hs] * st + _dot(vt, k_state[rs, hs].astype(BF))
        st_ref[h] = st

    g = g_ref[...]
    gate = g * (1.0 / (1.0 + jnp.exp(-g)))
    for h in range(HGRN_HEADS):
        hs = slice(h * HGRN_DK, (h + 1) * HGRN_DK)
        o_ref[:, hs] = (_rms(os_ref[:, hs], nw_ref[...]) * gate[:, hs]).astype(BF)


def _hgrn(gh, lb3, nw, B, S, tb=256):
    nb = S // tb
    col = lambda j: (lambda b, i: (b * nb + i, j))
    return pl.pallas_call(
        functools.partial(_hgrn_kernel, tb=tb),
        grid=(B, nb),
        in_specs=[pl.BlockSpec((tb, HGRN_W), col(0)), pl.BlockSpec((tb, HGRN_W), col(1)),
                  pl.BlockSpec((tb, HGRN_W), col(2)), pl.BlockSpec((tb, HGRN_W), col(3)),
                  pl.BlockSpec((3, HGRN_W), lambda b, i: (0, 0)),
                  pl.BlockSpec((1, HGRN_DV), lambda b, i: (0, 0))],
        out_specs=pl.BlockSpec((tb, HGRN_W), col(0)),
        out_shape=jax.ShapeDtypeStruct((B * S, HGRN_W), BF),
        scratch_shapes=[pltpu.VMEM((HGRN_HEADS, HGRN_DV, HGRN_DK), F32),
                        pltpu.VMEM((tb, HGRN_W), F32)],
        compiler_params=_cparams(("parallel", "arbitrary")),
        name="hgrn",
    )(gh, gh, gh, gh, lb3, nw)


def _softmax_step(s, m, l):
    m_new = jnp.maximum(m, jnp.max(s, axis=-1, keepdims=True))
    alpha = jnp.exp(m - m_new)
    p = jnp.exp(s - m_new)
    l_new = alpha * l + jnp.sum(p, axis=-1, keepdims=True)
    return p.astype(BF), m_new, l_new, alpha


def _mla_kernel(q_ref, k_ref, v_ref, o_ref, *, tq):
    i = pl.program_id(1)
    rr = lax.broadcasted_iota(jnp.int32, (tq, tq), 0)
    cc = lax.broadcasted_iota(jnp.int32, (tq, tq), 1)
    causal = cc <= rr
    outs = []
    for h in range(MLA_HEADS):
        hs = slice(h * HEAD_PAD, (h + 1) * HEAD_PAD)
        q = q_ref[:, hs]

        def tile(j, carry, mask):
            m, l, acc = carry
            ks = pl.ds(pl.multiple_of(j * tq, tq), tq)
            s = _dot_nt(q, k_ref[ks, hs])
            if mask:
                s = jnp.where(causal, s, NEG)
            p, m, l, alpha = _softmax_step(s, m, l)
            acc = alpha * acc + _dot(p, v_ref[ks, hs])
            return m, l, acc

        init = (jnp.full((tq, 1), NEG, F32), jnp.zeros((tq, 1), F32), jnp.zeros((tq, HEAD_PAD), F32))
        carry = lax.fori_loop(0, i, lambda j, c: tile(j, c, False), init)
        m, l, acc = tile(i, carry, True)
        outs.append((acc * (1.0 / l))[:, :MLA_V])
    o_ref[...] = jnp.concatenate(outs, axis=1).astype(BF)


def _mla_attn(q, k, v, B, S, tq=256):
    nq = S // tq
    hp = MLA_HEADS * HEAD_PAD
    return pl.pallas_call(
        functools.partial(_mla_kernel, tq=tq),
        grid=(B, nq),
        in_specs=[pl.BlockSpec((tq, hp), lambda b, i: (b * nq + i, 0)),
                  pl.BlockSpec((S, hp), lambda b, i: (b, 0)),
                  pl.BlockSpec((S, hp), lambda b, i: (b, 0))],
        out_specs=pl.BlockSpec((tq, MLA_HEADS * MLA_V), lambda b, i: (b * nq + i, 0)),
        out_shape=jax.ShapeDtypeStruct((B * S, MLA_HEADS * MLA_V), BF),
        compiler_params=_cparams(("parallel", "arbitrary")),
        name="mla_attn",
    )(q, k, v)


BISECT_CAP = 48


def _dsa_kernel(qi_ref, w_ref, qd_ref, pq_ref, ki_ref, kd_ref, vd_ref, pk_ref, tab_ref, o_ref,
                sc_ref, qm_ref, *, tq, n_sel):
    i = pl.program_id(1)
    lane = lax.broadcasted_iota(jnp.int32, (tq, LANES), 1)
    lo_half = lane < DSA_HEAD_DIM
    rr = lax.broadcasted_iota(jnp.int32, (tq, tq), 0)
    cc = lax.broadcasted_iota(jnp.int32, (tq, tq), 1)
    causal = cc <= rr

    zero_b = jnp.zeros((tq, LANES), BF)
    for p in range(IDX_HEADS // 2):
        qp = qi_ref[:, p * LANES:(p + 1) * LANES]
        qm_ref[2 * p] = jnp.where(lo_half, qp, zero_b)
        qm_ref[2 * p + 1] = jnp.where(lo_half, zero_b, qp)
    w_scale = (IDX_HEADS ** -0.5) * (IDX_DIM ** -0.5)
    wcols = [w_ref[:, h:h + 1] * w_scale for h in range(IDX_HEADS)]

    def score_tile(j):
        kt = ki_ref[pl.ds(pl.multiple_of(j * tq, tq), tq), :]
        acc = jnp.zeros((tq, tq), F32)
        for h in range(IDX_HEADS):
            acc = acc + jnp.maximum(_dot_nt(qm_ref[h], kt), 0.0) * wcols[h]
        return acc

    def p1_body(j, _):
        sc_ref[j] = score_tile(j)
        return 0

    lax.fori_loop(0, i, p1_body, 0)
    sc_ref[i] = jnp.where(causal, score_tile(i), -jnp.inf)

    ones_b = jnp.ones((tq, LANES), BF)

    def rep2(x):
        return jnp.concatenate([x] * (tq // LANES), axis=1)

    def count_gt(thr):
        thr2 = rep2(thr)

        def body(j, c):
            return c + jnp.where(sc_ref[j] > thr2, 1.0, 0.0)

        c = lax.fori_loop(0, i + 1, body, jnp.zeros((tq, tq), F32))
        return _dot(c.astype(BF), ones_b)

    def stat_body(j, c):
        mx, mn, cp, cz = c
        t = sc_ref[j]
        return (jnp.maximum(mx, t), jnp.minimum(mn, jnp.where(t == -jnp.inf, jnp.inf, t)),
                cp + jnp.where(t > 0.0, 1.0, 0.0), cz + jnp.where(t == 0.0, 1.0, 0.0))

    zt = jnp.zeros((tq, tq), F32)
    mx, mn, cp, cz = lax.fori_loop(0, i + 1, stat_body, (zt - jnp.inf, zt + jnp.inf, zt, zt))
    row_max = jnp.broadcast_to(jnp.max(mx, axis=-1, keepdims=True), (tq, LANES))
    row_min = jnp.broadcast_to(jnp.min(mn, axis=-1, keepdims=True), (tq, LANES))
    c_pos = _dot(cp.astype(BF), ones_b)
    c_zero = _dot(cz.astype(BF), ones_b)
    n_valid = (i * tq + 1 + lax.broadcasted_iota(jnp.int32, (tq, LANES), 0)).astype(F32)
    k_sel = float(n_sel)

    big = n_valid > k_sel
    pos_row = jnp.logical_and(big, c_pos >= k_sel)
    tie_row = jnp.logical_and(jnp.logical_and(big, c_pos < k_sel), c_pos + c_zero >= k_sel)
    neg_row = jnp.logical_and(big, c_pos + c_zero < k_sel)
    need = jnp.where(tie_row, k_sel - c_pos, 0.0)
    lo0 = jnp.where(jnp.logical_or(pos_row, tie_row), 0.0, -jnp.inf)
    hi0 = jnp.where(neg_row, 0.0, row_max)
    c_lo0 = jnp.where(pos_row, c_pos, jnp.where(tie_row, k_sel, n_valid))

    def bis_cond(c):
        it, _, _, _, flag = c
        return jnp.logical_and(it < BISECT_CAP, flag > 0.0)

    def bis_body(c):
        it, lo, hi, c_lo, _ = c
        active = c_lo > k_sel
        base = jnp.maximum(lo, row_min)
        mid = base + 0.5 * (hi - base)
        cnt = count_gt(mid)
        up = jnp.logical_and(active, cnt >= k_sel)
        dn = jnp.logical_and(active, cnt < k_sel)
        lo = jnp.where(up, mid, lo)
        c_lo = jnp.where(up, cnt, c_lo)
        hi = jnp.where(dn, mid, hi)
        flag = jnp.max(jnp.where(c_lo > k_sel, 1.0, 0.0))
        return it + 1, lo, hi, c_lo, flag

    flag0 = jnp.max(jnp.where(c_lo0 > k_sel, 1.0, 0.0))
    _, lo, _, _, _ = lax.while_loop(bis_cond, bis_body, (jnp.int32(0), lo0, hi0, c_lo0, flag0))
    lo2 = rep2(lo)

    def mask_plain(_):
        def body(j, c):
            sc_ref[j] = jnp.where(sc_ref[j] > lo2, 0.0, NEG)
            return c
        return lax.fori_loop(0, i + 1, body, 0)

    def mask_ties(_):
        need2 = rep2(need)
        upper = jnp.where(rr <= cc, 1.0, 0.0).astype(BF)

        def body(j, seen):
            t = sc_ref[j]
            zb = jnp.where(t == 0.0, 1.0, 0.0).astype(BF)
            rank = _dot(zb, upper) + rep2(seen)
            take = jnp.logical_and(t == 0.0, rank <= need2)
            sc_ref[j] = jnp.where(jnp.logical_or(t > lo2, take), 0.0, NEG)
            return seen + _dot(zb, ones_b[:, :LANES])
        lax.fori_loop(0, i + 1, body, jnp.zeros((tq, LANES), F32))
        return 0

    lax.cond(jnp.max(need) > 0.0, mask_ties, mask_plain, 0)

    qd = [qd_ref[:, p * LANES:(p + 1) * LANES] for p in range(DSA_HEADS // 2)]
    qh = []
    for p in range(DSA_HEADS // 2):
        qh.append(jnp.where(lo_half, qd[p], zero_b))
        qh.append(jnp.where(lo_half, zero_b, qd[p]))
    pq = pq_ref[...]
    pq_min = jnp.min(pq)
    far_bias = [tab_ref[NUM_BUCKETS - 1, h] for h in range(DSA_HEADS)]
    log_ratio = math.log(MAX_DISTANCE / MAX_EXACT)

    def pair_bias(pk):
        n = jnp.maximum(pq - pk, 0)
        nf = jnp.maximum(n, 1).astype(F32)
        large = MAX_EXACT + (jnp.log(nf / MAX_EXACT) / log_ratio
                             * (NUM_BUCKETS - MAX_EXACT)).astype(jnp.int32)
        large = jnp.minimum(large, NUM_BUCKETS - 1)
        bucket = jnp.where(n < MAX_EXACT, n, large)
        out = []
        for h in range(DSA_HEADS):
            bh = jnp.full((tq, tq), tab_ref[0, h], F32)
            for jb in range(1, NUM_BUCKETS):
                bh = jnp.where(bucket >= jb, tab_ref[jb, h], bh)
            out.append(bh)
        return out

    def attend(j, carry, near):
        ms, ls, accs = carry
        ks = pl.ds(pl.multiple_of(j * tq, tq), tq)
        madd = sc_ref[j]
        pk = pk_ref[pl.ds(j, 1), :]
        if near:
            bias = pair_bias(pk)
        else:
            bias = far_bias
        ms_n, ls_n, accs_n = [], [], []
        for p in range(DSA_HEADS // 2):
            kp = kd_ref[ks, p * LANES:(p + 1) * LANES]
            vp = vd_ref[ks, p * LANES:(p + 1) * LANES]
            pv, al = [], []
            for half in range(2):
                h = 2 * p + half
                s = _dot_nt(qh[h], kp) + (madd + bias[h])
                pr, m_n, l_n, alpha = _softmax_step(s, ms[h], ls[h])
                ms_n.append(m_n)
                ls_n.append(l_n)
                pv.append(_dot(pr, vp))
                al.append(alpha)
            alpha_p = jnp.where(lo_half, al[0], al[1])
            accs_n.append(alpha_p * accs[p] + jnp.where(lo_half, pv[0], pv[1]))
        return tuple(ms_n), tuple(ls_n), tuple(accs_n)

    def p3_body(j, carry):
        pk_max = jnp.max(pk_ref[pl.ds(j, 1), :])
        return lax.cond(pq_min - pk_max >= MAX_DISTANCE,
                        lambda c: attend(j, c, False), lambda c: attend(j, c, True), carry)

    init = (tuple(jnp.full((tq, 1), NEG, F32) for _ in range(DSA_HEADS)),
            tuple(jnp.zeros((tq, 1), F32) for _ in range(DSA_HEADS)),
            tuple(jnp.zeros((tq, LANES), F32) for _ in range(DSA_HEADS // 2)))
    ms, ls, accs = lax.fori_loop(0, i + 1, p3_body, init)
    outs = [accs[p] * jnp.where(lo_half, 1.0 / ls[2 * p], 1.0 / ls[2 * p + 1])
            for p in range(DSA_HEADS // 2)]
    o_ref[...] = jnp.concatenate(outs, axis=1).astype(BF)


def _dsa(gq, gk, gd, pos_col, ki2, pos_row, tab, B, S, tq=256):
    nq = S // tq
    n_sel = min(TOPK_MAX, S // 4)
    qrow = lambda c: (lambda b, i: (b * nq + i, c))
    return pl.pallas_call(
        functools.partial(_dsa_kernel, tq=tq, n_sel=n_sel),
        grid=(B, nq),
        in_specs=[pl.BlockSpec((tq, W_Q), qrow(0)),
                  pl.BlockSpec((tq, LANES), qrow(1)),
                  pl.BlockSpec((tq, DSA_WIDTH), qrow(0)),
                  pl.BlockSpec((tq, 1), qrow(0)),
                  pl.BlockSpec((S, LANES), lambda b, i: (b, 0)),
                  pl.BlockSpec((S, DSA_WIDTH), lambda b, i: (b, 1)),
                  pl.BlockSpec((S, DSA_WIDTH), lambda b, i: (b, 2)),
                  pl.BlockSpec((None, nq, tq), lambda b, i: (b, 0, 0)),
                  pl.BlockSpec(memory_space=pltpu.SMEM)],
        out_specs=pl.BlockSpec((tq, DSA_WIDTH), qrow(0)),
        out_shape=jax.ShapeDtypeStruct((B * S, DSA_WIDTH), BF),
        scratch_shapes=[pltpu.VMEM((nq, tq, tq), F32), pltpu.VMEM((IDX_HEADS, tq, LANES), BF)],
        compiler_params=_cparams(("parallel", "arbitrary")),
        name="dsa",
    )(gq, gk, gd, pos_col, ki2, gd, gd, pos_row, tab)


def _outmlp_kernel(h_ref, yh_ref, ym_ref, yd_ref, wo_ref, nw_ref, w1_ref, w2_ref, fw_ref, o_ref,
                   *, final, ff_chunk):
    mixed = jnp.concatenate([yh_ref[...], ym_ref[...], yd_ref[...]], axis=1)
    h = h_ref[...] + _dot(mixed, wo_ref[...])
    u = _rms(h, nw_ref[...]).astype(BF)
    out = h
    for c in range(D_FF // ff_chunk):
        cs = slice(c * ff_chunk, (c + 1) * ff_chunk)
        a = jnp.maximum(_dot(u, w1_ref[:, cs]), 0.0)
        out = out + _dot((a * a).astype(BF), w2_ref[cs, :])
    if final:
        out = _rms(out, fw_ref[...])
    o_ref[...] = out


def _outmlp(h, yh, ym, yd, wo, nw, w1, w2, fw, final, tm=512, ff_chunk=1024):
    T = h.shape[0]
    row = lambda i: (i, 0)
    return pl.pallas_call(
        functools.partial(_outmlp_kernel, final=final, ff_chunk=ff_chunk),
        grid=(T // tm,),
        in_specs=[pl.BlockSpec((tm, D_MODEL), row), pl.BlockSpec((tm, HGRN_W), row),
                  pl.BlockSpec((tm, MLA_HEADS * MLA_V), row), pl.BlockSpec((tm, DSA_WIDTH), row),
                  _const_spec((D_MODEL, D_MODEL)), _const_spec((1, D_MODEL)),
                  _const_spec((D_MODEL, D_FF)), _const_spec((D_FF, D_MODEL)),
                  _const_spec((1, D_MODEL))],
        out_specs=pl.BlockSpec((tm, D_MODEL), row),
        out_shape=jax.ShapeDtypeStruct((T, D_MODEL), F32),
        compiler_params=_cparams(("parallel",)),
        name="outmlp",
    )(h, yh, ym, yd, wo, nw, w1, w2, fw)


def _rot_cols(w):
    half = w.shape[-1] // 2
    return jnp.concatenate([-w[..., half:], w[..., :half]], axis=-1)


def _place(w, width, off):
    pad = [(0, 0)] * (w.ndim - 1) + [(off, width - off - w.shape[-1])]
    return jnp.pad(w, pad)


def _layout_w_in(w_in):
    sizes = (HGRN_W, HGRN_W, HGRN_W, HGRN_W, MLA_Q_LORA, MLA_KV_LORA + MLA_ROPE,
             DSA_WIDTH, DSA_WIDTH, DSA_WIDTH, IDX_HEADS * IDX_DIM, IDX_DIM, IDX_HEADS)
    offs = [0]
    for s in sizes:
        offs.append(offs[-1] + s)
    hq, hf, hi, hg, mqa, mkva, dq, dk, dv, iq, ik, iw = [
        w_in[..., offs[n]:offs[n + 1]] for n in range(len(sizes))]
    ckv, kpe = mkva[..., :MLA_KV_LORA], mkva[..., MLA_KV_LORA:]
    cols = [hq, hf, hi, hg,
            _place(mqa, 256, 0), ckv, _place(kpe, LANES, MLA_NOPE), _place(_rot_cols(kpe), LANES, MLA_NOPE),
            dq * (DSA_HEAD_DIM ** -0.5), dk, dv,
            iq,
            ik, ik, _place(iw, LANES, 0)]
    return jnp.concatenate(cols, axis=-1).astype(BF)


def _layout_mla(w_qb, w_kvb):
    L = w_qb.shape[0]
    dq = MLA_NOPE + MLA_ROPE
    wq = w_qb.reshape(L, MLA_Q_LORA, MLA_HEADS, dq)
    wq_rot = jnp.concatenate([jnp.zeros_like(wq[..., :MLA_NOPE]), _rot_cols(wq[..., MLA_NOPE:])], axis=-1)
    pad_q = lambda w: jnp.pad(w, ((0, 0), (0, 256 - MLA_Q_LORA), (0, 0), (0, HEAD_PAD - dq))).reshape(
        L, 256, MLA_HEADS * HEAD_PAD).astype(BF)
    wkv = w_kvb.reshape(L, MLA_KV_LORA, MLA_HEADS, MLA_NOPE + MLA_V)
    pad_kv = lambda w: jnp.pad(w, ((0, 0), (0, 0), (0, 0), (0, HEAD_PAD - w.shape[-1]))).reshape(
        L, MLA_KV_LORA, MLA_HEADS * HEAD_PAD).astype(BF)
    return pad_q(wq), pad_q(wq_rot), pad_kv(wkv[..., :MLA_NOPE]), pad_kv(wkv[..., MLA_NOPE:])


def kernel(x, positions, attn_norm_w, w_in, hgrn_lb_logits, hgrn_norm_w, mla_q_norm_w, mla_w_qb,
           mla_kv_norm_w, mla_w_kvb, idx_k_norm_w, idx_k_norm_b, rel_bias_table, w_out,
           mlp_norm_w, w_mlp_in, w_mlp_out, final_norm_w):
    B, S, _ = x.shape
    T = B * S
    depth = w_in.shape[0]

    inv_freq = 1.0 / (ROPE_THETA ** (jnp.arange(0, MLA_ROPE, 2, dtype=F32) / MLA_ROPE))
    ang = positions.astype(F32)[..., None] * inv_freq
    cos, sin = jnp.cos(ang).reshape(T, -1), jnp.sin(ang).reshape(T, -1)
    cos_t = jnp.concatenate([jnp.ones((T, MLA_NOPE), F32), cos, cos,
                             jnp.zeros((T, HEAD_PAD - MLA_NOPE - MLA_ROPE), F32)], axis=1)
    sin_t = jnp.concatenate([jnp.zeros((T, MLA_NOPE), F32), sin, sin,
                             jnp.zeros((T, HEAD_PAD - MLA_NOPE - MLA_ROPE), F32)], axis=1)
    pos_col = positions.reshape(T, 1)
    pos_row = positions.reshape(B, S // 256, 256)

    lb = jnp.cumsum(jax.nn.softmax(hgrn_lb_logits.astype(F32), axis=0), axis=0)
    lb = lb - lb[0:1]
    lb3 = jnp.stack([jnp.log(lb), jnp.log1p(-lb), 1.0 - lb], axis=1)

    w_cat = _layout_w_in(w_in)
    wq, wqr, wk, wv = _layout_mla(mla_w_qb, mla_w_kvb)
    qnw = jnp.pad(mla_q_norm_w, ((0, 0), (0, 256 - MLA_Q_LORA)))
    lnw = jnp.concatenate([idx_k_norm_w, idx_k_norm_w], axis=-1)
    lnb = jnp.concatenate([idx_k_norm_b, idx_k_norm_b], axis=-1)
    wo = w_out.astype(BF)
    w1 = w_mlp_in.astype(BF)
    w2 = w_mlp_out.astype(BF)
    tab = rel_bias_table.astype(F32)

    h = x.reshape(T, D_MODEL)
    for l in range(depth):
        gh, gm, gd, gq, gk = _inproj(h, attn_norm_w[l][None], w_cat[l])
        q_m, k_m, v_m, ki2 = _prep(gm, gk, cos_t, sin_t, qnw[l][None], wq[l], wqr[l],
                                   mla_kv_norm_w[l][None], wk[l], wv[l], lnw[l][None], lnb[l][None])
        y_h = _hgrn(gh, lb3[l], hgrn_norm_w[l][None], B, S)
        y_m = _mla_attn(q_m, k_m, v_m, B, S)
        y_d = _dsa(gq, gk, gd, pos_col, ki2, pos_row, tab, B, S)
        h = _outmlp(h, y_h, y_m, y_d, wo[l], mlp_norm_w[l][None], w1[l], w2[l],
                    final_norm_w[None], final=(l == depth - 1))
    return h.reshape(B, S, D_MODEL)
```

```python
import functools
import math

import jax
import jax.numpy as jnp
from jax import lax
from jax.experimental import pallas as pl
from jax.experimental.pallas import tpu as pltpu

D_MODEL = 1024
HGRN_HEADS = 4
HGRN_DK = 128
HGRN_DV = 128
HGRN_W = HGRN_HEADS * HGRN_DK
HGRN_CHUNK = 32
MLA_HEADS = 4
MLA_NOPE = 64
MLA_ROPE = 32
MLA_V = 64
MLA_Q_LORA = 192
MLA_KV_LORA = 128
ROPE_THETA = 10000.0
DSA_HEADS = 4
DSA_HEAD_DIM = 64
DSA_WIDTH = DSA_HEADS * DSA_HEAD_DIM
IDX_HEADS = 8
IDX_DIM = 64
TOPK_MAX = 256
NUM_BUCKETS = 32
MAX_EXACT = NUM_BUCKETS // 2
MAX_DISTANCE = 128
D_FF = 4 * D_MODEL
EPS = 1e-6

LANES = 128
HEAD_PAD = 128
NEG = -1e30
MLA_TILE = 256
VMEM_LIMIT = 52 * 1024 * 1024

C_H = 0
C_M = C_H + 4 * HGRN_W
W_M = 640
C_D = C_M + W_M
W_D = 3 * DSA_WIDTH
C_Q = C_D + W_D
W_Q = IDX_HEADS * IDX_DIM
C_K = C_Q + W_Q
W_K = 256
N_PAD = C_K + W_K

BF = jnp.bfloat16
F32 = jnp.float32


def _dot(a, b):
    return jnp.dot(a, b, preferred_element_type=F32)


def _dot_nt(a, b):
    return lax.dot_general(a, b, (((1,), (1,)), ((), ())), preferred_element_type=F32)


def _rms(x, w, n=None):
    n = x.shape[-1] if n is None else n
    ms = jnp.sum(x * x, axis=-1, keepdims=True) * (1.0 / n)
    return x * lax.rsqrt(ms + EPS) * w


def _cparams(sem):
    return pltpu.CompilerParams(dimension_semantics=sem, vmem_limit_bytes=VMEM_LIMIT)


def _const_spec(shape):
    nd = len(shape)
    return pl.BlockSpec(shape, lambda *_: (0,) * nd, pipeline_mode=pl.Buffered(1))


def _inproj_kernel(x_ref, nw_ref, w_ref, oh_ref, om_ref, od_ref, oq_ref, ok_ref):
    u = _rms(x_ref[...], nw_ref[...]).astype(BF)
    oh_ref[...] = _dot(u, w_ref[:, C_H:C_M])
    om_ref[...] = _dot(u, w_ref[:, C_M:C_D])
    od_ref[...] = _dot(u, w_ref[:, C_D:C_Q]).astype(BF)
    oq_ref[...] = _dot(u, w_ref[:, C_Q:C_K]).astype(BF)
    ok_ref[...] = _dot(u, w_ref[:, C_K:N_PAD])


def _inproj(h, nw, w, tm=512):
    T = h.shape[0]
    row = lambda i: (i, 0)
    return pl.pallas_call(
        _inproj_kernel,
        grid=(T // tm,),
        in_specs=[pl.BlockSpec((tm, D_MODEL), row), _const_spec((1, D_MODEL)),
                  _const_spec((D_MODEL, N_PAD))],
        out_specs=[pl.BlockSpec((tm, 4 * HGRN_W), row), pl.BlockSpec((tm, W_M), row),
                   pl.BlockSpec((tm, W_D), row), pl.BlockSpec((tm, W_Q), row),
                   pl.BlockSpec((tm, W_K), row)],
        out_shape=[jax.ShapeDtypeStruct((T, 4 * HGRN_W), F32), jax.ShapeDtypeStruct((T, W_M), F32),
                   jax.ShapeDtypeStruct((T, W_D), BF), jax.ShapeDtypeStruct((T, W_Q), BF),
                   jax.ShapeDtypeStruct((T, W_K), F32)],
        compiler_params=_cparams(("parallel",)),
        name="inproj",
    )(h, nw, w)


def _prep_kernel(gm_ref, gk_ref, cos_ref, sin_ref, qnw_ref, wq_ref, wqr_ref, kvnw_ref, wk_ref,
                 wvt_ref, lnw_ref, lnb_ref, q_ref, k_ref, vt_ref, ki_ref):
    cs = cos_ref[...]
    sn = sin_ref[...]
    cs4 = jnp.concatenate([cs] * MLA_HEADS, axis=1)
    sn4 = jnp.concatenate([sn] * MLA_HEADS, axis=1)
    qn = _rms(gm_ref[:, 0:256], qnw_ref[...], n=MLA_Q_LORA).astype(BF)
    scale = (MLA_NOPE + MLA_ROPE) ** -0.5 * math.log2(math.e)
    q = (_dot(qn, wq_ref[...]) * cs4 + _dot(qn, wqr_ref[...]) * sn4) * scale
    q_ref[...] = q.astype(BF)
    cn = _rms(gm_ref[:, 256:384], kvnw_ref[...]).astype(BF)
    kp = gm_ref[:, 384:512] * cs + gm_ref[:, 512:640] * sn
    k = _dot(cn, wk_ref[...]) + jnp.concatenate([kp] * MLA_HEADS, axis=1)
    k_ref[...] = k.astype(BF)
    vt = _dot_nt(wvt_ref[...], cn)
    row = lax.broadcasted_iota(jnp.int32, vt.shape, 0)
    vt_ref[...] = jnp.where(row % HEAD_PAD == MLA_V, 1.0, vt).astype(BF)
    x = gk_ref[:, 0:LANES]
    first = lax.broadcasted_iota(jnp.int32, x.shape, 1) < IDX_DIM
    mu = jnp.sum(jnp.where(first, x, 0.0), axis=-1, keepdims=True) * (1.0 / IDX_DIM)
    xc = x - mu
    var = jnp.sum(jnp.where(first, xc * xc, 0.0), axis=-1, keepdims=True) * (1.0 / IDX_DIM)
    ki_ref[...] = (xc * lax.rsqrt(var + EPS) * lnw_ref[...] + lnb_ref[...]).astype(BF)


def _prep(gm, gk, cos_t, sin_t, qnw, wq, wqr, kvnw, wk, wvt, lnw, lnb, tm=MLA_TILE):
    T = gm.shape[0]
    row = lambda i: (i, 0)
    hp = MLA_HEADS * HEAD_PAD
    return pl.pallas_call(
        _prep_kernel,
        grid=(T // tm,),
        in_specs=[pl.BlockSpec((tm, W_M), row), pl.BlockSpec((tm, W_K), row),
                  pl.BlockSpec((tm, LANES), row), pl.BlockSpec((tm, LANES), row),
                  _const_spec((1, 256)), _const_spec((256, hp)), _const_spec((256, hp)),
                  _const_spec((1, MLA_KV_LORA)), _const_spec((MLA_KV_LORA, hp)),
                  _const_spec((hp, MLA_KV_LORA)), _const_spec((1, LANES)), _const_spec((1, LANES))],
        out_specs=[pl.BlockSpec((tm, hp), row), pl.BlockSpec((tm, hp), row),
                   pl.BlockSpec((None, hp, tm), lambda i: (i, 0, 0)), pl.BlockSpec((tm, LANES), row)],
        out_shape=[jax.ShapeDtypeStruct((T, hp), BF), jax.ShapeDtypeStruct((T, hp), BF),
                   jax.ShapeDtypeStruct((T // tm, hp, tm), BF), jax.ShapeDtypeStruct((T, LANES), BF)],
        compiler_params=_cparams(("parallel",)),
        name="prep",
    )(gm, gk, cos_t, sin_t, qnw, wq, wqr, kvnw, wk, wvt, lnw, lnb)


def _split3(x):
    a = x.astype(BF)
    r = x - a.astype(F32)
    b = r.astype(BF)
    c = (r - b.astype(F32)).astype(BF)
    return a, b, c


def _hgrn_kernel(q_ref, f_ref, i_ref, g_ref, lb_ref, nw_ref, o_ref, st_ref, os_ref, *, tb):
    C = HGRN_CHUNK
    nc = tb // C

    @pl.when(pl.program_id(1) == 0)
    def _():
        st_ref[...] = jnp.zeros_like(st_ref)

    fp = f_ref[...]
    log_lb = lb_ref[0:1, :]
    log1m_lb = lb_ref[1:2, :]
    one_m_lb = lb_ref[2:3, :]
    ls = jnp.minimum(fp, 0.0) - jnp.log1p(jnp.exp(-jnp.abs(fp)))
    b = log1m_lb + ls
    log_f = jnp.maximum(log_lb, b) + jnp.log1p(jnp.exp(-jnp.abs(log_lb - b)))
    k = one_m_lb * (1.0 / (1.0 + jnp.exp(fp)))

    r = lax.broadcasted_iota(jnp.int32, (tb, tb), 0)
    c = lax.broadcasted_iota(jnp.int32, (tb, tb), 1)
    same = (r // C) == (c // C)
    tri = jnp.where(same & (c <= r), 1.0, 0.0).astype(BF)
    blk = jnp.where(same, 1.0, 0.0).astype(BF)
    a0, a1, a2 = _split3(log_f)
    G = _dot(tri, a0) + _dot(tri, a1) + _dot(tri, a2)
    G_last = _dot(blk, a0) + _dot(blk, a1) + _dot(blk, a2)

    eg = jnp.exp(G)
    q_dec = (q_ref[...] * (HGRN_DK ** -0.5) * eg).astype(BF)
    k_inv = (k * jnp.exp(-G)).astype(BF)
    k_state = k * jnp.exp(G_last - G)
    decay = jnp.exp(G_last)
    v = i_ref[...]
    vb = v.astype(BF)

    rr = lax.broadcasted_iota(jnp.int32, (C, C), 0)
    cc = lax.broadcasted_iota(jnp.int32, (C, C), 1)
    causal = cc <= rr

    for h in range(HGRN_HEADS):
        hs = slice(h * HGRN_DK, (h + 1) * HGRN_DK)
        st = st_ref[h]
        for n in range(nc):
            rs = slice(n * C, (n + 1) * C)
            qd = q_dec[rs, hs]
            A = jnp.where(causal, _dot_nt(qd, k_inv[rs, hs]), 0.0)
            o = _dot(A.astype(BF), vb[rs, hs]) + _dot_nt(qd, st.astype(BF))
            os_ref[rs, hs] = o
            vt = v[rs, hs].T.astype(BF)
            st = decay[n * C:n * C + 1, hs] * st + _dot(vt, k_state[rs, hs].astype(BF))
        st_ref[h] = st

    g = g_ref[...]
    gate = g * (1.0 / (1.0 + jnp.exp(-g)))
    for h in range(HGRN_HEADS):
        hs = slice(h * HGRN_DK, (h + 1) * HGRN_DK)
        o_ref[:, hs] = (_rms(os_ref[:, hs], nw_ref[...]) * gate[:, hs]).astype(BF)


def _hgrn(gh, lb3, nw, B, S, tb=256):
    nb = S // tb
    col = lambda j: (lambda b, i: (b * nb + i, j))
    return pl.pallas_call(
        functools.partial(_hgrn_kernel, tb=tb),
        grid=(B, nb),
        in_specs=[pl.BlockSpec((tb, HGRN_W), col(0)), pl.BlockSpec((tb, HGRN_W), col(1)),
                  pl.BlockSpec((tb, HGRN_W), col(2)), pl.BlockSpec((tb, HGRN_W), col(3)),
                  pl.BlockSpec((3, HGRN_W), lambda b, i: (0, 0)),
                  pl.BlockSpec((1, HGRN_DV), lambda b, i: (0, 0))],
        out_specs=pl.BlockSpec((tb, HGRN_W), col(0)),
        out_shape=jax.ShapeDtypeStruct((B * S, HGRN_W), BF),
        scratch_shapes=[pltpu.VMEM((HGRN_HEADS, HGRN_DV, HGRN_DK), F32),
                        pltpu.VMEM((tb, HGRN_W), F32)],
        compiler_params=_cparams(("parallel", "arbitrary")),
        name="hgrn",
    )(gh, gh, gh, gh, lb3, nw)


def _softmax_step(s, m, l):
    m_new = jnp.maximum(m, jnp.max(s, axis=-1, keepdims=True))
    alpha = jnp.exp(m - m_new)
    p = jnp.exp(s - m_new)
    l_new = alpha * l + jnp.sum(p, axis=-1, keepdims=True)
    return p.astype(BF), m_new, l_new, alpha


def _mla_kernel(q_ref, k_ref, vt_ref, o_ref, *, t):
    i = pl.program_id(1)
    key = lax.broadcasted_iota(jnp.int32, (t, t), 0)
    qry = lax.broadcasted_iota(jnp.int32, (t, t), 1)
    causal = key <= qry

    def tile(j, carry, mask):
        ks = pl.ds(pl.multiple_of(j * t, t), t)
        heads = [slice(h * HEAD_PAD, (h + 1) * HEAD_PAD) for h in range(MLA_HEADS)]
        ss = [_dot_nt(k_ref[ks, hs], q_ref[:, hs]) for hs in heads]
        if mask:
            ss = [jnp.where(causal, s, NEG) for s in ss]
        ms = [jnp.maximum(carry[h][0], jnp.max(ss[h], axis=0, keepdims=True))
              for h in range(MLA_HEADS)]
        ps = [jnp.exp2(ss[h] - ms[h]).astype(BF) for h in range(MLA_HEADS)]
        return tuple((ms[h], jnp.exp2(carry[h][0] - ms[h]) * carry[h][1]
                      + _dot(vt_ref[j, heads[h], :], ps[h])) for h in range(MLA_HEADS))

    init = tuple((jnp.full((1, t), NEG, F32), jnp.zeros((HEAD_PAD, t), F32))
                 for _ in range(MLA_HEADS))
    carry = lax.fori_loop(0, i, lambda j, c: tile(j, c, False), init)
    carry = tile(i, carry, True)
    outs = [acc[:MLA_V, :] * (1.0 / acc[MLA_V:MLA_V + 1, :]) for (_, acc) in carry]
    o_ref[...] = jnp.concatenate(outs, axis=0).T.astype(BF)


def _mla_attn(q, k, vt, B, S, t=MLA_TILE):
    nq = S // t
    hp = MLA_HEADS * HEAD_PAD
    return pl.pallas_call(
        functools.partial(_mla_kernel, t=t),
        grid=(B, nq),
        in_specs=[pl.BlockSpec((t, hp), lambda b, i: (b * nq + i, 0)),
                  pl.BlockSpec((S, hp), lambda b, i: (b, 0)),
                  pl.BlockSpec((nq, hp, t), lambda b, i: (b, 0, 0))],
        out_specs=pl.BlockSpec((t, MLA_HEADS * MLA_V), lambda b, i: (b * nq + i, 0)),
        out_shape=jax.ShapeDtypeStruct((B * S, MLA_HEADS * MLA_V), BF),
        compiler_params=_cparams(("parallel", "arbitrary")),
        name="mla_attn",
    )(q, k, vt)


BISECT_CAP = 48
_LAST_BUCKET_FROM = next(
    n for n in range(MAX_EXACT, 1 << 20)
    if MAX_EXACT + int(math.log(n / MAX_EXACT) / math.log(MAX_DISTANCE / MAX_EXACT)
                       * (NUM_BUCKETS - MAX_EXACT)) >= NUM_BUCKETS - 1)
assert _LAST_BUCKET_FROM <= LANES - 1 and _LAST_BUCKET_FROM <= MAX_DISTANCE


def _dsa_kernel(qi_ref, w_ref, qd_ref, pq_ref, ki_ref, kd_ref, vd_ref, pk_ref, tab_ref, o_ref,
                sc_ref, qm_ref, *, tq, n_sel):
    i = pl.program_id(1)
    lane = lax.broadcasted_iota(jnp.int32, (tq, LANES), 1)
    lo_half = lane < DSA_HEAD_DIM
    rr = lax.broadcasted_iota(jnp.int32, (tq, tq), 0)
    cc = lax.broadcasted_iota(jnp.int32, (tq, tq), 1)
    causal = cc <= rr

    zero_b = jnp.zeros((tq, LANES), BF)
    for p in range(IDX_HEADS // 2):
        qp = qi_ref[:, p * LANES:(p + 1) * LANES]
        qm_ref[2 * p] = jnp.where(lo_half, qp, zero_b)
        qm_ref[2 * p + 1] = jnp.where(lo_half, zero_b, qp)
    w_scale = (IDX_HEADS ** -0.5) * (IDX_DIM ** -0.5)
    wcols = [w_ref[:, h:h + 1] * w_scale for h in range(IDX_HEADS)]

    def score_tile(j):
        kt = ki_ref[pl.ds(pl.multiple_of(j * tq, tq), tq), :]
        acc = jnp.zeros((tq, tq), F32)
        for h in range(IDX_HEADS):
            acc = acc + jnp.maximum(_dot_nt(qm_ref[h], kt), 0.0) * wcols[h]
        return acc

    def p1_body(j, _):
        sc_ref[j] = score_tile(j)
        return 0

    lax.fori_loop(0, i, p1_body, 0)
    sc_ref[i] = jnp.where(causal, score_tile(i), -jnp.inf)

    RG = LANES
    n_groups = tq // RG
    n_chunks = tq // LANES
    ones_g = jnp.ones((RG, LANES), BF)
    k_sel = float(n_sel)

    def lane_chunks(t):
        return [t[:, q * LANES:(q + 1) * LANES] for q in range(n_chunks)]

    def over_tiles(body, init):
        c = lax.fori_loop(0, (i + 1) // 2, lambda jj, c: body(2 * jj + 1, body(2 * jj, c)), init)
        return lax.cond((i + 1) % 2 == 1, lambda c: body(i, c), lambda c: c, c)

    def row_count(c):
        return _dot(c.astype(BF), ones_g)

    def count_gt(g, thr):
        rs = slice(g * RG, (g + 1) * RG)

        def body(j, c):
            for t in lane_chunks(sc_ref[j, rs, :]):
                c = c + jnp.where(t > thr, 1.0, 0.0)
            return c

        return row_count(over_tiles(body, jnp.zeros((RG, LANES), F32)))

    def row_stats(g):
        rs = slice(g * RG, (g + 1) * RG)

        def body(j, c):
            mx, mn, cp, cz = c
            for t in lane_chunks(sc_ref[j, rs, :]):
                mx = jnp.maximum(mx, t)
                mn = jnp.minimum(mn, jnp.where(t == -jnp.inf, jnp.inf, t))
                cp = cp + jnp.where(t > 0.0, 1.0, 0.0)
                cz = cz + jnp.where(t == 0.0, 1.0, 0.0)
            return mx, mn, cp, cz

        z = jnp.zeros((RG, LANES), F32)
        mx, mn, cp, cz = over_tiles(body, (z - jnp.inf, z + jnp.inf, z, z))
        row_max = jnp.broadcast_to(jnp.max(mx, axis=-1, keepdims=True), (RG, LANES))
        row_min = jnp.broadcast_to(jnp.min(mn, axis=-1, keepdims=True), (RG, LANES))
        return row_max, row_min, row_count(cp), row_count(cz)

    row_mins, needs, lo0, hi0, c_lo0 = [], [], [], [], []
    for g in range(n_groups):
        row_max, row_min, c_pos, c_zero = row_stats(g)
        n_valid = (i * tq + g * RG + 1
                   + lax.broadcasted_iota(jnp.int32, (RG, LANES), 0)).astype(F32)
        big = n_valid > k_sel
        pos_row = jnp.logical_and(big, c_pos >= k_sel)
        tie_row = jnp.logical_and(jnp.logical_and(big, c_pos < k_sel), c_pos + c_zero >= k_sel)
        neg_row = jnp.logical_and(big, c_pos + c_zero < k_sel)
        row_mins.append(row_min)
        needs.append(jnp.where(tie_row, k_sel - c_pos, 0.0))
        lo0.append(jnp.where(jnp.logical_or(pos_row, tie_row), 0.0, -jnp.inf))
        hi0.append(jnp.where(neg_row, 0.0, row_max))
        c_lo0.append(jnp.where(pos_row, c_pos, jnp.where(tie_row, k_sel, n_valid)))

    def open_rows(c_lo):
        return functools.reduce(jnp.maximum, [jnp.max(jnp.where(c > k_sel, 1.0, 0.0)) for c in c_lo])

    def bis_cond(c):
        return jnp.logical_and(c[0] < BISECT_CAP, c[1] > 0.0)

    def bis_body(c):
        it, _, los, his, c_los = c
        los, his, c_los = list(los), list(his), list(c_los)
        for g in range(n_groups):
            active = c_los[g] > k_sel
            base = jnp.maximum(los[g], row_mins[g])
            mid = base + 0.5 * (his[g] - base)
            cnt = count_gt(g, mid)
            up = jnp.logical_and(active, cnt >= k_sel)
            dn = jnp.logical_and(active, cnt < k_sel)
            los[g] = jnp.where(up, mid, los[g])
            c_los[g] = jnp.where(up, cnt, c_los[g])
            his[g] = jnp.where(dn, mid, his[g])
        return it + 1, open_rows(c_los), tuple(los), tuple(his), tuple(c_los)

    _, _, los, _, _ = lax.while_loop(
        bis_cond, bis_body, (jnp.int32(0), open_rows(c_lo0), tuple(lo0), tuple(hi0), tuple(c_lo0)))

    def mask_plain(_):
        def body(j, c):
            for g in range(n_groups):
                rs = slice(g * RG, (g + 1) * RG)
                sel = [jnp.where(t > los[g], 0.0, NEG) for t in lane_chunks(sc_ref[j, rs, :])]
                sc_ref[j, rs, :] = jnp.concatenate(sel, axis=1)
            return c
        return lax.fori_loop(0, i + 1, body, 0)

    def mask_ties(_):
        upper = jnp.where(rr <= cc, 1.0, 0.0).astype(BF)

        def body(j, seen):
            seen = list(seen)
            for g in range(n_groups):
                rs = slice(g * RG, (g + 1) * RG)
                t = sc_ref[j, rs, :]
                zb = jnp.where(t == 0.0, 1.0, 0.0).astype(BF)
                rank = lane_chunks(_dot(zb, upper))
                out = []
                for q, tc in enumerate(lane_chunks(t)):
                    take = jnp.logical_and(tc == 0.0, rank[q] + seen[g] <= needs[g])
                    out.append(jnp.where(jnp.logical_or(tc > los[g], take), 0.0, NEG))
                sc_ref[j, rs, :] = jnp.concatenate(out, axis=1)
                seen[g] = seen[g] + _dot(zb, jnp.ones((tq, LANES), BF))
            return tuple(seen)
        lax.fori_loop(0, i + 1, body, tuple(jnp.zeros((RG, LANES), F32) for _ in range(n_groups)))
        return 0

    any_tie = functools.reduce(jnp.maximum, [jnp.max(n) for n in needs])
    lax.cond(any_tie > 0.0, mask_ties, mask_plain, 0)

    qd = [qd_ref[:, p * LANES:(p + 1) * LANES] for p in range(DSA_HEADS // 2)]
    qh = []
    for p in range(DSA_HEADS // 2):
        qh.append(jnp.where(lo_half, qd[p], zero_b))
        qh.append(jnp.where(lo_half, zero_b, qd[p]))
    pq = pq_ref[...]
    pq_min = jnp.min(pq)
    far_bias = [tab_ref[NUM_BUCKETS - 1, h] for h in range(DSA_HEADS)]
    log_ratio = math.log(MAX_DISTANCE / MAX_EXACT)

    dist = lax.broadcasted_iota(jnp.int32, (8, LANES), 1)
    large = MAX_EXACT + (jnp.log(jnp.maximum(dist, 1).astype(F32) / MAX_EXACT) / log_ratio
                         * (NUM_BUCKETS - MAX_EXACT)).astype(jnp.int32)
    bucket = jnp.where(dist < MAX_EXACT, dist, jnp.minimum(large, NUM_BUCKETS - 1))
    by_dist = []
    for h in range(DSA_HEADS):
        bh = jnp.full((8, LANES), tab_ref[0, h], F32)
        for jb in range(1, NUM_BUCKETS):
            bh = jnp.where(bucket >= jb, tab_ref[jb, h], bh)
        by_dist.append(pltpu.repeat(bh, tq // 8, axis=0))

    def pair_bias(pk):
        n = jnp.clip(pq - pk, 0, LANES - 1)
        return [jnp.concatenate([jnp.take_along_axis(by_dist[h], nc, axis=1)
                                 for nc in lane_chunks(n)], axis=1) for h in range(DSA_HEADS)]

    def attend(j, carry, near):
        ms, ls, accs = carry
        ks = pl.ds(pl.multiple_of(j * tq, tq), tq)
        madd = sc_ref[j]
        pk = pk_ref[pl.ds(j, 1), :]
        if near:
            bias = pair_bias(pk)
        else:
            bias = far_bias
        ms_n, ls_n, accs_n = [], [], []
        for p in range(DSA_HEADS // 2):
            kp = kd_ref[ks, p * LANES:(p + 1) * LANES]
            vp = vd_ref[ks, p * LANES:(p + 1) * LANES]
            pv, al = [], []
            for half in range(2):
                h = 2 * p + half
                s = _dot_nt(qh[h], kp) + (madd + bias[h])
                pr, m_n, l_n, alpha = _softmax_step(s, ms[h], ls[h])
                ms_n.append(m_n)
                ls_n.append(l_n)
                pv.append(_dot(pr, vp))
                al.append(alpha)
            alpha_p = jnp.where(lo_half, al[0], al[1])
            accs_n.append(alpha_p * accs[p] + jnp.where(lo_half, pv[0], pv[1]))
        return tuple(ms_n), tuple(ls_n), tuple(accs_n)

    def p3_body(j, carry):
        pk_max = jnp.max(pk_ref[pl.ds(j, 1), :])
        return lax.cond(pq_min - pk_max >= MAX_DISTANCE,
                        lambda c: attend(j, c, False), lambda c: attend(j, c, True), carry)

    init = (tuple(jnp.full((tq, 1), NEG, F32) for _ in range(DSA_HEADS)),
            tuple(jnp.zeros((tq, 1), F32) for _ in range(DSA_HEADS)),
            tuple(jnp.zeros((tq, LANES), F32) for _ in range(DSA_HEADS // 2)))
    ms, ls, accs = lax.fori_loop(0, i + 1, p3_body, init)
    outs = [accs[p] * jnp.where(lo_half, 1.0 / ls[2 * p], 1.0 / ls[2 * p + 1])
            for p in range(DSA_HEADS // 2)]
    o_ref[...] = jnp.concatenate(outs, axis=1).astype(BF)


def _dsa(gq, gk, gd, pos_col, ki2, pos_row, tab, B, S, tq=256):
    nq = S // tq
    n_sel = min(TOPK_MAX, S // 4)
    qrow = lambda c: (lambda b, i: (b * nq + i, c))
    return pl.pallas_call(
        functools.partial(_dsa_kernel, tq=tq, n_sel=n_sel),
        grid=(B, nq),
        in_specs=[pl.BlockSpec((tq, W_Q), qrow(0)),
                  pl.BlockSpec((tq, LANES), qrow(1)),
                  pl.BlockSpec((tq, DSA_WIDTH), qrow(0)),
                  pl.BlockSpec((tq, 1), qrow(0)),
                  pl.BlockSpec((S, LANES), lambda b, i: (b, 0)),
                  pl.BlockSpec((S, DSA_WIDTH), lambda b, i: (b, 1)),
                  pl.BlockSpec((S, DSA_WIDTH), lambda b, i: (b, 2)),
                  pl.BlockSpec((None, nq, tq), lambda b, i: (b, 0, 0)),
                  pl.BlockSpec(memory_space=pltpu.SMEM)],
        out_specs=pl.BlockSpec((tq, DSA_WIDTH), qrow(0)),
        out_shape=jax.ShapeDtypeStruct((B * S, DSA_WIDTH), BF),
        scratch_shapes=[pltpu.VMEM((nq, tq, tq), F32), pltpu.VMEM((IDX_HEADS, tq, LANES), BF)],
        compiler_params=_cparams(("parallel", "arbitrary")),
        name="dsa",
    )(gq, gk, gd, pos_col, ki2, gd, gd, pos_row, tab)


def _outmlp_kernel(h_ref, yh_ref, ym_ref, yd_ref, wo_ref, nw_ref, w1_ref, w2_ref, fw_ref, o_ref,
                   *, final, ff_chunk):
    mixed = jnp.concatenate([yh_ref[...], ym_ref[...], yd_ref[...]], axis=1)
    h = h_ref[...] + _dot(mixed, wo_ref[...])
    u = _rms(h, nw_ref[...]).astype(BF)
    out = h
    for c in range(D_FF // ff_chunk):
        cs = slice(c * ff_chunk, (c + 1) * ff_chunk)
        a = jnp.maximum(_dot(u, w1_ref[:, cs]), 0.0)
        out = out + _dot((a * a).astype(BF), w2_ref[cs, :])
    if final:
        out = _rms(out, fw_ref[...])
    o_ref[...] = out


def _outmlp(h, yh, ym, yd, wo, nw, w1, w2, fw, final, tm=512, ff_chunk=1024):
    T = h.shape[0]
    row = lambda i: (i, 0)
    return pl.pallas_call(
        functools.partial(_outmlp_kernel, final=final, ff_chunk=ff_chunk),
        grid=(T // tm,),
        in_specs=[pl.BlockSpec((tm, D_MODEL), row), pl.BlockSpec((tm, HGRN_W), row),
                  pl.BlockSpec((tm, MLA_HEADS * MLA_V), row), pl.BlockSpec((tm, DSA_WIDTH), row),
                  _const_spec((D_MODEL, D_MODEL)), _const_spec((1, D_MODEL)),
                  _const_spec((D_MODEL, D_FF)), _const_spec((D_FF, D_MODEL)),
                  _const_spec((1, D_MODEL))],
        out_specs=pl.BlockSpec((tm, D_MODEL), row),
        out_shape=jax.ShapeDtypeStruct((T, D_MODEL), F32),
        compiler_params=_cparams(("parallel",)),
        name="outmlp",
    )(h, yh, ym, yd, wo, nw, w1, w2, fw)


def _rot_cols(w):
    half = w.shape[-1] // 2
    return jnp.concatenate([-w[..., half:], w[..., :half]], axis=-1)


def _place(w, width, off):
    pad = [(0, 0)] * (w.ndim - 1) + [(off, width - off - w.shape[-1])]
    return jnp.pad(w, pad)


def _layout_w_in(w_in):
    sizes = (HGRN_W, HGRN_W, HGRN_W, HGRN_W, MLA_Q_LORA, MLA_KV_LORA + MLA_ROPE,
             DSA_WIDTH, DSA_WIDTH, DSA_WIDTH, IDX_HEADS * IDX_DIM, IDX_DIM, IDX_HEADS)
    offs = [0]
    for s in sizes:
        offs.append(offs[-1] + s)
    hq, hf, hi, hg, mqa, mkva, dq, dk, dv, iq, ik, iw = [
        w_in[..., offs[n]:offs[n + 1]] for n in range(len(sizes))]
    ckv, kpe = mkva[..., :MLA_KV_LORA], mkva[..., MLA_KV_LORA:]
    cols = [hq, hf, hi, hg,
            _place(mqa, 256, 0), ckv, _place(kpe, LANES, MLA_NOPE), _place(_rot_cols(kpe), LANES, MLA_NOPE),
            dq * (DSA_HEAD_DIM ** -0.5), dk, dv,
            iq,
            ik, ik, _place(iw, LANES, 0)]
    return jnp.concatenate(cols, axis=-1).astype(BF)


def _layout_mla(w_qb, w_kvb):
    L = w_qb.shape[0]
    dq = MLA_NOPE + MLA_ROPE
    wq = w_qb.reshape(L, MLA_Q_LORA, MLA_HEADS, dq)
    wq_rot = jnp.concatenate([jnp.zeros_like(wq[..., :MLA_NOPE]), _rot_cols(wq[..., MLA_NOPE:])], axis=-1)
    pad_q = lambda w: jnp.pad(w, ((0, 0), (0, 256 - MLA_Q_LORA), (0, 0), (0, HEAD_PAD - dq))).reshape(
        L, 256, MLA_HEADS * HEAD_PAD).astype(BF)
    wkv = w_kvb.reshape(L, MLA_KV_LORA, MLA_HEADS, MLA_NOPE + MLA_V)
    pad_kv = lambda w: jnp.pad(w, ((0, 0), (0, 0), (0, 0), (0, HEAD_PAD - w.shape[-1]))).reshape(
        L, MLA_KV_LORA, MLA_HEADS * HEAD_PAD).astype(BF)
    wvt = jnp.swapaxes(pad_kv(wkv[..., MLA_NOPE:]), 1, 2)
    return pad_q(wq), pad_q(wq_rot), pad_kv(wkv[..., :MLA_NOPE]), wvt


def kernel(x, positions, attn_norm_w, w_in, hgrn_lb_logits, hgrn_norm_w, mla_q_norm_w, mla_w_qb,
           mla_kv_norm_w, mla_w_kvb, idx_k_norm_w, idx_k_norm_b, rel_bias_table, w_out,
           mlp_norm_w, w_mlp_in, w_mlp_out, final_norm_w):
    B, S, _ = x.shape
    T = B * S
    depth = w_in.shape[0]

    inv_freq = 1.0 / (ROPE_THETA ** (jnp.arange(0, MLA_ROPE, 2, dtype=F32) / MLA_ROPE))
    ang = positions.astype(F32)[..., None] * inv_freq
    cos, sin = jnp.cos(ang).reshape(T, -1), jnp.sin(ang).reshape(T, -1)
    cos_t = jnp.concatenate([jnp.ones((T, MLA_NOPE), F32), cos, cos,
                             jnp.zeros((T, HEAD_PAD - MLA_NOPE - MLA_ROPE), F32)], axis=1)
    sin_t = jnp.concatenate([jnp.zeros((T, MLA_NOPE), F32), sin, sin,
                             jnp.zeros((T, HEAD_PAD - MLA_NOPE - MLA_ROPE), F32)], axis=1)
    pos_col = positions.reshape(T, 1)
    pos_row = positions.reshape(B, S // 256, 256)

    lb = jnp.cumsum(jax.nn.softmax(hgrn_lb_logits.astype(F32), axis=0), axis=0)
    lb = lb - lb[0:1]
    lb3 = jnp.stack([jnp.log(lb), jnp.log1p(-lb), 1.0 - lb], axis=1)

    w_cat = _layout_w_in(w_in)
    wq, wqr, wk, wvt = _layout_mla(mla_w_qb, mla_w_kvb)
    qnw = jnp.pad(mla_q_norm_w, ((0, 0), (0, 256 - MLA_Q_LORA)))
    lnw = jnp.concatenate([idx_k_norm_w, idx_k_norm_w], axis=-1)
    lnb = jnp.concatenate([idx_k_norm_b, idx_k_norm_b], axis=-1)
    wo = w_out.astype(BF)
    w1 = w_mlp_in.astype(BF)
    w2 = w_mlp_out.astype(BF)
    tab = rel_bias_table.astype(F32)

    h = x.reshape(T, D_MODEL)
    for l in range(depth):
        gh, gm, gd, gq, gk = _inproj(h, attn_norm_w[l][None], w_cat[l])
        q_m, k_m, v_m, ki2 = _prep(gm, gk, cos_t, sin_t, qnw[l][None], wq[l], wqr[l],
                                   mla_kv_norm_w[l][None], wk[l], wvt[l], lnw[l][None], lnb[l][None])
        y_h = _hgrn(gh, lb3[l], hgrn_norm_w[l][None], B, S)
        y_m = _mla_attn(q_m, k_m, v_m, B, S)
        y_d = _dsa(gq, gk, gd, pos_col, ki2, pos_row, tab, B, S)
        h = _outmlp(h, y_h, y_m, y_d, wo[l], mlp_norm_w[l][None], w1[l], w2[l],
                    final_norm_w[None], final=(l == depth - 1))
    return h.reshape(B, S, D_MODEL)
```

```python
import functools
import math

import jax
import jax.numpy as jnp
from jax import lax
from jax.experimental import pallas as pl
from jax.experimental.pallas import tpu as pltpu

D_MODEL = 1024
HGRN_HEADS = 4
HGRN_DK = 128
HGRN_DV = 128
HGRN_W = HGRN_HEADS * HGRN_DK
HGRN_CHUNK = 32
MLA_HEADS = 4
MLA_NOPE = 64
MLA_ROPE = 32
MLA_V = 64
MLA_Q_LORA = 192
MLA_KV_LORA = 128
ROPE_THETA = 10000.0
DSA_HEADS = 4
DSA_HEAD_DIM = 64
DSA_WIDTH = DSA_HEADS * DSA_HEAD_DIM
IDX_HEADS = 8
IDX_DIM = 64
TOPK_MAX = 256
NUM_BUCKETS = 32
MAX_EXACT = NUM_BUCKETS // 2
MAX_DISTANCE = 128
D_FF = 4 * D_MODEL
EPS = 1e-6

LANES = 128
SUBLANES = 8
BF16_ROWS = 16
HEAD_PAD = 128
NEG = -1e30
ATT_TILE = 256
ROW_TILE = 512
VMEM_LIMIT = 52 * 1024 * 1024
LOG2E = math.log2(math.e)

DSA_V_ROWS = DSA_HEAD_DIM + BF16_ROWS

C_H = 0
C_M = C_H + 4 * HGRN_W
W_M = 640
C_D = C_M + W_M
W_D = 2 * DSA_WIDTH
C_Q = C_D + W_D
W_Q = IDX_HEADS * IDX_DIM
C_K = C_Q + W_Q
W_K = 256
N_PAD = C_K + W_K

BF = jnp.bfloat16
F32 = jnp.float32


def _dot(a, b):
    return jnp.dot(a, b, preferred_element_type=F32)


def _dot_nt(a, b):
    return lax.dot_general(a, b, (((1,), (1,)), ((), ())), preferred_element_type=F32)


def _rms(x, w, n=None):
    n = x.shape[-1] if n is None else n
    ms = jnp.sum(x * x, axis=-1, keepdims=True) * (1.0 / n)
    return x * lax.rsqrt(ms + EPS) * w


def _cparams(sem):
    return pltpu.CompilerParams(dimension_semantics=sem, vmem_limit_bytes=VMEM_LIMIT)


def _const_spec(shape):
    nd = len(shape)
    return pl.BlockSpec(shape, lambda *_: (0,) * nd, pipeline_mode=pl.Buffered(1))


def _inproj_kernel(x_ref, nw_ref, w_ref, wvt_ref, oh_ref, om_ref, od_ref, oq_ref, ok_ref, ovt_ref):
    u = _rms(x_ref[...], nw_ref[...]).astype(BF)
    oh_ref[...] = _dot(u, w_ref[:, C_H:C_M])
    om_ref[...] = _dot(u, w_ref[:, C_M:C_D])
    od_ref[...] = _dot(u, w_ref[:, C_D:C_Q]).astype(BF)
    oq_ref[...] = _dot(u, w_ref[:, C_Q:C_K]).astype(BF)
    ok_ref[...] = _dot(u, w_ref[:, C_K:N_PAD])
    vt = _dot_nt(wvt_ref[...], u)
    row = lax.broadcasted_iota(jnp.int32, vt.shape, 0)
    vt = jnp.where(row % DSA_V_ROWS == DSA_HEAD_DIM, 1.0, vt).astype(BF)
    for n in range(ovt_ref.shape[0]):
        ovt_ref[n] = vt[:, n * ATT_TILE:(n + 1) * ATT_TILE]


def _inproj(h, nw, w, wvt, tm=ROW_TILE):
    T = h.shape[0]
    tm = min(tm, T)
    row = lambda i: (i, 0)
    vrows = DSA_HEADS * DSA_V_ROWS
    per = tm // ATT_TILE
    return pl.pallas_call(
        _inproj_kernel,
        grid=(T // tm,),
        in_specs=[pl.BlockSpec((tm, D_MODEL), row), _const_spec((1, D_MODEL)),
                  _const_spec((D_MODEL, N_PAD)), _const_spec((vrows, D_MODEL))],
        out_specs=[pl.BlockSpec((tm, 4 * HGRN_W), row), pl.BlockSpec((tm, W_M), row),
                   pl.BlockSpec((tm, W_D), row), pl.BlockSpec((tm, W_Q), row),
                   pl.BlockSpec((tm, W_K), row),
                   pl.BlockSpec((per, vrows, ATT_TILE), lambda i: (i, 0, 0))],
        out_shape=[jax.ShapeDtypeStruct((T, 4 * HGRN_W), F32), jax.ShapeDtypeStruct((T, W_M), F32),
                   jax.ShapeDtypeStruct((T, W_D), BF), jax.ShapeDtypeStruct((T, W_Q), BF),
                   jax.ShapeDtypeStruct((T, W_K), F32),
                   jax.ShapeDtypeStruct((T // ATT_TILE, vrows, ATT_TILE), BF)],
        compiler_params=_cparams(("parallel",)),
        name="inproj",
    )(h, nw, w, wvt)


def _prep_kernel(gm_ref, gk_ref, cos_ref, sin_ref, qnw_ref, wq_ref, wqr_ref, kvnw_ref, wk_ref,
                 wvt_ref, lnw_ref, lnb_ref, q_ref, k_ref, vt_ref, ki_ref):
    cs = cos_ref[...]
    sn = sin_ref[...]
    cs4 = jnp.concatenate([cs] * MLA_HEADS, axis=1)
    sn4 = jnp.concatenate([sn] * MLA_HEADS, axis=1)
    qn = _rms(gm_ref[:, 0:256], qnw_ref[...], n=MLA_Q_LORA).astype(BF)
    scale = (MLA_NOPE + MLA_ROPE) ** -0.5 * LOG2E
    q = (_dot(qn, wq_ref[...]) * cs4 + _dot(qn, wqr_ref[...]) * sn4) * scale
    q_ref[...] = q.astype(BF)
    cn = _rms(gm_ref[:, 256:384], kvnw_ref[...]).astype(BF)
    kp = gm_ref[:, 384:512] * cs + gm_ref[:, 512:640] * sn
    k = _dot(cn, wk_ref[...]) + jnp.concatenate([kp] * MLA_HEADS, axis=1)
    k_ref[...] = k.astype(BF)
    vt = _dot_nt(wvt_ref[...], cn)
    row = lax.broadcasted_iota(jnp.int32, vt.shape, 0)
    vt_ref[...] = jnp.where(row % HEAD_PAD == MLA_V, 1.0, vt).astype(BF)
    x = gk_ref[:, 0:LANES]
    first = lax.broadcasted_iota(jnp.int32, x.shape, 1) < IDX_DIM
    mu = jnp.sum(jnp.where(first, x, 0.0), axis=-1, keepdims=True) * (1.0 / IDX_DIM)
    xc = x - mu
    var = jnp.sum(jnp.where(first, xc * xc, 0.0), axis=-1, keepdims=True) * (1.0 / IDX_DIM)
    ki_ref[...] = (xc * lax.rsqrt(var + EPS) * lnw_ref[...] + lnb_ref[...]).astype(BF)


def _prep(gm, gk, cos_t, sin_t, qnw, wq, wqr, kvnw, wk, wvt, lnw, lnb, tm=ATT_TILE):
    T = gm.shape[0]
    row = lambda i: (i, 0)
    hp = MLA_HEADS * HEAD_PAD
    return pl.pallas_call(
        _prep_kernel,
        grid=(T // tm,),
        in_specs=[pl.BlockSpec((tm, W_M), row), pl.BlockSpec((tm, W_K), row),
                  pl.BlockSpec((tm, LANES), row), pl.BlockSpec((tm, LANES), row),
                  _const_spec((1, 256)), _const_spec((256, hp)), _const_spec((256, hp)),
                  _const_spec((1, MLA_KV_LORA)), _const_spec((MLA_KV_LORA, hp)),
                  _const_spec((hp, MLA_KV_LORA)), _const_spec((1, LANES)), _const_spec((1, LANES))],
        out_specs=[pl.BlockSpec((tm, hp), row), pl.BlockSpec((tm, hp), row),
                   pl.BlockSpec((None, hp, tm), lambda i: (i, 0, 0)), pl.BlockSpec((tm, LANES), row)],
        out_shape=[jax.ShapeDtypeStruct((T, hp), BF), jax.ShapeDtypeStruct((T, hp), BF),
                   jax.ShapeDtypeStruct((T // tm, hp, tm), BF), jax.ShapeDtypeStruct((T, LANES), BF)],
        compiler_params=_cparams(("parallel",)),
        name="prep",
    )(gm, gk, cos_t, sin_t, qnw, wq, wqr, kvnw, wk, wvt, lnw, lnb)


def _split3(x):
    a = x.astype(BF)
    r = x - a.astype(F32)
    b = r.astype(BF)
    c = (r - b.astype(F32)).astype(BF)
    return a, b, c


def _hgrn_kernel(q_ref, f_ref, i_ref, g_ref, lb_ref, nw_ref, o_ref, st_ref, os_ref, *, tb):
    C = HGRN_CHUNK
    nc = tb // C

    @pl.when(pl.program_id(1) == 0)
    def _():
        st_ref[...] = jnp.zeros_like(st_ref)

    fp = f_ref[...]
    log_lb = lb_ref[0:1, :]
    log1m_lb = lb_ref[1:2, :]
    one_m_lb = lb_ref[2:3, :]
    ls = jnp.minimum(fp, 0.0) - jnp.log1p(jnp.exp(-jnp.abs(fp)))
    b = log1m_lb + ls
    log_f = jnp.maximum(log_lb, b) + jnp.log1p(jnp.exp(-jnp.abs(log_lb - b)))
    k = one_m_lb * (1.0 / (1.0 + jnp.exp(fp)))

    r = lax.broadcasted_iota(jnp.int32, (tb, tb), 0)
    c = lax.broadcasted_iota(jnp.int32, (tb, tb), 1)
    same = (r // C) == (c // C)
    tri = jnp.where(same & (c <= r), 1.0, 0.0).astype(BF)
    blk = jnp.where(same, 1.0, 0.0).astype(BF)
    a0, a1, a2 = _split3(log_f)
    G = _dot(tri, a0) + _dot(tri, a1) + _dot(tri, a2)
    G_last = _dot(blk, a0) + _dot(blk, a1) + _dot(blk, a2)

    eg = jnp.exp(G)
    q_dec = (q_ref[...] * (HGRN_DK ** -0.5) * eg).astype(BF)
    k_inv = (k * jnp.exp(-G)).astype(BF)
    k_state = k * jnp.exp(G_last - G)
    decay = jnp.exp(G_last)
    v = i_ref[...]
    vb = v.astype(BF)

    rr = lax.broadcasted_iota(jnp.int32, (C, C), 0)
    cc = lax.broadcasted_iota(jnp.int32, (C, C), 1)
    causal = cc <= rr

    for h in range(HGRN_HEADS):
        hs = slice(h * HGRN_DK, (h + 1) * HGRN_DK)
        st = st_ref[h]
        for n in range(nc):
            rs = slice(n * C, (n + 1) * C)
            qd = q_dec[rs, hs]
            A = jnp.where(causal, _dot_nt(qd, k_inv[rs, hs]), 0.0)
            o = _dot(A.astype(BF), vb[rs, hs]) + _dot_nt(qd, st.astype(BF))
            os_ref[rs, hs] = o
            vt = v[rs, hs].T.astype(BF)
            st = decay[n * C:n * C + 1, hs] * st + _dot(vt, k_state[rs, hs].astype(BF))
        st_ref[h] = st

    g = g_ref[...]
    gate = g * (1.0 / (1.0 + jnp.exp(-g)))
    for h in range(HGRN_HEADS):
        hs = slice(h * HGRN_DK, (h + 1) * HGRN_DK)
        o_ref[:, hs] = (_rms(os_ref[:, hs], nw_ref[...]) * gate[:, hs]).astype(BF)


def _hgrn(gh, lb3, nw, B, S, tb=256):
    nb = S // tb
    col = lambda j: (lambda b, i: (b * nb + i, j))
    return pl.pallas_call(
        functools.partial(_hgrn_kernel, tb=tb),
        grid=(B, nb),
        in_specs=[pl.BlockSpec((tb, HGRN_W), col(0)), pl.BlockSpec((tb, HGRN_W), col(1)),
                  pl.BlockSpec((tb, HGRN_W), col(2)), pl.BlockSpec((tb, HGRN_W), col(3)),
                  pl.BlockSpec((3, HGRN_W), lambda b, i: (0, 0)),
                  pl.BlockSpec((1, HGRN_DV), lambda b, i: (0, 0))],
        out_specs=pl.BlockSpec((tb, HGRN_W), col(0)),
        out_shape=jax.ShapeDtypeStruct((B * S, HGRN_W), BF),
        scratch_shapes=[pltpu.VMEM((HGRN_HEADS, HGRN_DV, HGRN_DK), F32),
                        pltpu.VMEM((tb, HGRN_W), F32)],
        compiler_params=_cparams(("parallel", "arbitrary")),
        name="hgrn",
    )(gh, gh, gh, gh, lb3, nw)


def _mla_kernel(q_ref, k_ref, vt_ref, o_ref, *, t):
    i = pl.program_id(1)
    key = lax.broadcasted_iota(jnp.int32, (t, t), 0)
    qry = lax.broadcasted_iota(jnp.int32, (t, t), 1)
    causal = key <= qry

    def tile(j, carry, mask):
        ks = pl.ds(pl.multiple_of(j * t, t), t)
        heads = [slice(h * HEAD_PAD, (h + 1) * HEAD_PAD) for h in range(MLA_HEADS)]
        ss = [_dot_nt(k_ref[ks, hs], q_ref[:, hs]) for hs in heads]
        if mask:
            ss = [jnp.where(causal, s, NEG) for s in ss]
        ms = [jnp.maximum(carry[h][0], jnp.max(ss[h], axis=0, keepdims=True))
              for h in range(MLA_HEADS)]
        ps = [jnp.exp2(ss[h] - ms[h]).astype(BF) for h in range(MLA_HEADS)]
        return tuple((ms[h], jnp.exp2(carry[h][0] - ms[h]) * carry[h][1]
                      + _dot(vt_ref[j, heads[h], :], ps[h])) for h in range(MLA_HEADS))

    init = tuple((jnp.full((1, t), NEG, F32), jnp.zeros((HEAD_PAD, t), F32))
                 for _ in range(MLA_HEADS))
    carry = lax.fori_loop(0, i, lambda j, c: tile(j, c, False), init)
    carry = tile(i, carry, True)
    outs = [acc[:MLA_V, :] * (1.0 / acc[MLA_V:MLA_V + 1, :]) for (_, acc) in carry]
    o_ref[...] = jnp.concatenate(outs, axis=0).T.astype(BF)


def _mla_attn(q, k, vt, B, S, t=ATT_TILE):
    nq = S // t
    hp = MLA_HEADS * HEAD_PAD
    return pl.pallas_call(
        functools.partial(_mla_kernel, t=t),
        grid=(B, nq),
        in_specs=[pl.BlockSpec((t, hp), lambda b, i: (b * nq + i, 0)),
                  pl.BlockSpec((S, hp), lambda b, i: (b, 0)),
                  pl.BlockSpec((nq, hp, t), lambda b, i: (b, 0, 0))],
        out_specs=pl.BlockSpec((t, MLA_HEADS * MLA_V), lambda b, i: (b * nq + i, 0)),
        out_shape=jax.ShapeDtypeStruct((B * S, MLA_HEADS * MLA_V), BF),
        compiler_params=_cparams(("parallel", "arbitrary")),
        name="mla_attn",
    )(q, k, vt)


BISECT_CAP = 48
_LAST_BUCKET_FROM = next(
    n for n in range(MAX_EXACT, 1 << 20)
    if MAX_EXACT + int(math.log(n / MAX_EXACT) / math.log(MAX_DISTANCE / MAX_EXACT)
                       * (NUM_BUCKETS - MAX_EXACT)) >= NUM_BUCKETS - 1)
assert _LAST_BUCKET_FROM <= LANES - 1 and _LAST_BUCKET_FROM <= MAX_DISTANCE


def _dsa_kernel(qi_ref, w_ref, qd_ref, pos_ref, ki_ref, kd_ref, vt_ref, tab_ref, o_ref,
                sc_ref, qm_ref, *, t, n_sel):
    i = pl.program_id(1)
    lane = lax.broadcasted_iota(jnp.int32, (t, LANES), 1)
    lo_half = lane < DSA_HEAD_DIM
    key = lax.broadcasted_iota(jnp.int32, (t, t), 0)
    qry = lax.broadcasted_iota(jnp.int32, (t, t), 1)
    causal = key <= qry
    k_sel = float(n_sel)

    def key_rows(j):
        return pl.ds(pl.multiple_of(j * t, t), t)

    def over_tiles(body, init):
        c = lax.fori_loop(0, (i + 1) // 2, lambda jj, c: body(2 * jj + 1, body(2 * jj, c)), init)
        return lax.cond((i + 1) % 2 == 1, lambda c: body(i, c), lambda c: c, c)

    def fold(x, op):
        return op(x.reshape(t // SUBLANES, SUBLANES, t), axis=0)

    zero_b = jnp.zeros((t, LANES), BF)
    for p in range(IDX_HEADS // 2):
        qp = qi_ref[:, p * LANES:(p + 1) * LANES]
        qm_ref[2 * p] = jnp.where(lo_half, qp, zero_b)
        qm_ref[2 * p + 1] = jnp.where(lo_half, zero_b, qp)
    wt = w_ref[...].T * ((IDX_HEADS ** -0.5) * (IDX_DIM ** -0.5))

    def score_tile(j):
        kt = ki_ref[key_rows(j), :]
        acc = None
        for h0 in range(0, IDX_HEADS, 4):
            ss = [_dot_nt(kt, qm_ref[h]) for h in range(h0, h0 + 4)]
            for n, s in enumerate(ss):
                term = jnp.maximum(s, 0.0) * wt[h0 + n:h0 + n + 1, :]
                acc = term if acc is None else acc + term
        return acc

    def p1_body(j, _):
        sc_ref[j] = score_tile(j)
        return 0

    lax.fori_loop(0, i, p1_body, 0)
    sc_ref[i] = jnp.where(causal, score_tile(i), -jnp.inf)

    def total(c):
        return jnp.sum(c, axis=0, keepdims=True)

    def count_gt(thr):
        def body(j, c):
            return c + fold(jnp.where(sc_ref[j] > thr, 1.0, 0.0), jnp.sum)
        return total(over_tiles(body, jnp.zeros((SUBLANES, t), F32)))

    def stat_body(j, c):
        mx, mn, cp, cz = c
        x = sc_ref[j]
        return (jnp.maximum(mx, fold(x, jnp.max)),
                jnp.minimum(mn, fold(jnp.where(x == -jnp.inf, jnp.inf, x), jnp.min)),
                cp + fold(jnp.where(x > 0.0, 1.0, 0.0), jnp.sum),
                cz + fold(jnp.where(x == 0.0, 1.0, 0.0), jnp.sum))

    z8 = jnp.zeros((SUBLANES, t), F32)
    mx, mn, cp, cz = over_tiles(stat_body, (z8 - jnp.inf, z8 + jnp.inf, z8, z8))
    row_max = jnp.max(mx, axis=0, keepdims=True)
    row_min = jnp.min(mn, axis=0, keepdims=True)
    c_pos, c_zero = total(cp), total(cz)
    n_valid = (i * t + 1 + lax.broadcasted_iota(jnp.int32, (1, t), 1)).astype(F32)

    big = n_valid > k_sel
    pos_q = jnp.logical_and(big, c_pos >= k_sel)
    tie_q = jnp.logical_and(jnp.logical_and(big, c_pos < k_sel), c_pos + c_zero >= k_sel)
    neg_q = jnp.logical_and(big, c_pos + c_zero < k_sel)
    need = jnp.where(tie_q, k_sel - c_pos, 0.0)
    lo0 = jnp.where(jnp.logical_or(pos_q, tie_q), 0.0, -jnp.inf)
    hi0 = jnp.where(neg_q, 0.0, row_max)
    c_lo0 = jnp.where(pos_q, c_pos, jnp.where(tie_q, k_sel, n_valid))

    def open_queries(c_lo):
        return jnp.max(jnp.where(c_lo > k_sel, 1.0, 0.0))

    def bis_cond(c):
        return jnp.logical_and(c[0] < BISECT_CAP, c[1] > 0.0)

    def bis_body(c):
        it, _, lo, hi, c_lo = c
        active = c_lo > k_sel
        base = jnp.maximum(lo, row_min)
        mid = base + 0.5 * (hi - base)
        cnt = count_gt(mid)
        up = jnp.logical_and(active, cnt >= k_sel)
        dn = jnp.logical_and(active, cnt < k_sel)
        lo = jnp.where(up, mid, lo)
        c_lo = jnp.where(up, cnt, c_lo)
        hi = jnp.where(dn, mid, hi)
        return it + 1, open_queries(c_lo), lo, hi, c_lo

    _, _, lo, _, _ = lax.while_loop(
        bis_cond, bis_body, (jnp.int32(0), open_queries(c_lo0), lo0, hi0, c_lo0))

    def mask_plain(_):
        def body(j, c):
            sc_ref[j] = jnp.where(sc_ref[j] > lo, 0.0, NEG)
            return c
        return lax.fori_loop(0, i + 1, body, 0)

    def mask_ties(_):
        lower = jnp.where(qry <= key, 1.0, 0.0).astype(BF)

        def body(j, seen):
            x = sc_ref[j]
            zero = x == 0.0
            zf = jnp.where(zero, 1.0, 0.0)
            rank = _dot(lower, zf.astype(BF)) + seen
            take = jnp.logical_and(zero, rank <= need)
            sc_ref[j] = jnp.where(jnp.logical_or(x > lo, take), 0.0, NEG)
            return seen + total(fold(zf, jnp.sum))
        lax.fori_loop(0, i + 1, body, jnp.zeros((1, t), F32))
        return 0

    lax.cond(jnp.max(need) > 0.0, mask_ties, mask_plain, 0)

    qh = []
    for p in range(DSA_HEADS // 2):
        qp = qd_ref[:, p * LANES:(p + 1) * LANES]
        qh.append(jnp.where(lo_half, qp, zero_b))
        qh.append(jnp.where(lo_half, zero_b, qp))
    pq = pos_ref[pl.ds(i, 1), :]
    pq_min = jnp.min(pq)
    far_bias = [tab_ref[NUM_BUCKETS - 1, h] * LOG2E for h in range(DSA_HEADS)]
    log_ratio = math.log(MAX_DISTANCE / MAX_EXACT)

    dist = lax.broadcasted_iota(jnp.int32, (SUBLANES, LANES), 1)
    large = MAX_EXACT + (jnp.log(jnp.maximum(dist, 1).astype(F32) / MAX_EXACT) / log_ratio
                         * (NUM_BUCKETS - MAX_EXACT)).astype(jnp.int32)
    bucket = jnp.where(dist < MAX_EXACT, dist, jnp.minimum(large, NUM_BUCKETS - 1))
    by_dist = []
    for h in range(DSA_HEADS):
        bh = jnp.full((SUBLANES, LANES), tab_ref[0, h], F32)
        for jb in range(1, NUM_BUCKETS):
            bh = jnp.where(bucket >= jb, tab_ref[jb, h], bh)
        by_dist.append(jnp.concatenate([bh * LOG2E] * (t // SUBLANES), axis=0))

    def pair_bias(pk_row):
        pk = jnp.broadcast_to(pk_row, (SUBLANES, t)).T[:, 0:1]
        n = jnp.clip(pq - pk, 0, LANES - 1)
        return [jnp.concatenate([jnp.take_along_axis(by_dist[h], n[:, c * LANES:(c + 1) * LANES], axis=1)
                                 for c in range(t // LANES)], axis=1) for h in range(DSA_HEADS)]

    def attend(j, carry, near):
        madd = sc_ref[j]
        bias = pair_bias(pos_ref[pl.ds(j, 1), :]) if near else far_bias
        ks = key_rows(j)
        kps = [kd_ref[ks, p * LANES:(p + 1) * LANES] for p in range(DSA_HEADS // 2)]
        ss = [_dot_nt(kps[h // 2], qh[h]) + (madd + bias[h]) for h in range(DSA_HEADS)]
        ms = [jnp.maximum(carry[h][0], jnp.max(ss[h], axis=0, keepdims=True))
              for h in range(DSA_HEADS)]
        ps = [jnp.exp2(ss[h] - ms[h]).astype(BF) for h in range(DSA_HEADS)]
        return tuple((ms[h], jnp.exp2(carry[h][0] - ms[h]) * carry[h][1]
                      + _dot(vt_ref[j, h * DSA_V_ROWS:(h + 1) * DSA_V_ROWS, :], ps[h]))
                     for h in range(DSA_HEADS))

    def p3_body(j, carry):
        pk_max = jnp.max(pos_ref[pl.ds(j, 1), :])
        return lax.cond(pq_min - pk_max >= MAX_DISTANCE,
                        lambda c: attend(j, c, False), lambda c: attend(j, c, True), carry)

    init = tuple((jnp.full((1, t), NEG, F32), jnp.zeros((DSA_V_ROWS, t), F32))
                 for _ in range(DSA_HEADS))
    carry = lax.fori_loop(0, i + 1, p3_body, init)
    outs = [acc[:DSA_HEAD_DIM, :] * (1.0 / acc[DSA_HEAD_DIM:DSA_HEAD_DIM + 1, :])
            for (_, acc) in carry]
    o_ref[...] = jnp.concatenate(outs, axis=0).T.astype(BF)


def _dsa(gq, gk, gd, ki2, vdt, pos_tiles, tab, B, S, t=ATT_TILE):
    nq = S // t
    n_sel = min(TOPK_MAX, S // 4)
    vrows = DSA_HEADS * DSA_V_ROWS
    qrow = lambda c: (lambda b, i: (b * nq + i, c))
    return pl.pallas_call(
        functools.partial(_dsa_kernel, t=t, n_sel=n_sel),
        grid=(B, nq),
        in_specs=[pl.BlockSpec((t, W_Q), qrow(0)),
                  pl.BlockSpec((t, LANES), qrow(1)),
                  pl.BlockSpec((t, DSA_WIDTH), qrow(0)),
                  pl.BlockSpec((None, nq, t), lambda b, i: (b, 0, 0)),
                  pl.BlockSpec((S, LANES), lambda b, i: (b, 0)),
                  pl.BlockSpec((S, DSA_WIDTH), lambda b, i: (b, 1)),
                  pl.BlockSpec((nq, vrows, t), lambda b, i: (b, 0, 0)),
                  pl.BlockSpec(memory_space=pltpu.SMEM)],
        out_specs=pl.BlockSpec((t, DSA_WIDTH), qrow(0)),
        out_shape=jax.ShapeDtypeStruct((B * S, DSA_WIDTH), BF),
        scratch_shapes=[pltpu.VMEM((nq, t, t), F32), pltpu.VMEM((IDX_HEADS, t, LANES), BF)],
        compiler_params=_cparams(("parallel", "arbitrary")),
        name="dsa",
    )(gq, gk, gd, pos_tiles, ki2, gd, vdt, tab)


def _outmlp_kernel(h_ref, yh_ref, ym_ref, yd_ref, wo_ref, nw_ref, w1_ref, w2_ref, fw_ref, o_ref,
                   *, final, ff_chunk):
    mixed = jnp.concatenate([yh_ref[...], ym_ref[...], yd_ref[...]], axis=1)
    h = h_ref[...] + _dot(mixed, wo_ref[...])
    u = _rms(h, nw_ref[...]).astype(BF)
    out = h
    for c in range(D_FF // ff_chunk):
        cs = slice(c * ff_chunk, (c + 1) * ff_chunk)
        a = jnp.maximum(_dot(u, w1_ref[:, cs]), 0.0)
        out = out + _dot((a * a).astype(BF), w2_ref[cs, :])
    if final:
        out = _rms(out, fw_ref[...])
    o_ref[...] = out


def _outmlp(h, yh, ym, yd, wo, nw, w1, w2, fw, final, tm=ROW_TILE, ff_chunk=1024):
    T = h.shape[0]
    tm = min(tm, T)
    row = lambda i: (i, 0)
    return pl.pallas_call(
        functools.partial(_outmlp_kernel, final=final, ff_chunk=ff_chunk),
        grid=(T // tm,),
        in_specs=[pl.BlockSpec((tm, D_MODEL), row), pl.BlockSpec((tm, HGRN_W), row),
                  pl.BlockSpec((tm, MLA_HEADS * MLA_V), row), pl.BlockSpec((tm, DSA_WIDTH), row),
                  _const_spec((D_MODEL, D_MODEL)), _const_spec((1, D_MODEL)),
                  _const_spec((D_MODEL, D_FF)), _const_spec((D_FF, D_MODEL)),
                  _const_spec((1, D_MODEL))],
        out_specs=pl.BlockSpec((tm, D_MODEL), row),
        out_shape=jax.ShapeDtypeStruct((T, D_MODEL), F32),
        compiler_params=_cparams(("parallel",)),
        name="outmlp",
    )(h, yh, ym, yd, wo, nw, w1, w2, fw)


def _rot_cols(w):
    half = w.shape[-1] // 2
    return jnp.concatenate([-w[..., half:], w[..., :half]], axis=-1)


def _place(w, width, off):
    pad = [(0, 0)] * (w.ndim - 1) + [(off, width - off - w.shape[-1])]
    return jnp.pad(w, pad)


def _layout_w_in(w_in):
    sizes = (HGRN_W, HGRN_W, HGRN_W, HGRN_W, MLA_Q_LORA, MLA_KV_LORA + MLA_ROPE,
             DSA_WIDTH, DSA_WIDTH, DSA_WIDTH, IDX_HEADS * IDX_DIM, IDX_DIM, IDX_HEADS)
    offs = [0]
    for s in sizes:
        offs.append(offs[-1] + s)
    hq, hf, hi, hg, mqa, mkva, dq, dk, dv, iq, ik, iw = [
        w_in[..., offs[n]:offs[n + 1]] for n in range(len(sizes))]
    ckv, kpe = mkva[..., :MLA_KV_LORA], mkva[..., MLA_KV_LORA:]
    cols = [hq, hf, hi, hg,
            _place(mqa, 256, 0), ckv, _place(kpe, LANES, MLA_NOPE), _place(_rot_cols(kpe), LANES, MLA_NOPE),
            dq * (DSA_HEAD_DIM ** -0.5 * LOG2E), dk,
            iq,
            ik, ik, _place(iw, LANES, 0)]
    w_cat = jnp.concatenate(cols, axis=-1).astype(BF)
    L = w_in.shape[0]
    dvt = jnp.swapaxes(dv, 1, 2).reshape(L, DSA_HEADS, DSA_HEAD_DIM, D_MODEL)
    dvt = jnp.pad(dvt, ((0, 0), (0, 0), (0, DSA_V_ROWS - DSA_HEAD_DIM), (0, 0)))
    return w_cat, dvt.reshape(L, DSA_HEADS * DSA_V_ROWS, D_MODEL).astype(BF)


def _layout_mla(w_qb, w_kvb):
    L = w_qb.shape[0]
    dq = MLA_NOPE + MLA_ROPE
    wq = w_qb.reshape(L, MLA_Q_LORA, MLA_HEADS, dq)
    wq_rot = jnp.concatenate([jnp.zeros_like(wq[..., :MLA_NOPE]), _rot_cols(wq[..., MLA_NOPE:])], axis=-1)
    pad_q = lambda w: jnp.pad(w, ((0, 0), (0, 256 - MLA_Q_LORA), (0, 0), (0, HEAD_PAD - dq))).reshape(
        L, 256, MLA_HEADS * HEAD_PAD).astype(BF)
    wkv = w_kvb.reshape(L, MLA_KV_LORA, MLA_HEADS, MLA_NOPE + MLA_V)
    pad_kv = lambda w: jnp.pad(w, ((0, 0), (0, 0), (0, 0), (0, HEAD_PAD - w.shape[-1]))).reshape(
        L, MLA_KV_LORA, MLA_HEADS * HEAD_PAD).astype(BF)
    wvt = jnp.swapaxes(pad_kv(wkv[..., MLA_NOPE:]), 1, 2)
    return pad_q(wq), pad_q(wq_rot), pad_kv(wkv[..., :MLA_NOPE]), wvt


def kernel(x, positions, attn_norm_w, w_in, hgrn_lb_logits, hgrn_norm_w, mla_q_norm_w, mla_w_qb,
           mla_kv_norm_w, mla_w_kvb, idx_k_norm_w, idx_k_norm_b, rel_bias_table, w_out,
           mlp_norm_w, w_mlp_in, w_mlp_out, final_norm_w):
    B, S, _ = x.shape
    T = B * S
    depth = w_in.shape[0]

    inv_freq = 1.0 / (ROPE_THETA ** (jnp.arange(0, MLA_ROPE, 2, dtype=F32) / MLA_ROPE))
    ang = positions.astype(F32)[..., None] * inv_freq
    cos, sin = jnp.cos(ang).reshape(T, -1), jnp.sin(ang).reshape(T, -1)
    cos_t = jnp.concatenate([jnp.ones((T, MLA_NOPE), F32), cos, cos,
                             jnp.zeros((T, HEAD_PAD - MLA_NOPE - MLA_ROPE), F32)], axis=1)
    sin_t = jnp.concatenate([jnp.zeros((T, MLA_NOPE), F32), sin, sin,
                             jnp.zeros((T, HEAD_PAD - MLA_NOPE - MLA_ROPE), F32)], axis=1)
    pos_tiles = positions.reshape(B, S // ATT_TILE, ATT_TILE)

    lb = jnp.cumsum(jax.nn.softmax(hgrn_lb_logits.astype(F32), axis=0), axis=0)
    lb = lb - lb[0:1]
    lb3 = jnp.stack([jnp.log(lb), jnp.log1p(-lb), 1.0 - lb], axis=1)

    w_cat, w_dvt = _layout_w_in(w_in)
    wq, wqr, wk, wvt = _layout_mla(mla_w_qb, mla_w_kvb)
    qnw = jnp.pad(mla_q_norm_w, ((0, 0), (0, 256 - MLA_Q_LORA)))
    lnw = jnp.concatenate([idx_k_norm_w, idx_k_norm_w], axis=-1)
    lnb = jnp.concatenate([idx_k_norm_b, idx_k_norm_b], axis=-1)
    wo = w_out.astype(BF)
    w1 = w_mlp_in.astype(BF)
    w2 = w_mlp_out.astype(BF)
    tab = rel_bias_table.astype(F32)

    h = x.reshape(T, D_MODEL)
    for l in range(depth):
        gh, gm, gd, gq, gk, vdt = _inproj(h, attn_norm_w[l][None], w_cat[l], w_dvt[l])
        q_m, k_m, v_m, ki2 = _prep(gm, gk, cos_t, sin_t, qnw[l][None], wq[l], wqr[l],
                                   mla_kv_norm_w[l][None], wk[l], wvt[l], lnw[l][None], lnb[l][None])
        y_h = _hgrn(gh, lb3[l], hgrn_norm_w[l][None], B, S)
        y_m = _mla_attn(q_m, k_m, v_m, B, S)
        y_d = _dsa(gq, gk, gd, ki2, vdt, pos_tiles, tab, B, S)
        h = _outmlp(h, y_h, y_m, y_d, wo[l], mlp_norm_w[l][None], w1[l], w2[l],
                    final_norm_w[None], final=(l == depth - 1))
    return h.reshape(B, S, D_MODEL)
```

```python
import functools
import math

import jax
import jax.numpy as jnp
from jax import lax
from jax.experimental import pallas as pl
from jax.experimental.pallas import tpu as pltpu

D_MODEL = 1024
HGRN_HEADS = 4
HGRN_DK = 128
HGRN_DV = 128
HGRN_W = HGRN_HEADS * HGRN_DK
HGRN_CHUNK = 32
MLA_HEADS = 4
MLA_NOPE = 64
MLA_ROPE = 32
MLA_V = 64
MLA_Q_LORA = 192
MLA_KV_LORA = 128
ROPE_THETA = 10000.0
DSA_HEADS = 4
DSA_HEAD_DIM = 64
DSA_WIDTH = DSA_HEADS * DSA_HEAD_DIM
IDX_HEADS = 8
IDX_DIM = 64
TOPK_MAX = 256
NUM_BUCKETS = 32
MAX_EXACT = NUM_BUCKETS // 2
MAX_DISTANCE = 128
D_FF = 4 * D_MODEL
EPS = 1e-6

LANES = 128
SUBLANES = 8
BF16_ROWS = 16
HEAD_PAD = 128
NEG = -1e30
ATT_TILE = 256
ROW_TILE = 512
VMEM_LIMIT = 52 * 1024 * 1024
LOG2E = math.log2(math.e)

DSA_V_ROWS = DSA_HEAD_DIM + BF16_ROWS

C_H = 0
C_M = C_H + 4 * HGRN_W
W_M = 640
C_D = C_M + W_M
W_D = 2 * DSA_WIDTH
C_Q = C_D + W_D
W_Q = IDX_HEADS * IDX_DIM
C_K = C_Q + W_Q
W_K = 256
N_PAD = C_K + W_K

BF = jnp.bfloat16
F32 = jnp.float32


def _dot(a, b):
    return jnp.dot(a, b, preferred_element_type=F32)


def _dot_nt(a, b):
    return lax.dot_general(a, b, (((1,), (1,)), ((), ())), preferred_element_type=F32)


def _rms(x, w, n=None):
    n = x.shape[-1] if n is None else n
    ms = jnp.sum(x * x, axis=-1, keepdims=True) * (1.0 / n)
    return x * lax.rsqrt(ms + EPS) * w


def _cparams(sem):
    return pltpu.CompilerParams(dimension_semantics=sem, vmem_limit_bytes=VMEM_LIMIT)


def _const_spec(shape):
    nd = len(shape)
    return pl.BlockSpec(shape, lambda *_: (0,) * nd, pipeline_mode=pl.Buffered(1))


def _inproj_kernel(x_ref, nw_ref, w_ref, wvt_ref, oh_ref, om_ref, od_ref, oq_ref, ok_ref, ovt_ref):
    u = _rms(x_ref[...], nw_ref[...]).astype(BF)
    oh_ref[...] = _dot(u, w_ref[:, C_H:C_M])
    om_ref[...] = _dot(u, w_ref[:, C_M:C_D])
    od_ref[...] = _dot(u, w_ref[:, C_D:C_Q]).astype(BF)
    oq_ref[...] = _dot(u, w_ref[:, C_Q:C_K]).astype(BF)
    ok_ref[...] = _dot(u, w_ref[:, C_K:N_PAD])
    vt = _dot_nt(wvt_ref[...], u)
    row = lax.broadcasted_iota(jnp.int32, vt.shape, 0)
    vt = jnp.where(row % DSA_V_ROWS == DSA_HEAD_DIM, 1.0, vt).astype(BF)
    for n in range(ovt_ref.shape[0]):
        ovt_ref[n] = vt[:, n * ATT_TILE:(n + 1) * ATT_TILE]


def _inproj(h, nw, w, wvt, tm=ROW_TILE):
    T = h.shape[0]
    tm = min(tm, T)
    row = lambda i: (i, 0)
    vrows = DSA_HEADS * DSA_V_ROWS
    per = tm // ATT_TILE
    return pl.pallas_call(
        _inproj_kernel,
        grid=(T // tm,),
        in_specs=[pl.BlockSpec((tm, D_MODEL), row), _const_spec((1, D_MODEL)),
                  _const_spec((D_MODEL, N_PAD)), _const_spec((vrows, D_MODEL))],
        out_specs=[pl.BlockSpec((tm, 4 * HGRN_W), row), pl.BlockSpec((tm, W_M), row),
                   pl.BlockSpec((tm, W_D), row), pl.BlockSpec((tm, W_Q), row),
                   pl.BlockSpec((tm, W_K), row),
                   pl.BlockSpec((per, vrows, ATT_TILE), lambda i: (i, 0, 0))],
        out_shape=[jax.ShapeDtypeStruct((T, 4 * HGRN_W), F32), jax.ShapeDtypeStruct((T, W_M), F32),
                   jax.ShapeDtypeStruct((T, W_D), BF), jax.ShapeDtypeStruct((T, W_Q), BF),
                   jax.ShapeDtypeStruct((T, W_K), F32),
                   jax.ShapeDtypeStruct((T // ATT_TILE, vrows, ATT_TILE), BF)],
        compiler_params=_cparams(("parallel",)),
        name="inproj",
    )(h, nw, w, wvt)


def _prep_kernel(gm_ref, gk_ref, cos_ref, sin_ref, qnw_ref, wq_ref, wqr_ref, kvnw_ref, wk_ref,
                 wvt_ref, lnw_ref, lnb_ref, q_ref, k_ref, vt_ref, ki_ref):
    cs = cos_ref[...]
    sn = sin_ref[...]
    cs4 = jnp.concatenate([cs] * MLA_HEADS, axis=1)
    sn4 = jnp.concatenate([sn] * MLA_HEADS, axis=1)
    qn = _rms(gm_ref[:, 0:256], qnw_ref[...], n=MLA_Q_LORA).astype(BF)
    scale = (MLA_NOPE + MLA_ROPE) ** -0.5 * LOG2E
    q = (_dot(qn, wq_ref[...]) * cs4 + _dot(qn, wqr_ref[...]) * sn4) * scale
    q_ref[...] = q.astype(BF)
    cn = _rms(gm_ref[:, 256:384], kvnw_ref[...]).astype(BF)
    kp = gm_ref[:, 384:512] * cs + gm_ref[:, 512:640] * sn
    k = _dot(cn, wk_ref[...]) + jnp.concatenate([kp] * MLA_HEADS, axis=1)
    k_ref[...] = k.astype(BF)
    vt = _dot_nt(wvt_ref[...], cn)
    row = lax.broadcasted_iota(jnp.int32, vt.shape, 0)
    vt_ref[...] = jnp.where(row % HEAD_PAD == MLA_V, 1.0, vt).astype(BF)
    x = gk_ref[:, 0:LANES]
    first = lax.broadcasted_iota(jnp.int32, x.shape, 1) < IDX_DIM
    mu = jnp.sum(jnp.where(first, x, 0.0), axis=-1, keepdims=True) * (1.0 / IDX_DIM)
    xc = x - mu
    var = jnp.sum(jnp.where(first, xc * xc, 0.0), axis=-1, keepdims=True) * (1.0 / IDX_DIM)
    ki_ref[...] = (xc * lax.rsqrt(var + EPS) * lnw_ref[...] + lnb_ref[...]).astype(BF)


def _prep(gm, gk, cos_t, sin_t, qnw, wq, wqr, kvnw, wk, wvt, lnw, lnb, tm=ATT_TILE):
    T = gm.shape[0]
    row = lambda i: (i, 0)
    hp = MLA_HEADS * HEAD_PAD
    return pl.pallas_call(
        _prep_kernel,
        grid=(T // tm,),
        in_specs=[pl.BlockSpec((tm, W_M), row), pl.BlockSpec((tm, W_K), row),
                  pl.BlockSpec((tm, LANES), row), pl.BlockSpec((tm, LANES), row),
                  _const_spec((1, 256)), _const_spec((256, hp)), _const_spec((256, hp)),
                  _const_spec((1, MLA_KV_LORA)), _const_spec((MLA_KV_LORA, hp)),
                  _const_spec((hp, MLA_KV_LORA)), _const_spec((1, LANES)), _const_spec((1, LANES))],
        out_specs=[pl.BlockSpec((tm, hp), row), pl.BlockSpec((tm, hp), row),
                   pl.BlockSpec((None, hp, tm), lambda i: (i, 0, 0)), pl.BlockSpec((tm, LANES), row)],
        out_shape=[jax.ShapeDtypeStruct((T, hp), BF), jax.ShapeDtypeStruct((T, hp), BF),
                   jax.ShapeDtypeStruct((T // tm, hp, tm), BF), jax.ShapeDtypeStruct((T, LANES), BF)],
        compiler_params=_cparams(("parallel",)),
        name="prep",
    )(gm, gk, cos_t, sin_t, qnw, wq, wqr, kvnw, wk, wvt, lnw, lnb)


def _split3(x):
    a = x.astype(BF)
    r = x - a.astype(F32)
    b = r.astype(BF)
    c = (r - b.astype(F32)).astype(BF)
    return a, b, c


def _hgrn_kernel(q_ref, f_ref, i_ref, g_ref, lb_ref, nw_ref, o_ref, st_ref, os_ref, *, tb):
    C = HGRN_CHUNK
    nc = tb // C

    @pl.when(pl.program_id(1) == 0)
    def _():
        st_ref[...] = jnp.zeros_like(st_ref)

    fp = f_ref[...]
    log_lb = lb_ref[0:1, :]
    log1m_lb = lb_ref[1:2, :]
    one_m_lb = lb_ref[2:3, :]
    ls = jnp.minimum(fp, 0.0) - jnp.log1p(jnp.exp(-jnp.abs(fp)))
    b = log1m_lb + ls
    log_f = jnp.maximum(log_lb, b) + jnp.log1p(jnp.exp(-jnp.abs(log_lb - b)))
    k = one_m_lb * (1.0 / (1.0 + jnp.exp(fp)))

    r = lax.broadcasted_iota(jnp.int32, (tb, tb), 0)
    c = lax.broadcasted_iota(jnp.int32, (tb, tb), 1)
    same = (r // C) == (c // C)
    tri = jnp.where(same & (c <= r), 1.0, 0.0).astype(BF)
    blk = jnp.where(same, 1.0, 0.0).astype(BF)
    a0, a1, a2 = _split3(log_f)
    G = _dot(tri, a0) + _dot(tri, a1) + _dot(tri, a2)
    G_last = _dot(blk, a0) + _dot(blk, a1) + _dot(blk, a2)

    eg = jnp.exp(G)
    q_dec = (q_ref[...] * (HGRN_DK ** -0.5) * eg).astype(BF)
    k_inv = (k * jnp.exp(-G)).astype(BF)
    k_state = k * jnp.exp(G_last - G)
    decay = jnp.exp(G_last)
    v = i_ref[...]
    vb = v.astype(BF)

    rr = lax.broadcasted_iota(jnp.int32, (C, C), 0)
    cc = lax.broadcasted_iota(jnp.int32, (C, C), 1)
    causal = cc <= rr

    for h in range(HGRN_HEADS):
        hs = slice(h * HGRN_DK, (h + 1) * HGRN_DK)
        st = st_ref[h]
        for n in range(nc):
            rs = slice(n * C, (n + 1) * C)
            qd = q_dec[rs, hs]
            A = jnp.where(causal, _dot_nt(qd, k_inv[rs, hs]), 0.0)
            o = _dot(A.astype(BF), vb[rs, hs]) + _dot_nt(qd, st.astype(BF))
            os_ref[rs, hs] = o
            vt = v[rs, hs].T.astype(BF)
            st = decay[n * C:n * C + 1, hs] * st + _dot(vt, k_state[rs, hs].astype(BF))
        st_ref[h] = st

    g = g_ref[...]
    gate = g * (1.0 / (1.0 + jnp.exp(-g)))
    for h in range(HGRN_HEADS):
        hs = slice(h * HGRN_DK, (h + 1) * HGRN_DK)
        o_ref[:, hs] = (_rms(os_ref[:, hs], nw_ref[...]) * gate[:, hs]).astype(BF)


def _hgrn(gh, lb3, nw, B, S, tb=256):
    nb = S // tb
    col = lambda j: (lambda b, i: (b * nb + i, j))
    return pl.pallas_call(
        functools.partial(_hgrn_kernel, tb=tb),
        grid=(B, nb),
        in_specs=[pl.BlockSpec((tb, HGRN_W), col(0)), pl.BlockSpec((tb, HGRN_W), col(1)),
                  pl.BlockSpec((tb, HGRN_W), col(2)), pl.BlockSpec((tb, HGRN_W), col(3)),
                  pl.BlockSpec((3, HGRN_W), lambda b, i: (0, 0)),
                  pl.BlockSpec((1, HGRN_DV), lambda b, i: (0, 0))],
        out_specs=pl.BlockSpec((tb, HGRN_W), col(0)),
        out_shape=jax.ShapeDtypeStruct((B * S, HGRN_W), BF),
        scratch_shapes=[pltpu.VMEM((HGRN_HEADS, HGRN_DV, HGRN_DK), F32),
                        pltpu.VMEM((tb, HGRN_W), F32)],
        compiler_params=_cparams(("parallel", "arbitrary")),
        name="hgrn",
    )(gh, gh, gh, gh, lb3, nw)


def _mla_kernel(q_ref, k_ref, vt_ref, o_ref, *, t):
    i = pl.program_id(1)
    key = lax.broadcasted_iota(jnp.int32, (t, t), 0)
    qry = lax.broadcasted_iota(jnp.int32, (t, t), 1)
    causal = key <= qry

    heads = [slice(h * HEAD_PAD, (h + 1) * HEAD_PAD) for h in range(MLA_HEADS)]

    def logits(j):
        ks = pl.ds(pl.multiple_of(j * t, t), t)
        return tuple(_dot_nt(k_ref[ks, hs], q_ref[:, hs]) for hs in heads)

    def accumulate(j, ss, state, mask):
        if mask:
            ss = [jnp.where(causal, s, NEG) for s in ss]
        ms = [jnp.maximum(state[h][0], jnp.max(ss[h], axis=0, keepdims=True))
              for h in range(MLA_HEADS)]
        ps = [jnp.exp2(ss[h] - ms[h]).astype(BF) for h in range(MLA_HEADS)]
        return tuple((ms[h], jnp.exp2(state[h][0] - ms[h]) * state[h][1]
                      + _dot(vt_ref[j, heads[h], :], ps[h])) for h in range(MLA_HEADS))

    init = tuple((jnp.full((1, t), NEG, F32), jnp.zeros((HEAD_PAD, t), F32))
                 for _ in range(MLA_HEADS))
    state = lax.fori_loop(0, i, lambda j, c: accumulate(j, logits(j), c, False), init)
    state = accumulate(i, logits(i), state, True)
    outs = [acc[:MLA_V, :] * (1.0 / acc[MLA_V:MLA_V + 1, :]) for (_, acc) in state]
    o_ref[...] = jnp.concatenate(outs, axis=0).T.astype(BF)


def _mla_attn(q, k, vt, B, S, t=ATT_TILE):
    nq = S // t
    hp = MLA_HEADS * HEAD_PAD
    return pl.pallas_call(
        functools.partial(_mla_kernel, t=t),
        grid=(B, nq),
        in_specs=[pl.BlockSpec((t, hp), lambda b, i: (b * nq + i, 0)),
                  pl.BlockSpec((S, hp), lambda b, i: (b, 0)),
                  pl.BlockSpec((nq, hp, t), lambda b, i: (b, 0, 0))],
        out_specs=pl.BlockSpec((t, MLA_HEADS * MLA_V), lambda b, i: (b * nq + i, 0)),
        out_shape=jax.ShapeDtypeStruct((B * S, MLA_HEADS * MLA_V), BF),
        compiler_params=_cparams(("parallel", "arbitrary")),
        name="mla_attn",
    )(q, k, vt)


BISECT_CAP = 48
_LAST_BUCKET_FROM = next(
    n for n in range(MAX_EXACT, 1 << 20)
    if MAX_EXACT + int(math.log(n / MAX_EXACT) / math.log(MAX_DISTANCE / MAX_EXACT)
                       * (NUM_BUCKETS - MAX_EXACT)) >= NUM_BUCKETS - 1)
assert _LAST_BUCKET_FROM <= LANES - 1 and _LAST_BUCKET_FROM <= MAX_DISTANCE


def _dsa_kernel(qi_ref, w_ref, qd_ref, pos_ref, ki_ref, kd_ref, vt_ref, tab_ref, o_ref,
                sc_ref, qm_ref, *, t, n_sel):
    i = pl.program_id(1)
    lane = lax.broadcasted_iota(jnp.int32, (t, LANES), 1)
    lo_half = lane < DSA_HEAD_DIM
    key = lax.broadcasted_iota(jnp.int32, (t, t), 0)
    qry = lax.broadcasted_iota(jnp.int32, (t, t), 1)
    causal = key <= qry
    k_sel = float(n_sel)

    def key_rows(j):
        return pl.ds(pl.multiple_of(j * t, t), t)

    def over_tiles(body, init):
        c = lax.fori_loop(0, (i + 1) // 2, lambda jj, c: body(2 * jj + 1, body(2 * jj, c)), init)
        return lax.cond((i + 1) % 2 == 1, lambda c: body(i, c), lambda c: c, c)

    def fold(x, op):
        return op(x.reshape(t // SUBLANES, SUBLANES, t), axis=0)

    zero_b = jnp.zeros((t, LANES), BF)
    for p in range(IDX_HEADS // 2):
        qp = qi_ref[:, p * LANES:(p + 1) * LANES]
        qm_ref[2 * p] = jnp.where(lo_half, qp, zero_b)
        qm_ref[2 * p + 1] = jnp.where(lo_half, zero_b, qp)
    wt = w_ref[...].T * ((IDX_HEADS ** -0.5) * (IDX_DIM ** -0.5))

    def score_tile(j):
        kt = ki_ref[key_rows(j), :]
        acc = None
        for h0 in range(0, IDX_HEADS, 4):
            ss = [_dot_nt(kt, qm_ref[h]) for h in range(h0, h0 + 4)]
            for n, s in enumerate(ss):
                term = jnp.maximum(s, 0.0) * wt[h0 + n:h0 + n + 1, :]
                acc = term if acc is None else acc + term
        return acc

    def add_stats(x, c):
        mx, mn, cp, cz = c
        return (jnp.maximum(mx, fold(x, jnp.max)),
                jnp.minimum(mn, fold(jnp.where(x == -jnp.inf, jnp.inf, x), jnp.min)),
                cp + fold(jnp.where(x > 0.0, 1.0, 0.0), jnp.sum),
                cz + fold(jnp.where(x == 0.0, 1.0, 0.0), jnp.sum))

    def p1_body(j, c):
        x = score_tile(j)
        sc_ref[j] = x
        return add_stats(x, c)

    z8 = jnp.zeros((SUBLANES, t), F32)
    stats = lax.fori_loop(0, i, p1_body, (z8 - jnp.inf, z8 + jnp.inf, z8, z8))
    x_diag = jnp.where(causal, score_tile(i), -jnp.inf)
    sc_ref[i] = x_diag
    mx, mn, cp, cz = add_stats(x_diag, stats)

    def total(c):
        return jnp.sum(c, axis=0, keepdims=True)

    def count_gt(thr):
        def body(j, c):
            return c + fold(jnp.where(sc_ref[j] > thr, 1.0, 0.0), jnp.sum)
        return total(over_tiles(body, jnp.zeros((SUBLANES, t), F32)))

    row_max = jnp.max(mx, axis=0, keepdims=True)
    row_min = jnp.min(mn, axis=0, keepdims=True)
    c_pos, c_zero = total(cp), total(cz)
    n_valid = (i * t + 1 + lax.broadcasted_iota(jnp.int32, (1, t), 1)).astype(F32)

    big = n_valid > k_sel
    pos_q = jnp.logical_and(big, c_pos >= k_sel)
    tie_q = jnp.logical_and(jnp.logical_and(big, c_pos < k_sel), c_pos + c_zero >= k_sel)
    neg_q = jnp.logical_and(big, c_pos + c_zero < k_sel)
    need = jnp.where(tie_q, k_sel - c_pos, 0.0)
    lo0 = jnp.where(jnp.logical_or(pos_q, tie_q), 0.0, -jnp.inf)
    hi0 = jnp.where(neg_q, 0.0, row_max)
    c_lo0 = jnp.where(pos_q, c_pos, jnp.where(tie_q, k_sel, n_valid))

    def open_queries(c_lo):
        return jnp.max(jnp.where(c_lo > k_sel, 1.0, 0.0))

    def bis_cond(c):
        return jnp.logical_and(c[0] < BISECT_CAP, c[1] > 0.0)

    def bisect_once(lo, hi, c_lo):
        active = c_lo > k_sel
        base = jnp.maximum(lo, row_min)
        mid = base + 0.5 * (hi - base)
        cnt = count_gt(mid)
        up = jnp.logical_and(active, cnt >= k_sel)
        dn = jnp.logical_and(active, cnt < k_sel)
        return jnp.where(up, mid, lo), jnp.where(dn, mid, hi), jnp.where(up, cnt, c_lo)

    def bis_body(c):
        it, _, lo, hi, c_lo = c
        lo, hi, c_lo = bisect_once(*bisect_once(lo, hi, c_lo))
        return it + 2, open_queries(c_lo), lo, hi, c_lo

    _, _, lo, _, _ = lax.while_loop(
        bis_cond, bis_body, (jnp.int32(0), open_queries(c_lo0), lo0, hi0, c_lo0))

    def mask_plain(_):
        def body(j, c):
            sc_ref[j] = jnp.where(sc_ref[j] > lo, 0.0, NEG)
            return c
        return lax.fori_loop(0, i + 1, body, 0)

    def mask_ties(_):
        lower = jnp.where(qry <= key, 1.0, 0.0).astype(BF)

        def body(j, seen):
            x = sc_ref[j]
            zero = x == 0.0
            zf = jnp.where(zero, 1.0, 0.0)
            rank = _dot(lower, zf.astype(BF)) + seen
            take = jnp.logical_and(zero, rank <= need)
            sc_ref[j] = jnp.where(jnp.logical_or(x > lo, take), 0.0, NEG)
            return seen + total(fold(zf, jnp.sum))
        lax.fori_loop(0, i + 1, body, jnp.zeros((1, t), F32))
        return 0

    lax.cond(jnp.max(need) > 0.0, mask_ties, mask_plain, 0)

    qh = []
    for p in range(DSA_HEADS // 2):
        qp = qd_ref[:, p * LANES:(p + 1) * LANES]
        qh.append(jnp.where(lo_half, qp, zero_b))
        qh.append(jnp.where(lo_half, zero_b, qp))
    pq = pos_ref[pl.ds(i, 1), :]
    pq_min = jnp.min(pq)
    far_bias = [tab_ref[NUM_BUCKETS - 1, h] * LOG2E for h in range(DSA_HEADS)]
    log_ratio = math.log(MAX_DISTANCE / MAX_EXACT)

    dist = lax.broadcasted_iota(jnp.int32, (SUBLANES, LANES), 1)
    large = MAX_EXACT + (jnp.log(jnp.maximum(dist, 1).astype(F32) / MAX_EXACT) / log_ratio
                         * (NUM_BUCKETS - MAX_EXACT)).astype(jnp.int32)
    bucket = jnp.where(dist < MAX_EXACT, dist, jnp.minimum(large, NUM_BUCKETS - 1))
    by_dist = []
    for h in range(DSA_HEADS):
        bh = jnp.full((SUBLANES, LANES), tab_ref[0, h], F32)
        for jb in range(1, NUM_BUCKETS):
            bh = jnp.where(bucket >= jb, tab_ref[jb, h], bh)
        by_dist.append(jnp.concatenate([bh * LOG2E] * (t // SUBLANES), axis=0))

    def pair_bias(pk_row):
        pk = jnp.broadcast_to(pk_row, (SUBLANES, t)).T[:, 0:1]
        n = jnp.clip(pq - pk, 0, LANES - 1)
        return [jnp.concatenate([jnp.take_along_axis(by_dist[h], n[:, c * LANES:(c + 1) * LANES], axis=1)
                                 for c in range(t // LANES)], axis=1) for h in range(DSA_HEADS)]

    def attend(j, carry, near):
        madd = sc_ref[j]
        bias = pair_bias(pos_ref[pl.ds(j, 1), :]) if near else far_bias
        ks = key_rows(j)
        kps = [kd_ref[ks, p * LANES:(p + 1) * LANES] for p in range(DSA_HEADS // 2)]
        ss = [_dot_nt(kps[h // 2], qh[h]) + (madd + bias[h]) for h in range(DSA_HEADS)]
        ms = [jnp.maximum(carry[h][0], jnp.max(ss[h], axis=0, keepdims=True))
              for h in range(DSA_HEADS)]
        ps = [jnp.exp2(ss[h] - ms[h]).astype(BF) for h in range(DSA_HEADS)]
        return tuple((ms[h], jnp.exp2(carry[h][0] - ms[h]) * carry[h][1]
                      + _dot(vt_ref[j, h * DSA_V_ROWS:(h + 1) * DSA_V_ROWS, :], ps[h]))
                     for h in range(DSA_HEADS))

    def p3_body(j, carry):
        pk_max = jnp.max(pos_ref[pl.ds(j, 1), :])
        return lax.cond(pq_min - pk_max >= MAX_DISTANCE,
                        lambda c: attend(j, c, False), lambda c: attend(j, c, True), carry)

    init = tuple((jnp.full((1, t), NEG, F32), jnp.zeros((DSA_V_ROWS, t), F32))
                 for _ in range(DSA_HEADS))
    carry = lax.fori_loop(0, i + 1, p3_body, init)
    outs = [acc[:DSA_HEAD_DIM, :] * (1.0 / acc[DSA_HEAD_DIM:DSA_HEAD_DIM + 1, :])
            for (_, acc) in carry]
    o_ref[...] = jnp.concatenate(outs, axis=0).T.astype(BF)


def _dsa(gq, gk, gd, ki2, vdt, pos_tiles, tab, B, S, t=ATT_TILE):
    nq = S // t
    n_sel = min(TOPK_MAX, S // 4)
    vrows = DSA_HEADS * DSA_V_ROWS
    qrow = lambda c: (lambda b, i: (b * nq + i, c))
    return pl.pallas_call(
        functools.partial(_dsa_kernel, t=t, n_sel=n_sel),
        grid=(B, nq),
        in_specs=[pl.BlockSpec((t, W_Q), qrow(0)),
                  pl.BlockSpec((t, LANES), qrow(1)),
                  pl.BlockSpec((t, DSA_WIDTH), qrow(0)),
                  pl.BlockSpec((None, nq, t), lambda b, i: (b, 0, 0)),
                  pl.BlockSpec((S, LANES), lambda b, i: (b, 0)),
                  pl.BlockSpec((S, DSA_WIDTH), lambda b, i: (b, 1)),
                  pl.BlockSpec((nq, vrows, t), lambda b, i: (b, 0, 0)),
                  pl.BlockSpec(memory_space=pltpu.SMEM)],
        out_specs=pl.BlockSpec((t, DSA_WIDTH), qrow(0)),
        out_shape=jax.ShapeDtypeStruct((B * S, DSA_WIDTH), BF),
        scratch_shapes=[pltpu.VMEM((nq, t, t), F32), pltpu.VMEM((IDX_HEADS, t, LANES), BF)],
        compiler_params=_cparams(("parallel", "arbitrary")),
        name="dsa",
    )(gq, gk, gd, pos_tiles, ki2, gd, vdt, tab)


def _outmlp_kernel(h_ref, yh_ref, ym_ref, yd_ref, wo_ref, nw_ref, w1_ref, w2_ref, fw_ref, o_ref,
                   *, final, ff_chunk):
    mixed = jnp.concatenate([yh_ref[...], ym_ref[...], yd_ref[...]], axis=1)
    h = h_ref[...] + _dot(mixed, wo_ref[...])
    u = _rms(h, nw_ref[...]).astype(BF)
    out = h
    for c in range(D_FF // ff_chunk):
        cs = slice(c * ff_chunk, (c + 1) * ff_chunk)
        a = jnp.maximum(_dot(u, w1_ref[:, cs]), 0.0)
        out = out + _dot((a * a).astype(BF), w2_ref[cs, :])
    if final:
        out = _rms(out, fw_ref[...])
    o_ref[...] = out


def _outmlp(h, yh, ym, yd, wo, nw, w1, w2, fw, final, tm=ROW_TILE, ff_chunk=1024):
    T = h.shape[0]
    tm = min(tm, T)
    row = lambda i: (i, 0)
    return pl.pallas_call(
        functools.partial(_outmlp_kernel, final=final, ff_chunk=ff_chunk),
        grid=(T // tm,),
        in_specs=[pl.BlockSpec((tm, D_MODEL), row), pl.BlockSpec((tm, HGRN_W), row),
                  pl.BlockSpec((tm, MLA_HEADS * MLA_V), row), pl.BlockSpec((tm, DSA_WIDTH), row),
                  _const_spec((D_MODEL, D_MODEL)), _const_spec((1, D_MODEL)),
                  _const_spec((D_MODEL, D_FF)), _const_spec((D_FF, D_MODEL)),
                  _const_spec((1, D_MODEL))],
        out_specs=pl.BlockSpec((tm, D_MODEL), row),
        out_shape=jax.ShapeDtypeStruct((T, D_MODEL), F32),
        compiler_params=_cparams(("parallel",)),
        name="outmlp",
    )(h, yh, ym, yd, wo, nw, w1, w2, fw)


def _rot_cols(w):
    half = w.shape[-1] // 2
    return jnp.concatenate([-w[..., half:], w[..., :half]], axis=-1)


def _place(w, width, off):
    pad = [(0, 0)] * (w.ndim - 1) + [(off, width - off - w.shape[-1])]
    return jnp.pad(w, pad)


def _layout_w_in(w_in):
    sizes = (HGRN_W, HGRN_W, HGRN_W, HGRN_W, MLA_Q_LORA, MLA_KV_LORA + MLA_ROPE,
             DSA_WIDTH, DSA_WIDTH, DSA_WIDTH, IDX_HEADS * IDX_DIM, IDX_DIM, IDX_HEADS)
    offs = [0]
    for s in sizes:
        offs.append(offs[-1] + s)
    hq, hf, hi, hg, mqa, mkva, dq, dk, dv, iq, ik, iw = [
        w_in[..., offs[n]:offs[n + 1]] for n in range(len(sizes))]
    ckv, kpe = mkva[..., :MLA_KV_LORA], mkva[..., MLA_KV_LORA:]
    cols = [hq, hf, hi, hg,
            _place(mqa, 256, 0), ckv, _place(kpe, LANES, MLA_NOPE), _place(_rot_cols(kpe), LANES, MLA_NOPE),
            dq * (DSA_HEAD_DIM ** -0.5 * LOG2E), dk,
            iq,
            ik, ik, _place(iw, LANES, 0)]
    w_cat = jnp.concatenate(cols, axis=-1).astype(BF)
    L = w_in.shape[0]
    dvt = jnp.swapaxes(dv, 1, 2).reshape(L, DSA_HEADS, DSA_HEAD_DIM, D_MODEL)
    dvt = jnp.pad(dvt, ((0, 0), (0, 0), (0, DSA_V_ROWS - DSA_HEAD_DIM), (0, 0)))
    return w_cat, dvt.reshape(L, DSA_HEADS * DSA_V_ROWS, D_MODEL).astype(BF)


def _layout_mla(w_qb, w_kvb):
    L = w_qb.shape[0]
    dq = MLA_NOPE + MLA_ROPE
    wq = w_qb.reshape(L, MLA_Q_LORA, MLA_HEADS, dq)
    wq_rot = jnp.concatenate([jnp.zeros_like(wq[..., :MLA_NOPE]), _rot_cols(wq[..., MLA_NOPE:])], axis=-1)
    pad_q = lambda w: jnp.pad(w, ((0, 0), (0, 256 - MLA_Q_LORA), (0, 0), (0, HEAD_PAD - dq))).reshape(
        L, 256, MLA_HEADS * HEAD_PAD).astype(BF)
    wkv = w_kvb.reshape(L, MLA_KV_LORA, MLA_HEADS, MLA_NOPE + MLA_V)
    pad_kv = lambda w: jnp.pad(w, ((0, 0), (0, 0), (0, 0), (0, HEAD_PAD - w.shape[-1]))).reshape(
        L, MLA_KV_LORA, MLA_HEADS * HEAD_PAD).astype(BF)
    wvt = jnp.swapaxes(pad_kv(wkv[..., MLA_NOPE:]), 1, 2)
    return pad_q(wq), pad_q(wq_rot), pad_kv(wkv[..., :MLA_NOPE]), wvt


def kernel(x, positions, attn_norm_w, w_in, hgrn_lb_logits, hgrn_norm_w, mla_q_norm_w, mla_w_qb,
           mla_kv_norm_w, mla_w_kvb, idx_k_norm_w, idx_k_norm_b, rel_bias_table, w_out,
           mlp_norm_w, w_mlp_in, w_mlp_out, final_norm_w):
    B, S, _ = x.shape
    T = B * S
    depth = w_in.shape[0]

    inv_freq = 1.0 / (ROPE_THETA ** (jnp.arange(0, MLA_ROPE, 2, dtype=F32) / MLA_ROPE))
    ang = positions.astype(F32)[..., None] * inv_freq
    cos, sin = jnp.cos(ang).reshape(T, -1), jnp.sin(ang).reshape(T, -1)
    cos_t = jnp.concatenate([jnp.ones((T, MLA_NOPE), F32), cos, cos,
                             jnp.zeros((T, HEAD_PAD - MLA_NOPE - MLA_ROPE), F32)], axis=1)
    sin_t = jnp.concatenate([jnp.zeros((T, MLA_NOPE), F32), sin, sin,
                             jnp.zeros((T, HEAD_PAD - MLA_NOPE - MLA_ROPE), F32)], axis=1)
    pos_tiles = positions.reshape(B, S // ATT_TILE, ATT_TILE)

    lb = jnp.cumsum(jax.nn.softmax(hgrn_lb_logits.astype(F32), axis=0), axis=0)
    lb = lb - lb[0:1]
    lb3 = jnp.stack([jnp.log(lb), jnp.log1p(-lb), 1.0 - lb], axis=1)

    w_cat, w_dvt = _layout_w_in(w_in)
    wq, wqr, wk, wvt = _layout_mla(mla_w_qb, mla_w_kvb)
    qnw = jnp.pad(mla_q_norm_w, ((0, 0), (0, 256 - MLA_Q_LORA)))
    lnw = jnp.concatenate([idx_k_norm_w, idx_k_norm_w], axis=-1)
    lnb = jnp.concatenate([idx_k_norm_b, idx_k_norm_b], axis=-1)
    wo = w_out.astype(BF)
    w1 = w_mlp_in.astype(BF)
    w2 = w_mlp_out.astype(BF)
    tab = rel_bias_table.astype(F32)

    h = x.reshape(T, D_MODEL)
    for l in range(depth):
        gh, gm, gd, gq, gk, vdt = _inproj(h, attn_norm_w[l][None], w_cat[l], w_dvt[l])
        q_m, k_m, v_m, ki2 = _prep(gm, gk, cos_t, sin_t, qnw[l][None], wq[l], wqr[l],
                                   mla_kv_norm_w[l][None], wk[l], wvt[l], lnw[l][None], lnb[l][None])
        y_h = _hgrn(gh, lb3[l], hgrn_norm_w[l][None], B, S)
        y_m = _mla_attn(q_m, k_m, v_m, B, S)
        y_d = _dsa(gq, gk, gd, ki2, vdt, pos_tiles, tab, B, S)
        h = _outmlp(h, y_h, y_m, y_d, wo[l], mlp_norm_w[l][None], w1[l], w2[l],
                    final_norm_w[None], final=(l == depth - 1))
    return h.reshape(B, S, D_MODEL)
```

```python
import functools
import math

import jax
import jax.numpy as jnp
from jax import lax
from jax.experimental import pallas as pl
from jax.experimental.pallas import tpu as pltpu

D_MODEL = 1024
HGRN_HEADS = 4
HGRN_DK = 128
HGRN_DV = 128
HGRN_W = HGRN_HEADS * HGRN_DK
HGRN_CHUNK = 32
MLA_HEADS = 4
MLA_NOPE = 64
MLA_ROPE = 32
MLA_V = 64
MLA_Q_LORA = 192
MLA_KV_LORA = 128
ROPE_THETA = 10000.0
DSA_HEADS = 4
DSA_HEAD_DIM = 64
DSA_WIDTH = DSA_HEADS * DSA_HEAD_DIM
IDX_HEADS = 8
IDX_DIM = 64
TOPK_MAX = 256
NUM_BUCKETS = 32
MAX_EXACT = NUM_BUCKETS // 2
MAX_DISTANCE = 128
D_FF = 4 * D_MODEL
EPS = 1e-6

LANES = 128
SUBLANES = 8
BF16_ROWS = 16
HEAD_PAD = 128
NEG = -1e30
ATT_TILE = 256
MLA_KEY_TILE = 512
ROW_TILE = 512
VMEM_LIMIT = 52 * 1024 * 1024
LOG2E = math.log2(math.e)

DSA_V_ROWS = DSA_HEAD_DIM + BF16_ROWS

C_H = 0
C_M = C_H + 4 * HGRN_W
W_M = 640
C_D = C_M + W_M
W_D = 2 * DSA_WIDTH
C_Q = C_D + W_D
W_Q = IDX_HEADS * IDX_DIM
C_K = C_Q + W_Q
W_K = 256
N_PAD = C_K + W_K

BF = jnp.bfloat16
F32 = jnp.float32


def _dot(a, b):
    return jnp.dot(a, b, preferred_element_type=F32)


def _dot_nt(a, b):
    return lax.dot_general(a, b, (((1,), (1,)), ((), ())), preferred_element_type=F32)


def _rms(x, w, n=None):
    n = x.shape[-1] if n is None else n
    ms = jnp.sum(x * x, axis=-1, keepdims=True) * (1.0 / n)
    return x * lax.rsqrt(ms + EPS) * w


def _cparams(sem):
    return pltpu.CompilerParams(dimension_semantics=sem, vmem_limit_bytes=VMEM_LIMIT)


def _const_spec(shape):
    nd = len(shape)
    return pl.BlockSpec(shape, lambda *_: (0,) * nd, pipeline_mode=pl.Buffered(1))


def _inproj_kernel(x_ref, nw_ref, w_ref, wvt_ref, oh_ref, om_ref, od_ref, oq_ref, ok_ref, ovt_ref):
    u = _rms(x_ref[...], nw_ref[...]).astype(BF)
    oh_ref[...] = _dot(u, w_ref[:, C_H:C_M])
    om_ref[...] = _dot(u, w_ref[:, C_M:C_D])
    od_ref[...] = _dot(u, w_ref[:, C_D:C_Q]).astype(BF)
    oq_ref[...] = _dot(u, w_ref[:, C_Q:C_K]).astype(BF)
    ok_ref[...] = _dot(u, w_ref[:, C_K:N_PAD])
    vt = _dot_nt(wvt_ref[...], u)
    row = lax.broadcasted_iota(jnp.int32, vt.shape, 0)
    vt = jnp.where(row % DSA_V_ROWS == DSA_HEAD_DIM, 1.0, vt).astype(BF)
    for n in range(ovt_ref.shape[0]):
        ovt_ref[n] = vt[:, n * ATT_TILE:(n + 1) * ATT_TILE]


def _inproj(h, nw, w, wvt, tm=ROW_TILE):
    T = h.shape[0]
    tm = min(tm, T)
    row = lambda i: (i, 0)
    vrows = DSA_HEADS * DSA_V_ROWS
    per = tm // ATT_TILE
    return pl.pallas_call(
        _inproj_kernel,
        grid=(T // tm,),
        in_specs=[pl.BlockSpec((tm, D_MODEL), row), _const_spec((1, D_MODEL)),
                  _const_spec((D_MODEL, N_PAD)), _const_spec((vrows, D_MODEL))],
        out_specs=[pl.BlockSpec((tm, 4 * HGRN_W), row), pl.BlockSpec((tm, W_M), row),
                   pl.BlockSpec((tm, W_D), row), pl.BlockSpec((tm, W_Q), row),
                   pl.BlockSpec((tm, W_K), row),
                   pl.BlockSpec((per, vrows, ATT_TILE), lambda i: (i, 0, 0))],
        out_shape=[jax.ShapeDtypeStruct((T, 4 * HGRN_W), F32), jax.ShapeDtypeStruct((T, W_M), F32),
                   jax.ShapeDtypeStruct((T, W_D), BF), jax.ShapeDtypeStruct((T, W_Q), BF),
                   jax.ShapeDtypeStruct((T, W_K), F32),
                   jax.ShapeDtypeStruct((T // ATT_TILE, vrows, ATT_TILE), BF)],
        compiler_params=_cparams(("parallel",)),
        name="inproj",
    )(h, nw, w, wvt)


def _prep_kernel(gm_ref, gk_ref, cos_ref, sin_ref, qnw_ref, wq_ref, wqr_ref, kvnw_ref, wk_ref,
                 wvt_ref, lnw_ref, lnb_ref, q_ref, k_ref, vt_ref, ki_ref):
    cs = cos_ref[...]
    sn = sin_ref[...]
    cs4 = jnp.concatenate([cs] * MLA_HEADS, axis=1)
    sn4 = jnp.concatenate([sn] * MLA_HEADS, axis=1)
    qn = _rms(gm_ref[:, 0:256], qnw_ref[...], n=MLA_Q_LORA).astype(BF)
    scale = (MLA_NOPE + MLA_ROPE) ** -0.5 * LOG2E
    q = (_dot(qn, wq_ref[...]) * cs4 + _dot(qn, wqr_ref[...]) * sn4) * scale
    q_ref[...] = q.astype(BF)
    cn = _rms(gm_ref[:, 256:384], kvnw_ref[...]).astype(BF)
    kp = gm_ref[:, 384:512] * cs + gm_ref[:, 512:640] * sn
    k = _dot(cn, wk_ref[...]) + jnp.concatenate([kp] * MLA_HEADS, axis=1)
    k_ref[...] = k.astype(BF)
    vt = _dot_nt(wvt_ref[...], cn)
    row = lax.broadcasted_iota(jnp.int32, vt.shape, 0)
    vt_ref[...] = jnp.where(row % HEAD_PAD == MLA_V, 1.0, vt).astype(BF)
    x = gk_ref[:, 0:LANES]
    first = lax.broadcasted_iota(jnp.int32, x.shape, 1) < IDX_DIM
    mu = jnp.sum(jnp.where(first, x, 0.0), axis=-1, keepdims=True) * (1.0 / IDX_DIM)
    xc = x - mu
    var = jnp.sum(jnp.where(first, xc * xc, 0.0), axis=-1, keepdims=True) * (1.0 / IDX_DIM)
    ki_ref[...] = (xc * lax.rsqrt(var + EPS) * lnw_ref[...] + lnb_ref[...]).astype(BF)


def _prep(gm, gk, cos_t, sin_t, qnw, wq, wqr, kvnw, wk, wvt, lnw, lnb, tm=MLA_KEY_TILE):
    T = gm.shape[0]
    row = lambda i: (i, 0)
    hp = MLA_HEADS * HEAD_PAD
    return pl.pallas_call(
        _prep_kernel,
        grid=(T // tm,),
        in_specs=[pl.BlockSpec((tm, W_M), row), pl.BlockSpec((tm, W_K), row),
                  pl.BlockSpec((tm, LANES), row), pl.BlockSpec((tm, LANES), row),
                  _const_spec((1, 256)), _const_spec((256, hp)), _const_spec((256, hp)),
                  _const_spec((1, MLA_KV_LORA)), _const_spec((MLA_KV_LORA, hp)),
                  _const_spec((hp, MLA_KV_LORA)), _const_spec((1, LANES)), _const_spec((1, LANES))],
        out_specs=[pl.BlockSpec((tm, hp), row), pl.BlockSpec((tm, hp), row),
                   pl.BlockSpec((None, hp, tm), lambda i: (i, 0, 0)), pl.BlockSpec((tm, LANES), row)],
        out_shape=[jax.ShapeDtypeStruct((T, hp), BF), jax.ShapeDtypeStruct((T, hp), BF),
                   jax.ShapeDtypeStruct((T // tm, hp, tm), BF), jax.ShapeDtypeStruct((T, LANES), BF)],
        compiler_params=_cparams(("parallel",)),
        name="prep",
    )(gm, gk, cos_t, sin_t, qnw, wq, wqr, kvnw, wk, wvt, lnw, lnb)


def _split3(x):
    a = x.astype(BF)
    r = x - a.astype(F32)
    b = r.astype(BF)
    c = (r - b.astype(F32)).astype(BF)
    return a, b, c


def _hgrn_kernel(q_ref, f_ref, i_ref, g_ref, lb_ref, nw_ref, o_ref, st_ref, os_ref, *, tb):
    C = HGRN_CHUNK
    nc = tb // C

    @pl.when(pl.program_id(1) == 0)
    def _():
        st_ref[...] = jnp.zeros_like(st_ref)

    fp = f_ref[...]
    log_lb = lb_ref[0:1, :]
    log1m_lb = lb_ref[1:2, :]
    one_m_lb = lb_ref[2:3, :]
    ls = jnp.minimum(fp, 0.0) - jnp.log1p(jnp.exp(-jnp.abs(fp)))
    b = log1m_lb + ls
    log_f = jnp.maximum(log_lb, b) + jnp.log1p(jnp.exp(-jnp.abs(log_lb - b)))
    k = one_m_lb * (1.0 / (1.0 + jnp.exp(fp)))

    r = lax.broadcasted_iota(jnp.int32, (tb, tb), 0)
    c = lax.broadcasted_iota(jnp.int32, (tb, tb), 1)
    same = (r // C) == (c // C)
    tri = jnp.where(same & (c <= r), 1.0, 0.0).astype(BF)
    blk = jnp.where(same, 1.0, 0.0).astype(BF)
    a0, a1, a2 = _split3(log_f)
    G = _dot(tri, a0) + _dot(tri, a1) + _dot(tri, a2)
    G_last = _dot(blk, a0) + _dot(blk, a1) + _dot(blk, a2)

    eg = jnp.exp(G)
    q_dec = (q_ref[...] * (HGRN_DK ** -0.5) * eg).astype(BF)
    k_inv = (k * jnp.exp(-G)).astype(BF)
    k_state = k * jnp.exp(G_last - G)
    decay = jnp.exp(G_last)
    v = i_ref[...]
    vb = v.astype(BF)

    rr = lax.broadcasted_iota(jnp.int32, (C, C), 0)
    cc = lax.broadcasted_iota(jnp.int32, (C, C), 1)
    causal = cc <= rr

    for h in range(HGRN_HEADS):
        hs = slice(h * HGRN_DK, (h + 1) * HGRN_DK)
        st = st_ref[h]
        for n in range(nc):
            rs = slice(n * C, (n + 1) * C)
            qd = q_dec[rs, hs]
            A = jnp.where(causal, _dot_nt(qd, k_inv[rs, hs]), 0.0)
            o = _dot(A.astype(BF), vb[rs, hs]) + _dot_nt(qd, st.astype(BF))
            os_ref[rs, hs] = o
            vt = v[rs, hs].T.astype(BF)
            st = decay[n * C:n * C + 1, hs] * st + _dot(vt, k_state[rs, hs].astype(BF))
        st_ref[h] = st

    g = g_ref[...]
    gate = g * (1.0 / (1.0 + jnp.exp(-g)))
    for h in range(HGRN_HEADS):
        hs = slice(h * HGRN_DK, (h + 1) * HGRN_DK)
        o_ref[:, hs] = (_rms(os_ref[:, hs], nw_ref[...]) * gate[:, hs]).astype(BF)


def _hgrn(gh, lb3, nw, B, S, tb=256):
    nb = S // tb
    col = lambda j: (lambda b, i: (b * nb + i, j))
    return pl.pallas_call(
        functools.partial(_hgrn_kernel, tb=tb),
        grid=(B, nb),
        in_specs=[pl.BlockSpec((tb, HGRN_W), col(0)), pl.BlockSpec((tb, HGRN_W), col(1)),
                  pl.BlockSpec((tb, HGRN_W), col(2)), pl.BlockSpec((tb, HGRN_W), col(3)),
                  pl.BlockSpec((3, HGRN_W), lambda b, i: (0, 0)),
                  pl.BlockSpec((1, HGRN_DV), lambda b, i: (0, 0))],
        out_specs=pl.BlockSpec((tb, HGRN_W), col(0)),
        out_shape=jax.ShapeDtypeStruct((B * S, HGRN_W), BF),
        scratch_shapes=[pltpu.VMEM((HGRN_HEADS, HGRN_DV, HGRN_DK), F32),
                        pltpu.VMEM((tb, HGRN_W), F32)],
        compiler_params=_cparams(("parallel", "arbitrary")),
        name="hgrn",
    )(gh, gh, gh, gh, lb3, nw)


def _mla_kernel(q_ref, k_ref, vt_ref, o_ref, *, t, tk):
    i = pl.program_id(1)
    n_full = (i * t) // tk
    key = lax.broadcasted_iota(jnp.int32, (tk, t), 0)
    qry = lax.broadcasted_iota(jnp.int32, (tk, t), 1)
    causal = n_full * tk + key <= i * t + qry

    heads = [slice(h * HEAD_PAD, (h + 1) * HEAD_PAD) for h in range(MLA_HEADS)]

    def logits(j):
        ks = pl.ds(pl.multiple_of(j * tk, tk), tk)
        return tuple(_dot_nt(k_ref[ks, hs], q_ref[:, hs]) for hs in heads)

    def accumulate(j, ss, state, mask):
        if mask:
            ss = [jnp.where(causal, s, NEG) for s in ss]
        ms = [jnp.maximum(state[h][0], jnp.max(ss[h], axis=0, keepdims=True))
              for h in range(MLA_HEADS)]
        ps = [jnp.exp2(ss[h] - ms[h]).astype(BF) for h in range(MLA_HEADS)]
        return tuple((ms[h], jnp.exp2(state[h][0] - ms[h]) * state[h][1]
                      + _dot(vt_ref[j, heads[h], :], ps[h])) for h in range(MLA_HEADS))

    init = tuple((jnp.full((1, t), NEG, F32), jnp.zeros((HEAD_PAD, t), F32))
                 for _ in range(MLA_HEADS))
    state = lax.fori_loop(0, n_full, lambda j, c: accumulate(j, logits(j), c, False), init)
    state = accumulate(n_full, logits(n_full), state, True)
    outs = [acc[:MLA_V, :] * (1.0 / acc[MLA_V:MLA_V + 1, :]) for (_, acc) in state]
    o_ref[...] = jnp.concatenate(outs, axis=0).T.astype(BF)


def _mla_attn(q, k, vt, B, S, t=ATT_TILE, tk=MLA_KEY_TILE):
    nq = S // t
    nk = S // tk
    hp = MLA_HEADS * HEAD_PAD
    return pl.pallas_call(
        functools.partial(_mla_kernel, t=t, tk=tk),
        grid=(B, nq),
        in_specs=[pl.BlockSpec((t, hp), lambda b, i: (b * nq + i, 0)),
                  pl.BlockSpec((S, hp), lambda b, i: (b, 0)),
                  pl.BlockSpec((nk, hp, tk), lambda b, i: (b, 0, 0))],
        out_specs=pl.BlockSpec((t, MLA_HEADS * MLA_V), lambda b, i: (b * nq + i, 0)),
        out_shape=jax.ShapeDtypeStruct((B * S, MLA_HEADS * MLA_V), BF),
        compiler_params=_cparams(("parallel", "arbitrary")),
        name="mla_attn",
    )(q, k, vt)


BISECT_CAP = 48
_LAST_BUCKET_FROM = next(
    n for n in range(MAX_EXACT, 1 << 20)
    if MAX_EXACT + int(math.log(n / MAX_EXACT) / math.log(MAX_DISTANCE / MAX_EXACT)
                       * (NUM_BUCKETS - MAX_EXACT)) >= NUM_BUCKETS - 1)
assert _LAST_BUCKET_FROM <= LANES - 1 and _LAST_BUCKET_FROM <= MAX_DISTANCE


def _dsa_kernel(qi_ref, w_ref, qd_ref, pos_ref, ki_ref, kd_ref, vt_ref, tab_ref, pmin_ref, pmax_ref,
                o_ref, sc_ref, qm_ref, *, t, n_sel):
    i = pl.program_id(1)
    lane = lax.broadcasted_iota(jnp.int32, (t, LANES), 1)
    lo_half = lane < DSA_HEAD_DIM
    key = lax.broadcasted_iota(jnp.int32, (t, t), 0)
    qry = lax.broadcasted_iota(jnp.int32, (t, t), 1)
    causal = key <= qry
    k_sel = float(n_sel)

    def key_rows(j):
        return pl.ds(pl.multiple_of(j * t, t), t)

    def over_tiles(body, init):
        c = lax.fori_loop(0, (i + 1) // 2, lambda jj, c: body(2 * jj + 1, body(2 * jj, c)), init)
        return lax.cond((i + 1) % 2 == 1, lambda c: body(i, c), lambda c: c, c)

    def fold(x, op):
        return op(x.reshape(t // SUBLANES, SUBLANES, t), axis=0)

    zero_b = jnp.zeros((t, LANES), BF)
    for p in range(IDX_HEADS // 2):
        qp = qi_ref[:, p * LANES:(p + 1) * LANES]
        qm_ref[2 * p] = jnp.where(lo_half, qp, zero_b)
        qm_ref[2 * p + 1] = jnp.where(lo_half, zero_b, qp)
    wt = w_ref[...].T * ((IDX_HEADS ** -0.5) * (IDX_DIM ** -0.5))

    def score_tile(j):
        kt = ki_ref[key_rows(j), :]
        acc = None
        for h0 in range(0, IDX_HEADS, 4):
            ss = [_dot_nt(kt, qm_ref[h]) for h in range(h0, h0 + 4)]
            for n, s in enumerate(ss):
                term = jnp.maximum(s, 0.0) * wt[h0 + n:h0 + n + 1, :]
                acc = term if acc is None else acc + term
        return acc

    def add_stats(x, c):
        mx, mn, cp, cz = c
        return (jnp.maximum(mx, fold(x, jnp.max)),
                jnp.minimum(mn, fold(jnp.where(x == -jnp.inf, jnp.inf, x), jnp.min)),
                cp + fold(jnp.where(x > 0.0, 1.0, 0.0), jnp.sum),
                cz + fold(jnp.where(x == 0.0, 1.0, 0.0), jnp.sum))

    def p1_body(j, c):
        x = score_tile(j)
        sc_ref[j] = x
        return add_stats(x, c)

    z8 = jnp.zeros((SUBLANES, t), F32)
    stats = lax.fori_loop(0, i, p1_body, (z8 - jnp.inf, z8 + jnp.inf, z8, z8))
    x_diag = jnp.where(causal, score_tile(i), -jnp.inf)
    sc_ref[i] = x_diag
    mx, mn, cp, cz = add_stats(x_diag, stats)

    def total(c):
        return jnp.sum(c, axis=0, keepdims=True)

    def count_gt(thr):
        def body(j, c):
            return c + fold(jnp.where(sc_ref[j] > thr, 1.0, 0.0), jnp.sum)
        return total(over_tiles(body, jnp.zeros((SUBLANES, t), F32)))

    row_max = jnp.max(mx, axis=0, keepdims=True)
    row_min = jnp.min(mn, axis=0, keepdims=True)
    c_pos, c_zero = total(cp), total(cz)
    n_valid = (i * t + 1 + lax.broadcasted_iota(jnp.int32, (1, t), 1)).astype(F32)

    big = n_valid > k_sel
    pos_q = jnp.logical_and(big, c_pos >= k_sel)
    tie_q = jnp.logical_and(jnp.logical_and(big, c_pos < k_sel), c_pos + c_zero >= k_sel)
    neg_q = jnp.logical_and(big, c_pos + c_zero < k_sel)
    need = jnp.where(tie_q, k_sel - c_pos, 0.0)
    lo0 = jnp.where(jnp.logical_or(pos_q, tie_q), 0.0, -jnp.inf)
    hi0 = jnp.where(neg_q, 0.0, row_max)
    c_lo0 = jnp.where(pos_q, c_pos, jnp.where(tie_q, k_sel, n_valid))

    def open_queries(c_lo):
        return jnp.max(jnp.where(c_lo > k_sel, 1.0, 0.0))

    def bis_cond(c):
        return jnp.logical_and(c[0] < BISECT_CAP, c[1] > 0.0)

    def bisect_once(lo, hi, c_lo):
        active = c_lo > k_sel
        base = jnp.maximum(lo, row_min)
        mid = base + 0.5 * (hi - base)
        cnt = count_gt(mid)
        up = jnp.logical_and(active, cnt >= k_sel)
        dn = jnp.logical_and(active, cnt < k_sel)
        return jnp.where(up, mid, lo), jnp.where(dn, mid, hi), jnp.where(up, cnt, c_lo)

    def bis_body(c):
        it, _, lo, hi, c_lo = c
        lo, hi, c_lo = bisect_once(*bisect_once(lo, hi, c_lo))
        return it + 2, open_queries(c_lo), lo, hi, c_lo

    _, _, lo, _, _ = lax.while_loop(
        bis_cond, bis_body, (jnp.int32(0), open_queries(c_lo0), lo0, hi0, c_lo0))

    def mask_plain(_):
        def body(j, c):
            sc_ref[j] = jnp.where(sc_ref[j] > lo, 0.0, NEG)
            return c
        return lax.fori_loop(0, i + 1, body, 0)

    def mask_ties(_):
        lower = jnp.where(qry <= key, 1.0, 0.0).astype(BF)

        def body(j, seen):
            x = sc_ref[j]
            zero = x == 0.0
            zf = jnp.where(zero, 1.0, 0.0)
            rank = _dot(lower, zf.astype(BF)) + seen
            take = jnp.logical_and(zero, rank <= need)
            sc_ref[j] = jnp.where(jnp.logical_or(x > lo, take), 0.0, NEG)
            return seen + total(fold(zf, jnp.sum))
        lax.fori_loop(0, i + 1, body, jnp.zeros((1, t), F32))
        return 0

    lax.cond(jnp.max(need) > 0.0, mask_ties, mask_plain, 0)

    qh = []
    for p in range(DSA_HEADS // 2):
        qp = qd_ref[:, p * LANES:(p + 1) * LANES]
        qh.append(jnp.where(lo_half, qp, zero_b))
        qh.append(jnp.where(lo_half, zero_b, qp))
    pq = pos_ref[pl.ds(i, 1), :]
    pq_min = pmin_ref[pl.program_id(0), i]
    far_bias = [tab_ref[NUM_BUCKETS - 1, h] * LOG2E for h in range(DSA_HEADS)]
    log_ratio = math.log(MAX_DISTANCE / MAX_EXACT)

    dist = lax.broadcasted_iota(jnp.int32, (SUBLANES, LANES), 1)
    large = MAX_EXACT + (jnp.log(jnp.maximum(dist, 1).astype(F32) / MAX_EXACT) / log_ratio
                         * (NUM_BUCKETS - MAX_EXACT)).astype(jnp.int32)
    bucket = jnp.where(dist < MAX_EXACT, dist, jnp.minimum(large, NUM_BUCKETS - 1))
    by_dist = []
    for h in range(DSA_HEADS):
        bh = jnp.full((SUBLANES, LANES), tab_ref[0, h], F32)
        for jb in range(1, NUM_BUCKETS):
            bh = jnp.where(bucket >= jb, tab_ref[jb, h], bh)
        by_dist.append(jnp.concatenate([bh * LOG2E] * (t // SUBLANES), axis=0))

    def pair_bias(pk_row):
        pk = jnp.broadcast_to(pk_row, (SUBLANES, t)).T[:, 0:1]
        n = jnp.clip(pq - pk, 0, LANES - 1)
        return [jnp.concatenate([jnp.take_along_axis(by_dist[h], n[:, c * LANES:(c + 1) * LANES], axis=1)
                                 for c in range(t // LANES)], axis=1) for h in range(DSA_HEADS)]

    def attend(j, carry, near):
        madd = sc_ref[j]
        bias = pair_bias(pos_ref[pl.ds(j, 1), :]) if near else far_bias
        ks = key_rows(j)
        kps = [kd_ref[ks, p * LANES:(p + 1) * LANES] for p in range(DSA_HEADS // 2)]
        ss = [_dot_nt(kps[h // 2], qh[h]) + (madd + bias[h]) for h in range(DSA_HEADS)]
        ms = [jnp.maximum(carry[h][0], jnp.max(ss[h], axis=0, keepdims=True))
              for h in range(DSA_HEADS)]
        ps = [jnp.exp2(ss[h] - ms[h]).astype(BF) for h in range(DSA_HEADS)]
        return tuple((ms[h], jnp.exp2(carry[h][0] - ms[h]) * carry[h][1]
                      + _dot(vt_ref[j, h * DSA_V_ROWS:(h + 1) * DSA_V_ROWS, :], ps[h]))
                     for h in range(DSA_HEADS))

    def p3_body(j, carry):
        pk_max = pmax_ref[pl.program_id(0), j]
        return lax.cond(pq_min - pk_max >= MAX_DISTANCE,
                        lambda c: attend(j, c, False), lambda c: attend(j, c, True), carry)

    init = tuple((jnp.full((1, t), NEG, F32), jnp.zeros((DSA_V_ROWS, t), F32))
                 for _ in range(DSA_HEADS))
    carry = lax.fori_loop(0, i + 1, p3_body, init)
    outs = [acc[:DSA_HEAD_DIM, :] * (1.0 / acc[DSA_HEAD_DIM:DSA_HEAD_DIM + 1, :])
            for (_, acc) in carry]
    o_ref[...] = jnp.concatenate(outs, axis=0).T.astype(BF)


def _dsa(gq, gk, gd, ki2, vdt, pos_tiles, tab, B, S, t=ATT_TILE):
    nq = S // t
    n_sel = min(TOPK_MAX, S // 4)
    vrows = DSA_HEADS * DSA_V_ROWS
    qrow = lambda c: (lambda b, i: (b * nq + i, c))
    return pl.pallas_call(
        functools.partial(_dsa_kernel, t=t, n_sel=n_sel),
        grid=(B, nq),
        in_specs=[pl.BlockSpec((t, W_Q), qrow(0)),
                  pl.BlockSpec((t, LANES), qrow(1)),
                  pl.BlockSpec((t, DSA_WIDTH), qrow(0)),
                  pl.BlockSpec((None, nq, t), lambda b, i: (b, 0, 0)),
                  pl.BlockSpec((S, LANES), lambda b, i: (b, 0)),
                  pl.BlockSpec((S, DSA_WIDTH), lambda b, i: (b, 1)),
                  pl.BlockSpec((nq, vrows, t), lambda b, i: (b, 0, 0)),
                  pl.BlockSpec(memory_space=pltpu.SMEM), pl.BlockSpec(memory_space=pltpu.SMEM),
                  pl.BlockSpec(memory_space=pltpu.SMEM)],
        out_specs=pl.BlockSpec((t, DSA_WIDTH), qrow(0)),
        out_shape=jax.ShapeDtypeStruct((B * S, DSA_WIDTH), BF),
        scratch_shapes=[pltpu.VMEM((nq, t, t), F32), pltpu.VMEM((IDX_HEADS, t, LANES), BF)],
        compiler_params=_cparams(("parallel", "arbitrary")),
        name="dsa",
    )(gq, gk, gd, pos_tiles, ki2, gd, vdt, tab, jnp.min(pos_tiles, axis=-1), jnp.max(pos_tiles, axis=-1))


def _outmlp_kernel(h_ref, yh_ref, ym_ref, yd_ref, wo_ref, nw_ref, w1_ref, w2_ref, fw_ref, o_ref,
                   *, final, ff_chunk):
    mixed = jnp.concatenate([yh_ref[...], ym_ref[...], yd_ref[...]], axis=1)
    h = h_ref[...] + _dot(mixed, wo_ref[...])
    u = _rms(h, nw_ref[...]).astype(BF)
    out = h
    for c in range(D_FF // ff_chunk):
        cs = slice(c * ff_chunk, (c + 1) * ff_chunk)
        a = jnp.maximum(_dot(u, w1_ref[:, cs]), 0.0)
        out = out + _dot((a * a).astype(BF), w2_ref[cs, :])
    if final:
        out = _rms(out, fw_ref[...])
    o_ref[...] = out


def _outmlp(h, yh, ym, yd, wo, nw, w1, w2, fw, final, tm=ROW_TILE, ff_chunk=1024):
    T = h.shape[0]
    tm = min(tm, T)
    row = lambda i: (i, 0)
    return pl.pallas_call(
        functools.partial(_outmlp_kernel, final=final, ff_chunk=ff_chunk),
        grid=(T // tm,),
        in_specs=[pl.BlockSpec((tm, D_MODEL), row), pl.BlockSpec((tm, HGRN_W), row),
                  pl.BlockSpec((tm, MLA_HEADS * MLA_V), row), pl.BlockSpec((tm, DSA_WIDTH), row),
                  _const_spec((D_MODEL, D_MODEL)), _const_spec((1, D_MODEL)),
                  _const_spec((D_MODEL, D_FF)), _const_spec((D_FF, D_MODEL)),
                  _const_spec((1, D_MODEL))],
        out_specs=pl.BlockSpec((tm, D_MODEL), row),
        out_shape=jax.ShapeDtypeStruct((T, D_MODEL), F32),
        compiler_params=_cparams(("parallel",)),
        name="outmlp",
    )(h, yh, ym, yd, wo, nw, w1, w2, fw)


def _rot_cols(w):
    half = w.shape[-1] // 2
    return jnp.concatenate([-w[..., half:], w[..., :half]], axis=-1)


def _place(w, width, off):
    pad = [(0, 0)] * (w.ndim - 1) + [(off, width - off - w.shape[-1])]
    return jnp.pad(w, pad)


def _layout_w_in(w_in):
    sizes = (HGRN_W, HGRN_W, HGRN_W, HGRN_W, MLA_Q_LORA, MLA_KV_LORA + MLA_ROPE,
             DSA_WIDTH, DSA_WIDTH, DSA_WIDTH, IDX_HEADS * IDX_DIM, IDX_DIM, IDX_HEADS)
    offs = [0]
    for s in sizes:
        offs.append(offs[-1] + s)
    hq, hf, hi, hg, mqa, mkva, dq, dk, dv, iq, ik, iw = [
        w_in[..., offs[n]:offs[n + 1]] for n in range(len(sizes))]
    ckv, kpe = mkva[..., :MLA_KV_LORA], mkva[..., MLA_KV_LORA:]
    cols = [hq, hf, hi, hg,
            _place(mqa, 256, 0), ckv, _place(kpe, LANES, MLA_NOPE), _place(_rot_cols(kpe), LANES, MLA_NOPE),
            dq * (DSA_HEAD_DIM ** -0.5 * LOG2E), dk,
            iq,
            ik, ik, _place(iw, LANES, 0)]
    w_cat = jnp.concatenate(cols, axis=-1).astype(BF)
    L = w_in.shape[0]
    dvt = jnp.swapaxes(dv, 1, 2).reshape(L, DSA_HEADS, DSA_HEAD_DIM, D_MODEL)
    dvt = jnp.pad(dvt, ((0, 0), (0, 0), (0, DSA_V_ROWS - DSA_HEAD_DIM), (0, 0)))
    return w_cat, dvt.reshape(L, DSA_HEADS * DSA_V_ROWS, D_MODEL).astype(BF)


def _layout_mla(w_qb, w_kvb):
    L = w_qb.shape[0]
    dq = MLA_NOPE + MLA_ROPE
    wq = w_qb.reshape(L, MLA_Q_LORA, MLA_HEADS, dq)
    wq_rot = jnp.concatenate([jnp.zeros_like(wq[..., :MLA_NOPE]), _rot_cols(wq[..., MLA_NOPE:])], axis=-1)
    pad_q = lambda w: jnp.pad(w, ((0, 0), (0, 256 - MLA_Q_LORA), (0, 0), (0, HEAD_PAD - dq))).reshape(
        L, 256, MLA_HEADS * HEAD_PAD).astype(BF)
    wkv = w_kvb.reshape(L, MLA_KV_LORA, MLA_HEADS, MLA_NOPE + MLA_V)
    pad_kv = lambda w: jnp.pad(w, ((0, 0), (0, 0), (0, 0), (0, HEAD_PAD - w.shape[-1]))).reshape(
        L, MLA_KV_LORA, MLA_HEADS * HEAD_PAD).astype(BF)
    wvt = jnp.swapaxes(pad_kv(wkv[..., MLA_NOPE:]), 1, 2)
    return pad_q(wq), pad_q(wq_rot), pad_kv(wkv[..., :MLA_NOPE]), wvt


def kernel(x, positions, attn_norm_w, w_in, hgrn_lb_logits, hgrn_norm_w, mla_q_norm_w, mla_w_qb,
           mla_kv_norm_w, mla_w_kvb, idx_k_norm_w, idx_k_norm_b, rel_bias_table, w_out,
           mlp_norm_w, w_mlp_in, w_mlp_out, final_norm_w):
    B, S, _ = x.shape
    T = B * S
    depth = w_in.shape[0]

    inv_freq = 1.0 / (ROPE_THETA ** (jnp.arange(0, MLA_ROPE, 2, dtype=F32) / MLA_ROPE))
    ang = positions.astype(F32)[..., None] * inv_freq
    cos, sin = jnp.cos(ang).reshape(T, -1), jnp.sin(ang).reshape(T, -1)
    cos_t = jnp.concatenate([jnp.ones((T, MLA_NOPE), F32), cos, cos,
                             jnp.zeros((T, HEAD_PAD - MLA_NOPE - MLA_ROPE), F32)], axis=1)
    sin_t = jnp.concatenate([jnp.zeros((T, MLA_NOPE), F32), sin, sin,
                             jnp.zeros((T, HEAD_PAD - MLA_NOPE - MLA_ROPE), F32)], axis=1)
    pos_tiles = positions.reshape(B, S // ATT_TILE, ATT_TILE)

    lb = jnp.cumsum(jax.nn.softmax(hgrn_lb_logits.astype(F32), axis=0), axis=0)
    lb = lb - lb[0:1]
    lb3 = jnp.stack([jnp.log(lb), jnp.log1p(-lb), 1.0 - lb], axis=1)

    w_cat, w_dvt = _layout_w_in(w_in)
    wq, wqr, wk, wvt = _layout_mla(mla_w_qb, mla_w_kvb)
    qnw = jnp.pad(mla_q_norm_w, ((0, 0), (0, 256 - MLA_Q_LORA)))
    lnw = jnp.concatenate([idx_k_norm_w, idx_k_norm_w], axis=-1)
    lnb = jnp.concatenate([idx_k_norm_b, idx_k_norm_b], axis=-1)
    wo = w_out.astype(BF)
    w1 = w_mlp_in.astype(BF)
    w2 = w_mlp_out.astype(BF)
    tab = rel_bias_table.astype(F32)

    h = x.reshape(T, D_MODEL)
    for l in range(depth):
        gh, gm, gd, gq, gk, vdt = _inproj(h, attn_norm_w[l][None], w_cat[l], w_dvt[l])
        q_m, k_m, v_m, ki2 = _prep(gm, gk, cos_t, sin_t, qnw[l][None], wq[l], wqr[l],
                                   mla_kv_norm_w[l][None], wk[l], wvt[l], lnw[l][None], lnb[l][None])
        y_h = _hgrn(gh, lb3[l], hgrn_norm_w[l][None], B, S)
        y_m = _mla_attn(q_m, k_m, v_m, B, S)
        y_d = _dsa(gq, gk, gd, ki2, vdt, pos_tiles, tab, B, S)
        h = _outmlp(h, y_h, y_m, y_d, wo[l], mlp_norm_w[l][None], w1[l], w2[l],
                    final_norm_w[None], final=(l == depth - 1))
    return h.reshape(B, S, D_MODEL)
```

```python
import functools
import math

import jax
import jax.numpy as jnp
from jax import lax
from jax.experimental import pallas as pl
from jax.experimental.pallas import tpu as pltpu

D_MODEL = 1024
HGRN_HEADS = 4
HGRN_DK = 128
HGRN_DV = 128
HGRN_W = HGRN_HEADS * HGRN_DK
HGRN_CHUNK = 32
MLA_HEADS = 4
MLA_NOPE = 64
MLA_ROPE = 32
MLA_V = 64
MLA_Q_LORA = 192
MLA_KV_LORA = 128
ROPE_THETA = 10000.0
DSA_HEADS = 4
DSA_HEAD_DIM = 64
DSA_WIDTH = DSA_HEADS * DSA_HEAD_DIM
IDX_HEADS = 8
IDX_DIM = 64
TOPK_MAX = 256
NUM_BUCKETS = 32
MAX_EXACT = NUM_BUCKETS // 2
MAX_DISTANCE = 128
D_FF = 4 * D_MODEL
EPS = 1e-6

LANES = 128
SUBLANES = 8
BF16_ROWS = 16
HEAD_PAD = 128
NEG = -1e30
ATT_TILE = 256
MLA_KEY_TILE = 512
ROW_TILE = 512
VMEM_LIMIT = 52 * 1024 * 1024
LOG2E = math.log2(math.e)

DSA_V_ROWS = DSA_HEAD_DIM + BF16_ROWS

C_H = 0
C_M = C_H + 4 * HGRN_W
W_M = 640
C_D = C_M + W_M
W_D = 2 * DSA_WIDTH
C_Q = C_D + W_D
W_Q = IDX_HEADS * IDX_DIM
C_K = C_Q + W_Q
W_K = 256
N_PAD = C_K + W_K

BF = jnp.bfloat16
F32 = jnp.float32


def _dot(a, b):
    return jnp.dot(a, b, preferred_element_type=F32)


def _dot_nt(a, b):
    return lax.dot_general(a, b, (((1,), (1,)), ((), ())), preferred_element_type=F32)


def _rms(x, w, n=None):
    n = x.shape[-1] if n is None else n
    ms = jnp.sum(x * x, axis=-1, keepdims=True) * (1.0 / n)
    return x * lax.rsqrt(ms + EPS) * w


def _cparams(sem):
    return pltpu.CompilerParams(dimension_semantics=sem, vmem_limit_bytes=VMEM_LIMIT)


def _const_spec(shape):
    nd = len(shape)
    return pl.BlockSpec(shape, lambda *_: (0,) * nd, pipeline_mode=pl.Buffered(1))


def _inproj_kernel(x_ref, nw_ref, w_ref, wvt_ref, oh_ref, om_ref, od_ref, oq_ref, ok_ref, ovt_ref):
    u = _rms(x_ref[...], nw_ref[...]).astype(BF)
    oh_ref[...] = _dot(u, w_ref[:, C_H:C_M])
    om_ref[...] = _dot(u, w_ref[:, C_M:C_D])
    od_ref[...] = _dot(u, w_ref[:, C_D:C_Q]).astype(BF)
    oq_ref[...] = _dot(u, w_ref[:, C_Q:C_K]).astype(BF)
    ok_ref[...] = _dot(u, w_ref[:, C_K:N_PAD])
    vt = _dot_nt(wvt_ref[...], u)
    row = lax.broadcasted_iota(jnp.int32, vt.shape, 0)
    vt = jnp.where(row % DSA_V_ROWS == DSA_HEAD_DIM, 1.0, vt).astype(BF)
    for n in range(ovt_ref.shape[0]):
        ovt_ref[n] = vt[:, n * ATT_TILE:(n + 1) * ATT_TILE]


def _inproj(h, nw, w, wvt, tm=ROW_TILE):
    T = h.shape[0]
    tm = min(tm, T)
    row = lambda i: (i, 0)
    vrows = DSA_HEADS * DSA_V_ROWS
    per = tm // ATT_TILE
    return pl.pallas_call(
        _inproj_kernel,
        grid=(T // tm,),
        in_specs=[pl.BlockSpec((tm, D_MODEL), row), _const_spec((1, D_MODEL)),
                  _const_spec((D_MODEL, N_PAD)), _const_spec((vrows, D_MODEL))],
        out_specs=[pl.BlockSpec((tm, 4 * HGRN_W), row), pl.BlockSpec((tm, W_M), row),
                   pl.BlockSpec((tm, W_D), row), pl.BlockSpec((tm, W_Q), row),
                   pl.BlockSpec((tm, W_K), row),
                   pl.BlockSpec((per, vrows, ATT_TILE), lambda i: (i, 0, 0))],
        out_shape=[jax.ShapeDtypeStruct((T, 4 * HGRN_W), F32), jax.ShapeDtypeStruct((T, W_M), F32),
                   jax.ShapeDtypeStruct((T, W_D), BF), jax.ShapeDtypeStruct((T, W_Q), BF),
                   jax.ShapeDtypeStruct((T, W_K), F32),
                   jax.ShapeDtypeStruct((T // ATT_TILE, vrows, ATT_TILE), BF)],
        compiler_params=_cparams(("parallel",)),
        name="inproj",
    )(h, nw, w, wvt)


def _prep_kernel(gm_ref, gk_ref, cos_ref, sin_ref, qnw_ref, wq_ref, wqr_ref, kvnw_ref, wk_ref,
                 wvt_ref, lnw_ref, lnb_ref, q_ref, k_ref, vt_ref, ki_ref):
    cs = cos_ref[...]
    sn = sin_ref[...]
    cs4 = jnp.concatenate([cs] * MLA_HEADS, axis=1)
    sn4 = jnp.concatenate([sn] * MLA_HEADS, axis=1)
    qn = _rms(gm_ref[:, 0:256], qnw_ref[...], n=MLA_Q_LORA).astype(BF)
    scale = (MLA_NOPE + MLA_ROPE) ** -0.5 * LOG2E
    q = (_dot(qn, wq_ref[...]) * cs4 + _dot(qn, wqr_ref[...]) * sn4) * scale
    q_ref[...] = q.astype(BF)
    cn = _rms(gm_ref[:, 256:384], kvnw_ref[...]).astype(BF)
    kp = gm_ref[:, 384:512] * cs + gm_ref[:, 512:640] * sn
    k = _dot(cn, wk_ref[...]) + jnp.concatenate([kp] * MLA_HEADS, axis=1)
    k_ref[...] = k.astype(BF)
    vt = _dot_nt(wvt_ref[...], cn)
    row = lax.broadcasted_iota(jnp.int32, vt.shape, 0)
    vt_ref[...] = jnp.where(row % HEAD_PAD == MLA_V, 1.0, vt).astype(BF)
    x = gk_ref[:, 0:LANES]
    first = lax.broadcasted_iota(jnp.int32, x.shape, 1) < IDX_DIM
    mu = jnp.sum(jnp.where(first, x, 0.0), axis=-1, keepdims=True) * (1.0 / IDX_DIM)
    xc = x - mu
    var = jnp.sum(jnp.where(first, xc * xc, 0.0), axis=-1, keepdims=True) * (1.0 / IDX_DIM)
    ki_ref[...] = (xc * lax.rsqrt(var + EPS) * lnw_ref[...] + lnb_ref[...]).astype(BF)


def _prep(gm, gk, cos_t, sin_t, qnw, wq, wqr, kvnw, wk, wvt, lnw, lnb, tm=MLA_KEY_TILE):
    T = gm.shape[0]
    row = lambda i: (i, 0)
    hp = MLA_HEADS * HEAD_PAD
    return pl.pallas_call(
        _prep_kernel,
        grid=(T // tm,),
        in_specs=[pl.BlockSpec((tm, W_M), row), pl.BlockSpec((tm, W_K), row),
                  pl.BlockSpec((tm, LANES), row), pl.BlockSpec((tm, LANES), row),
                  _const_spec((1, 256)), _const_spec((256, hp)), _const_spec((256, hp)),
                  _const_spec((1, MLA_KV_LORA)), _const_spec((MLA_KV_LORA, hp)),
                  _const_spec((hp, MLA_KV_LORA)), _const_spec((1, LANES)), _const_spec((1, LANES))],
        out_specs=[pl.BlockSpec((tm, hp), row), pl.BlockSpec((tm, hp), row),
                   pl.BlockSpec((None, hp, tm), lambda i: (i, 0, 0)), pl.BlockSpec((tm, LANES), row)],
        out_shape=[jax.ShapeDtypeStruct((T, hp), BF), jax.ShapeDtypeStruct((T, hp), BF),
                   jax.ShapeDtypeStruct((T // tm, hp, tm), BF), jax.ShapeDtypeStruct((T, LANES), BF)],
        compiler_params=_cparams(("parallel",)),
        name="prep",
    )(gm, gk, cos_t, sin_t, qnw, wq, wqr, kvnw, wk, wvt, lnw, lnb)


def _split3(x):
    a = x.astype(BF)
    r = x - a.astype(F32)
    b = r.astype(BF)
    c = (r - b.astype(F32)).astype(BF)
    return a, b, c


def _hgrn_kernel(q_ref, f_ref, i_ref, g_ref, lb_ref, nw_ref, o_ref, st_ref, os_ref, *, tb):
    C = HGRN_CHUNK
    nc = tb // C

    @pl.when(pl.program_id(1) == 0)
    def _():
        st_ref[...] = jnp.zeros_like(st_ref)

    fp = f_ref[...]
    log_lb = lb_ref[0:1, :]
    log1m_lb = lb_ref[1:2, :]
    one_m_lb = lb_ref[2:3, :]
    ls = jnp.minimum(fp, 0.0) - jnp.log1p(jnp.exp(-jnp.abs(fp)))
    b = log1m_lb + ls
    log_f = jnp.maximum(log_lb, b) + jnp.log1p(jnp.exp(-jnp.abs(log_lb - b)))
    k = one_m_lb * (1.0 / (1.0 + jnp.exp(fp)))

    r = lax.broadcasted_iota(jnp.int32, (tb, tb), 0)
    c = lax.broadcasted_iota(jnp.int32, (tb, tb), 1)
    same = (r // C) == (c // C)
    tri = jnp.where(same & (c <= r), 1.0, 0.0).astype(BF)
    blk = jnp.where(same, 1.0, 0.0).astype(BF)
    a0, a1, a2 = _split3(log_f)
    G = _dot(tri, a0) + _dot(tri, a1) + _dot(tri, a2)
    G_last = _dot(blk, a0) + _dot(blk, a1) + _dot(blk, a2)

    eg = jnp.exp(G)
    q_dec = (q_ref[...] * (HGRN_DK ** -0.5) * eg).astype(BF)
    k_inv = (k * jnp.exp(-G)).astype(BF)
    k_state = k * jnp.exp(G_last - G)
    decay = jnp.exp(G_last)
    v = i_ref[...]
    vb = v.astype(BF)

    rr = lax.broadcasted_iota(jnp.int32, (C, C), 0)
    cc = lax.broadcasted_iota(jnp.int32, (C, C), 1)
    causal = cc <= rr

    heads = [slice(h * HGRN_DK, (h + 1) * HGRN_DK) for h in range(HGRN_HEADS)]
    sts = [st_ref[h] for h in range(HGRN_HEADS)]
    for n in range(nc):
        rs = slice(n * C, (n + 1) * C)
        qds = [q_dec[rs, hs] for hs in heads]
        As = [_dot_nt(qds[h], k_inv[rs, heads[h]]) for h in range(HGRN_HEADS)]
        kvs = [_dot(v[rs, heads[h]].T.astype(BF), k_state[rs, heads[h]].astype(BF))
               for h in range(HGRN_HEADS)]
        inter = [_dot_nt(qds[h], sts[h].astype(BF)) for h in range(HGRN_HEADS)]
        for h in range(HGRN_HEADS):
            A = jnp.where(causal, As[h], 0.0).astype(BF)
            os_ref[rs, heads[h]] = _dot(A, vb[rs, heads[h]]) + inter[h]
            sts[h] = decay[n * C:n * C + 1, heads[h]] * sts[h] + kvs[h]
    for h in range(HGRN_HEADS):
        st_ref[h] = sts[h]

    g = g_ref[...]
    gate = g * (1.0 / (1.0 + jnp.exp(-g)))
    for h in range(HGRN_HEADS):
        hs = slice(h * HGRN_DK, (h + 1) * HGRN_DK)
        o_ref[:, hs] = (_rms(os_ref[:, hs], nw_ref[...]) * gate[:, hs]).astype(BF)


def _hgrn(gh, lb3, nw, B, S, tb=256):
    nb = S // tb
    col = lambda j: (lambda b, i: (b * nb + i, j))
    return pl.pallas_call(
        functools.partial(_hgrn_kernel, tb=tb),
        grid=(B, nb),
        in_specs=[pl.BlockSpec((tb, HGRN_W), col(0)), pl.BlockSpec((tb, HGRN_W), col(1)),
                  pl.BlockSpec((tb, HGRN_W), col(2)), pl.BlockSpec((tb, HGRN_W), col(3)),
                  pl.BlockSpec((3, HGRN_W), lambda b, i: (0, 0)),
                  pl.BlockSpec((1, HGRN_DV), lambda b, i: (0, 0))],
        out_specs=pl.BlockSpec((tb, HGRN_W), col(0)),
        out_shape=jax.ShapeDtypeStruct((B * S, HGRN_W), BF),
        scratch_shapes=[pltpu.VMEM((HGRN_HEADS, HGRN_DV, HGRN_DK), F32),
                        pltpu.VMEM((tb, HGRN_W), F32)],
        compiler_params=_cparams(("parallel", "arbitrary")),
        name="hgrn",
    )(gh, gh, gh, gh, lb3, nw)


def _mla_kernel(q_ref, k_ref, vt_ref, o_ref, *, t, tk):
    i = pl.program_id(1)
    n_full = (i * t) // tk
    key = lax.broadcasted_iota(jnp.int32, (tk, t), 0)
    qry = lax.broadcasted_iota(jnp.int32, (tk, t), 1)
    causal = n_full * tk + key <= i * t + qry

    heads = [slice(h * HEAD_PAD, (h + 1) * HEAD_PAD) for h in range(MLA_HEADS)]

    def logits(j):
        ks = pl.ds(pl.multiple_of(j * tk, tk), tk)
        return tuple(_dot_nt(k_ref[ks, hs], q_ref[:, hs]) for hs in heads)

    def accumulate(j, ss, state, mask):
        if mask:
            ss = [jnp.where(causal, s, NEG) for s in ss]
        ms = [jnp.maximum(state[h][0], jnp.max(ss[h], axis=0, keepdims=True))
              for h in range(MLA_HEADS)]
        ps = [jnp.exp2(ss[h] - ms[h]).astype(BF) for h in range(MLA_HEADS)]
        return tuple((ms[h], jnp.exp2(state[h][0] - ms[h]) * state[h][1]
                      + _dot(vt_ref[j, heads[h], :], ps[h])) for h in range(MLA_HEADS))

    init = tuple((jnp.full((1, t), NEG, F32), jnp.zeros((HEAD_PAD, t), F32))
                 for _ in range(MLA_HEADS))
    state = lax.fori_loop(0, n_full, lambda j, c: accumulate(j, logits(j), c, False), init)
    state = accumulate(n_full, logits(n_full), state, True)
    outs = [acc[:MLA_V, :] * (1.0 / acc[MLA_V:MLA_V + 1, :]) for (_, acc) in state]
    o_ref[...] = jnp.concatenate(outs, axis=0).T.astype(BF)


def _mla_attn(q, k, vt, B, S, t=ATT_TILE, tk=MLA_KEY_TILE):
    nq = S // t
    nk = S // tk
    hp = MLA_HEADS * HEAD_PAD
    return pl.pallas_call(
        functools.partial(_mla_kernel, t=t, tk=tk),
        grid=(B, nq),
        in_specs=[pl.BlockSpec((t, hp), lambda b, i: (b * nq + i, 0)),
                  pl.BlockSpec((S, hp), lambda b, i: (b, 0)),
                  pl.BlockSpec((nk, hp, tk), lambda b, i: (b, 0, 0))],
        out_specs=pl.BlockSpec((t, MLA_HEADS * MLA_V), lambda b, i: (b * nq + i, 0)),
        out_shape=jax.ShapeDtypeStruct((B * S, MLA_HEADS * MLA_V), BF),
        compiler_params=_cparams(("parallel", "arbitrary")),
        name="mla_attn",
    )(q, k, vt)


BISECT_CAP = 48
_LAST_BUCKET_FROM = next(
    n for n in range(MAX_EXACT, 1 << 20)
    if MAX_EXACT + int(math.log(n / MAX_EXACT) / math.log(MAX_DISTANCE / MAX_EXACT)
                       * (NUM_BUCKETS - MAX_EXACT)) >= NUM_BUCKETS - 1)
assert _LAST_BUCKET_FROM <= LANES - 1 and _LAST_BUCKET_FROM <= MAX_DISTANCE


def _dsa_kernel(qi_ref, w_ref, qd_ref, pos_ref, ki_ref, kd_ref, vt_ref, tab_ref, pmin_ref, pmax_ref,
                o_ref, sc_ref, qm_ref, *, t, n_sel):
    i = pl.program_id(1)
    lane = lax.broadcasted_iota(jnp.int32, (t, LANES), 1)
    lo_half = lane < DSA_HEAD_DIM
    key = lax.broadcasted_iota(jnp.int32, (t, t), 0)
    qry = lax.broadcasted_iota(jnp.int32, (t, t), 1)
    causal = key <= qry
    k_sel = float(n_sel)

    def key_rows(j):
        return pl.ds(pl.multiple_of(j * t, t), t)

    def over_tiles(body, init):
        c = lax.fori_loop(0, (i + 1) // 2, lambda jj, c: body(2 * jj + 1, body(2 * jj, c)), init)
        return lax.cond((i + 1) % 2 == 1, lambda c: body(i, c), lambda c: c, c)

    def fold(x, op):
        return op(x.reshape(t // SUBLANES, SUBLANES, t), axis=0)

    zero_b = jnp.zeros((t, LANES), BF)
    for p in range(IDX_HEADS // 2):
        qp = qi_ref[:, p * LANES:(p + 1) * LANES]
        qm_ref[2 * p] = jnp.where(lo_half, qp, zero_b)
        qm_ref[2 * p + 1] = jnp.where(lo_half, zero_b, qp)
    wt = w_ref[...].T * ((IDX_HEADS ** -0.5) * (IDX_DIM ** -0.5))

    def score_tile(j):
        kt = ki_ref[key_rows(j), :]
        acc = None
        for h0 in range(0, IDX_HEADS, 4):
            ss = [_dot_nt(kt, qm_ref[h]) for h in range(h0, h0 + 4)]
            for n, s in enumerate(ss):
                term = jnp.maximum(s, 0.0) * wt[h0 + n:h0 + n + 1, :]
                acc = term if acc is None else acc + term
        return acc

    def add_stats(x, c):
        mx, mn, cp, cz = c
        return (jnp.maximum(mx, fold(x, jnp.max)),
                jnp.minimum(mn, fold(jnp.where(x == -jnp.inf, jnp.inf, x), jnp.min)),
                cp + fold(jnp.where(x > 0.0, 1.0, 0.0), jnp.sum),
                cz + fold(jnp.where(x == 0.0, 1.0, 0.0), jnp.sum))

    def p1_body(j, c):
        x = score_tile(j)
        sc_ref[j] = x
        return add_stats(x, c)

    z8 = jnp.zeros((SUBLANES, t), F32)
    stats = lax.fori_loop(0, i, p1_body, (z8 - jnp.inf, z8 + jnp.inf, z8, z8))
    x_diag = jnp.where(causal, score_tile(i), -jnp.inf)
    sc_ref[i] = x_diag
    mx, mn, cp, cz = add_stats(x_diag, stats)

    def total(c):
        return jnp.sum(c, axis=0, keepdims=True)

    def count_gt(thr):
        def body(j, c):
            return c + fold(jnp.where(sc_ref[j] > thr, 1.0, 0.0), jnp.sum)
        return total(over_tiles(body, jnp.zeros((SUBLANES, t), F32)))

    row_max = jnp.max(mx, axis=0, keepdims=True)
    row_min = jnp.min(mn, axis=0, keepdims=True)
    c_pos, c_zero = total(cp), total(cz)
    n_valid = (i * t + 1 + lax.broadcasted_iota(jnp.int32, (1, t), 1)).astype(F32)

    big = n_valid > k_sel
    pos_q = jnp.logical_and(big, c_pos >= k_sel)
    tie_q = jnp.logical_and(jnp.logical_and(big, c_pos < k_sel), c_pos + c_zero >= k_sel)
    neg_q = jnp.logical_and(big, c_pos + c_zero < k_sel)
    need = jnp.where(tie_q, k_sel - c_pos, 0.0)
    lo0 = jnp.where(jnp.logical_or(pos_q, tie_q), 0.0, -jnp.inf)
    hi0 = jnp.where(neg_q, 0.0, row_max)
    c_lo0 = jnp.where(pos_q, c_pos, jnp.where(tie_q, k_sel, n_valid))

    def open_queries(c_lo):
        return jnp.max(jnp.where(c_lo > k_sel, 1.0, 0.0))

    def bis_cond(c):
        return jnp.logical_and(c[0] < BISECT_CAP, c[1] > 0.0)

    def bisect_once(lo, hi, c_lo):
        active = c_lo > k_sel
        base = jnp.maximum(lo, row_min)
        mid = base + 0.5 * (hi - base)
        cnt = count_gt(mid)
        up = jnp.logical_and(active, cnt >= k_sel)
        dn = jnp.logical_and(active, cnt < k_sel)
        return jnp.where(up, mid, lo), jnp.where(dn, mid, hi), jnp.where(up, cnt, c_lo)

    def bis_body(c):
        it, _, lo, hi, c_lo = c
        lo, hi, c_lo = bisect_once(*bisect_once(lo, hi, c_lo))
        return it + 2, open_queries(c_lo), lo, hi, c_lo

    _, _, lo, _, _ = lax.while_loop(
        bis_cond, bis_body, (jnp.int32(0), open_queries(c_lo0), lo0, hi0, c_lo0))

    def mask_plain(_):
        def body(j, c):
            sc_ref[j] = jnp.where(sc_ref[j] > lo, 0.0, NEG)
            return c
        return lax.fori_loop(0, i + 1, body, 0)

    def mask_ties(_):
        lower = jnp.where(qry <= key, 1.0, 0.0).astype(BF)

        def body(j, seen):
            x = sc_ref[j]
            zero = x == 0.0
            zf = jnp.where(zero, 1.0, 0.0)
            rank = _dot(lower, zf.astype(BF)) + seen
            take = jnp.logical_and(zero, rank <= need)
            sc_ref[j] = jnp.where(jnp.logical_or(x > lo, take), 0.0, NEG)
            return seen + total(fold(zf, jnp.sum))
        lax.fori_loop(0, i + 1, body, jnp.zeros((1, t), F32))
        return 0

    lax.cond(jnp.max(need) > 0.0, mask_ties, mask_plain, 0)

    qh = []
    for p in range(DSA_HEADS // 2):
        qp = qd_ref[:, p * LANES:(p + 1) * LANES]
        qh.append(jnp.where(lo_half, qp, zero_b))
        qh.append(jnp.where(lo_half, zero_b, qp))
    pq = pos_ref[pl.ds(i, 1), :]
    pq_min = pmin_ref[pl.program_id(0), i]
    far_bias = [tab_ref[NUM_BUCKETS - 1, h] * LOG2E for h in range(DSA_HEADS)]
    log_ratio = math.log(MAX_DISTANCE / MAX_EXACT)

    dist = lax.broadcasted_iota(jnp.int32, (SUBLANES, LANES), 1)
    large = MAX_EXACT + (jnp.log(jnp.maximum(dist, 1).astype(F32) / MAX_EXACT) / log_ratio
                         * (NUM_BUCKETS - MAX_EXACT)).astype(jnp.int32)
    bucket = jnp.where(dist < MAX_EXACT, dist, jnp.minimum(large, NUM_BUCKETS - 1))
    by_dist = []
    for h in range(DSA_HEADS):
        bh = jnp.full((SUBLANES, LANES), tab_ref[0, h], F32)
        for jb in range(1, NUM_BUCKETS):
            bh = jnp.where(bucket >= jb, tab_ref[jb, h], bh)
        by_dist.append(jnp.concatenate([bh * LOG2E] * (t // SUBLANES), axis=0))

    def pair_bias(pk_row):
        pk = jnp.broadcast_to(pk_row, (SUBLANES, t)).T[:, 0:1]
        n = jnp.clip(pq - pk, 0, LANES - 1)
        return [jnp.concatenate([jnp.take_along_axis(by_dist[h], n[:, c * LANES:(c + 1) * LANES], axis=1)
                                 for c in range(t // LANES)], axis=1) for h in range(DSA_HEADS)]

    def attend(j, carry, near):
        madd = sc_ref[j]
        bias = pair_bias(pos_ref[pl.ds(j, 1), :]) if near else far_bias
        ks = key_rows(j)
        kps = [kd_ref[ks, p * LANES:(p + 1) * LANES] for p in range(DSA_HEADS // 2)]
        ss = [_dot_nt(kps[h // 2], qh[h]) + (madd + bias[h]) for h in range(DSA_HEADS)]
        ms = [jnp.maximum(carry[h][0], jnp.max(ss[h], axis=0, keepdims=True))
              for h in range(DSA_HEADS)]
        ps = [jnp.exp2(ss[h] - ms[h]).astype(BF) for h in range(DSA_HEADS)]
        return tuple((ms[h], jnp.exp2(carry[h][0] - ms[h]) * carry[h][1]
                      + _dot(vt_ref[j, h * DSA_V_ROWS:(h + 1) * DSA_V_ROWS, :], ps[h]))
                     for h in range(DSA_HEADS))

    def p3_body(j, carry):
        pk_max = pmax_ref[pl.program_id(0), j]
        return lax.cond(pq_min - pk_max >= MAX_DISTANCE,
                        lambda c: attend(j, c, False), lambda c: attend(j, c, True), carry)

    init = tuple((jnp.full((1, t), NEG, F32), jnp.zeros((DSA_V_ROWS, t), F32))
                 for _ in range(DSA_HEADS))
    carry = lax.fori_loop(0, i + 1, p3_body, init)
    outs = [acc[:DSA_HEAD_DIM, :] * (1.0 / acc[DSA_HEAD_DIM:DSA_HEAD_DIM + 1, :])
            for (_, acc) in carry]
    o_ref[...] = jnp.concatenate(outs, axis=0).T.astype(BF)


def _dsa(gq, gk, gd, ki2, vdt, pos_tiles, tab, B, S, t=ATT_TILE):
    nq = S // t
    n_sel = min(TOPK_MAX, S // 4)
    vrows = DSA_HEADS * DSA_V_ROWS
    qrow = lambda c: (lambda b, i: (b * nq + i, c))
    return pl.pallas_call(
        functools.partial(_dsa_kernel, t=t, n_sel=n_sel),
        grid=(B, nq),
        in_specs=[pl.BlockSpec((t, W_Q), qrow(0)),
                  pl.BlockSpec((t, LANES), qrow(1)),
                  pl.BlockSpec((t, DSA_WIDTH), qrow(0)),
                  pl.BlockSpec((None, nq, t), lambda b, i: (b, 0, 0)),
                  pl.BlockSpec((S, LANES), lambda b, i: (b, 0)),
                  pl.BlockSpec((S, DSA_WIDTH), lambda b, i: (b, 1)),
                  pl.BlockSpec((nq, vrows, t), lambda b, i: (b, 0, 0)),
                  pl.BlockSpec(memory_space=pltpu.SMEM), pl.BlockSpec(memory_space=pltpu.SMEM),
                  pl.BlockSpec(memory_space=pltpu.SMEM)],
        out_specs=pl.BlockSpec((t, DSA_WIDTH), qrow(0)),
        out_shape=jax.ShapeDtypeStruct((B * S, DSA_WIDTH), BF),
        scratch_shapes=[pltpu.VMEM((nq, t, t), F32), pltpu.VMEM((IDX_HEADS, t, LANES), BF)],
        compiler_params=_cparams(("parallel", "arbitrary")),
        name="dsa",
    )(gq, gk, gd, pos_tiles, ki2, gd, vdt, tab, jnp.min(pos_tiles, axis=-1), jnp.max(pos_tiles, axis=-1))


def _outmlp_kernel(h_ref, yh_ref, ym_ref, yd_ref, wo_ref, nw_ref, w1_ref, w2_ref, fw_ref, o_ref,
                   *, final, ff_chunk):
    mixed = jnp.concatenate([yh_ref[...], ym_ref[...], yd_ref[...]], axis=1)
    h = h_ref[...] + _dot(mixed, wo_ref[...])
    u = _rms(h, nw_ref[...]).astype(BF)
    out = h
    for c in range(D_FF // ff_chunk):
        cs = slice(c * ff_chunk, (c + 1) * ff_chunk)
        a = jnp.maximum(_dot(u, w1_ref[:, cs]), 0.0)
        out = out + _dot((a * a).astype(BF), w2_ref[cs, :])
    if final:
        out = _rms(out, fw_ref[...])
    o_ref[...] = out


def _outmlp(h, yh, ym, yd, wo, nw, w1, w2, fw, final, tm=ROW_TILE, ff_chunk=1024):
    T = h.shape[0]
    tm = min(tm, T)
    row = lambda i: (i, 0)
    return pl.pallas_call(
        functools.partial(_outmlp_kernel, final=final, ff_chunk=ff_chunk),
        grid=(T // tm,),
        in_specs=[pl.BlockSpec((tm, D_MODEL), row), pl.BlockSpec((tm, HGRN_W), row),
                  pl.BlockSpec((tm, MLA_HEADS * MLA_V), row), pl.BlockSpec((tm, DSA_WIDTH), row),
                  _const_spec((D_MODEL, D_MODEL)), _const_spec((1, D_MODEL)),
                  _const_spec((D_MODEL, D_FF)), _const_spec((D_FF, D_MODEL)),
                  _const_spec((1, D_MODEL))],
        out_specs=pl.BlockSpec((tm, D_MODEL), row),
        out_shape=jax.ShapeDtypeStruct((T, D_MODEL), F32),
        compiler_params=_cparams(("parallel",)),
        name="outmlp",
    )(h, yh, ym, yd, wo, nw, w1, w2, fw)


def _rot_cols(w):
    half = w.shape[-1] // 2
    return jnp.concatenate([-w[..., half:], w[..., :half]], axis=-1)


def _place(w, width, off):
    pad = [(0, 0)] * (w.ndim - 1) + [(off, width - off - w.shape[-1])]
    return jnp.pad(w, pad)


def _layout_w_in(w_in):
    sizes = (HGRN_W, HGRN_W, HGRN_W, HGRN_W, MLA_Q_LORA, MLA_KV_LORA + MLA_ROPE,
             DSA_WIDTH, DSA_WIDTH, DSA_WIDTH, IDX_HEADS * IDX_DIM, IDX_DIM, IDX_HEADS)
    offs = [0]
    for s in sizes:
        offs.append(offs[-1] + s)
    hq, hf, hi, hg, mqa, mkva, dq, dk, dv, iq, ik, iw = [
        w_in[..., offs[n]:offs[n + 1]] for n in range(len(sizes))]
    ckv, kpe = mkva[..., :MLA_KV_LORA], mkva[..., MLA_KV_LORA:]
    cols = [hq, hf, hi, hg,
            _place(mqa, 256, 0), ckv, _place(kpe, LANES, MLA_NOPE), _place(_rot_cols(kpe), LANES, MLA_NOPE),
            dq * (DSA_HEAD_DIM ** -0.5 * LOG2E), dk,
            iq,
            ik, ik, _place(iw, LANES, 0)]
    w_cat = jnp.concatenate(cols, axis=-1).astype(BF)
    L = w_in.shape[0]
    dvt = jnp.swapaxes(dv, 1, 2).reshape(L, DSA_HEADS, DSA_HEAD_DIM, D_MODEL)
    dvt = jnp.pad(dvt, ((0, 0), (0, 0), (0, DSA_V_ROWS - DSA_HEAD_DIM), (0, 0)))
    return w_cat, dvt.reshape(L, DSA_HEADS * DSA_V_ROWS, D_MODEL).astype(BF)


def _layout_mla(w_qb, w_kvb):
    L = w_qb.shape[0]
    dq = MLA_NOPE + MLA_ROPE
    wq = w_qb.reshape(L, MLA_Q_LORA, MLA_HEADS, dq)
    wq_rot = jnp.concatenate([jnp.zeros_like(wq[..., :MLA_NOPE]), _rot_cols(wq[..., MLA_NOPE:])], axis=-1)
    pad_q = lambda w: jnp.pad(w, ((0, 0), (0, 256 - MLA_Q_LORA), (0, 0), (0, HEAD_PAD - dq))).reshape(
        L, 256, MLA_HEADS * HEAD_PAD).astype(BF)
    wkv = w_kvb.reshape(L, MLA_KV_LORA, MLA_HEADS, MLA_NOPE + MLA_V)
    pad_kv = lambda w: jnp.pad(w, ((0, 0), (0, 0), (0, 0), (0, HEAD_PAD - w.shape[-1]))).reshape(
        L, MLA_KV_LORA, MLA_HEADS * HEAD_PAD).astype(BF)
    wvt = jnp.swapaxes(pad_kv(wkv[..., MLA_NOPE:]), 1, 2)
    return pad_q(wq), pad_q(wq_rot), pad_kv(wkv[..., :MLA_NOPE]), wvt


def kernel(x, positions, attn_norm_w, w_in, hgrn_lb_logits, hgrn_norm_w, mla_q_norm_w, mla_w_qb,
           mla_kv_norm_w, mla_w_kvb, idx_k_norm_w, idx_k_norm_b, rel_bias_table, w_out,
           mlp_norm_w, w_mlp_in, w_mlp_out, final_norm_w):
    B, S, _ = x.shape
    T = B * S
    depth = w_in.shape[0]

    inv_freq = 1.0 / (ROPE_THETA ** (jnp.arange(0, MLA_ROPE, 2, dtype=F32) / MLA_ROPE))
    ang = positions.astype(F32)[..., None] * inv_freq
    cos, sin = jnp.cos(ang).reshape(T, -1), jnp.sin(ang).reshape(T, -1)
    cos_t = jnp.concatenate([jnp.ones((T, MLA_NOPE), F32), cos, cos,
                             jnp.zeros((T, HEAD_PAD - MLA_NOPE - MLA_ROPE), F32)], axis=1)
    sin_t = jnp.concatenate([jnp.zeros((T, MLA_NOPE), F32), sin, sin,
                             jnp.zeros((T, HEAD_PAD - MLA_NOPE - MLA_ROPE), F32)], axis=1)
    pos_tiles = positions.reshape(B, S // ATT_TILE, ATT_TILE)

    lb = jnp.cumsum(jax.nn.softmax(hgrn_lb_logits.astype(F32), axis=0), axis=0)
    lb = lb - lb[0:1]
    lb3 = jnp.stack([jnp.log(lb), jnp.log1p(-lb), 1.0 - lb], axis=1)

    w_cat, w_dvt = _layout_w_in(w_in)
    wq, wqr, wk, wvt = _layout_mla(mla_w_qb, mla_w_kvb)
    qnw = jnp.pad(mla_q_norm_w, ((0, 0), (0, 256 - MLA_Q_LORA)))
    lnw = jnp.concatenate([idx_k_norm_w, idx_k_norm_w], axis=-1)
    lnb = jnp.concatenate([idx_k_norm_b, idx_k_norm_b], axis=-1)
    wo = w_out.astype(BF)
    w1 = w_mlp_in.astype(BF)
    w2 = w_mlp_out.astype(BF)
    tab = rel_bias_table.astype(F32)

    h = x.reshape(T, D_MODEL)
    for l in range(depth):
        gh, gm, gd, gq, gk, vdt = _inproj(h, attn_norm_w[l][None], w_cat[l], w_dvt[l])
        q_m, k_m, v_m, ki2 = _prep(gm, gk, cos_t, sin_t, qnw[l][None], wq[l], wqr[l],
                                   mla_kv_norm_w[l][None], wk[l], wvt[l], lnw[l][None], lnb[l][None])
        y_h = _hgrn(gh, lb3[l], hgrn_norm_w[l][None], B, S)
        y_m = _mla_attn(q_m, k_m, v_m, B, S)
        y_d = _dsa(gq, gk, gd, ki2, vdt, pos_tiles, tab, B, S)
        h = _outmlp(h, y_h, y_m, y_d, wo[l], mlp_norm_w[l][None], w1[l], w2[l],
                    final_norm_w[None], final=(l == depth - 1))
    return h.reshape(B, S, D_MODEL)
```

```python
import functools
import math

import jax
import jax.numpy as jnp
from jax import lax
from jax.experimental import pallas as pl
from jax.experimental.pallas import tpu as pltpu

D_MODEL = 1024
HGRN_HEADS = 4
HGRN_DK = 128
HGRN_DV = 128
HGRN_W = HGRN_HEADS * HGRN_DK
HGRN_CHUNK = 32
MLA_HEADS = 4
MLA_NOPE = 64
MLA_ROPE = 32
MLA_V = 64
MLA_Q_LORA = 192
MLA_KV_LORA = 128
ROPE_THETA = 10000.0
DSA_HEADS = 4
DSA_HEAD_DIM = 64
DSA_WIDTH = DSA_HEADS * DSA_HEAD_DIM
IDX_HEADS = 8
IDX_DIM = 64
TOPK_MAX = 256
NUM_BUCKETS = 32
MAX_EXACT = NUM_BUCKETS // 2
MAX_DISTANCE = 128
D_FF = 4 * D_MODEL
EPS = 1e-6

LANES = 128
SUBLANES = 8
BF16_ROWS = 16
HEAD_PAD = 128
NEG = -1e30
ATT_TILE = 256
MLA_KEY_TILE = 512
ROW_TILE = 512
VMEM_LIMIT = 52 * 1024 * 1024
LOG2E = math.log2(math.e)

DSA_V_ROWS = DSA_HEAD_DIM + BF16_ROWS

C_H = 0
C_M = C_H + 4 * HGRN_W
W_M = 640
C_D = C_M + W_M
W_D = 2 * DSA_WIDTH
C_Q = C_D + W_D
W_Q = IDX_HEADS * IDX_DIM
C_K = C_Q + W_Q
W_K = 256
N_PAD = C_K + W_K

BF = jnp.bfloat16
F32 = jnp.float32


def _dot(a, b):
    return jnp.dot(a, b, preferred_element_type=F32)


def _dot_nt(a, b):
    return lax.dot_general(a, b, (((1,), (1,)), ((), ())), preferred_element_type=F32)


def _rms(x, w, n=None):
    n = x.shape[-1] if n is None else n
    ms = jnp.sum(x * x, axis=-1, keepdims=True) * (1.0 / n)
    return x * lax.rsqrt(ms + EPS) * w


def _cparams(sem):
    return pltpu.CompilerParams(dimension_semantics=sem, vmem_limit_bytes=VMEM_LIMIT)


def _const_spec(shape):
    nd = len(shape)
    return pl.BlockSpec(shape, lambda *_: (0,) * nd, pipeline_mode=pl.Buffered(1))


def _inproj_kernel(x_ref, nw_ref, w_ref, wvt_ref, oh_ref, om_ref, od_ref, oq_ref, ok_ref, ovt_ref):
    u = _rms(x_ref[...], nw_ref[...]).astype(BF)
    oh_ref[...] = _dot(u, w_ref[:, C_H:C_M])
    om_ref[...] = _dot(u, w_ref[:, C_M:C_D])
    od_ref[...] = _dot(u, w_ref[:, C_D:C_Q]).astype(BF)
    oq_ref[...] = _dot(u, w_ref[:, C_Q:C_K]).astype(BF)
    ok_ref[...] = _dot(u, w_ref[:, C_K:N_PAD])
    vt = _dot_nt(wvt_ref[...], u)
    row = lax.broadcasted_iota(jnp.int32, vt.shape, 0)
    vt = jnp.where(row % DSA_V_ROWS == DSA_HEAD_DIM, 1.0, vt).astype(BF)
    for n in range(ovt_ref.shape[0]):
        ovt_ref[n] = vt[:, n * ATT_TILE:(n + 1) * ATT_TILE]


def _inproj(h, nw, w, wvt, tm=ROW_TILE):
    T = h.shape[0]
    tm = min(tm, T)
    row = lambda i: (i, 0)
    vrows = DSA_HEADS * DSA_V_ROWS
    per = tm // ATT_TILE
    return pl.pallas_call(
        _inproj_kernel,
        grid=(T // tm,),
        in_specs=[pl.BlockSpec((tm, D_MODEL), row), _const_spec((1, D_MODEL)),
                  _const_spec((D_MODEL, N_PAD)), _const_spec((vrows, D_MODEL))],
        out_specs=[pl.BlockSpec((tm, 4 * HGRN_W), row), pl.BlockSpec((tm, W_M), row),
                   pl.BlockSpec((tm, W_D), row), pl.BlockSpec((tm, W_Q), row),
                   pl.BlockSpec((tm, W_K), row),
                   pl.BlockSpec((per, vrows, ATT_TILE), lambda i: (i, 0, 0))],
        out_shape=[jax.ShapeDtypeStruct((T, 4 * HGRN_W), F32), jax.ShapeDtypeStruct((T, W_M), F32),
                   jax.ShapeDtypeStruct((T, W_D), BF), jax.ShapeDtypeStruct((T, W_Q), BF),
                   jax.ShapeDtypeStruct((T, W_K), F32),
                   jax.ShapeDtypeStruct((T // ATT_TILE, vrows, ATT_TILE), BF)],
        compiler_params=_cparams(("parallel",)),
        name="inproj",
    )(h, nw, w, wvt)


def _prep_kernel(gm_ref, gk_ref, cos_ref, sin_ref, qnw_ref, wq_ref, wqr_ref, kvnw_ref, wk_ref,
                 wvt_ref, lnw_ref, lnb_ref, q_ref, k_ref, vt_ref, ki_ref):
    cs = cos_ref[...]
    sn = sin_ref[...]
    cs4 = jnp.concatenate([cs] * MLA_HEADS, axis=1)
    sn4 = jnp.concatenate([sn] * MLA_HEADS, axis=1)
    qn = _rms(gm_ref[:, 0:256], qnw_ref[...], n=MLA_Q_LORA).astype(BF)
    scale = (MLA_NOPE + MLA_ROPE) ** -0.5 * LOG2E
    q = (_dot(qn, wq_ref[...]) * cs4 + _dot(qn, wqr_ref[...]) * sn4) * scale
    q_ref[...] = q.astype(BF)
    cn = _rms(gm_ref[:, 256:384], kvnw_ref[...]).astype(BF)
    kp = gm_ref[:, 384:512] * cs + gm_ref[:, 512:640] * sn
    k = _dot(cn, wk_ref[...]) + jnp.concatenate([kp] * MLA_HEADS, axis=1)
    k_ref[...] = k.astype(BF)
    vt = _dot_nt(wvt_ref[...], cn)
    row = lax.broadcasted_iota(jnp.int32, vt.shape, 0)
    vt_ref[...] = jnp.where(row % HEAD_PAD == MLA_V, 1.0, vt).astype(BF)
    x = gk_ref[:, 0:LANES]
    first = lax.broadcasted_iota(jnp.int32, x.shape, 1) < IDX_DIM
    mu = jnp.sum(jnp.where(first, x, 0.0), axis=-1, keepdims=True) * (1.0 / IDX_DIM)
    xc = x - mu
    var = jnp.sum(jnp.where(first, xc * xc, 0.0), axis=-1, keepdims=True) * (1.0 / IDX_DIM)
    ki_ref[...] = (xc * lax.rsqrt(var + EPS) * lnw_ref[...] + lnb_ref[...]).astype(BF)


def _prep(gm, gk, cos_t, sin_t, qnw, wq, wqr, kvnw, wk, wvt, lnw, lnb, tm=MLA_KEY_TILE):
    T = gm.shape[0]
    row = lambda i: (i, 0)
    hp = MLA_HEADS * HEAD_PAD
    return pl.pallas_call(
        _prep_kernel,
        grid=(T // tm,),
        in_specs=[pl.BlockSpec((tm, W_M), row), pl.BlockSpec((tm, W_K), row),
                  pl.BlockSpec((tm, LANES), row), pl.BlockSpec((tm, LANES), row),
                  _const_spec((1, 256)), _const_spec((256, hp)), _const_spec((256, hp)),
                  _const_spec((1, MLA_KV_LORA)), _const_spec((MLA_KV_LORA, hp)),
                  _const_spec((hp, MLA_KV_LORA)), _const_spec((1, LANES)), _const_spec((1, LANES))],
        out_specs=[pl.BlockSpec((tm, hp), row), pl.BlockSpec((tm, hp), row),
                   pl.BlockSpec((None, hp, tm), lambda i: (i, 0, 0)), pl.BlockSpec((tm, LANES), row)],
        out_shape=[jax.ShapeDtypeStruct((T, hp), BF), jax.ShapeDtypeStruct((T, hp), BF),
                   jax.ShapeDtypeStruct((T // tm, hp, tm), BF), jax.ShapeDtypeStruct((T, LANES), BF)],
        compiler_params=_cparams(("parallel",)),
        name="prep",
    )(gm, gk, cos_t, sin_t, qnw, wq, wqr, kvnw, wk, wvt, lnw, lnb)


def _split3(x):
    a = x.astype(BF)
    r = x - a.astype(F32)
    b = r.astype(BF)
    c = (r - b.astype(F32)).astype(BF)
    return a, b, c


def _hgrn_kernel(q_ref, f_ref, i_ref, g_ref, lb_ref, nw_ref, o_ref, st_ref, os_ref, *, tb):
    C = HGRN_CHUNK
    nc = tb // C

    @pl.when(pl.program_id(1) == 0)
    def _():
        st_ref[...] = jnp.zeros_like(st_ref)

    fp = f_ref[...]
    log_lb = lb_ref[0:1, :]
    log1m_lb = lb_ref[1:2, :]
    one_m_lb = lb_ref[2:3, :]
    ls = jnp.minimum(fp, 0.0) - jnp.log1p(jnp.exp(-jnp.abs(fp)))
    b = log1m_lb + ls
    log_f = jnp.maximum(log_lb, b) + jnp.log1p(jnp.exp(-jnp.abs(log_lb - b)))
    k = one_m_lb * (1.0 / (1.0 + jnp.exp(fp)))

    r = lax.broadcasted_iota(jnp.int32, (tb, tb), 0)
    c = lax.broadcasted_iota(jnp.int32, (tb, tb), 1)
    same = (r // C) == (c // C)
    tri = jnp.where(same & (c <= r), 1.0, 0.0).astype(BF)
    blk = jnp.where(same, 1.0, 0.0).astype(BF)
    a0, a1, a2 = _split3(log_f)
    G = _dot(tri, a0) + _dot(tri, a1) + _dot(tri, a2)
    G_last = _dot(blk, a0) + _dot(blk, a1) + _dot(blk, a2)

    eg = jnp.exp(G)
    q_dec = (q_ref[...] * (HGRN_DK ** -0.5) * eg).astype(BF)
    k_inv = (k * jnp.exp(-G)).astype(BF)
    k_state = k * jnp.exp(G_last - G)
    decay = jnp.exp(G_last)
    v = i_ref[...]
    vb = v.astype(BF)

    rr = lax.broadcasted_iota(jnp.int32, (C, C), 0)
    cc = lax.broadcasted_iota(jnp.int32, (C, C), 1)
    causal = cc <= rr

    heads = [slice(h * HGRN_DK, (h + 1) * HGRN_DK) for h in range(HGRN_HEADS)]
    sts = [st_ref[h] for h in range(HGRN_HEADS)]
    for n in range(nc):
        rs = slice(n * C, (n + 1) * C)
        qds = [q_dec[rs, hs] for hs in heads]
        As = [_dot_nt(qds[h], k_inv[rs, heads[h]]) for h in range(HGRN_HEADS)]
        kvs = [_dot(v[rs, heads[h]].T.astype(BF), k_state[rs, heads[h]].astype(BF))
               for h in range(HGRN_HEADS)]
        inter = [_dot_nt(qds[h], sts[h].astype(BF)) for h in range(HGRN_HEADS)]
        for h in range(HGRN_HEADS):
            A = jnp.where(causal, As[h], 0.0).astype(BF)
            os_ref[rs, heads[h]] = _dot(A, vb[rs, heads[h]]) + inter[h]
            sts[h] = decay[n * C:n * C + 1, heads[h]] * sts[h] + kvs[h]
    for h in range(HGRN_HEADS):
        st_ref[h] = sts[h]

    g = g_ref[...]
    gate = g * (1.0 / (1.0 + jnp.exp(-g)))
    for h in range(HGRN_HEADS):
        hs = slice(h * HGRN_DK, (h + 1) * HGRN_DK)
        o_ref[:, hs] = (_rms(os_ref[:, hs], nw_ref[...]) * gate[:, hs]).astype(BF)


def _hgrn(gh, lb3, nw, B, S, tb=256):
    nb = S // tb
    col = lambda j: (lambda b, i: (b * nb + i, j))
    return pl.pallas_call(
        functools.partial(_hgrn_kernel, tb=tb),
        grid=(B, nb),
        in_specs=[pl.BlockSpec((tb, HGRN_W), col(0)), pl.BlockSpec((tb, HGRN_W), col(1)),
                  pl.BlockSpec((tb, HGRN_W), col(2)), pl.BlockSpec((tb, HGRN_W), col(3)),
                  pl.BlockSpec((3, HGRN_W), lambda b, i: (0, 0)),
                  pl.BlockSpec((1, HGRN_DV), lambda b, i: (0, 0))],
        out_specs=pl.BlockSpec((tb, HGRN_W), col(0)),
        out_shape=jax.ShapeDtypeStruct((B * S, HGRN_W), BF),
        scratch_shapes=[pltpu.VMEM((HGRN_HEADS, HGRN_DV, HGRN_DK), F32),
                        pltpu.VMEM((tb, HGRN_W), F32)],
        compiler_params=_cparams(("parallel", "arbitrary")),
        name="hgrn",
    )(gh, gh, gh, gh, lb3, nw)


def _mla_kernel(q_ref, k_ref, vt_ref, o_ref, *, t, tk):
    i = pl.program_id(1)
    n_full = (i * t) // tk
    key = lax.broadcasted_iota(jnp.int32, (tk, t), 0)
    qry = lax.broadcasted_iota(jnp.int32, (tk, t), 1)
    causal = n_full * tk + key <= i * t + qry

    heads = [slice(h * HEAD_PAD, (h + 1) * HEAD_PAD) for h in range(MLA_HEADS)]

    def logits(j):
        ks = pl.ds(pl.multiple_of(j * tk, tk), tk)
        return tuple(_dot_nt(k_ref[ks, hs], q_ref[:, hs]) for hs in heads)

    def accumulate(j, ss, state, mask):
        if mask:
            ss = [jnp.where(causal, s, NEG) for s in ss]
        ms = [jnp.maximum(state[h][0], jnp.max(ss[h], axis=0, keepdims=True))
              for h in range(MLA_HEADS)]
        ps = [jnp.exp2(ss[h] - ms[h]).astype(BF) for h in range(MLA_HEADS)]
        return tuple((ms[h], jnp.exp2(state[h][0] - ms[h]) * state[h][1]
                      + _dot(vt_ref[j, heads[h], :], ps[h])) for h in range(MLA_HEADS))

    init = tuple((jnp.full((1, t), NEG, F32), jnp.zeros((HEAD_PAD, t), F32))
                 for _ in range(MLA_HEADS))
    state = lax.fori_loop(0, n_full, lambda j, c: accumulate(j, logits(j), c, False), init)
    state = accumulate(n_full, logits(n_full), state, True)
    outs = [acc[:MLA_V, :] * (1.0 / acc[MLA_V:MLA_V + 1, :]) for (_, acc) in state]
    o_ref[...] = jnp.concatenate(outs, axis=0).T.astype(BF)


def _mla_attn(q, k, vt, B, S, t=ATT_TILE, tk=MLA_KEY_TILE):
    nq = S // t
    nk = S // tk
    hp = MLA_HEADS * HEAD_PAD
    return pl.pallas_call(
        functools.partial(_mla_kernel, t=t, tk=tk),
        grid=(B, nq),
        in_specs=[pl.BlockSpec((t, hp), lambda b, i: (b * nq + i, 0)),
                  pl.BlockSpec((S, hp), lambda b, i: (b, 0)),
                  pl.BlockSpec((nk, hp, tk), lambda b, i: (b, 0, 0))],
        out_specs=pl.BlockSpec((t, MLA_HEADS * MLA_V), lambda b, i: (b * nq + i, 0)),
        out_shape=jax.ShapeDtypeStruct((B * S, MLA_HEADS * MLA_V), BF),
        compiler_params=_cparams(("parallel", "arbitrary")),
        name="mla_attn",
    )(q, k, vt)


BISECT_CAP = 48
_LAST_BUCKET_FROM = next(
    n for n in range(MAX_EXACT, 1 << 20)
    if MAX_EXACT + int(math.log(n / MAX_EXACT) / math.log(MAX_DISTANCE / MAX_EXACT)
                       * (NUM_BUCKETS - MAX_EXACT)) >= NUM_BUCKETS - 1)
assert _LAST_BUCKET_FROM <= LANES - 1 and _LAST_BUCKET_FROM <= MAX_DISTANCE


def _dsa_kernel(qi_ref, w_ref, qd_ref, pos_ref, ki_ref, kd_ref, vt_ref, tab_ref, pmin_ref, pmax_ref,
                o_ref, sc_ref, qm_ref, *, t, n_sel):
    i = pl.program_id(1)
    lane = lax.broadcasted_iota(jnp.int32, (t, LANES), 1)
    lo_half = lane < DSA_HEAD_DIM
    key = lax.broadcasted_iota(jnp.int32, (t, t), 0)
    qry = lax.broadcasted_iota(jnp.int32, (t, t), 1)
    causal = key <= qry
    k_sel = float(n_sel)

    def key_rows(j):
        return pl.ds(pl.multiple_of(j * t, t), t)

    @pl.when(i + 1 < sc_ref.shape[0])
    def _():
        sc_ref[i + 1] = jnp.full((t, t), -jnp.inf, F32)

    def over_tiles(body, init):
        return lax.fori_loop(0, (i + 2) // 2, lambda jj, c: body(2 * jj + 1, body(2 * jj, c)), init)

    def fold(x, op):
        return op(x.reshape(t // SUBLANES, SUBLANES, t), axis=0)

    zero_b = jnp.zeros((t, LANES), BF)
    for p in range(IDX_HEADS // 2):
        qp = qi_ref[:, p * LANES:(p + 1) * LANES]
        qm_ref[2 * p] = jnp.where(lo_half, qp, zero_b)
        qm_ref[2 * p + 1] = jnp.where(lo_half, zero_b, qp)
    wt = w_ref[...].T * ((IDX_HEADS ** -0.5) * (IDX_DIM ** -0.5))

    def score_tile(j):
        kt = ki_ref[key_rows(j), :]
        acc = None
        for h0 in range(0, IDX_HEADS, 4):
            ss = [_dot_nt(kt, qm_ref[h]) for h in range(h0, h0 + 4)]
            for n, s in enumerate(ss):
                term = jnp.maximum(s, 0.0) * wt[h0 + n:h0 + n + 1, :]
                acc = term if acc is None else acc + term
        return acc

    def add_stats(x, c):
        mx, mn, cp, cz = c
        return (jnp.maximum(mx, fold(x, jnp.max)),
                jnp.minimum(mn, fold(jnp.where(x == -jnp.inf, jnp.inf, x), jnp.min)),
                cp + fold(jnp.where(x > 0.0, 1.0, 0.0), jnp.sum),
                cz + fold(jnp.where(x == 0.0, 1.0, 0.0), jnp.sum))

    def p1_body(j, c):
        x = score_tile(j)
        sc_ref[j] = x
        return add_stats(x, c)

    z8 = jnp.zeros((SUBLANES, t), F32)
    stats = lax.fori_loop(0, i, p1_body, (z8 - jnp.inf, z8 + jnp.inf, z8, z8))
    x_diag = jnp.where(causal, score_tile(i), -jnp.inf)
    sc_ref[i] = x_diag
    mx, mn, cp, cz = add_stats(x_diag, stats)

    def total(c):
        return jnp.sum(c, axis=0, keepdims=True)

    def count_gt(thr):
        def body(j, c):
            return c + fold(jnp.where(sc_ref[j] > thr, 1.0, 0.0), jnp.sum)
        return total(over_tiles(body, jnp.zeros((SUBLANES, t), F32)))

    row_max = jnp.max(mx, axis=0, keepdims=True)
    row_min = jnp.min(mn, axis=0, keepdims=True)
    c_pos, c_zero = total(cp), total(cz)
    n_valid = (i * t + 1 + lax.broadcasted_iota(jnp.int32, (1, t), 1)).astype(F32)

    big = n_valid > k_sel
    pos_q = jnp.logical_and(big, c_pos >= k_sel)
    tie_q = jnp.logical_and(jnp.logical_and(big, c_pos < k_sel), c_pos + c_zero >= k_sel)
    neg_q = jnp.logical_and(big, c_pos + c_zero < k_sel)
    need = jnp.where(tie_q, k_sel - c_pos, 0.0)
    lo0 = jnp.where(jnp.logical_or(pos_q, tie_q), 0.0, -jnp.inf)
    hi0 = jnp.where(neg_q, 0.0, row_max)
    c_lo0 = jnp.where(pos_q, c_pos, jnp.where(tie_q, k_sel, n_valid))

    def open_queries(c_lo):
        return jnp.max(jnp.where(c_lo > k_sel, 1.0, 0.0))

    def bis_cond(c):
        return jnp.logical_and(c[0] < BISECT_CAP, c[1] > 0.0)

    def bisect_once(lo, hi, c_lo):
        active = c_lo > k_sel
        base = jnp.maximum(lo, row_min)
        mid = base + 0.5 * (hi - base)
        cnt = count_gt(mid)
        up = jnp.logical_and(active, cnt >= k_sel)
        dn = jnp.logical_and(active, cnt < k_sel)
        return jnp.where(up, mid, lo), jnp.where(dn, mid, hi), jnp.where(up, cnt, c_lo)

    def bis_body(c):
        it, _, lo, hi, c_lo = c
        lo, hi, c_lo = bisect_once(*bisect_once(lo, hi, c_lo))
        return it + 2, open_queries(c_lo), lo, hi, c_lo

    _, _, lo, _, c_lo = lax.while_loop(
        bis_cond, bis_body, (jnp.int32(0), open_queries(c_lo0), lo0, hi0, c_lo0))

    still_open = c_lo > k_sel

    def exact_kth(_):
        def to_float(kk):
            return lax.bitcast_convert_type(jnp.where(kk < 0, kk ^ jnp.int32(0x7FFFFFFF), kk), F32)

        def count_ge(thr):
            def body(j, c):
                return c + fold(jnp.where(sc_ref[j] >= thr, 1.0, 0.0), jnp.sum)
            return total(over_tiles(body, jnp.zeros((SUBLANES, t), F32)))

        def body(b, kth):
            cand = kth + lax.shift_left(jnp.int32(1), 31 - b)
            return jnp.where(count_ge(to_float(cand)) >= k_sel, cand, kth)

        kth = to_float(lax.fori_loop(0, 32, body, jnp.full((1, t), -2 ** 31, jnp.int32)))
        return kth, count_gt(kth)

    kth, above = lax.cond(jnp.max(jnp.where(still_open, 1.0, 0.0)) > 0.0, exact_kth,
                          lambda _: (jnp.zeros((1, t), F32), jnp.zeros((1, t), F32)), 0)
    lo = jnp.where(still_open, kth, lo)
    need = jnp.where(still_open, k_sel - above, need)
    tie_val = jnp.where(still_open, kth, 0.0)

    def mask_plain(_):
        def body(j, c):
            sc_ref[j] = jnp.where(sc_ref[j] > lo, 0.0, NEG)
            return c
        return lax.fori_loop(0, i + 1, body, 0)

    def mask_ties(_):
        lower = jnp.where(qry <= key, 1.0, 0.0).astype(BF)

        def body(j, seen):
            x = sc_ref[j]
            tied = x == tie_val
            tf = jnp.where(tied, 1.0, 0.0)
            rank = _dot(lower, tf.astype(BF)) + seen
            take = jnp.logical_and(tied, rank <= need)
            sc_ref[j] = jnp.where(jnp.logical_or(x > lo, take), 0.0, NEG)
            return seen + total(fold(tf, jnp.sum))
        lax.fori_loop(0, i + 1, body, jnp.zeros((1, t), F32))
        return 0

    lax.cond(jnp.max(need) > 0.0, mask_ties, mask_plain, 0)

    qh = []
    for p in range(DSA_HEADS // 2):
        qp = qd_ref[:, p * LANES:(p + 1) * LANES]
        qh.append(jnp.where(lo_half, qp, zero_b))
        qh.append(jnp.where(lo_half, zero_b, qp))
    pq = pos_ref[pl.ds(i, 1), :]
    pq_min = pmin_ref[pl.program_id(0), i]
    far_bias = [tab_ref[NUM_BUCKETS - 1, h] * LOG2E for h in range(DSA_HEADS)]
    log_ratio = math.log(MAX_DISTANCE / MAX_EXACT)

    dist = lax.broadcasted_iota(jnp.int32, (SUBLANES, LANES), 1)
    large = MAX_EXACT + (jnp.log(jnp.maximum(dist, 1).astype(F32) / MAX_EXACT) / log_ratio
                         * (NUM_BUCKETS - MAX_EXACT)).astype(jnp.int32)
    bucket = jnp.where(dist < MAX_EXACT, dist, jnp.minimum(large, NUM_BUCKETS - 1))
    by_dist = []
    for h in range(DSA_HEADS):
        bh = jnp.full((SUBLANES, LANES), tab_ref[0, h], F32)
        for jb in range(1, NUM_BUCKETS):
            bh = jnp.where(bucket >= jb, tab_ref[jb, h], bh)
        by_dist.append(jnp.concatenate([bh * LOG2E] * (t // SUBLANES), axis=0))

    def pair_bias(pk_row):
        pk = jnp.broadcast_to(pk_row, (SUBLANES, t)).T[:, 0:1]
        n = jnp.clip(pq - pk, 0, LANES - 1)
        return [jnp.concatenate([jnp.take_along_axis(by_dist[h], n[:, c * LANES:(c + 1) * LANES], axis=1)
                                 for c in range(t // LANES)], axis=1) for h in range(DSA_HEADS)]

    def attend(j, carry, near):
        madd = sc_ref[j]
        bias = pair_bias(pos_ref[pl.ds(j, 1), :]) if near else far_bias
        ks = key_rows(j)
        kps = [kd_ref[ks, p * LANES:(p + 1) * LANES] for p in range(DSA_HEADS // 2)]
        ss = [_dot_nt(kps[h // 2], qh[h]) + (madd + bias[h]) for h in range(DSA_HEADS)]
        ms = [jnp.maximum(carry[h][0], jnp.max(ss[h], axis=0, keepdims=True))
              for h in range(DSA_HEADS)]
        ps = [jnp.exp2(ss[h] - ms[h]).astype(BF) for h in range(DSA_HEADS)]
        return tuple((ms[h], jnp.exp2(carry[h][0] - ms[h]) * carry[h][1]
                      + _dot(vt_ref[j, h * DSA_V_ROWS:(h + 1) * DSA_V_ROWS, :], ps[h]))
                     for h in range(DSA_HEADS))

    def p3_body(j, carry):
        pk_max = pmax_ref[pl.program_id(0), j]
        return lax.cond(pq_min - pk_max >= MAX_DISTANCE,
                        lambda c: attend(j, c, False), lambda c: attend(j, c, True), carry)

    init = tuple((jnp.full((1, t), NEG, F32), jnp.zeros((DSA_V_ROWS, t), F32))
                 for _ in range(DSA_HEADS))
    carry = lax.fori_loop(0, i + 1, p3_body, init)
    outs = [acc[:DSA_HEAD_DIM, :] * (1.0 / acc[DSA_HEAD_DIM:DSA_HEAD_DIM + 1, :])
            for (_, acc) in carry]
    o_ref[...] = jnp.concatenate(outs, axis=0).T.astype(BF)


def _dsa(gq, gk, gd, ki2, vdt, pos_tiles, tab, B, S, t=ATT_TILE):
    nq = S // t
    n_sel = min(TOPK_MAX, S // 4)
    vrows = DSA_HEADS * DSA_V_ROWS
    qrow = lambda c: (lambda b, i: (b * nq + i, c))
    return pl.pallas_call(
        functools.partial(_dsa_kernel, t=t, n_sel=n_sel),
        grid=(B, nq),
        in_specs=[pl.BlockSpec((t, W_Q), qrow(0)),
                  pl.BlockSpec((t, LANES), qrow(1)),
                  pl.BlockSpec((t, DSA_WIDTH), qrow(0)),
                  pl.BlockSpec((None, nq, t), lambda b, i: (b, 0, 0)),
                  pl.BlockSpec((S, LANES), lambda b, i: (b, 0)),
                  pl.BlockSpec((S, DSA_WIDTH), lambda b, i: (b, 1)),
                  pl.BlockSpec((nq, vrows, t), lambda b, i: (b, 0, 0)),
                  pl.BlockSpec(memory_space=pltpu.SMEM), pl.BlockSpec(memory_space=pltpu.SMEM),
                  pl.BlockSpec(memory_space=pltpu.SMEM)],
        out_specs=pl.BlockSpec((t, DSA_WIDTH), qrow(0)),
        out_shape=jax.ShapeDtypeStruct((B * S, DSA_WIDTH), BF),
        scratch_shapes=[pltpu.VMEM((nq, t, t), F32), pltpu.VMEM((IDX_HEADS, t, LANES), BF)],
        compiler_params=_cparams(("parallel", "arbitrary")),
        name="dsa",
    )(gq, gk, gd, pos_tiles, ki2, gd, vdt, tab, jnp.min(pos_tiles, axis=-1), jnp.max(pos_tiles, axis=-1))


def _outmlp_kernel(h_ref, yh_ref, ym_ref, yd_ref, wo_ref, nw_ref, w1_ref, w2_ref, fw_ref, o_ref,
                   *, final, ff_chunk):
    mixed = jnp.concatenate([yh_ref[...], ym_ref[...], yd_ref[...]], axis=1)
    h = h_ref[...] + _dot(mixed, wo_ref[...])
    u = _rms(h, nw_ref[...]).astype(BF)
    out = h
    for c in range(D_FF // ff_chunk):
        cs = slice(c * ff_chunk, (c + 1) * ff_chunk)
        a = jnp.maximum(_dot(u, w1_ref[:, cs]), 0.0)
        out = out + _dot((a * a).astype(BF), w2_ref[cs, :])
    if final:
        out = _rms(out, fw_ref[...])
    o_ref[...] = out


def _outmlp(h, yh, ym, yd, wo, nw, w1, w2, fw, final, tm=ROW_TILE, ff_chunk=1024):
    T = h.shape[0]
    tm = min(tm, T)
    row = lambda i: (i, 0)
    return pl.pallas_call(
        functools.partial(_outmlp_kernel, final=final, ff_chunk=ff_chunk),
        grid=(T // tm,),
        in_specs=[pl.BlockSpec((tm, D_MODEL), row), pl.BlockSpec((tm, HGRN_W), row),
                  pl.BlockSpec((tm, MLA_HEADS * MLA_V), row), pl.BlockSpec((tm, DSA_WIDTH), row),
                  _const_spec((D_MODEL, D_MODEL)), _const_spec((1, D_MODEL)),
                  _const_spec((D_MODEL, D_FF)), _const_spec((D_FF, D_MODEL)),
                  _const_spec((1, D_MODEL))],
        out_specs=pl.BlockSpec((tm, D_MODEL), row),
        out_shape=jax.ShapeDtypeStruct((T, D_MODEL), F32),
        compiler_params=_cparams(("parallel",)),
        name="outmlp",
    )(h, yh, ym, yd, wo, nw, w1, w2, fw)


def _rot_cols(w):
    half = w.shape[-1] // 2
    return jnp.concatenate([-w[..., half:], w[..., :half]], axis=-1)


def _place(w, width, off):
    pad = [(0, 0)] * (w.ndim - 1) + [(off, width - off - w.shape[-1])]
    return jnp.pad(w, pad)


def _layout_w_in(w_in):
    sizes = (HGRN_W, HGRN_W, HGRN_W, HGRN_W, MLA_Q_LORA, MLA_KV_LORA + MLA_ROPE,
             DSA_WIDTH, DSA_WIDTH, DSA_WIDTH, IDX_HEADS * IDX_DIM, IDX_DIM, IDX_HEADS)
    offs = [0]
    for s in sizes:
        offs.append(offs[-1] + s)
    hq, hf, hi, hg, mqa, mkva, dq, dk, dv, iq, ik, iw = [
        w_in[..., offs[n]:offs[n + 1]] for n in range(len(sizes))]
    ckv, kpe = mkva[..., :MLA_KV_LORA], mkva[..., MLA_KV_LORA:]
    cols = [hq, hf, hi, hg,
            _place(mqa, 256, 0), ckv, _place(kpe, LANES, MLA_NOPE), _place(_rot_cols(kpe), LANES, MLA_NOPE),
            dq * (DSA_HEAD_DIM ** -0.5 * LOG2E), dk,
            iq,
            ik, ik, _place(iw, LANES, 0)]
    w_cat = jnp.concatenate(cols, axis=-1).astype(BF)
    L = w_in.shape[0]
    dvt = jnp.swapaxes(dv, 1, 2).reshape(L, DSA_HEADS, DSA_HEAD_DIM, D_MODEL)
    dvt = jnp.pad(dvt, ((0, 0), (0, 0), (0, DSA_V_ROWS - DSA_HEAD_DIM), (0, 0)))
    return w_cat, dvt.reshape(L, DSA_HEADS * DSA_V_ROWS, D_MODEL).astype(BF)


def _layout_mla(w_qb, w_kvb):
    L = w_qb.shape[0]
    dq = MLA_NOPE + MLA_ROPE
    wq = w_qb.reshape(L, MLA_Q_LORA, MLA_HEADS, dq)
    wq_rot = jnp.concatenate([jnp.zeros_like(wq[..., :MLA_NOPE]), _rot_cols(wq[..., MLA_NOPE:])], axis=-1)
    pad_q = lambda w: jnp.pad(w, ((0, 0), (0, 256 - MLA_Q_LORA), (0, 0), (0, HEAD_PAD - dq))).reshape(
        L, 256, MLA_HEADS * HEAD_PAD).astype(BF)
    wkv = w_kvb.reshape(L, MLA_KV_LORA, MLA_HEADS, MLA_NOPE + MLA_V)
    pad_kv = lambda w: jnp.pad(w, ((0, 0), (0, 0), (0, 0), (0, HEAD_PAD - w.shape[-1]))).reshape(
        L, MLA_KV_LORA, MLA_HEADS * HEAD_PAD).astype(BF)
    wvt = jnp.swapaxes(pad_kv(wkv[..., MLA_NOPE:]), 1, 2)
    return pad_q(wq), pad_q(wq_rot), pad_kv(wkv[..., :MLA_NOPE]), wvt


def kernel(x, positions, attn_norm_w, w_in, hgrn_lb_logits, hgrn_norm_w, mla_q_norm_w, mla_w_qb,
           mla_kv_norm_w, mla_w_kvb, idx_k_norm_w, idx_k_norm_b, rel_bias_table, w_out,
           mlp_norm_w, w_mlp_in, w_mlp_out, final_norm_w):
    B, S, _ = x.shape
    T = B * S
    depth = w_in.shape[0]

    inv_freq = 1.0 / (ROPE_THETA ** (jnp.arange(0, MLA_ROPE, 2, dtype=F32) / MLA_ROPE))
    ang = positions.astype(F32)[..., None] * inv_freq
    cos, sin = jnp.cos(ang).reshape(T, -1), jnp.sin(ang).reshape(T, -1)
    cos_t = jnp.concatenate([jnp.ones((T, MLA_NOPE), F32), cos, cos,
                             jnp.zeros((T, HEAD_PAD - MLA_NOPE - MLA_ROPE), F32)], axis=1)
    sin_t = jnp.concatenate([jnp.zeros((T, MLA_NOPE), F32), sin, sin,
                             jnp.zeros((T, HEAD_PAD - MLA_NOPE - MLA_ROPE), F32)], axis=1)
    pos_tiles = positions.reshape(B, S // ATT_TILE, ATT_TILE)

    lb = jnp.cumsum(jax.nn.softmax(hgrn_lb_logits.astype(F32), axis=0), axis=0)
    lb = lb - lb[0:1]
    lb3 = jnp.stack([jnp.log(lb), jnp.log1p(-lb), 1.0 - lb], axis=1)

    w_cat, w_dvt = _layout_w_in(w_in)
    wq, wqr, wk, wvt = _layout_mla(mla_w_qb, mla_w_kvb)
    qnw = jnp.pad(mla_q_norm_w, ((0, 0), (0, 256 - MLA_Q_LORA)))
    lnw = jnp.concatenate([idx_k_norm_w, idx_k_norm_w], axis=-1)
    lnb = jnp.concatenate([idx_k_norm_b, idx_k_norm_b], axis=-1)
    wo = w_out.astype(BF)
    w1 = w_mlp_in.astype(BF)
    w2 = w_mlp_out.astype(BF)
    tab = rel_bias_table.astype(F32)

    h = x.reshape(T, D_MODEL)
    for l in range(depth):
        gh, gm, gd, gq, gk, vdt = _inproj(h, attn_norm_w[l][None], w_cat[l], w_dvt[l])
        q_m, k_m, v_m, ki2 = _prep(gm, gk, cos_t, sin_t, qnw[l][None], wq[l], wqr[l],
                                   mla_kv_norm_w[l][None], wk[l], wvt[l], lnw[l][None], lnb[l][None])
        y_h = _hgrn(gh, lb3[l], hgrn_norm_w[l][None], B, S)
        y_m = _mla_attn(q_m, k_m, v_m, B, S)
        y_d = _dsa(gq, gk, gd, ki2, vdt, pos_tiles, tab, B, S)
        h = _outmlp(h, y_h, y_m, y_d, wo[l], mlp_norm_w[l][None], w1[l], w2[l],
                    final_norm_w[None], final=(l == depth - 1))
    return h.reshape(B, S, D_MODEL)
```

```python
import functools
import math

import jax
import jax.numpy as jnp
from jax import lax
from jax.experimental import pallas as pl
from jax.experimental.pallas import tpu as pltpu

D_MODEL = 1024
HGRN_HEADS = 4
HGRN_DK = 128
HGRN_DV = 128
HGRN_W = HGRN_HEADS * HGRN_DK
HGRN_CHUNK = 32
MLA_HEADS = 4
MLA_NOPE = 64
MLA_ROPE = 32
MLA_V = 64
MLA_Q_LORA = 192
MLA_KV_LORA = 128
ROPE_THETA = 10000.0
DSA_HEADS = 4
DSA_HEAD_DIM = 64
DSA_WIDTH = DSA_HEADS * DSA_HEAD_DIM
IDX_HEADS = 8
IDX_DIM = 64
TOPK_MAX = 256
NUM_BUCKETS = 32
MAX_EXACT = NUM_BUCKETS // 2
MAX_DISTANCE = 128
D_FF = 4 * D_MODEL
EPS = 1e-6

LANES = 128
SUBLANES = 8
BF16_ROWS = 16
HEAD_PAD = 128
NEG = -1e30
ATT_TILE = 256
MLA_KEY_TILE = 512
ROW_TILE = 512
VMEM_LIMIT = 52 * 1024 * 1024
LOG2E = math.log2(math.e)

DSA_V_ROWS = DSA_HEAD_DIM + BF16_ROWS

C_H = 0
C_M = C_H + 4 * HGRN_W
W_M = 640
C_D = C_M + W_M
W_D = 2 * DSA_WIDTH
C_Q = C_D + W_D
W_Q = IDX_HEADS * IDX_DIM
C_K = C_Q + W_Q
W_K = 256
N_PAD = C_K + W_K

BF = jnp.bfloat16
F32 = jnp.float32


def _dot(a, b):
    return jnp.dot(a, b, preferred_element_type=F32)


def _dot_nt(a, b):
    return lax.dot_general(a, b, (((1,), (1,)), ((), ())), preferred_element_type=F32)


def _rms(x, w, n=None):
    n = x.shape[-1] if n is None else n
    ms = jnp.sum(x * x, axis=-1, keepdims=True) * (1.0 / n)
    return x * lax.rsqrt(ms + EPS) * w


def _cparams(sem):
    return pltpu.CompilerParams(dimension_semantics=sem, vmem_limit_bytes=VMEM_LIMIT)


def _const_spec(shape):
    nd = len(shape)
    return pl.BlockSpec(shape, lambda *_: (0,) * nd, pipeline_mode=pl.Buffered(1))


def _inproj_kernel(x_ref, nw_ref, w_ref, wvt_ref, oh_ref, om_ref, od_ref, oq_ref, ok_ref, ovt_ref):
    u = _rms(x_ref[...], nw_ref[...]).astype(BF)
    oh_ref[...] = _dot(u, w_ref[:, C_H:C_M])
    om_ref[...] = _dot(u, w_ref[:, C_M:C_D])
    od_ref[...] = _dot(u, w_ref[:, C_D:C_Q]).astype(BF)
    oq_ref[...] = _dot(u, w_ref[:, C_Q:C_K]).astype(BF)
    ok_ref[...] = _dot(u, w_ref[:, C_K:N_PAD])
    vt = _dot_nt(wvt_ref[...], u)
    row = lax.broadcasted_iota(jnp.int32, vt.shape, 0)
    vt = jnp.where(row % DSA_V_ROWS == DSA_HEAD_DIM, 1.0, vt).astype(BF)
    for n in range(ovt_ref.shape[0]):
        ovt_ref[n] = vt[:, n * ATT_TILE:(n + 1) * ATT_TILE]


def _inproj(h, nw, w, wvt, tm=ROW_TILE):
    T = h.shape[0]
    tm = min(tm, T)
    row = lambda i: (i, 0)
    vrows = DSA_HEADS * DSA_V_ROWS
    per = tm // ATT_TILE
    return pl.pallas_call(
        _inproj_kernel,
        grid=(T // tm,),
        in_specs=[pl.BlockSpec((tm, D_MODEL), row), _const_spec((1, D_MODEL)),
                  _const_spec((D_MODEL, N_PAD)), _const_spec((vrows, D_MODEL))],
        out_specs=[pl.BlockSpec((tm, 4 * HGRN_W), row), pl.BlockSpec((tm, W_M), row),
                   pl.BlockSpec((tm, W_D), row), pl.BlockSpec((tm, W_Q), row),
                   pl.BlockSpec((tm, W_K), row),
                   pl.BlockSpec((per, vrows, ATT_TILE), lambda i: (i, 0, 0))],
        out_shape=[jax.ShapeDtypeStruct((T, 4 * HGRN_W), F32), jax.ShapeDtypeStruct((T, W_M), F32),
                   jax.ShapeDtypeStruct((T, W_D), BF), jax.ShapeDtypeStruct((T, W_Q), BF),
                   jax.ShapeDtypeStruct((T, W_K), F32),
                   jax.ShapeDtypeStruct((T // ATT_TILE, vrows, ATT_TILE), BF)],
        compiler_params=_cparams(("parallel",)),
        name="inproj",
    )(h, nw, w, wvt)


def _prep_kernel(gm_ref, gk_ref, cos_ref, sin_ref, qnw_ref, wq_ref, wqr_ref, kvnw_ref, wk_ref,
                 wvt_ref, lnw_ref, lnb_ref, q_ref, k_ref, vt_ref, ki_ref):
    cs = cos_ref[...]
    sn = sin_ref[...]
    cs4 = jnp.concatenate([cs] * MLA_HEADS, axis=1)
    sn4 = jnp.concatenate([sn] * MLA_HEADS, axis=1)
    qn = _rms(gm_ref[:, 0:256], qnw_ref[...], n=MLA_Q_LORA).astype(BF)
    scale = (MLA_NOPE + MLA_ROPE) ** -0.5 * LOG2E
    q = (_dot(qn, wq_ref[...]) * cs4 + _dot(qn, wqr_ref[...]) * sn4) * scale
    q_ref[...] = q.astype(BF)
    cn = _rms(gm_ref[:, 256:384], kvnw_ref[...]).astype(BF)
    kp = gm_ref[:, 384:512] * cs + gm_ref[:, 512:640] * sn
    k = _dot(cn, wk_ref[...]) + jnp.concatenate([kp] * MLA_HEADS, axis=1)
    k_ref[...] = k.astype(BF)
    vt = _dot_nt(wvt_ref[...], cn)
    row = lax.broadcasted_iota(jnp.int32, vt.shape, 0)
    vt_ref[...] = jnp.where(row % HEAD_PAD == MLA_V, 1.0, vt).astype(BF)
    x = gk_ref[:, 0:LANES]
    first = lax.broadcasted_iota(jnp.int32, x.shape, 1) < IDX_DIM
    mu = jnp.sum(jnp.where(first, x, 0.0), axis=-1, keepdims=True) * (1.0 / IDX_DIM)
    xc = x - mu
    var = jnp.sum(jnp.where(first, xc * xc, 0.0), axis=-1, keepdims=True) * (1.0 / IDX_DIM)
    ki_ref[...] = (xc * lax.rsqrt(var + EPS) * lnw_ref[...] + lnb_ref[...]).astype(BF)


def _prep(gm, gk, cos_t, sin_t, qnw, wq, wqr, kvnw, wk, wvt, lnw, lnb, tm=MLA_KEY_TILE):
    T = gm.shape[0]
    row = lambda i: (i, 0)
    hp = MLA_HEADS * HEAD_PAD
    return pl.pallas_call(
        _prep_kernel,
        grid=(T // tm,),
        in_specs=[pl.BlockSpec((tm, W_M), row), pl.BlockSpec((tm, W_K), row),
                  pl.BlockSpec((tm, LANES), row), pl.BlockSpec((tm, LANES), row),
                  _const_spec((1, 256)), _const_spec((256, hp)), _const_spec((256, hp)),
                  _const_spec((1, MLA_KV_LORA)), _const_spec((MLA_KV_LORA, hp)),
                  _const_spec((hp, MLA_KV_LORA)), _const_spec((1, LANES)), _const_spec((1, LANES))],
        out_specs=[pl.BlockSpec((tm, hp), row), pl.BlockSpec((tm, hp), row),
                   pl.BlockSpec((None, hp, tm), lambda i: (i, 0, 0)), pl.BlockSpec((tm, LANES), row)],
        out_shape=[jax.ShapeDtypeStruct((T, hp), BF), jax.ShapeDtypeStruct((T, hp), BF),
                   jax.ShapeDtypeStruct((T // tm, hp, tm), BF), jax.ShapeDtypeStruct((T, LANES), BF)],
        compiler_params=_cparams(("parallel",)),
        name="prep",
    )(gm, gk, cos_t, sin_t, qnw, wq, wqr, kvnw, wk, wvt, lnw, lnb)


def _split3(x):
    a = x.astype(BF)
    r = x - a.astype(F32)
    b = r.astype(BF)
    c = (r - b.astype(F32)).astype(BF)
    return a, b, c


def _hgrn_kernel(q_ref, f_ref, i_ref, g_ref, lb_ref, nw_ref, o_ref, st_ref, os_ref, *, tb):
    C = HGRN_CHUNK
    nc = tb // C

    @pl.when(pl.program_id(1) == 0)
    def _():
        st_ref[...] = jnp.zeros_like(st_ref)

    fp = f_ref[...]
    log_lb = lb_ref[0:1, :]
    log1m_lb = lb_ref[1:2, :]
    one_m_lb = lb_ref[2:3, :]
    ls = jnp.minimum(fp, 0.0) - jnp.log1p(jnp.exp(-jnp.abs(fp)))
    b = log1m_lb + ls
    log_f = jnp.maximum(log_lb, b) + jnp.log1p(jnp.exp(-jnp.abs(log_lb - b)))
    k = one_m_lb * (1.0 / (1.0 + jnp.exp(fp)))

    r = lax.broadcasted_iota(jnp.int32, (tb, tb), 0)
    c = lax.broadcasted_iota(jnp.int32, (tb, tb), 1)
    same = (r // C) == (c // C)
    tri = jnp.where(same & (c <= r), 1.0, 0.0).astype(BF)
    blk = jnp.where(same, 1.0, 0.0).astype(BF)
    a0, a1, a2 = _split3(log_f)
    G = _dot(tri, a0) + _dot(tri, a1) + _dot(tri, a2)
    G_last = _dot(blk, a0) + _dot(blk, a1) + _dot(blk, a2)

    eg = jnp.exp(G)
    q_dec = (q_ref[...] * (HGRN_DK ** -0.5) * eg).astype(BF)
    k_inv = (k * jnp.exp(-G)).astype(BF)
    k_state = k * jnp.exp(G_last - G)
    decay = jnp.exp(G_last)
    v = i_ref[...]
    vb = v.astype(BF)

    rr = lax.broadcasted_iota(jnp.int32, (C, C), 0)
    cc = lax.broadcasted_iota(jnp.int32, (C, C), 1)
    causal = cc <= rr

    heads = [slice(h * HGRN_DK, (h + 1) * HGRN_DK) for h in range(HGRN_HEADS)]
    sts = [st_ref[h] for h in range(HGRN_HEADS)]
    for n in range(nc):
        rs = slice(n * C, (n + 1) * C)
        qds = [q_dec[rs, hs] for hs in heads]
        As = [_dot_nt(qds[h], k_inv[rs, heads[h]]) for h in range(HGRN_HEADS)]
        kvs = [_dot(v[rs, heads[h]].T.astype(BF), k_state[rs, heads[h]].astype(BF))
               for h in range(HGRN_HEADS)]
        inter = [_dot_nt(qds[h], sts[h].astype(BF)) for h in range(HGRN_HEADS)]
        for h in range(HGRN_HEADS):
            A = jnp.where(causal, As[h], 0.0).astype(BF)
            os_ref[rs, heads[h]] = _dot(A, vb[rs, heads[h]]) + inter[h]
            sts[h] = decay[n * C:n * C + 1, heads[h]] * sts[h] + kvs[h]
    for h in range(HGRN_HEADS):
        st_ref[h] = sts[h]

    g = g_ref[...]
    gate = g * (1.0 / (1.0 + jnp.exp(-g)))
    for h in range(HGRN_HEADS):
        hs = slice(h * HGRN_DK, (h + 1) * HGRN_DK)
        o_ref[:, hs] = (_rms(os_ref[:, hs], nw_ref[...]) * gate[:, hs]).astype(BF)


def _hgrn(gh, lb3, nw, B, S, tb=256):
    nb = S // tb
    col = lambda j: (lambda b, i: (b * nb + i, j))
    return pl.pallas_call(
        functools.partial(_hgrn_kernel, tb=tb),
        grid=(B, nb),
        in_specs=[pl.BlockSpec((tb, HGRN_W), col(0)), pl.BlockSpec((tb, HGRN_W), col(1)),
                  pl.BlockSpec((tb, HGRN_W), col(2)), pl.BlockSpec((tb, HGRN_W), col(3)),
                  pl.BlockSpec((3, HGRN_W), lambda b, i: (0, 0)),
                  pl.BlockSpec((1, HGRN_DV), lambda b, i: (0, 0))],
        out_specs=pl.BlockSpec((tb, HGRN_W), col(0)),
        out_shape=jax.ShapeDtypeStruct((B * S, HGRN_W), BF),
        scratch_shapes=[pltpu.VMEM((HGRN_HEADS, HGRN_DV, HGRN_DK), F32),
                        pltpu.VMEM((tb, HGRN_W), F32)],
        compiler_params=_cparams(("parallel", "arbitrary")),
        name="hgrn",
    )(gh, gh, gh, gh, lb3, nw)


def _mla_kernel(q_ref, k_ref, vt_ref, o_ref, *, t, tk):
    i = pl.program_id(1)
    n_full = (i * t) // tk
    key = lax.broadcasted_iota(jnp.int32, (tk, t), 0)
    qry = lax.broadcasted_iota(jnp.int32, (tk, t), 1)
    causal = n_full * tk + key <= i * t + qry

    heads = [slice(h * HEAD_PAD, (h + 1) * HEAD_PAD) for h in range(MLA_HEADS)]

    def logits(j):
        ks = pl.ds(pl.multiple_of(j * tk, tk), tk)
        return tuple(_dot_nt(k_ref[ks, hs], q_ref[:, hs]) for hs in heads)

    def accumulate(j, ss, state, mask):
        if mask:
            ss = [jnp.where(causal, s, NEG) for s in ss]
        ms = [jnp.maximum(state[h][0], jnp.max(ss[h], axis=0, keepdims=True))
              for h in range(MLA_HEADS)]
        ps = [jnp.exp2(ss[h] - ms[h]).astype(BF) for h in range(MLA_HEADS)]
        return tuple((ms[h], jnp.exp2(state[h][0] - ms[h]) * state[h][1]
                      + _dot(vt_ref[j, heads[h], :], ps[h])) for h in range(MLA_HEADS))

    init = tuple((jnp.full((1, t), NEG, F32), jnp.zeros((HEAD_PAD, t), F32))
                 for _ in range(MLA_HEADS))
    state = lax.fori_loop(0, n_full, lambda j, c: accumulate(j, logits(j), c, False), init)
    state = accumulate(n_full, logits(n_full), state, True)
    outs = [acc[:MLA_V, :] * (1.0 / acc[MLA_V:MLA_V + 1, :]) for (_, acc) in state]
    o_ref[...] = jnp.concatenate(outs, axis=0).T.astype(BF)


def _mla_attn(q, k, vt, B, S, t=ATT_TILE, tk=MLA_KEY_TILE):
    nq = S // t
    nk = S // tk
    hp = MLA_HEADS * HEAD_PAD
    return pl.pallas_call(
        functools.partial(_mla_kernel, t=t, tk=tk),
        grid=(B, nq),
        in_specs=[pl.BlockSpec((t, hp), lambda b, i: (b * nq + i, 0)),
                  pl.BlockSpec((S, hp), lambda b, i: (b, 0)),
                  pl.BlockSpec((nk, hp, tk), lambda b, i: (b, 0, 0))],
        out_specs=pl.BlockSpec((t, MLA_HEADS * MLA_V), lambda b, i: (b * nq + i, 0)),
        out_shape=jax.ShapeDtypeStruct((B * S, MLA_HEADS * MLA_V), BF),
        compiler_params=_cparams(("parallel", "arbitrary")),
        name="mla_attn",
    )(q, k, vt)


BISECT_CAP = 48
_LAST_BUCKET_FROM = next(
    n for n in range(MAX_EXACT, 1 << 20)
    if MAX_EXACT + int(math.log(n / MAX_EXACT) / math.log(MAX_DISTANCE / MAX_EXACT)
                       * (NUM_BUCKETS - MAX_EXACT)) >= NUM_BUCKETS - 1)
assert _LAST_BUCKET_FROM <= LANES - 1 and _LAST_BUCKET_FROM <= MAX_DISTANCE


def _dsa_kernel(qi_ref, w_ref, qd_ref, pos_ref, ki_ref, kd_ref, vt_ref, tab_ref, pmin_ref, pmax_ref,
                run_ref, o_ref, sc_ref, qm_ref, bias_ref, *, t, n_sel):
    i = pl.program_id(1)
    lane = lax.broadcasted_iota(jnp.int32, (t, LANES), 1)
    lo_half = lane < DSA_HEAD_DIM
    key = lax.broadcasted_iota(jnp.int32, (t, t), 0)
    qry = lax.broadcasted_iota(jnp.int32, (t, t), 1)
    causal = key <= qry
    k_sel = float(n_sel)

    def key_rows(j):
        return pl.ds(pl.multiple_of(j * t, t), t)

    @pl.when(i + 1 < sc_ref.shape[0])
    def _():
        sc_ref[i + 1] = jnp.full((t, t), -jnp.inf, F32)

    def over_tiles(body, init):
        return lax.fori_loop(0, (i + 2) // 2, lambda jj, c: body(2 * jj + 1, body(2 * jj, c)), init)

    def fold(x, op):
        return op(x.reshape(t // SUBLANES, SUBLANES, t), axis=0)

    zero_b = jnp.zeros((t, LANES), BF)
    for p in range(IDX_HEADS // 2):
        qp = qi_ref[:, p * LANES:(p + 1) * LANES]
        qm_ref[2 * p] = jnp.where(lo_half, qp, zero_b)
        qm_ref[2 * p + 1] = jnp.where(lo_half, zero_b, qp)
    wt = w_ref[...].T * ((IDX_HEADS ** -0.5) * (IDX_DIM ** -0.5))

    def score_tile(j):
        kt = ki_ref[key_rows(j), :]
        acc = None
        for h0 in range(0, IDX_HEADS, 4):
            ss = [_dot_nt(kt, qm_ref[h]) for h in range(h0, h0 + 4)]
            for n, s in enumerate(ss):
                term = jnp.maximum(s, 0.0) * wt[h0 + n:h0 + n + 1, :]
                acc = term if acc is None else acc + term
        return acc

    def add_stats(x, c):
        mx, mn, cp, cz = c
        return (jnp.maximum(mx, fold(x, jnp.max)),
                jnp.minimum(mn, fold(jnp.where(x == -jnp.inf, jnp.inf, x), jnp.min)),
                cp + fold(jnp.where(x > 0.0, 1.0, 0.0), jnp.sum),
                cz + fold(jnp.where(x == 0.0, 1.0, 0.0), jnp.sum))

    def p1_body(j, c):
        x = score_tile(j)
        sc_ref[j] = x
        return add_stats(x, c)

    z8 = jnp.zeros((SUBLANES, t), F32)
    stats = lax.fori_loop(0, i, p1_body, (z8 - jnp.inf, z8 + jnp.inf, z8, z8))
    x_diag = jnp.where(causal, score_tile(i), -jnp.inf)
    sc_ref[i] = x_diag
    mx, mn, cp, cz = add_stats(x_diag, stats)

    def total(c):
        return jnp.sum(c, axis=0, keepdims=True)

    def count_gt(thr):
        def body(j, c):
            return c + fold(jnp.where(sc_ref[j] > thr, 1.0, 0.0), jnp.sum)
        return total(over_tiles(body, jnp.zeros((SUBLANES, t), F32)))

    row_max = jnp.max(mx, axis=0, keepdims=True)
    row_min = jnp.min(mn, axis=0, keepdims=True)
    c_pos, c_zero = total(cp), total(cz)
    n_valid = (i * t + 1 + lax.broadcasted_iota(jnp.int32, (1, t), 1)).astype(F32)

    big = n_valid > k_sel
    pos_q = jnp.logical_and(big, c_pos >= k_sel)
    tie_q = jnp.logical_and(jnp.logical_and(big, c_pos < k_sel), c_pos + c_zero >= k_sel)
    neg_q = jnp.logical_and(big, c_pos + c_zero < k_sel)
    need = jnp.where(tie_q, k_sel - c_pos, 0.0)
    lo0 = jnp.where(jnp.logical_or(pos_q, tie_q), 0.0, -jnp.inf)
    hi0 = jnp.where(neg_q, 0.0, row_max)
    c_lo0 = jnp.where(pos_q, c_pos, jnp.where(tie_q, k_sel, n_valid))

    def open_queries(c_lo):
        return jnp.max(jnp.where(c_lo > k_sel, 1.0, 0.0))

    def bis_cond(c):
        return jnp.logical_and(c[0] < BISECT_CAP, c[1] > 0.0)

    def bisect_once(lo, hi, c_lo):
        active = c_lo > k_sel
        base = jnp.maximum(lo, row_min)
        mid = base + 0.5 * (hi - base)
        cnt = count_gt(mid)
        up = jnp.logical_and(active, cnt >= k_sel)
        dn = jnp.logical_and(active, cnt < k_sel)
        return jnp.where(up, mid, lo), jnp.where(dn, mid, hi), jnp.where(up, cnt, c_lo)

    def bis_body(c):
        it, _, lo, hi, c_lo = c
        lo, hi, c_lo = bisect_once(*bisect_once(lo, hi, c_lo))
        return it + 2, open_queries(c_lo), lo, hi, c_lo

    _, _, lo, _, c_lo = lax.while_loop(
        bis_cond, bis_body, (jnp.int32(0), open_queries(c_lo0), lo0, hi0, c_lo0))

    still_open = c_lo > k_sel

    def exact_kth(_):
        def to_float(kk):
            return lax.bitcast_convert_type(jnp.where(kk < 0, kk ^ jnp.int32(0x7FFFFFFF), kk), F32)

        def count_ge(thr):
            def body(j, c):
                return c + fold(jnp.where(sc_ref[j] >= thr, 1.0, 0.0), jnp.sum)
            return total(over_tiles(body, jnp.zeros((SUBLANES, t), F32)))

        def body(b, kth):
            cand = kth + lax.shift_left(jnp.int32(1), 31 - b)
            return jnp.where(count_ge(to_float(cand)) >= k_sel, cand, kth)

        kth = to_float(lax.fori_loop(0, 32, body, jnp.full((1, t), -2 ** 31, jnp.int32)))
        return kth, count_gt(kth)

    kth, above = lax.cond(jnp.max(jnp.where(still_open, 1.0, 0.0)) > 0.0, exact_kth,
                          lambda _: (jnp.zeros((1, t), F32), jnp.zeros((1, t), F32)), 0)
    lo = jnp.where(still_open, kth, lo)
    need = jnp.where(still_open, k_sel - above, need)
    tie_val = jnp.where(still_open, kth, 0.0)

    def mask_plain(_):
        def body(j, c):
            sc_ref[j] = jnp.where(sc_ref[j] > lo, 0.0, NEG)
            return c
        return lax.fori_loop(0, i + 1, body, 0)

    def mask_ties(_):
        lower = jnp.where(qry <= key, 1.0, 0.0).astype(BF)

        def body(j, seen):
            x = sc_ref[j]
            tied = x == tie_val
            tf = jnp.where(tied, 1.0, 0.0)
            rank = _dot(lower, tf.astype(BF)) + seen
            take = jnp.logical_and(tied, rank <= need)
            sc_ref[j] = jnp.where(jnp.logical_or(x > lo, take), 0.0, NEG)
            return seen + total(fold(tf, jnp.sum))
        lax.fori_loop(0, i + 1, body, jnp.zeros((1, t), F32))
        return 0

    lax.cond(jnp.max(need) > 0.0, mask_ties, mask_plain, 0)

    qh = []
    for p in range(DSA_HEADS // 2):
        qp = qd_ref[:, p * LANES:(p + 1) * LANES]
        qh.append(jnp.where(lo_half, qp, zero_b))
        qh.append(jnp.where(lo_half, zero_b, qp))
    pq = pos_ref[pl.ds(i, 1), :]
    pq_min = pmin_ref[pl.program_id(0), i]
    far_bias = [tab_ref[NUM_BUCKETS - 1, h] * LOG2E for h in range(DSA_HEADS)]
    log_ratio = math.log(MAX_DISTANCE / MAX_EXACT)

    dist = lax.broadcasted_iota(jnp.int32, (SUBLANES, LANES), 1)
    large = MAX_EXACT + (jnp.log(jnp.maximum(dist, 1).astype(F32) / MAX_EXACT) / log_ratio
                         * (NUM_BUCKETS - MAX_EXACT)).astype(jnp.int32)
    bucket = jnp.where(dist < MAX_EXACT, dist, jnp.minimum(large, NUM_BUCKETS - 1))
    by_dist = []
    for h in range(DSA_HEADS):
        bh = jnp.full((SUBLANES, LANES), tab_ref[0, h], F32)
        for jb in range(1, NUM_BUCKETS):
            bh = jnp.where(bucket >= jb, tab_ref[jb, h], bh)
        by_dist.append(jnp.concatenate([bh * LOG2E] * (t // SUBLANES), axis=0))

    def bias_of(n):
        n = jnp.clip(n, 0, LANES - 1)
        return [jnp.concatenate([jnp.take_along_axis(by_dist[h], n[:, c * LANES:(c + 1) * LANES], axis=1)
                                 for c in range(t // LANES)], axis=1) for h in range(DSA_HEADS)]

    def pair_bias(pk_row):
        pk = jnp.broadcast_to(pk_row, (SUBLANES, t)).T[:, 0:1]
        return bias_of(pq - pk)

    b_idx = pl.program_id(0)

    @pl.when(jnp.logical_and(b_idx == 0, i == 0))
    def _():
        for gap in range(2):
            for h, tile in enumerate(bias_of(gap * t + qry - key)):
                bias_ref[gap * DSA_HEADS + h] = tile

    def attend(j, carry, bias):
        madd = sc_ref[j]
        ks = key_rows(j)
        kps = [kd_ref[ks, p * LANES:(p + 1) * LANES] for p in range(DSA_HEADS // 2)]
        ss = [_dot_nt(kps[h // 2], qh[h]) + (madd + bias[h]) for h in range(DSA_HEADS)]
        ms = [jnp.maximum(carry[h][0], jnp.max(ss[h], axis=0, keepdims=True))
              for h in range(DSA_HEADS)]
        ps = [jnp.exp2(ss[h] - ms[h]).astype(BF) for h in range(DSA_HEADS)]
        return tuple((ms[h], jnp.exp2(carry[h][0] - ms[h]) * carry[h][1]
                      + _dot(vt_ref[j, h * DSA_V_ROWS:(h + 1) * DSA_V_ROWS, :], ps[h]))
                     for h in range(DSA_HEADS))

    def p3_body(j, carry):
        gap = i - j
        far = pq_min - pmax_ref[b_idx, j] >= MAX_DISTANCE
        consecutive = jnp.logical_and(
            jnp.logical_and(run_ref[b_idx, i] == 1, run_ref[b_idx, j] == 1),
            jnp.logical_and(gap <= 1, pq_min - pmin_ref[b_idx, j] == gap * t))

        def near(c):
            return lax.cond(
                consecutive,
                lambda c: attend(j, c, [bias_ref[gap * DSA_HEADS + h] for h in range(DSA_HEADS)]),
                lambda c: attend(j, c, pair_bias(pos_ref[pl.ds(j, 1), :])), c)

        return lax.cond(far, lambda c: attend(j, c, far_bias), near, carry)

    init = tuple((jnp.full((1, t), NEG, F32), jnp.zeros((DSA_V_ROWS, t), F32))
                 for _ in range(DSA_HEADS))
    carry = lax.fori_loop(0, i + 1, p3_body, init)
    outs = [acc[:DSA_HEAD_DIM, :] * (1.0 / acc[DSA_HEAD_DIM:DSA_HEAD_DIM + 1, :])
            for (_, acc) in carry]
    o_ref[...] = jnp.concatenate(outs, axis=0).T.astype(BF)


def _dsa(gq, gk, gd, ki2, vdt, pos_tiles, tab, B, S, t=ATT_TILE):
    nq = S // t
    n_sel = min(TOPK_MAX, S // 4)
    vrows = DSA_HEADS * DSA_V_ROWS
    qrow = lambda c: (lambda b, i: (b * nq + i, c))
    return pl.pallas_call(
        functools.partial(_dsa_kernel, t=t, n_sel=n_sel),
        grid=(B, nq),
        in_specs=[pl.BlockSpec((t, W_Q), qrow(0)),
                  pl.BlockSpec((t, LANES), qrow(1)),
                  pl.BlockSpec((t, DSA_WIDTH), qrow(0)),
                  pl.BlockSpec((None, nq, t), lambda b, i: (b, 0, 0)),
                  pl.BlockSpec((S, LANES), lambda b, i: (b, 0)),
                  pl.BlockSpec((S, DSA_WIDTH), lambda b, i: (b, 1)),
                  pl.BlockSpec((nq, vrows, t), lambda b, i: (b, 0, 0)),
                  pl.BlockSpec(memory_space=pltpu.SMEM), pl.BlockSpec(memory_space=pltpu.SMEM),
                  pl.BlockSpec(memory_space=pltpu.SMEM), pl.BlockSpec(memory_space=pltpu.SMEM)],
        out_specs=pl.BlockSpec((t, DSA_WIDTH), qrow(0)),
        out_shape=jax.ShapeDtypeStruct((B * S, DSA_WIDTH), BF),
        scratch_shapes=[pltpu.VMEM((nq, t, t), F32), pltpu.VMEM((IDX_HEADS, t, LANES), BF),
                        pltpu.VMEM((2 * DSA_HEADS, t, t), F32)],
        compiler_params=_cparams(("arbitrary", "arbitrary")),
        name="dsa",
    )(gq, gk, gd, pos_tiles, ki2, gd, vdt, tab, jnp.min(pos_tiles, axis=-1), jnp.max(pos_tiles, axis=-1),
      jnp.all(pos_tiles[..., 1:] - pos_tiles[..., :-1] == 1, axis=-1).astype(jnp.int32))


def _outmlp_kernel(h_ref, yh_ref, ym_ref, yd_ref, wo_ref, nw_ref, w1_ref, w2_ref, fw_ref, o_ref,
                   *, final, ff_chunk):
    mixed = jnp.concatenate([yh_ref[...], ym_ref[...], yd_ref[...]], axis=1)
    h = h_ref[...] + _dot(mixed, wo_ref[...])
    u = _rms(h, nw_ref[...]).astype(BF)
    out = h
    for c in range(D_FF // ff_chunk):
        cs = slice(c * ff_chunk, (c + 1) * ff_chunk)
        a = jnp.maximum(_dot(u, w1_ref[:, cs]), 0.0)
        out = out + _dot((a * a).astype(BF), w2_ref[cs, :])
    if final:
        out = _rms(out, fw_ref[...])
    o_ref[...] = out


def _outmlp(h, yh, ym, yd, wo, nw, w1, w2, fw, final, tm=ROW_TILE, ff_chunk=1024):
    T = h.shape[0]
    tm = min(tm, T)
    row = lambda i: (i, 0)
    return pl.pallas_call(
        functools.partial(_outmlp_kernel, final=final, ff_chunk=ff_chunk),
        grid=(T // tm,),
        in_specs=[pl.BlockSpec((tm, D_MODEL), row), pl.BlockSpec((tm, HGRN_W), row),
                  pl.BlockSpec((tm, MLA_HEADS * MLA_V), row), pl.BlockSpec((tm, DSA_WIDTH), row),
                  _const_spec((D_MODEL, D_MODEL)), _const_spec((1, D_MODEL)),
                  _const_spec((D_MODEL, D_FF)), _const_spec((D_FF, D_MODEL)),
                  _const_spec((1, D_MODEL))],
        out_specs=pl.BlockSpec((tm, D_MODEL), row),
        out_shape=jax.ShapeDtypeStruct((T, D_MODEL), F32),
        compiler_params=_cparams(("parallel",)),
        name="outmlp",
    )(h, yh, ym, yd, wo, nw, w1, w2, fw)


def _rot_cols(w):
    half = w.shape[-1] // 2
    return jnp.concatenate([-w[..., half:], w[..., :half]], axis=-1)


def _place(w, width, off):
    pad = [(0, 0)] * (w.ndim - 1) + [(off, width - off - w.shape[-1])]
    return jnp.pad(w, pad)


def _layout_w_in(w_in):
    sizes = (HGRN_W, HGRN_W, HGRN_W, HGRN_W, MLA_Q_LORA, MLA_KV_LORA + MLA_ROPE,
             DSA_WIDTH, DSA_WIDTH, DSA_WIDTH, IDX_HEADS * IDX_DIM, IDX_DIM, IDX_HEADS)
    offs = [0]
    for s in sizes:
        offs.append(offs[-1] + s)
    hq, hf, hi, hg, mqa, mkva, dq, dk, dv, iq, ik, iw = [
        w_in[..., offs[n]:offs[n + 1]] for n in range(len(sizes))]
    ckv, kpe = mkva[..., :MLA_KV_LORA], mkva[..., MLA_KV_LORA:]
    cols = [hq, hf, hi, hg,
            _place(mqa, 256, 0), ckv, _place(kpe, LANES, MLA_NOPE), _place(_rot_cols(kpe), LANES, MLA_NOPE),
            dq * (DSA_HEAD_DIM ** -0.5 * LOG2E), dk,
            iq,
            ik, ik, _place(iw, LANES, 0)]
    w_cat = jnp.concatenate(cols, axis=-1).astype(BF)
    L = w_in.shape[0]
    dvt = jnp.swapaxes(dv, 1, 2).reshape(L, DSA_HEADS, DSA_HEAD_DIM, D_MODEL)
    dvt = jnp.pad(dvt, ((0, 0), (0, 0), (0, DSA_V_ROWS - DSA_HEAD_DIM), (0, 0)))
    return w_cat, dvt.reshape(L, DSA_HEADS * DSA_V_ROWS, D_MODEL).astype(BF)


def _layout_mla(w_qb, w_kvb):
    L = w_qb.shape[0]
    dq = MLA_NOPE + MLA_ROPE
    wq = w_qb.reshape(L, MLA_Q_LORA, MLA_HEADS, dq)
    wq_rot = jnp.concatenate([jnp.zeros_like(wq[..., :MLA_NOPE]), _rot_cols(wq[..., MLA_NOPE:])], axis=-1)
    pad_q = lambda w: jnp.pad(w, ((0, 0), (0, 256 - MLA_Q_LORA), (0, 0), (0, HEAD_PAD - dq))).reshape(
        L, 256, MLA_HEADS * HEAD_PAD).astype(BF)
    wkv = w_kvb.reshape(L, MLA_KV_LORA, MLA_HEADS, MLA_NOPE + MLA_V)
    pad_kv = lambda w: jnp.pad(w, ((0, 0), (0, 0), (0, 0), (0, HEAD_PAD - w.shape[-1]))).reshape(
        L, MLA_KV_LORA, MLA_HEADS * HEAD_PAD).astype(BF)
    wvt = jnp.swapaxes(pad_kv(wkv[..., MLA_NOPE:]), 1, 2)
    return pad_q(wq), pad_q(wq_rot), pad_kv(wkv[..., :MLA_NOPE]), wvt


def kernel(x, positions, attn_norm_w, w_in, hgrn_lb_logits, hgrn_norm_w, mla_q_norm_w, mla_w_qb,
           mla_kv_norm_w, mla_w_kvb, idx_k_norm_w, idx_k_norm_b, rel_bias_table, w_out,
           mlp_norm_w, w_mlp_in, w_mlp_out, final_norm_w):
    B, S, _ = x.shape
    T = B * S
    depth = w_in.shape[0]

    inv_freq = 1.0 / (ROPE_THETA ** (jnp.arange(0, MLA_ROPE, 2, dtype=F32) / MLA_ROPE))
    ang = positions.astype(F32)[..., None] * inv_freq
    cos, sin = jnp.cos(ang).reshape(T, -1), jnp.sin(ang).reshape(T, -1)
    cos_t = jnp.concatenate([jnp.ones((T, MLA_NOPE), F32), cos, cos,
                             jnp.zeros((T, HEAD_PAD - MLA_NOPE - MLA_ROPE), F32)], axis=1)
    sin_t = jnp.concatenate([jnp.zeros((T, MLA_NOPE), F32), sin, sin,
                             jnp.zeros((T, HEAD_PAD - MLA_NOPE - MLA_ROPE), F32)], axis=1)
    pos_tiles = positions.reshape(B, S // ATT_TILE, ATT_TILE)

    lb = jnp.cumsum(jax.nn.softmax(hgrn_lb_logits.astype(F32), axis=0), axis=0)
    lb = lb - lb[0:1]
    lb3 = jnp.stack([jnp.log(lb), jnp.log1p(-lb), 1.0 - lb], axis=1)

    w_cat, w_dvt = _layout_w_in(w_in)
    wq, wqr, wk, wvt = _layout_mla(mla_w_qb, mla_w_kvb)
    qnw = jnp.pad(mla_q_norm_w, ((0, 0), (0, 256 - MLA_Q_LORA)))
    lnw = jnp.concatenate([idx_k_norm_w, idx_k_norm_w], axis=-1)
    lnb = jnp.concatenate([idx_k_norm_b, idx_k_norm_b], axis=-1)
    wo = w_out.astype(BF)
    w1 = w_mlp_in.astype(BF)
    w2 = w_mlp_out.astype(BF)
    tab = rel_bias_table.astype(F32)

    h = x.reshape(T, D_MODEL)
    for l in range(depth):
        gh, gm, gd, gq, gk, vdt = _inproj(h, attn_norm_w[l][None], w_cat[l], w_dvt[l])
        q_m, k_m, v_m, ki2 = _prep(gm, gk, cos_t, sin_t, qnw[l][None], wq[l], wqr[l],
                                   mla_kv_norm_w[l][None], wk[l], wvt[l], lnw[l][None], lnb[l][None])
        y_h = _hgrn(gh, lb3[l], hgrn_norm_w[l][None], B, S)
        y_m = _mla_attn(q_m, k_m, v_m, B, S)
        y_d = _dsa(gq, gk, gd, ki2, vdt, pos_tiles, tab, B, S)
        h = _outmlp(h, y_h, y_m, y_d, wo[l], mlp_norm_w[l][None], w1[l], w2[l],
                    final_norm_w[None], final=(l == depth - 1))
    return h.reshape(B, S, D_MODEL)
```

```python
import functools
import math

import jax
import jax.numpy as jnp
from jax import lax
from jax.experimental import pallas as pl
from jax.experimental.pallas import tpu as pltpu

D_MODEL = 1024
HGRN_HEADS = 4
HGRN_DK = 128
HGRN_DV = 128
HGRN_W = HGRN_HEADS * HGRN_DK
HGRN_CHUNK = 32
MLA_HEADS = 4
MLA_NOPE = 64
MLA_ROPE = 32
MLA_V = 64
MLA_Q_LORA = 192
MLA_KV_LORA = 128
ROPE_THETA = 10000.0
DSA_HEADS = 4
DSA_HEAD_DIM = 64
DSA_WIDTH = DSA_HEADS * DSA_HEAD_DIM
IDX_HEADS = 8
IDX_DIM = 64
TOPK_MAX = 256
NUM_BUCKETS = 32
MAX_EXACT = NUM_BUCKETS // 2
MAX_DISTANCE = 128
D_FF = 4 * D_MODEL
EPS = 1e-6

LANES = 128
SUBLANES = 8
BF16_ROWS = 16
HEAD_PAD = 128
NEG = -1e30
ATT_TILE = 256
MLA_KEY_TILE = 512
ROW_TILE = 512
VMEM_LIMIT = 52 * 1024 * 1024
LOG2E = math.log2(math.e)

DSA_V_ROWS = DSA_HEAD_DIM + BF16_ROWS

C_H = 0
C_M = C_H + 4 * HGRN_W
W_M = 640
C_D = C_M + W_M
W_D = 2 * DSA_WIDTH
C_Q = C_D + W_D
W_Q = IDX_HEADS * IDX_DIM
C_K = C_Q + W_Q
W_K = 256
N_PAD = C_K + W_K

BF = jnp.bfloat16
F32 = jnp.float32


def _dot(a, b):
    return jnp.dot(a, b, preferred_element_type=F32)


def _dot_nt(a, b):
    return lax.dot_general(a, b, (((1,), (1,)), ((), ())), preferred_element_type=F32)


def _rms(x, w, n=None):
    n = x.shape[-1] if n is None else n
    ms = jnp.sum(x * x, axis=-1, keepdims=True) * (1.0 / n)
    return x * lax.rsqrt(ms + EPS) * w


def _cparams(sem):
    return pltpu.CompilerParams(dimension_semantics=sem, vmem_limit_bytes=VMEM_LIMIT)


def _const_spec(shape):
    nd = len(shape)
    return pl.BlockSpec(shape, lambda *_: (0,) * nd, pipeline_mode=pl.Buffered(1))


def _inproj_kernel(x_ref, nw_ref, w_ref, wvt_ref, oh_ref, om_ref, od_ref, oq_ref, ok_ref, ovt_ref):
    u = _rms(x_ref[...], nw_ref[...]).astype(BF)
    oh_ref[...] = _dot(u, w_ref[:, C_H:C_M])
    om_ref[...] = _dot(u, w_ref[:, C_M:C_D])
    od_ref[...] = _dot(u, w_ref[:, C_D:C_Q]).astype(BF)
    oq_ref[...] = _dot(u, w_ref[:, C_Q:C_K]).astype(BF)
    ok_ref[...] = _dot(u, w_ref[:, C_K:N_PAD])
    vt = _dot_nt(wvt_ref[...], u)
    row = lax.broadcasted_iota(jnp.int32, vt.shape, 0)
    vt = jnp.where(row % DSA_V_ROWS == DSA_HEAD_DIM, 1.0, vt).astype(BF)
    for n in range(ovt_ref.shape[0]):
        ovt_ref[n] = vt[:, n * ATT_TILE:(n + 1) * ATT_TILE]


def _inproj(h, nw, w, wvt, tm=ROW_TILE):
    T = h.shape[0]
    tm = min(tm, T)
    row = lambda i: (i, 0)
    vrows = DSA_HEADS * DSA_V_ROWS
    per = tm // ATT_TILE
    return pl.pallas_call(
        _inproj_kernel,
        grid=(T // tm,),
        in_specs=[pl.BlockSpec((tm, D_MODEL), row), _const_spec((1, D_MODEL)),
                  _const_spec((D_MODEL, N_PAD)), _const_spec((vrows, D_MODEL))],
        out_specs=[pl.BlockSpec((tm, 4 * HGRN_W), row), pl.BlockSpec((tm, W_M), row),
                   pl.BlockSpec((tm, W_D), row), pl.BlockSpec((tm, W_Q), row),
                   pl.BlockSpec((tm, W_K), row),
                   pl.BlockSpec((per, vrows, ATT_TILE), lambda i: (i, 0, 0))],
        out_shape=[jax.ShapeDtypeStruct((T, 4 * HGRN_W), F32), jax.ShapeDtypeStruct((T, W_M), F32),
                   jax.ShapeDtypeStruct((T, W_D), BF), jax.ShapeDtypeStruct((T, W_Q), BF),
                   jax.ShapeDtypeStruct((T, W_K), F32),
                   jax.ShapeDtypeStruct((T // ATT_TILE, vrows, ATT_TILE), BF)],
        compiler_params=_cparams(("parallel",)),
        name="inproj",
    )(h, nw, w, wvt)


def _prep_kernel(gm_ref, gk_ref, cos_ref, sin_ref, qnw_ref, wq_ref, wqr_ref, kvnw_ref, wk_ref,
                 wvt_ref, lnw_ref, lnb_ref, q_ref, k_ref, vt_ref, ki_ref):
    cs = cos_ref[...]
    sn = sin_ref[...]
    cs4 = jnp.concatenate([cs] * MLA_HEADS, axis=1)
    sn4 = jnp.concatenate([sn] * MLA_HEADS, axis=1)
    qn = _rms(gm_ref[:, 0:256], qnw_ref[...], n=MLA_Q_LORA).astype(BF)
    scale = (MLA_NOPE + MLA_ROPE) ** -0.5 * LOG2E
    q = (_dot(qn, wq_ref[...]) * cs4 + _dot(qn, wqr_ref[...]) * sn4) * scale
    q_ref[...] = q.astype(BF)
    cn = _rms(gm_ref[:, 256:384], kvnw_ref[...]).astype(BF)
    kp = gm_ref[:, 384:512] * cs + gm_ref[:, 512:640] * sn
    k = _dot(cn, wk_ref[...]) + jnp.concatenate([kp] * MLA_HEADS, axis=1)
    k_ref[...] = k.astype(BF)
    vt = _dot_nt(wvt_ref[...], cn)
    row = lax.broadcasted_iota(jnp.int32, vt.shape, 0)
    vt_ref[...] = jnp.where(row % HEAD_PAD == MLA_V, 1.0, vt).astype(BF)
    x = gk_ref[:, 0:LANES]
    first = lax.broadcasted_iota(jnp.int32, x.shape, 1) < IDX_DIM
    mu = jnp.sum(jnp.where(first, x, 0.0), axis=-1, keepdims=True) * (1.0 / IDX_DIM)
    xc = x - mu
    var = jnp.sum(jnp.where(first, xc * xc, 0.0), axis=-1, keepdims=True) * (1.0 / IDX_DIM)
    ki_ref[...] = (xc * lax.rsqrt(var + EPS) * lnw_ref[...] + lnb_ref[...]).astype(BF)


def _prep(gm, gk, cos_t, sin_t, qnw, wq, wqr, kvnw, wk, wvt, lnw, lnb, tm=MLA_KEY_TILE):
    T = gm.shape[0]
    row = lambda i: (i, 0)
    hp = MLA_HEADS * HEAD_PAD
    return pl.pallas_call(
        _prep_kernel,
        grid=(T // tm,),
        in_specs=[pl.BlockSpec((tm, W_M), row), pl.BlockSpec((tm, W_K), row),
                  pl.BlockSpec((tm, LANES), row), pl.BlockSpec((tm, LANES), row),
                  _const_spec((1, 256)), _const_spec((256, hp)), _const_spec((256, hp)),
                  _const_spec((1, MLA_KV_LORA)), _const_spec((MLA_KV_LORA, hp)),
                  _const_spec((hp, MLA_KV_LORA)), _const_spec((1, LANES)), _const_spec((1, LANES))],
        out_specs=[pl.BlockSpec((tm, hp), row), pl.BlockSpec((tm, hp), row),
                   pl.BlockSpec((None, hp, tm), lambda i: (i, 0, 0)), pl.BlockSpec((tm, LANES), row)],
        out_shape=[jax.ShapeDtypeStruct((T, hp), BF), jax.ShapeDtypeStruct((T, hp), BF),
                   jax.ShapeDtypeStruct((T // tm, hp, tm), BF), jax.ShapeDtypeStruct((T, LANES), BF)],
        compiler_params=_cparams(("parallel",)),
        name="prep",
    )(gm, gk, cos_t, sin_t, qnw, wq, wqr, kvnw, wk, wvt, lnw, lnb)


def _split3(x):
    a = x.astype(BF)
    r = x - a.astype(F32)
    b = r.astype(BF)
    c = (r - b.astype(F32)).astype(BF)
    return a, b, c


def _hgrn_kernel(q_ref, f_ref, i_ref, g_ref, lb_ref, nw_ref, o_ref, st_ref, os_ref, *, tb):
    C = HGRN_CHUNK
    nc = tb // C

    @pl.when(pl.program_id(1) == 0)
    def _():
        st_ref[...] = jnp.zeros_like(st_ref)

    fp = f_ref[...]
    log_lb = lb_ref[0:1, :]
    log1m_lb = lb_ref[1:2, :]
    one_m_lb = lb_ref[2:3, :]
    ls = jnp.minimum(fp, 0.0) - jnp.log1p(jnp.exp(-jnp.abs(fp)))
    b = log1m_lb + ls
    log_f = jnp.maximum(log_lb, b) + jnp.log1p(jnp.exp(-jnp.abs(log_lb - b)))
    k = one_m_lb * (1.0 / (1.0 + jnp.exp(fp)))

    r = lax.broadcasted_iota(jnp.int32, (tb, tb), 0)
    c = lax.broadcasted_iota(jnp.int32, (tb, tb), 1)
    same = (r // C) == (c // C)
    tri = jnp.where(same & (c <= r), 1.0, 0.0).astype(BF)
    blk = jnp.where(same, 1.0, 0.0).astype(BF)
    a0, a1, a2 = _split3(log_f)
    G = _dot(tri, a0) + _dot(tri, a1) + _dot(tri, a2)
    G_last = _dot(blk, a0) + _dot(blk, a1) + _dot(blk, a2)

    eg = jnp.exp(G)
    q_dec = (q_ref[...] * (HGRN_DK ** -0.5) * eg).astype(BF)
    k_inv = (k * jnp.exp(-G)).astype(BF)
    k_state = k * jnp.exp(G_last - G)
    decay = jnp.exp(G_last)
    v = i_ref[...]
    vb = v.astype(BF)

    rr = lax.broadcasted_iota(jnp.int32, (C, C), 0)
    cc = lax.broadcasted_iota(jnp.int32, (C, C), 1)
    causal = cc <= rr

    heads = [slice(h * HGRN_DK, (h + 1) * HGRN_DK) for h in range(HGRN_HEADS)]
    sts = [st_ref[h] for h in range(HGRN_HEADS)]
    for n in range(nc):
        rs = slice(n * C, (n + 1) * C)
        qds = [q_dec[rs, hs] for hs in heads]
        As = [_dot_nt(qds[h], k_inv[rs, heads[h]]) for h in range(HGRN_HEADS)]
        kvs = [_dot(v[rs, heads[h]].T.astype(BF), k_state[rs, heads[h]].astype(BF))
               for h in range(HGRN_HEADS)]
        inter = [_dot_nt(qds[h], sts[h].astype(BF)) for h in range(HGRN_HEADS)]
        for h in range(HGRN_HEADS):
            A = jnp.where(causal, As[h], 0.0).astype(BF)
            os_ref[rs, heads[h]] = _dot(A, vb[rs, heads[h]]) + inter[h]
            sts[h] = decay[n * C:n * C + 1, heads[h]] * sts[h] + kvs[h]
    for h in range(HGRN_HEADS):
        st_ref[h] = sts[h]

    g = g_ref[...]
    gate = g * (1.0 / (1.0 + jnp.exp(-g)))
    for h in range(HGRN_HEADS):
        hs = slice(h * HGRN_DK, (h + 1) * HGRN_DK)
        o_ref[:, hs] = (_rms(os_ref[:, hs], nw_ref[...]) * gate[:, hs]).astype(BF)


def _hgrn(gh, lb3, nw, B, S, tb=256):
    nb = S // tb
    col = lambda j: (lambda b, i: (b * nb + i, j))
    return pl.pallas_call(
        functools.partial(_hgrn_kernel, tb=tb),
        grid=(B, nb),
        in_specs=[pl.BlockSpec((tb, HGRN_W), col(0)), pl.BlockSpec((tb, HGRN_W), col(1)),
                  pl.BlockSpec((tb, HGRN_W), col(2)), pl.BlockSpec((tb, HGRN_W), col(3)),
                  pl.BlockSpec((3, HGRN_W), lambda b, i: (0, 0)),
                  pl.BlockSpec((1, HGRN_DV), lambda b, i: (0, 0))],
        out_specs=pl.BlockSpec((tb, HGRN_W), col(0)),
        out_shape=jax.ShapeDtypeStruct((B * S, HGRN_W), BF),
        scratch_shapes=[pltpu.VMEM((HGRN_HEADS, HGRN_DV, HGRN_DK), F32),
                        pltpu.VMEM((tb, HGRN_W), F32)],
        compiler_params=_cparams(("parallel", "arbitrary")),
        name="hgrn",
    )(gh, gh, gh, gh, lb3, nw)


def _mla_kernel(q_ref, k_ref, vt_ref, o_ref, *, t, tk):
    i = pl.program_id(1)
    n_full = (i * t) // tk
    key = lax.broadcasted_iota(jnp.int32, (tk, t), 0)
    qry = lax.broadcasted_iota(jnp.int32, (tk, t), 1)
    causal = n_full * tk + key <= i * t + qry

    heads = [slice(h * HEAD_PAD, (h + 1) * HEAD_PAD) for h in range(MLA_HEADS)]

    def logits(j):
        ks = pl.ds(pl.multiple_of(j * tk, tk), tk)
        return tuple(_dot_nt(k_ref[ks, hs], q_ref[:, hs]) for hs in heads)

    def accumulate(j, ss, state, mask):
        if mask:
            ss = [jnp.where(causal, s, NEG) for s in ss]
        ms = [jnp.maximum(state[h][0], jnp.max(ss[h], axis=0, keepdims=True))
              for h in range(MLA_HEADS)]
        ps = [jnp.exp2(ss[h] - ms[h]).astype(BF) for h in range(MLA_HEADS)]
        return tuple((ms[h], jnp.exp2(state[h][0] - ms[h]) * state[h][1]
                      + _dot(vt_ref[j, heads[h], :], ps[h])) for h in range(MLA_HEADS))

    init = tuple((jnp.full((1, t), NEG, F32), jnp.zeros((HEAD_PAD, t), F32))
                 for _ in range(MLA_HEADS))
    state = lax.fori_loop(0, n_full, lambda j, c: accumulate(j, logits(j), c, False), init)
    state = accumulate(n_full, logits(n_full), state, True)
    outs = [acc[:MLA_V, :] * (1.0 / acc[MLA_V:MLA_V + 1, :]) for (_, acc) in state]
    o_ref[...] = jnp.concatenate(outs, axis=0).T.astype(BF)


def _mla_attn(q, k, vt, B, S, t=ATT_TILE, tk=MLA_KEY_TILE):
    nq = S // t
    nk = S // tk
    hp = MLA_HEADS * HEAD_PAD
    return pl.pallas_call(
        functools.partial(_mla_kernel, t=t, tk=tk),
        grid=(B, nq),
        in_specs=[pl.BlockSpec((t, hp), lambda b, i: (b * nq + i, 0)),
                  pl.BlockSpec((S, hp), lambda b, i: (b, 0)),
                  pl.BlockSpec((nk, hp, tk), lambda b, i: (b, 0, 0))],
        out_specs=pl.BlockSpec((t, MLA_HEADS * MLA_V), lambda b, i: (b * nq + i, 0)),
        out_shape=jax.ShapeDtypeStruct((B * S, MLA_HEADS * MLA_V), BF),
        compiler_params=_cparams(("parallel", "arbitrary")),
        name="mla_attn",
    )(q, k, vt)


BISECT_CAP = 48
END_STEPS = 4
_LAST_BUCKET_FROM = next(
    n for n in range(MAX_EXACT, 1 << 20)
    if MAX_EXACT + int(math.log(n / MAX_EXACT) / math.log(MAX_DISTANCE / MAX_EXACT)
                       * (NUM_BUCKETS - MAX_EXACT)) >= NUM_BUCKETS - 1)
assert _LAST_BUCKET_FROM <= LANES - 1 and _LAST_BUCKET_FROM <= MAX_DISTANCE


def _dsa_kernel(qi_ref, w_ref, qd_ref, pos_ref, ki_ref, kd_ref, vt_ref, tab_ref, pmin_ref, pmax_ref,
                run_ref, o_ref, sc_ref, qm_ref, bias_ref, *, t, n_sel):
    i = pl.program_id(1)
    lane = lax.broadcasted_iota(jnp.int32, (t, LANES), 1)
    lo_half = lane < DSA_HEAD_DIM
    key = lax.broadcasted_iota(jnp.int32, (t, t), 0)
    qry = lax.broadcasted_iota(jnp.int32, (t, t), 1)
    causal = key <= qry
    k_sel = float(n_sel)

    def key_rows(j):
        return pl.ds(pl.multiple_of(j * t, t), t)

    @pl.when(i + 1 < sc_ref.shape[0])
    def _():
        sc_ref[i + 1] = jnp.full((t, t), -jnp.inf, F32)

    def over_tiles(body, init):
        return lax.fori_loop(0, (i + 2) // 2, lambda jj, c: body(2 * jj + 1, body(2 * jj, c)), init)

    def fold(x, op):
        return op(x.reshape(t // SUBLANES, SUBLANES, t), axis=0)

    zero_b = jnp.zeros((t, LANES), BF)
    for p in range(IDX_HEADS // 2):
        qp = qi_ref[:, p * LANES:(p + 1) * LANES]
        qm_ref[2 * p] = jnp.where(lo_half, qp, zero_b)
        qm_ref[2 * p + 1] = jnp.where(lo_half, zero_b, qp)
    wt = w_ref[...].T * ((IDX_HEADS ** -0.5) * (IDX_DIM ** -0.5))

    def score_tile(j):
        kt = ki_ref[key_rows(j), :]
        acc = None
        for h0 in range(0, IDX_HEADS, 4):
            ss = [_dot_nt(kt, qm_ref[h]) for h in range(h0, h0 + 4)]
            for n, s in enumerate(ss):
                term = jnp.maximum(s, 0.0) * wt[h0 + n:h0 + n + 1, :]
                acc = term if acc is None else acc + term
        return acc

    def add_stats(x, c):
        mx, mn, cp, cz = c
        return (jnp.maximum(mx, fold(x, jnp.max)),
                jnp.minimum(mn, fold(jnp.where(x == -jnp.inf, jnp.inf, x), jnp.min)),
                cp + fold(jnp.where(x > 0.0, 1.0, 0.0), jnp.sum),
                cz + fold(jnp.where(x == 0.0, 1.0, 0.0), jnp.sum))

    def p1_body(j, c):
        x = score_tile(j)
        sc_ref[j] = x
        return add_stats(x, c)

    z8 = jnp.zeros((SUBLANES, t), F32)
    stats = lax.fori_loop(0, i, p1_body, (z8 - jnp.inf, z8 + jnp.inf, z8, z8))
    x_diag = jnp.where(causal, score_tile(i), -jnp.inf)
    sc_ref[i] = x_diag
    mx, mn, cp, cz = add_stats(x_diag, stats)

    def total(c):
        return jnp.sum(c, axis=0, keepdims=True)

    def count_gt(thr):
        def body(j, c):
            return c + fold(jnp.where(sc_ref[j] > thr, 1.0, 0.0), jnp.sum)
        return total(over_tiles(body, jnp.zeros((SUBLANES, t), F32)))

    row_max = jnp.max(mx, axis=0, keepdims=True)
    row_min = jnp.min(mn, axis=0, keepdims=True)
    c_pos, c_zero = total(cp), total(cz)
    n_valid = (i * t + 1 + lax.broadcasted_iota(jnp.int32, (1, t), 1)).astype(F32)

    big = n_valid > k_sel
    pos_q = jnp.logical_and(big, c_pos >= k_sel)
    tie_q = jnp.logical_and(jnp.logical_and(big, c_pos < k_sel), c_pos + c_zero >= k_sel)
    neg_q = jnp.logical_and(big, c_pos + c_zero < k_sel)
    need = jnp.where(tie_q, k_sel - c_pos, 0.0)
    lo0 = jnp.where(jnp.logical_or(pos_q, tie_q), 0.0, -jnp.inf)
    hi0 = jnp.where(neg_q, 0.0, row_max)
    c_lo0 = jnp.where(pos_q, c_pos, jnp.where(tie_q, k_sel, n_valid))

    searched = c_lo0 > k_sel

    def open_queries(c_lo):
        return jnp.max(jnp.where(c_lo > k_sel + END_STEPS, 1.0, 0.0))

    def next_above(thr):
        def body(j, m):
            x = sc_ref[j]
            return jnp.minimum(m, fold(jnp.where(x > thr, x, jnp.inf), jnp.min))
        return jnp.min(over_tiles(body, jnp.full((SUBLANES, t), jnp.inf, F32)), axis=0, keepdims=True)

    def bis_cond(c):
        return jnp.logical_and(c[0] < BISECT_CAP, c[1] > 0.0)

    def bisect_once(lo, hi, c_lo):
        active = c_lo > k_sel
        base = jnp.maximum(lo, row_min)
        mid = base + 0.5 * (hi - base)
        cnt = count_gt(mid)
        up = jnp.logical_and(active, cnt >= k_sel)
        dn = jnp.logical_and(active, cnt < k_sel)
        return jnp.where(up, mid, lo), jnp.where(dn, mid, hi), jnp.where(up, cnt, c_lo)

    def bis_body(c):
        it, _, lo, hi, c_lo = c
        lo, hi, c_lo = bisect_once(*bisect_once(lo, hi, c_lo))
        return it + 2, open_queries(c_lo), lo, hi, c_lo

    _, _, lo, _, c_lo = lax.while_loop(
        bis_cond, bis_body, (jnp.int32(0), open_queries(c_lo0), lo0, hi0, c_lo0))

    def step_body(_, c):
        lo, c_lo = c
        more = c_lo > k_sel
        return jnp.where(more, next_above(lo), lo), jnp.where(more, c_lo - 1.0, c_lo)

    lo, _ = lax.fori_loop(0, END_STEPS, step_body, (lo, c_lo))
    c_lo = jnp.where(searched, count_gt(lo), c_lo)

    still_open = c_lo != k_sel
    still_open = jnp.logical_and(searched, still_open)

    def exact_kth(_):
        def to_float(kk):
            return lax.bitcast_convert_type(jnp.where(kk < 0, kk ^ jnp.int32(0x7FFFFFFF), kk), F32)

        def count_ge(thr):
            def body(j, c):
                return c + fold(jnp.where(sc_ref[j] >= thr, 1.0, 0.0), jnp.sum)
            return total(over_tiles(body, jnp.zeros((SUBLANES, t), F32)))

        def body(b, kth):
            cand = kth + lax.shift_left(jnp.int32(1), 31 - b)
            return jnp.where(count_ge(to_float(cand)) >= k_sel, cand, kth)

        kth = to_float(lax.fori_loop(0, 32, body, jnp.full((1, t), -2 ** 31, jnp.int32)))
        return kth, count_gt(kth)

    kth, above = lax.cond(jnp.max(jnp.where(still_open, 1.0, 0.0)) > 0.0, exact_kth,
                          lambda _: (jnp.zeros((1, t), F32), jnp.zeros((1, t), F32)), 0)
    lo = jnp.where(still_open, kth, lo)
    need = jnp.where(still_open, k_sel - above, need)
    tie_val = jnp.where(still_open, kth, 0.0)

    def mask_plain(_):
        def body(j, c):
            sc_ref[j] = jnp.where(sc_ref[j] > lo, 0.0, NEG)
            return c
        return lax.fori_loop(0, i + 1, body, 0)

    def mask_ties(_):
        lower = jnp.where(qry <= key, 1.0, 0.0).astype(BF)

        def body(j, seen):
            x = sc_ref[j]
            tied = x == tie_val
            tf = jnp.where(tied, 1.0, 0.0)
            rank = _dot(lower, tf.astype(BF)) + seen
            take = jnp.logical_and(tied, rank <= need)
            sc_ref[j] = jnp.where(jnp.logical_or(x > lo, take), 0.0, NEG)
            return seen + total(fold(tf, jnp.sum))
        lax.fori_loop(0, i + 1, body, jnp.zeros((1, t), F32))
        return 0

    lax.cond(jnp.max(need) > 0.0, mask_ties, mask_plain, 0)

    qh = []
    for p in range(DSA_HEADS // 2):
        qp = qd_ref[:, p * LANES:(p + 1) * LANES]
        qh.append(jnp.where(lo_half, qp, zero_b))
        qh.append(jnp.where(lo_half, zero_b, qp))
    pq = pos_ref[pl.ds(i, 1), :]
    pq_min = pmin_ref[pl.program_id(0), i]
    far_bias = [tab_ref[NUM_BUCKETS - 1, h] * LOG2E for h in range(DSA_HEADS)]
    log_ratio = math.log(MAX_DISTANCE / MAX_EXACT)

    dist = lax.broadcasted_iota(jnp.int32, (SUBLANES, LANES), 1)
    large = MAX_EXACT + (jnp.log(jnp.maximum(dist, 1).astype(F32) / MAX_EXACT) / log_ratio
                         * (NUM_BUCKETS - MAX_EXACT)).astype(jnp.int32)
    bucket = jnp.where(dist < MAX_EXACT, dist, jnp.minimum(large, NUM_BUCKETS - 1))
    by_dist = []
    for h in range(DSA_HEADS):
        bh = jnp.full((SUBLANES, LANES), tab_ref[0, h], F32)
        for jb in range(1, NUM_BUCKETS):
            bh = jnp.where(bucket >= jb, tab_ref[jb, h], bh)
        by_dist.append(jnp.concatenate([bh * LOG2E] * (t // SUBLANES), axis=0))

    def bias_of(n):
        n = jnp.clip(n, 0, LANES - 1)
        return [jnp.concatenate([jnp.take_along_axis(by_dist[h], n[:, c * LANES:(c + 1) * LANES], axis=1)
                                 for c in range(t // LANES)], axis=1) for h in range(DSA_HEADS)]

    def pair_bias(pk_row):
        pk = jnp.broadcast_to(pk_row, (SUBLANES, t)).T[:, 0:1]
        return bias_of(pq - pk)

    b_idx = pl.program_id(0)

    @pl.when(jnp.logical_and(b_idx == 0, i == 0))
    def _():
        for gap in range(2):
            for h, tile in enumerate(bias_of(gap * t + qry - key)):
                bias_ref[gap * DSA_HEADS + h] = tile

    def attend(j, carry, bias):
        madd = sc_ref[j]
        ks = key_rows(j)
        kps = [kd_ref[ks, p * LANES:(p + 1) * LANES] for p in range(DSA_HEADS // 2)]
        ss = [_dot_nt(kps[h // 2], qh[h]) + (madd + bias[h]) for h in range(DSA_HEADS)]
        ms = [jnp.maximum(carry[h][0], jnp.max(ss[h], axis=0, keepdims=True))
              for h in range(DSA_HEADS)]
        ps = [jnp.exp2(ss[h] - ms[h]).astype(BF) for h in range(DSA_HEADS)]
        return tuple((ms[h], jnp.exp2(carry[h][0] - ms[h]) * carry[h][1]
                      + _dot(vt_ref[j, h * DSA_V_ROWS:(h + 1) * DSA_V_ROWS, :], ps[h]))
                     for h in range(DSA_HEADS))

    def p3_body(j, carry):
        gap = i - j
        far = pq_min - pmax_ref[b_idx, j] >= MAX_DISTANCE
        consecutive = jnp.logical_and(
            jnp.logical_and(run_ref[b_idx, i] == 1, run_ref[b_idx, j] == 1),
            jnp.logical_and(gap <= 1, pq_min - pmin_ref[b_idx, j] == gap * t))

        def near(c):
            return lax.cond(
                consecutive,
                lambda c: attend(j, c, [bias_ref[gap * DSA_HEADS + h] for h in range(DSA_HEADS)]),
                lambda c: attend(j, c, pair_bias(pos_ref[pl.ds(j, 1), :])), c)

        return lax.cond(far, lambda c: attend(j, c, far_bias), near, carry)

    init = tuple((jnp.full((1, t), NEG, F32), jnp.zeros((DSA_V_ROWS, t), F32))
                 for _ in range(DSA_HEADS))
    carry = lax.fori_loop(0, i + 1, p3_body, init)
    outs = [acc[:DSA_HEAD_DIM, :] * (1.0 / acc[DSA_HEAD_DIM:DSA_HEAD_DIM + 1, :])
            for (_, acc) in carry]
    o_ref[...] = jnp.concatenate(outs, axis=0).T.astype(BF)


def _dsa(gq, gk, gd, ki2, vdt, pos_tiles, tab, B, S, t=ATT_TILE):
    nq = S // t
    n_sel = min(TOPK_MAX, S // 4)
    vrows = DSA_HEADS * DSA_V_ROWS
    qrow = lambda c: (lambda b, i: (b * nq + i, c))
    return pl.pallas_call(
        functools.partial(_dsa_kernel, t=t, n_sel=n_sel),
        grid=(B, nq),
        in_specs=[pl.BlockSpec((t, W_Q), qrow(0)),
                  pl.BlockSpec((t, LANES), qrow(1)),
                  pl.BlockSpec((t, DSA_WIDTH), qrow(0)),
                  pl.BlockSpec((None, nq, t), lambda b, i: (b, 0, 0)),
                  pl.BlockSpec((S, LANES), lambda b, i: (b, 0)),
                  pl.BlockSpec((S, DSA_WIDTH), lambda b, i: (b, 1)),
                  pl.BlockSpec((nq, vrows, t), lambda b, i: (b, 0, 0)),
                  pl.BlockSpec(memory_space=pltpu.SMEM), pl.BlockSpec(memory_space=pltpu.SMEM),
                  pl.BlockSpec(memory_space=pltpu.SMEM), pl.BlockSpec(memory_space=pltpu.SMEM)],
        out_specs=pl.BlockSpec((t, DSA_WIDTH), qrow(0)),
        out_shape=jax.ShapeDtypeStruct((B * S, DSA_WIDTH), BF),
        scratch_shapes=[pltpu.VMEM((nq, t, t), F32), pltpu.VMEM((IDX_HEADS, t, LANES), BF),
                        pltpu.VMEM((2 * DSA_HEADS, t, t), F32)],
        compiler_params=_cparams(("arbitrary", "arbitrary")),
        name="dsa",
    )(gq, gk, gd, pos_tiles, ki2, gd, vdt, tab, jnp.min(pos_tiles, axis=-1), jnp.max(pos_tiles, axis=-1),
      jnp.all(pos_tiles[..., 1:] - pos_tiles[..., :-1] == 1, axis=-1).astype(jnp.int32))


def _outmlp_kernel(h_ref, yh_ref, ym_ref, yd_ref, wo_ref, nw_ref, w1_ref, w2_ref, fw_ref, o_ref,
                   *, final, ff_chunk):
    mixed = jnp.concatenate([yh_ref[...], ym_ref[...], yd_ref[...]], axis=1)
    h = h_ref[...] + _dot(mixed, wo_ref[...])
    u = _rms(h, nw_ref[...]).astype(BF)
    out = h
    for c in range(D_FF // ff_chunk):
        cs = slice(c * ff_chunk, (c + 1) * ff_chunk)
        a = jnp.maximum(_dot(u, w1_ref[:, cs]), 0.0)
        out = out + _dot((a * a).astype(BF), w2_ref[cs, :])
    if final:
        out = _rms(out, fw_ref[...])
    o_ref[...] = out


def _outmlp(h, yh, ym, yd, wo, nw, w1, w2, fw, final, tm=ROW_TILE, ff_chunk=1024):
    T = h.shape[0]
    tm = min(tm, T)
    row = lambda i: (i, 0)
    return pl.pallas_call(
        functools.partial(_outmlp_kernel, final=final, ff_chunk=ff_chunk),
        grid=(T // tm,),
        in_specs=[pl.BlockSpec((tm, D_MODEL), row), pl.BlockSpec((tm, HGRN_W), row),
                  pl.BlockSpec((tm, MLA_HEADS * MLA_V), row), pl.BlockSpec((tm, DSA_WIDTH), row),
                  _const_spec((D_MODEL, D_MODEL)), _const_spec((1, D_MODEL)),
                  _const_spec((D_MODEL, D_FF)), _const_spec((D_FF, D_MODEL)),
                  _const_spec((1, D_MODEL))],
        out_specs=pl.BlockSpec((tm, D_MODEL), row),
        out_shape=jax.ShapeDtypeStruct((T, D_MODEL), F32),
        compiler_params=_cparams(("parallel",)),
        name="outmlp",
    )(h, yh, ym, yd, wo, nw, w1, w2, fw)


def _rot_cols(w):
    half = w.shape[-1] // 2
    return jnp.concatenate([-w[..., half:], w[..., :half]], axis=-1)


def _place(w, width, off):
    pad = [(0, 0)] * (w.ndim - 1) + [(off, width - off - w.shape[-1])]
    return jnp.pad(w, pad)


def _layout_w_in(w_in):
    sizes = (HGRN_W, HGRN_W, HGRN_W, HGRN_W, MLA_Q_LORA, MLA_KV_LORA + MLA_ROPE,
             DSA_WIDTH, DSA_WIDTH, DSA_WIDTH, IDX_HEADS * IDX_DIM, IDX_DIM, IDX_HEADS)
    offs = [0]
    for s in sizes:
        offs.append(offs[-1] + s)
    hq, hf, hi, hg, mqa, mkva, dq, dk, dv, iq, ik, iw = [
        w_in[..., offs[n]:offs[n + 1]] for n in range(len(sizes))]
    ckv, kpe = mkva[..., :MLA_KV_LORA], mkva[..., MLA_KV_LORA:]
    cols = [hq, hf, hi, hg,
            _place(mqa, 256, 0), ckv, _place(kpe, LANES, MLA_NOPE), _place(_rot_cols(kpe), LANES, MLA_NOPE),
            dq * (DSA_HEAD_DIM ** -0.5 * LOG2E), dk,
            iq,
            ik, ik, _place(iw, LANES, 0)]
    w_cat = jnp.concatenate(cols, axis=-1).astype(BF)
    L = w_in.shape[0]
    dvt = jnp.swapaxes(dv, 1, 2).reshape(L, DSA_HEADS, DSA_HEAD_DIM, D_MODEL)
    dvt = jnp.pad(dvt, ((0, 0), (0, 0), (0, DSA_V_ROWS - DSA_HEAD_DIM), (0, 0)))
    return w_cat, dvt.reshape(L, DSA_HEADS * DSA_V_ROWS, D_MODEL).astype(BF)


def _layout_mla(w_qb, w_kvb):
    L = w_qb.shape[0]
    dq = MLA_NOPE + MLA_ROPE
    wq = w_qb.reshape(L, MLA_Q_LORA, MLA_HEADS, dq)
    wq_rot = jnp.concatenate([jnp.zeros_like(wq[..., :MLA_NOPE]), _rot_cols(wq[..., MLA_NOPE:])], axis=-1)
    pad_q = lambda w: jnp.pad(w, ((0, 0), (0, 256 - MLA_Q_LORA), (0, 0), (0, HEAD_PAD - dq))).reshape(
        L, 256, MLA_HEADS * HEAD_PAD).astype(BF)
    wkv = w_kvb.reshape(L, MLA_KV_LORA, MLA_HEADS, MLA_NOPE + MLA_V)
    pad_kv = lambda w: jnp.pad(w, ((0, 0), (0, 0), (0, 0), (0, HEAD_PAD - w.shape[-1]))).reshape(
        L, MLA_KV_LORA, MLA_HEADS * HEAD_PAD).astype(BF)
    wvt = jnp.swapaxes(pad_kv(wkv[..., MLA_NOPE:]), 1, 2)
    return pad_q(wq), pad_q(wq_rot), pad_kv(wkv[..., :MLA_NOPE]), wvt


def kernel(x, positions, attn_norm_w, w_in, hgrn_lb_logits, hgrn_norm_w, mla_q_norm_w, mla_w_qb,
           mla_kv_norm_w, mla_w_kvb, idx_k_norm_w, idx_k_norm_b, rel_bias_table, w_out,
           mlp_norm_w, w_mlp_in, w_mlp_out, final_norm_w):
    B, S, _ = x.shape
    T = B * S
    depth = w_in.shape[0]

    inv_freq = 1.0 / (ROPE_THETA ** (jnp.arange(0, MLA_ROPE, 2, dtype=F32) / MLA_ROPE))
    ang = positions.astype(F32)[..., None] * inv_freq
    cos, sin = jnp.cos(ang).reshape(T, -1), jnp.sin(ang).reshape(T, -1)
    cos_t = jnp.concatenate([jnp.ones((T, MLA_NOPE), F32), cos, cos,
                             jnp.zeros((T, HEAD_PAD - MLA_NOPE - MLA_ROPE), F32)], axis=1)
    sin_t = jnp.concatenate([jnp.zeros((T, MLA_NOPE), F32), sin, sin,
                             jnp.zeros((T, HEAD_PAD - MLA_NOPE - MLA_ROPE), F32)], axis=1)
    pos_tiles = positions.reshape(B, S // ATT_TILE, ATT_TILE)

    lb = jnp.cumsum(jax.nn.softmax(hgrn_lb_logits.astype(F32), axis=0), axis=0)
    lb = lb - lb[0:1]
    lb3 = jnp.stack([jnp.log(lb), jnp.log1p(-lb), 1.0 - lb], axis=1)

    w_cat, w_dvt = _layout_w_in(w_in)
    wq, wqr, wk, wvt = _layout_mla(mla_w_qb, mla_w_kvb)
    qnw = jnp.pad(mla_q_norm_w, ((0, 0), (0, 256 - MLA_Q_LORA)))
    lnw = jnp.concatenate([idx_k_norm_w, idx_k_norm_w], axis=-1)
    lnb = jnp.concatenate([idx_k_norm_b, idx_k_norm_b], axis=-1)
    wo = w_out.astype(BF)
    w1 = w_mlp_in.astype(BF)
    w2 = w_mlp_out.astype(BF)
    tab = rel_bias_table.astype(F32)

    h = x.reshape(T, D_MODEL)
    for l in range(depth):
        gh, gm, gd, gq, gk, vdt = _inproj(h, attn_norm_w[l][None], w_cat[l], w_dvt[l])
        q_m, k_m, v_m, ki2 = _prep(gm, gk, cos_t, sin_t, qnw[l][None], wq[l], wqr[l],
                                   mla_kv_norm_w[l][None], wk[l], wvt[l], lnw[l][None], lnb[l][None])
        y_h = _hgrn(gh, lb3[l], hgrn_norm_w[l][None], B, S)
        y_m = _mla_attn(q_m, k_m, v_m, B, S)
        y_d = _dsa(gq, gk, gd, ki2, vdt, pos_tiles, tab, B, S)
        h = _outmlp(h, y_h, y_m, y_d, wo[l], mlp_norm_w[l][None], w1[l], w2[l],
                    final_norm_w[None], final=(l == depth - 1))
    return h.reshape(B, S, D_MODEL)
```

```python
import functools
import math

import jax
import jax.numpy as jnp
from jax import lax
from jax.experimental import pallas as pl
from jax.experimental.pallas import tpu as pltpu

D_MODEL = 1024
HGRN_HEADS = 4
HGRN_DK = 128
HGRN_DV = 128
HGRN_W = HGRN_HEADS * HGRN_DK
HGRN_CHUNK = 32
MLA_HEADS = 4
MLA_NOPE = 64
MLA_ROPE = 32
MLA_V = 64
MLA_Q_LORA = 192
MLA_KV_LORA = 128
ROPE_THETA = 10000.0
DSA_HEADS = 4
DSA_HEAD_DIM = 64
DSA_WIDTH = DSA_HEADS * DSA_HEAD_DIM
IDX_HEADS = 8
IDX_DIM = 64
TOPK_MAX = 256
NUM_BUCKETS = 32
MAX_EXACT = NUM_BUCKETS // 2
MAX_DISTANCE = 128
D_FF = 4 * D_MODEL
EPS = 1e-6

LANES = 128
SUBLANES = 8
BF16_ROWS = 16
HEAD_PAD = 128
NEG = -1e30
ATT_TILE = 256
MLA_KEY_TILE = 512
ROW_TILE = 512
VMEM_LIMIT = 52 * 1024 * 1024
LOG2E = math.log2(math.e)

DSA_V_ROWS = DSA_HEAD_DIM + BF16_ROWS

C_H = 0
C_M = C_H + 4 * HGRN_W
W_M = 640
C_D = C_M + W_M
W_D = 2 * DSA_WIDTH
C_Q = C_D + W_D
W_Q = IDX_HEADS * IDX_DIM
C_K = C_Q + W_Q
W_K = 256
N_PAD = C_K + W_K

BF = jnp.bfloat16
F32 = jnp.float32


def _dot(a, b):
    return jnp.dot(a, b, preferred_element_type=F32)


def _dot_nt(a, b):
    return lax.dot_general(a, b, (((1,), (1,)), ((), ())), preferred_element_type=F32)


def _rms(x, w, n=None):
    n = x.shape[-1] if n is None else n
    ms = jnp.sum(x * x, axis=-1, keepdims=True) * (1.0 / n)
    return x * lax.rsqrt(ms + EPS) * w


def _cparams(sem):
    return pltpu.CompilerParams(dimension_semantics=sem, vmem_limit_bytes=VMEM_LIMIT)


def _const_spec(shape):
    nd = len(shape)
    return pl.BlockSpec(shape, lambda *_: (0,) * nd, pipeline_mode=pl.Buffered(1))


def _inproj_kernel(x_ref, nw_ref, w_ref, wvt_ref, oh_ref, om_ref, od_ref, oq_ref, ok_ref, ovt_ref):
    u = _rms(x_ref[...], nw_ref[...]).astype(BF)
    oh_ref[...] = _dot(u, w_ref[:, C_H:C_M])
    om_ref[...] = _dot(u, w_ref[:, C_M:C_D])
    od_ref[...] = _dot(u, w_ref[:, C_D:C_Q]).astype(BF)
    oq_ref[...] = _dot(u, w_ref[:, C_Q:C_K]).astype(BF)
    ok_ref[...] = _dot(u, w_ref[:, C_K:N_PAD])
    vt = _dot_nt(wvt_ref[...], u)
    row = lax.broadcasted_iota(jnp.int32, vt.shape, 0)
    vt = jnp.where(row % DSA_V_ROWS == DSA_HEAD_DIM, 1.0, vt).astype(BF)
    for n in range(ovt_ref.shape[0]):
        ovt_ref[n] = vt[:, n * ATT_TILE:(n + 1) * ATT_TILE]


def _inproj(h, nw, w, wvt, tm=ROW_TILE):
    T = h.shape[0]
    tm = min(tm, T)
    row = lambda i: (i, 0)
    vrows = DSA_HEADS * DSA_V_ROWS
    per = tm // ATT_TILE
    return pl.pallas_call(
        _inproj_kernel,
        grid=(T // tm,),
        in_specs=[pl.BlockSpec((tm, D_MODEL), row), _const_spec((1, D_MODEL)),
                  _const_spec((D_MODEL, N_PAD)), _const_spec((vrows, D_MODEL))],
        out_specs=[pl.BlockSpec((tm, 4 * HGRN_W), row), pl.BlockSpec((tm, W_M), row),
                   pl.BlockSpec((tm, W_D), row), pl.BlockSpec((tm, W_Q), row),
                   pl.BlockSpec((tm, W_K), row),
                   pl.BlockSpec((per, vrows, ATT_TILE), lambda i: (i, 0, 0))],
        out_shape=[jax.ShapeDtypeStruct((T, 4 * HGRN_W), F32), jax.ShapeDtypeStruct((T, W_M), F32),
                   jax.ShapeDtypeStruct((T, W_D), BF), jax.ShapeDtypeStruct((T, W_Q), BF),
                   jax.ShapeDtypeStruct((T, W_K), F32),
                   jax.ShapeDtypeStruct((T // ATT_TILE, vrows, ATT_TILE), BF)],
        compiler_params=_cparams(("parallel",)),
        name="inproj",
    )(h, nw, w, wvt)


def _prep_kernel(gm_ref, gk_ref, cos_ref, sin_ref, qnw_ref, wq_ref, wqr_ref, kvnw_ref, wk_ref,
                 wvt_ref, lnw_ref, lnb_ref, q_ref, k_ref, vt_ref, ki_ref):
    cs = cos_ref[...]
    sn = sin_ref[...]
    cs4 = jnp.concatenate([cs] * MLA_HEADS, axis=1)
    sn4 = jnp.concatenate([sn] * MLA_HEADS, axis=1)
    qn = _rms(gm_ref[:, 0:256], qnw_ref[...], n=MLA_Q_LORA).astype(BF)
    scale = (MLA_NOPE + MLA_ROPE) ** -0.5 * LOG2E
    q = (_dot(qn, wq_ref[...]) * cs4 + _dot(qn, wqr_ref[...]) * sn4) * scale
    q_ref[...] = q.astype(BF)
    cn = _rms(gm_ref[:, 256:384], kvnw_ref[...]).astype(BF)
    kp = gm_ref[:, 384:512] * cs + gm_ref[:, 512:640] * sn
    k = _dot(cn, wk_ref[...]) + jnp.concatenate([kp] * MLA_HEADS, axis=1)
    k_ref[...] = k.astype(BF)
    vt = _dot_nt(wvt_ref[...], cn)
    row = lax.broadcasted_iota(jnp.int32, vt.shape, 0)
    vt_ref[...] = jnp.where(row % HEAD_PAD == MLA_V, 1.0, vt).astype(BF)
    x = gk_ref[:, 0:LANES]
    first = lax.broadcasted_iota(jnp.int32, x.shape, 1) < IDX_DIM
    mu = jnp.sum(jnp.where(first, x, 0.0), axis=-1, keepdims=True) * (1.0 / IDX_DIM)
    xc = x - mu
    var = jnp.sum(jnp.where(first, xc * xc, 0.0), axis=-1, keepdims=True) * (1.0 / IDX_DIM)
    ki_ref[...] = (xc * lax.rsqrt(var + EPS) * lnw_ref[...] + lnb_ref[...]).astype(BF)


def _prep(gm, gk, cos_t, sin_t, qnw, wq, wqr, kvnw, wk, wvt, lnw, lnb, tm=MLA_KEY_TILE):
    T = gm.shape[0]
    row = lambda i: (i, 0)
    hp = MLA_HEADS * HEAD_PAD
    return pl.pallas_call(
        _prep_kernel,
        grid=(T // tm,),
        in_specs=[pl.BlockSpec((tm, W_M), row), pl.BlockSpec((tm, W_K), row),
                  pl.BlockSpec((tm, LANES), row), pl.BlockSpec((tm, LANES), row),
                  _const_spec((1, 256)), _const_spec((256, hp)), _const_spec((256, hp)),
                  _const_spec((1, MLA_KV_LORA)), _const_spec((MLA_KV_LORA, hp)),
                  _const_spec((hp, MLA_KV_LORA)), _const_spec((1, LANES)), _const_spec((1, LANES))],
        out_specs=[pl.BlockSpec((tm, hp), row), pl.BlockSpec((tm, hp), row),
                   pl.BlockSpec((None, hp, tm), lambda i: (i, 0, 0)), pl.BlockSpec((tm, LANES), row)],
        out_shape=[jax.ShapeDtypeStruct((T, hp), BF), jax.ShapeDtypeStruct((T, hp), BF),
                   jax.ShapeDtypeStruct((T // tm, hp, tm), BF), jax.ShapeDtypeStruct((T, LANES), BF)],
        compiler_params=_cparams(("parallel",)),
        name="prep",
    )(gm, gk, cos_t, sin_t, qnw, wq, wqr, kvnw, wk, wvt, lnw, lnb)


def _split3(x):
    a = x.astype(BF)
    r = x - a.astype(F32)
    b = r.astype(BF)
    c = (r - b.astype(F32)).astype(BF)
    return a, b, c


def _hgrn_kernel(q_ref, f_ref, i_ref, g_ref, lb_ref, nw_ref, o_ref, st_ref, os_ref, *, tb):
    C = HGRN_CHUNK
    nc = tb // C

    @pl.when(pl.program_id(1) == 0)
    def _():
        st_ref[...] = jnp.zeros_like(st_ref)

    fp = f_ref[...]
    log_lb = lb_ref[0:1, :]
    log1m_lb = lb_ref[1:2, :]
    one_m_lb = lb_ref[2:3, :]
    ls = jnp.minimum(fp, 0.0) - jnp.log1p(jnp.exp(-jnp.abs(fp)))
    b = log1m_lb + ls
    log_f = jnp.maximum(log_lb, b) + jnp.log1p(jnp.exp(-jnp.abs(log_lb - b)))
    k = one_m_lb * (1.0 / (1.0 + jnp.exp(fp)))

    r = lax.broadcasted_iota(jnp.int32, (tb, tb), 0)
    c = lax.broadcasted_iota(jnp.int32, (tb, tb), 1)
    same = (r // C) == (c // C)
    tri = jnp.where(same & (c <= r), 1.0, 0.0).astype(BF)
    blk = jnp.where(same, 1.0, 0.0).astype(BF)
    a0, a1, a2 = _split3(log_f)
    G = _dot(tri, a0) + _dot(tri, a1) + _dot(tri, a2)
    G_last = _dot(blk, a0) + _dot(blk, a1) + _dot(blk, a2)

    eg = jnp.exp(G)
    q_dec = (q_ref[...] * (HGRN_DK ** -0.5) * eg).astype(BF)
    k_inv = (k * jnp.exp(-G)).astype(BF)
    k_state = k * jnp.exp(G_last - G)
    decay = jnp.exp(G_last)
    v = i_ref[...]
    vb = v.astype(BF)

    rr = lax.broadcasted_iota(jnp.int32, (C, C), 0)
    cc = lax.broadcasted_iota(jnp.int32, (C, C), 1)
    causal = cc <= rr

    heads = [slice(h * HGRN_DK, (h + 1) * HGRN_DK) for h in range(HGRN_HEADS)]
    sts = [st_ref[h] for h in range(HGRN_HEADS)]
    for n in range(nc):
        rs = slice(n * C, (n + 1) * C)
        qds = [q_dec[rs, hs] for hs in heads]
        As = [_dot_nt(qds[h], k_inv[rs, heads[h]]) for h in range(HGRN_HEADS)]
        kvs = [_dot(v[rs, heads[h]].T.astype(BF), k_state[rs, heads[h]].astype(BF))
               for h in range(HGRN_HEADS)]
        inter = [_dot_nt(qds[h], sts[h].astype(BF)) for h in range(HGRN_HEADS)]
        for h in range(HGRN_HEADS):
            A = jnp.where(causal, As[h], 0.0).astype(BF)
            os_ref[rs, heads[h]] = _dot(A, vb[rs, heads[h]]) + inter[h]
            sts[h] = decay[n * C:n * C + 1, heads[h]] * sts[h] + kvs[h]
    for h in range(HGRN_HEADS):
        st_ref[h] = sts[h]

    g = g_ref[...]
    gate = g * (1.0 / (1.0 + jnp.exp(-g)))
    for h in range(HGRN_HEADS):
        hs = slice(h * HGRN_DK, (h + 1) * HGRN_DK)
        o_ref[:, hs] = (_rms(os_ref[:, hs], nw_ref[...]) * gate[:, hs]).astype(BF)


def _hgrn(gh, lb3, nw, B, S, tb=256):
    nb = S // tb
    col = lambda j: (lambda b, i: (b * nb + i, j))
    return pl.pallas_call(
        functools.partial(_hgrn_kernel, tb=tb),
        grid=(B, nb),
        in_specs=[pl.BlockSpec((tb, HGRN_W), col(0)), pl.BlockSpec((tb, HGRN_W), col(1)),
                  pl.BlockSpec((tb, HGRN_W), col(2)), pl.BlockSpec((tb, HGRN_W), col(3)),
                  pl.BlockSpec((3, HGRN_W), lambda b, i: (0, 0)),
                  pl.BlockSpec((1, HGRN_DV), lambda b, i: (0, 0))],
        out_specs=pl.BlockSpec((tb, HGRN_W), col(0)),
        out_shape=jax.ShapeDtypeStruct((B * S, HGRN_W), BF),
        scratch_shapes=[pltpu.VMEM((HGRN_HEADS, HGRN_DV, HGRN_DK), F32),
                        pltpu.VMEM((tb, HGRN_W), F32)],
        compiler_params=_cparams(("parallel", "arbitrary")),
        name="hgrn",
    )(gh, gh, gh, gh, lb3, nw)


def _mla_kernel(q_ref, k_ref, vt_ref, o_ref, *, t, tk):
    i = pl.program_id(1)
    n_full = (i * t) // tk
    key = lax.broadcasted_iota(jnp.int32, (tk, t), 0)
    qry = lax.broadcasted_iota(jnp.int32, (tk, t), 1)
    causal = n_full * tk + key <= i * t + qry

    heads = [slice(h * HEAD_PAD, (h + 1) * HEAD_PAD) for h in range(MLA_HEADS)]

    def logits(j):
        ks = pl.ds(pl.multiple_of(j * tk, tk), tk)
        return tuple(_dot_nt(k_ref[ks, hs], q_ref[:, hs]) for hs in heads)

    def accumulate(j, ss, state, mask):
        if mask:
            ss = [jnp.where(causal, s, NEG) for s in ss]
        ms = [jnp.maximum(state[h][0], jnp.max(ss[h], axis=0, keepdims=True))
              for h in range(MLA_HEADS)]
        ps = [jnp.exp2(ss[h] - ms[h]).astype(BF) for h in range(MLA_HEADS)]
        return tuple((ms[h], jnp.exp2(state[h][0] - ms[h]) * state[h][1]
                      + _dot(vt_ref[j, heads[h], :], ps[h])) for h in range(MLA_HEADS))

    init = tuple((jnp.full((1, t), NEG, F32), jnp.zeros((HEAD_PAD, t), F32))
                 for _ in range(MLA_HEADS))
    state = lax.fori_loop(0, n_full, lambda j, c: accumulate(j, logits(j), c, False), init)
    state = accumulate(n_full, logits(n_full), state, True)
    outs = [acc[:MLA_V, :] * (1.0 / acc[MLA_V:MLA_V + 1, :]) for (_, acc) in state]
    o_ref[...] = jnp.concatenate(outs, axis=0).T.astype(BF)


def _mla_attn(q, k, vt, B, S, t=ATT_TILE, tk=MLA_KEY_TILE):
    nq = S // t
    nk = S // tk
    hp = MLA_HEADS * HEAD_PAD
    return pl.pallas_call(
        functools.partial(_mla_kernel, t=t, tk=tk),
        grid=(B, nq),
        in_specs=[pl.BlockSpec((t, hp), lambda b, i: (b * nq + i, 0)),
                  pl.BlockSpec((S, hp), lambda b, i: (b, 0)),
                  pl.BlockSpec((nk, hp, tk), lambda b, i: (b, 0, 0))],
        out_specs=pl.BlockSpec((t, MLA_HEADS * MLA_V), lambda b, i: (b * nq + i, 0)),
        out_shape=jax.ShapeDtypeStruct((B * S, MLA_HEADS * MLA_V), BF),
        compiler_params=_cparams(("parallel", "arbitrary")),
        name="mla_attn",
    )(q, k, vt)


BISECT_CAP = 48
END_STEPS = 4
_LAST_BUCKET_FROM = next(
    n for n in range(MAX_EXACT, 1 << 20)
    if MAX_EXACT + int(math.log(n / MAX_EXACT) / math.log(MAX_DISTANCE / MAX_EXACT)
                       * (NUM_BUCKETS - MAX_EXACT)) >= NUM_BUCKETS - 1)
assert _LAST_BUCKET_FROM <= LANES - 1 and _LAST_BUCKET_FROM <= MAX_DISTANCE


def _dsa_kernel(qi_ref, w_ref, qd_ref, pos_ref, ki_ref, kd_ref, vt_ref, tab_ref, pmin_ref, pmax_ref,
                run_ref, o_ref, sc_ref, qm_ref, bias_ref, *, t, n_sel):
    i = pl.program_id(1)
    lane = lax.broadcasted_iota(jnp.int32, (t, LANES), 1)
    lo_half = lane < DSA_HEAD_DIM
    key = lax.broadcasted_iota(jnp.int32, (t, t), 0)
    qry = lax.broadcasted_iota(jnp.int32, (t, t), 1)
    causal = key <= qry
    k_sel = float(n_sel)

    def key_rows(j):
        return pl.ds(pl.multiple_of(j * t, t), t)

    @pl.when(i + 1 < sc_ref.shape[0])
    def _():
        sc_ref[i + 1] = jnp.full((t, t), -jnp.inf, F32)

    def over_tiles(body, init):
        return lax.fori_loop(0, (i + 2) // 2, lambda jj, c: body(2 * jj + 1, body(2 * jj, c)), init)

    def fold(x, op):
        return op(x.reshape(t // SUBLANES, SUBLANES, t), axis=0)

    zero_b = jnp.zeros((t, LANES), BF)
    for p in range(IDX_HEADS // 2):
        qp = qi_ref[:, p * LANES:(p + 1) * LANES]
        qm_ref[2 * p] = jnp.where(lo_half, qp, zero_b)
        qm_ref[2 * p + 1] = jnp.where(lo_half, zero_b, qp)
    wt = w_ref[...].T * ((IDX_HEADS ** -0.5) * (IDX_DIM ** -0.5))

    def score_tile(j):
        kt = ki_ref[key_rows(j), :]
        acc = None
        for h0 in range(0, IDX_HEADS, 4):
            ss = [_dot_nt(kt, qm_ref[h]) for h in range(h0, h0 + 4)]
            for n, s in enumerate(ss):
                term = jnp.maximum(s, 0.0) * wt[h0 + n:h0 + n + 1, :]
                acc = term if acc is None else acc + term
        return acc

    def add_stats(x, c):
        mx, mn, cp, cz = c
        return (jnp.maximum(mx, fold(x, jnp.max)),
                jnp.minimum(mn, fold(jnp.where(x == -jnp.inf, jnp.inf, x), jnp.min)),
                cp + fold(jnp.where(x > 0.0, 1.0, 0.0), jnp.sum),
                cz + fold(jnp.where(x == 0.0, 1.0, 0.0), jnp.sum))

    def p1_body(j, c):
        x = score_tile(j)
        sc_ref[j] = x
        return add_stats(x, c)

    z8 = jnp.zeros((SUBLANES, t), F32)
    stats = lax.fori_loop(0, i, p1_body, (z8 - jnp.inf, z8 + jnp.inf, z8, z8))
    x_diag = jnp.where(causal, score_tile(i), -jnp.inf)
    sc_ref[i] = x_diag
    mx, mn, cp, cz = add_stats(x_diag, stats)

    def total(c):
        return jnp.sum(c, axis=0, keepdims=True)

    def count_gt(thr):
        def body(j, c):
            return c + fold(jnp.where(sc_ref[j] > thr, 1.0, 0.0), jnp.sum)
        return total(over_tiles(body, jnp.zeros((SUBLANES, t), F32)))

    row_max = jnp.max(mx, axis=0, keepdims=True)
    row_min = jnp.min(mn, axis=0, keepdims=True)
    c_pos, c_zero = total(cp), total(cz)
    n_valid = (i * t + 1 + lax.broadcasted_iota(jnp.int32, (1, t), 1)).astype(F32)

    big = n_valid > k_sel
    pos_q = jnp.logical_and(big, c_pos >= k_sel)
    tie_q = jnp.logical_and(jnp.logical_and(big, c_pos < k_sel), c_pos + c_zero >= k_sel)
    neg_q = jnp.logical_and(big, c_pos + c_zero < k_sel)
    need = jnp.where(tie_q, k_sel - c_pos, 0.0)
    lo0 = jnp.where(jnp.logical_or(pos_q, tie_q), 0.0, -jnp.inf)
    hi0 = jnp.where(neg_q, 0.0, row_max)
    c_lo0 = jnp.where(pos_q, c_pos, jnp.where(tie_q, k_sel, n_valid))

    searched = c_lo0 > k_sel

    def open_queries(c_lo):
        return jnp.max(jnp.where(c_lo > k_sel + END_STEPS, 1.0, 0.0))

    def next_above(thr):
        def body(j, m):
            x = sc_ref[j]
            return jnp.minimum(m, fold(jnp.where(x > thr, x, jnp.inf), jnp.min))
        return jnp.min(over_tiles(body, jnp.full((SUBLANES, t), jnp.inf, F32)), axis=0, keepdims=True)

    def bis_cond(c):
        return jnp.logical_and(c[0] < BISECT_CAP, c[1] > 0.0)

    def bisect_once(lo, hi, c_lo):
        active = c_lo > k_sel
        base = jnp.maximum(lo, row_min)
        mid = base + 0.5 * (hi - base)
        cnt = count_gt(mid)
        up = jnp.logical_and(active, cnt >= k_sel)
        dn = jnp.logical_and(active, cnt < k_sel)
        return jnp.where(up, mid, lo), jnp.where(dn, mid, hi), jnp.where(up, cnt, c_lo)

    def bis_body(c):
        it, _, lo, hi, c_lo = c
        lo, hi, c_lo = bisect_once(*bisect_once(lo, hi, c_lo))
        return it + 2, open_queries(c_lo), lo, hi, c_lo

    _, _, lo, _, c_lo = lax.while_loop(
        bis_cond, bis_body, (jnp.int32(0), open_queries(c_lo0), lo0, hi0, c_lo0))

    def step_body(_, c):
        lo, c_lo = c
        more = c_lo > k_sel
        return jnp.where(more, next_above(lo), lo), jnp.where(more, c_lo - 1.0, c_lo)

    lo, _ = lax.fori_loop(0, END_STEPS, step_body, (lo, c_lo))
    c_lo = jnp.where(searched, count_gt(lo), c_lo)

    still_open = c_lo != k_sel
    still_open = jnp.logical_and(searched, still_open)

    def exact_kth(_):
        def to_float(kk):
            return lax.bitcast_convert_type(jnp.where(kk < 0, kk ^ jnp.int32(0x7FFFFFFF), kk), F32)

        def count_ge(thr):
            def body(j, c):
                return c + fold(jnp.where(sc_ref[j] >= thr, 1.0, 0.0), jnp.sum)
            return total(over_tiles(body, jnp.zeros((SUBLANES, t), F32)))

        def body(b, kth):
            cand = kth + lax.shift_left(jnp.int32(1), 31 - b)
            return jnp.where(count_ge(to_float(cand)) >= k_sel, cand, kth)

        kth = to_float(lax.fori_loop(0, 32, body, jnp.full((1, t), -2 ** 31, jnp.int32)))
        return kth, count_gt(kth)

    kth, above = lax.cond(jnp.max(jnp.where(still_open, 1.0, 0.0)) > 0.0, exact_kth,
                          lambda _: (jnp.zeros((1, t), F32), jnp.zeros((1, t), F32)), 0)
    lo = jnp.where(still_open, kth, lo)
    need = jnp.where(still_open, k_sel - above, need)
    tie_val = jnp.where(still_open, kth, 0.0)

    def mask_plain(_):
        def body(j, c):
            sc_ref[j] = jnp.where(sc_ref[j] > lo, 0.0, NEG)
            return c
        return lax.fori_loop(0, i + 1, body, 0)

    def mask_ties(_):
        lower = jnp.where(qry <= key, 1.0, 0.0).astype(BF)

        def body(jj, seen):
            xs = [sc_ref[2 * jj + n] for n in range(2)]
            tied = [x == tie_val for x in xs]
            tfs = [jnp.where(m, 1.0, 0.0) for m in tied]
            ranks = [_dot(lower, tf.astype(BF)) for tf in tfs]
            for n in range(2):
                take = jnp.logical_and(tied[n], ranks[n] + seen <= need)
                sc_ref[2 * jj + n] = jnp.where(jnp.logical_or(xs[n] > lo, take), 0.0, NEG)
                seen = seen + total(fold(tfs[n], jnp.sum))
            return seen
        lax.fori_loop(0, (i + 2) // 2, body, jnp.zeros((1, t), F32))
        return 0

    lax.cond(jnp.max(need) > 0.0, mask_ties, mask_plain, 0)

    qh = []
    for p in range(DSA_HEADS // 2):
        qp = qd_ref[:, p * LANES:(p + 1) * LANES]
        qh.append(jnp.where(lo_half, qp, zero_b))
        qh.append(jnp.where(lo_half, zero_b, qp))
    pq = pos_ref[pl.ds(i, 1), :]
    pq_min = pmin_ref[pl.program_id(0), i]
    far_bias = [tab_ref[NUM_BUCKETS - 1, h] * LOG2E for h in range(DSA_HEADS)]
    log_ratio = math.log(MAX_DISTANCE / MAX_EXACT)

    dist = lax.broadcasted_iota(jnp.int32, (SUBLANES, LANES), 1)
    large = MAX_EXACT + (jnp.log(jnp.maximum(dist, 1).astype(F32) / MAX_EXACT) / log_ratio
                         * (NUM_BUCKETS - MAX_EXACT)).astype(jnp.int32)
    bucket = jnp.where(dist < MAX_EXACT, dist, jnp.minimum(large, NUM_BUCKETS - 1))
    by_dist = []
    for h in range(DSA_HEADS):
        bh = jnp.full((SUBLANES, LANES), tab_ref[0, h], F32)
        for jb in range(1, NUM_BUCKETS):
            bh = jnp.where(bucket >= jb, tab_ref[jb, h], bh)
        by_dist.append(jnp.concatenate([bh * LOG2E] * (t // SUBLANES), axis=0))

    def bias_of(n):
        n = jnp.clip(n, 0, LANES - 1)
        return [jnp.concatenate([jnp.take_along_axis(by_dist[h], n[:, c * LANES:(c + 1) * LANES], axis=1)
                                 for c in range(t // LANES)], axis=1) for h in range(DSA_HEADS)]

    def pair_bias(pk_row):
        pk = jnp.broadcast_to(pk_row, (SUBLANES, t)).T[:, 0:1]
        return bias_of(pq - pk)

    b_idx = pl.program_id(0)

    @pl.when(jnp.logical_and(b_idx == 0, i == 0))
    def _():
        for gap in range(2):
            for h, tile in enumerate(bias_of(gap * t + qry - key)):
                bias_ref[gap * DSA_HEADS + h] = tile

    def attend(tiles, carry, bias):
        ss = []
        for j in tiles:
            madd = sc_ref[j]
            ks = key_rows(j)
            kps = [kd_ref[ks, p * LANES:(p + 1) * LANES] for p in range(DSA_HEADS // 2)]
            ss.append([_dot_nt(kps[h // 2], qh[h]) + (madd + bias[h]) for h in range(DSA_HEADS)])
        ms = []
        for h in range(DSA_HEADS):
            m = carry[h][0]
            for s in ss:
                m = jnp.maximum(m, jnp.max(s[h], axis=0, keepdims=True))
            ms.append(m)
        out = []
        for h in range(DSA_HEADS):
            acc = jnp.exp2(carry[h][0] - ms[h]) * carry[h][1]
            for j, s in zip(tiles, ss):
                p = jnp.exp2(s[h] - ms[h]).astype(BF)
                acc = acc + _dot(vt_ref[j, h * DSA_V_ROWS:(h + 1) * DSA_V_ROWS, :], p)
            out.append((ms[h], acc))
        return tuple(out)

    def is_far(j):
        return pq_min - pmax_ref[b_idx, j] >= MAX_DISTANCE

    def p3_body(j, carry):
        gap = i - j
        consecutive = jnp.logical_and(
            jnp.logical_and(run_ref[b_idx, i] == 1, run_ref[b_idx, j] == 1),
            jnp.logical_and(gap <= 1, pq_min - pmin_ref[b_idx, j] == gap * t))

        def near(c):
            return lax.cond(
                consecutive,
                lambda c: attend([j], c, [bias_ref[gap * DSA_HEADS + h] for h in range(DSA_HEADS)]),
                lambda c: attend([j], c, pair_bias(pos_ref[pl.ds(j, 1), :])), c)

        return lax.cond(is_far(j), lambda c: attend([j], c, far_bias), near, carry)

    def p3_pair(jj, carry):
        j0 = 2 * jj
        n_here = jnp.minimum(i + 1 - j0, 2)
        both_far = jnp.logical_and(n_here == 2,
                                   jnp.logical_and(is_far(j0), is_far(jnp.minimum(j0 + 1, i))))
        return lax.cond(both_far,
                        lambda c: attend([j0, j0 + 1], c, far_bias),
                        lambda c: lax.fori_loop(0, n_here, lambda n, c: p3_body(j0 + n, c), c),
                        carry)

    init = tuple((jnp.full((1, t), NEG, F32), jnp.zeros((DSA_V_ROWS, t), F32))
                 for _ in range(DSA_HEADS))
    carry = lax.fori_loop(0, (i + 2) // 2, p3_pair, init)
    outs = [acc[:DSA_HEAD_DIM, :] * (1.0 / acc[DSA_HEAD_DIM:DSA_HEAD_DIM + 1, :])
            for (_, acc) in carry]
    o_ref[...] = jnp.concatenate(outs, axis=0).T.astype(BF)


def _dsa(gq, gk, gd, ki2, vdt, pos_tiles, tab, B, S, t=ATT_TILE):
    nq = S // t
    n_sel = min(TOPK_MAX, S // 4)
    vrows = DSA_HEADS * DSA_V_ROWS
    qrow = lambda c: (lambda b, i: (b * nq + i, c))
    return pl.pallas_call(
        functools.partial(_dsa_kernel, t=t, n_sel=n_sel),
        grid=(B, nq),
        in_specs=[pl.BlockSpec((t, W_Q), qrow(0)),
                  pl.BlockSpec((t, LANES), qrow(1)),
                  pl.BlockSpec((t, DSA_WIDTH), qrow(0)),
                  pl.BlockSpec((None, nq, t), lambda b, i: (b, 0, 0)),
                  pl.BlockSpec((S, LANES), lambda b, i: (b, 0)),
                  pl.BlockSpec((S, DSA_WIDTH), lambda b, i: (b, 1)),
                  pl.BlockSpec((nq, vrows, t), lambda b, i: (b, 0, 0)),
                  pl.BlockSpec(memory_space=pltpu.SMEM), pl.BlockSpec(memory_space=pltpu.SMEM),
                  pl.BlockSpec(memory_space=pltpu.SMEM), pl.BlockSpec(memory_space=pltpu.SMEM)],
        out_specs=pl.BlockSpec((t, DSA_WIDTH), qrow(0)),
        out_shape=jax.ShapeDtypeStruct((B * S, DSA_WIDTH), BF),
        scratch_shapes=[pltpu.VMEM((nq, t, t), F32), pltpu.VMEM((IDX_HEADS, t, LANES), BF),
                        pltpu.VMEM((2 * DSA_HEADS, t, t), F32)],
        compiler_params=_cparams(("arbitrary", "arbitrary")),
        name="dsa",
    )(gq, gk, gd, pos_tiles, ki2, gd, vdt, tab, jnp.min(pos_tiles, axis=-1), jnp.max(pos_tiles, axis=-1),
      jnp.all(pos_tiles[..., 1:] - pos_tiles[..., :-1] == 1, axis=-1).astype(jnp.int32))


def _outmlp_kernel(h_ref, yh_ref, ym_ref, yd_ref, wo_ref, nw_ref, w1_ref, w2_ref, fw_ref, o_ref,
                   *, final, ff_chunk):
    mixed = jnp.concatenate([yh_ref[...], ym_ref[...], yd_ref[...]], axis=1)
    h = h_ref[...] + _dot(mixed, wo_ref[...])
    u = _rms(h, nw_ref[...]).astype(BF)
    out = h
    for c in range(D_FF // ff_chunk):
        cs = slice(c * ff_chunk, (c + 1) * ff_chunk)
        a = jnp.maximum(_dot(u, w1_ref[:, cs]), 0.0)
        out = out + _dot((a * a).astype(BF), w2_ref[cs, :])
    if final:
        out = _rms(out, fw_ref[...])
    o_ref[...] = out


def _outmlp(h, yh, ym, yd, wo, nw, w1, w2, fw, final, tm=ROW_TILE, ff_chunk=1024):
    T = h.shape[0]
    tm = min(tm, T)
    row = lambda i: (i, 0)
    return pl.pallas_call(
        functools.partial(_outmlp_kernel, final=final, ff_chunk=ff_chunk),
        grid=(T // tm,),
        in_specs=[pl.BlockSpec((tm, D_MODEL), row), pl.BlockSpec((tm, HGRN_W), row),
                  pl.BlockSpec((tm, MLA_HEADS * MLA_V), row), pl.BlockSpec((tm, DSA_WIDTH), row),
                  _const_spec((D_MODEL, D_MODEL)), _const_spec((1, D_MODEL)),
                  _const_spec((D_MODEL, D_FF)), _const_spec((D_FF, D_MODEL)),
                  _const_spec((1, D_MODEL))],
        out_specs=pl.BlockSpec((tm, D_MODEL), row),
        out_shape=jax.ShapeDtypeStruct((T, D_MODEL), F32),
        compiler_params=_cparams(("parallel",)),
        name="outmlp",
    )(h, yh, ym, yd, wo, nw, w1, w2, fw)


def _rot_cols(w):
    half = w.shape[-1] // 2
    return jnp.concatenate([-w[..., half:], w[..., :half]], axis=-1)


def _place(w, width, off):
    pad = [(0, 0)] * (w.ndim - 1) + [(off, width - off - w.shape[-1])]
    return jnp.pad(w, pad)


def _layout_w_in(w_in):
    sizes = (HGRN_W, HGRN_W, HGRN_W, HGRN_W, MLA_Q_LORA, MLA_KV_LORA + MLA_ROPE,
             DSA_WIDTH, DSA_WIDTH, DSA_WIDTH, IDX_HEADS * IDX_DIM, IDX_DIM, IDX_HEADS)
    offs = [0]
    for s in sizes:
        offs.append(offs[-1] + s)
    hq, hf, hi, hg, mqa, mkva, dq, dk, dv, iq, ik, iw = [
        w_in[..., offs[n]:offs[n + 1]] for n in range(len(sizes))]
    ckv, kpe = mkva[..., :MLA_KV_LORA], mkva[..., MLA_KV_LORA:]
    cols = [hq, hf, hi, hg,
            _place(mqa, 256, 0), ckv, _place(kpe, LANES, MLA_NOPE), _place(_rot_cols(kpe), LANES, MLA_NOPE),
            dq * (DSA_HEAD_DIM ** -0.5 * LOG2E), dk,
            iq,
            ik, ik, _place(iw, LANES, 0)]
    w_cat = jnp.concatenate(cols, axis=-1).astype(BF)
    L = w_in.shape[0]
    dvt = jnp.swapaxes(dv, 1, 2).reshape(L, DSA_HEADS, DSA_HEAD_DIM, D_MODEL)
    dvt = jnp.pad(dvt, ((0, 0), (0, 0), (0, DSA_V_ROWS - DSA_HEAD_DIM), (0, 0)))
    return w_cat, dvt.reshape(L, DSA_HEADS * DSA_V_ROWS, D_MODEL).astype(BF)


def _layout_mla(w_qb, w_kvb):
    L = w_qb.shape[0]
    dq = MLA_NOPE + MLA_ROPE
    wq = w_qb.reshape(L, MLA_Q_LORA, MLA_HEADS, dq)
    wq_rot = jnp.concatenate([jnp.zeros_like(wq[..., :MLA_NOPE]), _rot_cols(wq[..., MLA_NOPE:])], axis=-1)
    pad_q = lambda w: jnp.pad(w, ((0, 0), (0, 256 - MLA_Q_LORA), (0, 0), (0, HEAD_PAD - dq))).reshape(
        L, 256, MLA_HEADS * HEAD_PAD).astype(BF)
    wkv = w_kvb.reshape(L, MLA_KV_LORA, MLA_HEADS, MLA_NOPE + MLA_V)
    pad_kv = lambda w: jnp.pad(w, ((0, 0), (0, 0), (0, 0), (0, HEAD_PAD - w.shape[-1]))).reshape(
        L, MLA_KV_LORA, MLA_HEADS * HEAD_PAD).astype(BF)
    wvt = jnp.swapaxes(pad_kv(wkv[..., MLA_NOPE:]), 1, 2)
    return pad_q(wq), pad_q(wq_rot), pad_kv(wkv[..., :MLA_NOPE]), wvt


def kernel(x, positions, attn_norm_w, w_in, hgrn_lb_logits, hgrn_norm_w, mla_q_norm_w, mla_w_qb,
           mla_kv_norm_w, mla_w_kvb, idx_k_norm_w, idx_k_norm_b, rel_bias_table, w_out,
           mlp_norm_w, w_mlp_in, w_mlp_out, final_norm_w):
    B, S, _ = x.shape
    T = B * S
    depth = w_in.shape[0]

    inv_freq = 1.0 / (ROPE_THETA ** (jnp.arange(0, MLA_ROPE, 2, dtype=F32) / MLA_ROPE))
    ang = positions.astype(F32)[..., None] * inv_freq
    cos, sin = jnp.cos(ang).reshape(T, -1), jnp.sin(ang).reshape(T, -1)
    cos_t = jnp.concatenate([jnp.ones((T, MLA_NOPE), F32), cos, cos,
                             jnp.zeros((T, HEAD_PAD - MLA_NOPE - MLA_ROPE), F32)], axis=1)
    sin_t = jnp.concatenate([jnp.zeros((T, MLA_NOPE), F32), sin, sin,
                             jnp.zeros((T, HEAD_PAD - MLA_NOPE - MLA_ROPE), F32)], axis=1)
    pos_tiles = positions.reshape(B, S // ATT_TILE, ATT_TILE)

    lb = jnp.cumsum(jax.nn.softmax(hgrn_lb_logits.astype(F32), axis=0), axis=0)
    lb = lb - lb[0:1]
    lb3 = jnp.stack([jnp.log(lb), jnp.log1p(-lb), 1.0 - lb], axis=1)

    w_cat, w_dvt = _layout_w_in(w_in)
    wq, wqr, wk, wvt = _layout_mla(mla_w_qb, mla_w_kvb)
    qnw = jnp.pad(mla_q_norm_w, ((0, 0), (0, 256 - MLA_Q_LORA)))
    lnw = jnp.concatenate([idx_k_norm_w, idx_k_norm_w], axis=-1)
    lnb = jnp.concatenate([idx_k_norm_b, idx_k_norm_b], axis=-1)
    wo = w_out.astype(BF)
    w1 = w_mlp_in.astype(BF)
    w2 = w_mlp_out.astype(BF)
    tab = rel_bias_table.astype(F32)

    h = x.reshape(T, D_MODEL)
    for l in range(depth):
        gh, gm, gd, gq, gk, vdt = _inproj(h, attn_norm_w[l][None], w_cat[l], w_dvt[l])
        q_m, k_m, v_m, ki2 = _prep(gm, gk, cos_t, sin_t, qnw[l][None], wq[l], wqr[l],
                                   mla_kv_norm_w[l][None], wk[l], wvt[l], lnw[l][None], lnb[l][None])
        y_h = _hgrn(gh, lb3[l], hgrn_norm_w[l][None], B, S)
        y_m = _mla_attn(q_m, k_m, v_m, B, S)
        y_d = _dsa(gq, gk, gd, ki2, vdt, pos_tiles, tab, B, S)
        h = _outmlp(h, y_h, y_m, y_d, wo[l], mlp_norm_w[l][None], w1[l], w2[l],
                    final_norm_w[None], final=(l == depth - 1))
    return h.reshape(B, S, D_MODEL)
```

```python
import functools
import math

import jax
import jax.numpy as jnp
from jax import lax
from jax.experimental import pallas as pl
from jax.experimental.pallas import tpu as pltpu

D_MODEL = 1024
HGRN_HEADS = 4
HGRN_DK = 128
HGRN_DV = 128
HGRN_W = HGRN_HEADS * HGRN_DK
HGRN_CHUNK = 32
MLA_HEADS = 4
MLA_NOPE = 64
MLA_ROPE = 32
MLA_V = 64
MLA_Q_LORA = 192
MLA_KV_LORA = 128
ROPE_THETA = 10000.0
DSA_HEADS = 4
DSA_HEAD_DIM = 64
DSA_WIDTH = DSA_HEADS * DSA_HEAD_DIM
IDX_HEADS = 8
IDX_DIM = 64
TOPK_MAX = 256
NUM_BUCKETS = 32
MAX_EXACT = NUM_BUCKETS // 2
MAX_DISTANCE = 128
D_FF = 4 * D_MODEL
EPS = 1e-6

LANES = 128
SUBLANES = 8
BF16_ROWS = 16
HEAD_PAD = 128
NEG = -1e30
ATT_TILE = 256
MLA_KEY_TILE = 512
ROW_TILE = 512
VMEM_LIMIT = 52 * 1024 * 1024
LOG2E = math.log2(math.e)

DSA_V_ROWS = DSA_HEAD_DIM + BF16_ROWS

C_H = 0
C_M = C_H + 4 * HGRN_W
W_M = 640
C_D = C_M + W_M
W_D = 2 * DSA_WIDTH
C_Q = C_D + W_D
W_Q = IDX_HEADS * IDX_DIM
C_K = C_Q + W_Q
W_K = 256
N_PAD = C_K + W_K

BF = jnp.bfloat16
F32 = jnp.float32


def _dot(a, b):
    return jnp.dot(a, b, preferred_element_type=F32)


def _dot_nt(a, b):
    return lax.dot_general(a, b, (((1,), (1,)), ((), ())), preferred_element_type=F32)


def _rms(x, w, n=None):
    n = x.shape[-1] if n is None else n
    ms = jnp.sum(x * x, axis=-1, keepdims=True) * (1.0 / n)
    return x * lax.rsqrt(ms + EPS) * w


def _cparams(sem):
    return pltpu.CompilerParams(dimension_semantics=sem, vmem_limit_bytes=VMEM_LIMIT)


def _const_spec(shape):
    nd = len(shape)
    return pl.BlockSpec(shape, lambda *_: (0,) * nd, pipeline_mode=pl.Buffered(1))


def _inproj_kernel(x_ref, nw_ref, w_ref, wvt_ref, oh_ref, om_ref, od_ref, oq_ref, ok_ref, ovt_ref):
    u = _rms(x_ref[...], nw_ref[...]).astype(BF)
    oh_ref[...] = _dot(u, w_ref[:, C_H:C_M])
    om_ref[...] = _dot(u, w_ref[:, C_M:C_D])
    od_ref[...] = _dot(u, w_ref[:, C_D:C_Q]).astype(BF)
    oq_ref[...] = _dot(u, w_ref[:, C_Q:C_K]).astype(BF)
    ok_ref[...] = _dot(u, w_ref[:, C_K:N_PAD])
    vt = _dot_nt(wvt_ref[...], u)
    row = lax.broadcasted_iota(jnp.int32, vt.shape, 0)
    vt = jnp.where(row % DSA_V_ROWS == DSA_HEAD_DIM, 1.0, vt).astype(BF)
    for n in range(ovt_ref.shape[0]):
        ovt_ref[n] = vt[:, n * ATT_TILE:(n + 1) * ATT_TILE]


def _inproj(h, nw, w, wvt, tm=ROW_TILE):
    T = h.shape[0]
    tm = min(tm, T)
    row = lambda i: (i, 0)
    vrows = DSA_HEADS * DSA_V_ROWS
    per = tm // ATT_TILE
    return pl.pallas_call(
        _inproj_kernel,
        grid=(T // tm,),
        in_specs=[pl.BlockSpec((tm, D_MODEL), row), _const_spec((1, D_MODEL)),
                  _const_spec((D_MODEL, N_PAD)), _const_spec((vrows, D_MODEL))],
        out_specs=[pl.BlockSpec((tm, 4 * HGRN_W), row), pl.BlockSpec((tm, W_M), row),
                   pl.BlockSpec((tm, W_D), row), pl.BlockSpec((tm, W_Q), row),
                   pl.BlockSpec((tm, W_K), row),
                   pl.BlockSpec((per, vrows, ATT_TILE), lambda i: (i, 0, 0))],
        out_shape=[jax.ShapeDtypeStruct((T, 4 * HGRN_W), F32), jax.ShapeDtypeStruct((T, W_M), F32),
                   jax.ShapeDtypeStruct((T, W_D), BF), jax.ShapeDtypeStruct((T, W_Q), BF),
                   jax.ShapeDtypeStruct((T, W_K), F32),
                   jax.ShapeDtypeStruct((T // ATT_TILE, vrows, ATT_TILE), BF)],
        compiler_params=_cparams(("parallel",)),
        name="inproj",
    )(h, nw, w, wvt)


def _prep_kernel(gm_ref, gk_ref, cos_ref, sin_ref, qnw_ref, wq_ref, wqr_ref, kvnw_ref, wk_ref,
                 wvt_ref, lnw_ref, lnb_ref, q_ref, k_ref, vt_ref, ki_ref):
    cs = cos_ref[...]
    sn = sin_ref[...]
    cs4 = jnp.concatenate([cs] * MLA_HEADS, axis=1)
    sn4 = jnp.concatenate([sn] * MLA_HEADS, axis=1)
    qn = _rms(gm_ref[:, 0:256], qnw_ref[...], n=MLA_Q_LORA).astype(BF)
    scale = (MLA_NOPE + MLA_ROPE) ** -0.5 * LOG2E
    q = (_dot(qn, wq_ref[...]) * cs4 + _dot(qn, wqr_ref[...]) * sn4) * scale
    q_ref[...] = q.astype(BF)
    cn = _rms(gm_ref[:, 256:384], kvnw_ref[...]).astype(BF)
    kp = gm_ref[:, 384:512] * cs + gm_ref[:, 512:640] * sn
    k = _dot(cn, wk_ref[...]) + jnp.concatenate([kp] * MLA_HEADS, axis=1)
    k_ref[...] = k.astype(BF)
    vt = _dot_nt(wvt_ref[...], cn)
    row = lax.broadcasted_iota(jnp.int32, vt.shape, 0)
    vt_ref[...] = jnp.where(row % HEAD_PAD == MLA_V, 1.0, vt).astype(BF)
    x = gk_ref[:, 0:LANES]
    first = lax.broadcasted_iota(jnp.int32, x.shape, 1) < IDX_DIM
    mu = jnp.sum(jnp.where(first, x, 0.0), axis=-1, keepdims=True) * (1.0 / IDX_DIM)
    xc = x - mu
    var = jnp.sum(jnp.where(first, xc * xc, 0.0), axis=-1, keepdims=True) * (1.0 / IDX_DIM)
    ki_ref[...] = (xc * lax.rsqrt(var + EPS) * lnw_ref[...] + lnb_ref[...]).astype(BF)


def _prep(gm, gk, cos_t, sin_t, qnw, wq, wqr, kvnw, wk, wvt, lnw, lnb, tm=MLA_KEY_TILE):
    T = gm.shape[0]
    row = lambda i: (i, 0)
    hp = MLA_HEADS * HEAD_PAD
    return pl.pallas_call(
        _prep_kernel,
        grid=(T // tm,),
        in_specs=[pl.BlockSpec((tm, W_M), row), pl.BlockSpec((tm, W_K), row),
                  pl.BlockSpec((tm, LANES), row), pl.BlockSpec((tm, LANES), row),
                  _const_spec((1, 256)), _const_spec((256, hp)), _const_spec((256, hp)),
                  _const_spec((1, MLA_KV_LORA)), _const_spec((MLA_KV_LORA, hp)),
                  _const_spec((hp, MLA_KV_LORA)), _const_spec((1, LANES)), _const_spec((1, LANES))],
        out_specs=[pl.BlockSpec((tm, hp), row), pl.BlockSpec((tm, hp), row),
                   pl.BlockSpec((None, hp, tm), lambda i: (i, 0, 0)), pl.BlockSpec((tm, LANES), row)],
        out_shape=[jax.ShapeDtypeStruct((T, hp), BF), jax.ShapeDtypeStruct((T, hp), BF),
                   jax.ShapeDtypeStruct((T // tm, hp, tm), BF), jax.ShapeDtypeStruct((T, LANES), BF)],
        compiler_params=_cparams(("parallel",)),
        name="prep",
    )(gm, gk, cos_t, sin_t, qnw, wq, wqr, kvnw, wk, wvt, lnw, lnb)


def _split3(x):
    a = x.astype(BF)
    r = x - a.astype(F32)
    b = r.astype(BF)
    c = (r - b.astype(F32)).astype(BF)
    return a, b, c


def _hgrn_kernel(q_ref, f_ref, i_ref, g_ref, lb_ref, nw_ref, o_ref, st_ref, os_ref, *, tb):
    C = HGRN_CHUNK
    nc = tb // C

    @pl.when(pl.program_id(1) == 0)
    def _():
        st_ref[...] = jnp.zeros_like(st_ref)

    fp = f_ref[...]
    log_lb = lb_ref[0:1, :]
    log1m_lb = lb_ref[1:2, :]
    one_m_lb = lb_ref[2:3, :]
    ls = jnp.minimum(fp, 0.0) - jnp.log1p(jnp.exp(-jnp.abs(fp)))
    b = log1m_lb + ls
    log_f = jnp.maximum(log_lb, b) + jnp.log1p(jnp.exp(-jnp.abs(log_lb - b)))
    k = one_m_lb * (1.0 / (1.0 + jnp.exp(fp)))

    r = lax.broadcasted_iota(jnp.int32, (tb, tb), 0)
    c = lax.broadcasted_iota(jnp.int32, (tb, tb), 1)
    same = (r // C) == (c // C)
    tri = jnp.where(same & (c <= r), 1.0, 0.0).astype(BF)
    blk = jnp.where(same, 1.0, 0.0).astype(BF)
    a0, a1, a2 = _split3(log_f)
    G = _dot(tri, a0) + _dot(tri, a1) + _dot(tri, a2)
    G_last = _dot(blk, a0) + _dot(blk, a1) + _dot(blk, a2)

    eg = jnp.exp(G)
    q_dec = (q_ref[...] * (HGRN_DK ** -0.5) * eg).astype(BF)
    k_inv = (k * jnp.exp(-G)).astype(BF)
    k_state = k * jnp.exp(G_last - G)
    decay = jnp.exp(G_last)
    v = i_ref[...]
    vb = v.astype(BF)

    rr = lax.broadcasted_iota(jnp.int32, (C, C), 0)
    cc = lax.broadcasted_iota(jnp.int32, (C, C), 1)
    causal = cc <= rr

    heads = [slice(h * HGRN_DK, (h + 1) * HGRN_DK) for h in range(HGRN_HEADS)]
    sts = [st_ref[h] for h in range(HGRN_HEADS)]
    for n in range(nc):
        rs = slice(n * C, (n + 1) * C)
        qds = [q_dec[rs, hs] for hs in heads]
        As = [_dot_nt(qds[h], k_inv[rs, heads[h]]) for h in range(HGRN_HEADS)]
        kvs = [_dot(v[rs, heads[h]].T.astype(BF), k_state[rs, heads[h]].astype(BF))
               for h in range(HGRN_HEADS)]
        inter = [_dot_nt(qds[h], sts[h].astype(BF)) for h in range(HGRN_HEADS)]
        for h in range(HGRN_HEADS):
            A = jnp.where(causal, As[h], 0.0).astype(BF)
            os_ref[rs, heads[h]] = _dot(A, vb[rs, heads[h]]) + inter[h]
            sts[h] = decay[n * C:n * C + 1, heads[h]] * sts[h] + kvs[h]
    for h in range(HGRN_HEADS):
        st_ref[h] = sts[h]

    g = g_ref[...]
    gate = g * (1.0 / (1.0 + jnp.exp(-g)))
    for h in range(HGRN_HEADS):
        hs = slice(h * HGRN_DK, (h + 1) * HGRN_DK)
        o_ref[:, hs] = (_rms(os_ref[:, hs], nw_ref[...]) * gate[:, hs]).astype(BF)


def _hgrn(gh, lb3, nw, B, S, tb=256):
    nb = S // tb
    col = lambda j: (lambda b, i: (b * nb + i, j))
    return pl.pallas_call(
        functools.partial(_hgrn_kernel, tb=tb),
        grid=(B, nb),
        in_specs=[pl.BlockSpec((tb, HGRN_W), col(0)), pl.BlockSpec((tb, HGRN_W), col(1)),
                  pl.BlockSpec((tb, HGRN_W), col(2)), pl.BlockSpec((tb, HGRN_W), col(3)),
                  pl.BlockSpec((3, HGRN_W), lambda b, i: (0, 0)),
                  pl.BlockSpec((1, HGRN_DV), lambda b, i: (0, 0))],
        out_specs=pl.BlockSpec((tb, HGRN_W), col(0)),
        out_shape=jax.ShapeDtypeStruct((B * S, HGRN_W), BF),
        scratch_shapes=[pltpu.VMEM((HGRN_HEADS, HGRN_DV, HGRN_DK), F32),
                        pltpu.VMEM((tb, HGRN_W), F32)],
        compiler_params=_cparams(("parallel", "arbitrary")),
        name="hgrn",
    )(gh, gh, gh, gh, lb3, nw)


def _mla_kernel(q_ref, k_ref, vt_ref, o_ref, *, t, tk):
    i = pl.program_id(1)
    n_full = (i * t) // tk
    key = lax.broadcasted_iota(jnp.int32, (tk, t), 0)
    qry = lax.broadcasted_iota(jnp.int32, (tk, t), 1)
    causal = n_full * tk + key <= i * t + qry

    heads = [slice(h * HEAD_PAD, (h + 1) * HEAD_PAD) for h in range(MLA_HEADS)]

    def logits(j):
        ks = pl.ds(pl.multiple_of(j * tk, tk), tk)
        return tuple(_dot_nt(k_ref[ks, hs], q_ref[:, hs]) for hs in heads)

    def accumulate(j, ss, state, mask):
        if mask:
            ss = [jnp.where(causal, s, NEG) for s in ss]
        ms = [jnp.maximum(state[h][0], jnp.max(ss[h], axis=0, keepdims=True))
              for h in range(MLA_HEADS)]
        ps = [jnp.exp2(ss[h] - ms[h]).astype(BF) for h in range(MLA_HEADS)]
        return tuple((ms[h], jnp.exp2(state[h][0] - ms[h]) * state[h][1]
                      + _dot(vt_ref[j, heads[h], :], ps[h])) for h in range(MLA_HEADS))

    init = tuple((jnp.full((1, t), NEG, F32), jnp.zeros((HEAD_PAD, t), F32))
                 for _ in range(MLA_HEADS))
    state = lax.fori_loop(0, n_full, lambda j, c: accumulate(j, logits(j), c, False), init)
    state = accumulate(n_full, logits(n_full), state, True)
    outs = [acc[:MLA_V, :] * (1.0 / acc[MLA_V:MLA_V + 1, :]) for (_, acc) in state]
    o_ref[...] = jnp.concatenate(outs, axis=0).T.astype(BF)


def _mla_attn(q, k, vt, B, S, t=ATT_TILE, tk=MLA_KEY_TILE):
    nq = S // t
    nk = S // tk
    hp = MLA_HEADS * HEAD_PAD
    return pl.pallas_call(
        functools.partial(_mla_kernel, t=t, tk=tk),
        grid=(B, nq),
        in_specs=[pl.BlockSpec((t, hp), lambda b, i: (b * nq + i, 0)),
                  pl.BlockSpec((S, hp), lambda b, i: (b, 0)),
                  pl.BlockSpec((nk, hp, tk), lambda b, i: (b, 0, 0))],
        out_specs=pl.BlockSpec((t, MLA_HEADS * MLA_V), lambda b, i: (b * nq + i, 0)),
        out_shape=jax.ShapeDtypeStruct((B * S, MLA_HEADS * MLA_V), BF),
        compiler_params=_cparams(("parallel", "arbitrary")),
        name="mla_attn",
    )(q, k, vt)


BISECT_CAP = 48
END_STEPS = 4
_LAST_BUCKET_FROM = next(
    n for n in range(MAX_EXACT, 1 << 20)
    if MAX_EXACT + int(math.log(n / MAX_EXACT) / math.log(MAX_DISTANCE / MAX_EXACT)
                       * (NUM_BUCKETS - MAX_EXACT)) >= NUM_BUCKETS - 1)
assert _LAST_BUCKET_FROM <= LANES - 1 and _LAST_BUCKET_FROM <= MAX_DISTANCE


def _dsa_kernel(qi_ref, w_ref, qd_ref, pos_ref, ki_ref, kd_ref, vt_ref, tab_ref, pmin_ref, pmax_ref,
                run_ref, o_ref, sc_ref, qm_ref, bias_ref, *, t, n_sel):
    i = pl.program_id(1)
    lane = lax.broadcasted_iota(jnp.int32, (t, LANES), 1)
    lo_half = lane < DSA_HEAD_DIM
    key = lax.broadcasted_iota(jnp.int32, (t, t), 0)
    qry = lax.broadcasted_iota(jnp.int32, (t, t), 1)
    causal = key <= qry
    k_sel = float(n_sel)

    def key_rows(j):
        return pl.ds(pl.multiple_of(j * t, t), t)

    @pl.when(i + 1 < sc_ref.shape[0])
    def _():
        sc_ref[i + 1] = jnp.full((t, t), -jnp.inf, F32)

    def over_tiles(body, init):
        return lax.fori_loop(0, (i + 2) // 2, lambda jj, c: body(2 * jj + 1, body(2 * jj, c)), init)

    def fold(x, op):
        return op(x.reshape(t // SUBLANES, SUBLANES, t), axis=0)

    zero_b = jnp.zeros((t, LANES), BF)
    for p in range(IDX_HEADS // 2):
        qp = qi_ref[:, p * LANES:(p + 1) * LANES]
        qm_ref[2 * p] = jnp.where(lo_half, qp, zero_b)
        qm_ref[2 * p + 1] = jnp.where(lo_half, zero_b, qp)
    wt = w_ref[...].T * ((IDX_HEADS ** -0.5) * (IDX_DIM ** -0.5))

    def score_tiles(js):
        kts = [ki_ref[key_rows(j), :] for j in js]
        accs = [None] * len(js)
        for h0 in range(0, IDX_HEADS, 4):
            ss = [[_dot_nt(kt, qm_ref[h]) for h in range(h0, h0 + 4)] for kt in kts]
            for n in range(len(js)):
                for m, s in enumerate(ss[n]):
                    term = jnp.maximum(s, 0.0) * wt[h0 + m:h0 + m + 1, :]
                    accs[n] = term if accs[n] is None else accs[n] + term
        return accs

    def add_stats(x, c):
        mx, mn, cp, cz = c
        return (jnp.maximum(mx, fold(x, jnp.max)),
                jnp.minimum(mn, fold(jnp.where(x == -jnp.inf, jnp.inf, x), jnp.min)),
                cp + fold(jnp.where(x > 0.0, 1.0, 0.0), jnp.sum),
                cz + fold(jnp.where(x == 0.0, 1.0, 0.0), jnp.sum))

    def p1_tiles(js, c, diag_last):
        xs = score_tiles(js)
        if diag_last:
            xs[-1] = jnp.where(causal, xs[-1], -jnp.inf)
        for j, x in zip(js, xs):
            sc_ref[j] = x
            c = add_stats(x, c)
        return c

    z8 = jnp.zeros((SUBLANES, t), F32)
    stats = lax.fori_loop(0, i // 2, lambda jj, c: p1_tiles([2 * jj, 2 * jj + 1], c, False),
                          (z8 - jnp.inf, z8 + jnp.inf, z8, z8))
    mx, mn, cp, cz = lax.cond(i % 2 == 1,
                              lambda c: p1_tiles([i - 1, i], c, True),
                              lambda c: p1_tiles([i], c, True), stats)

    def total(c):
        return jnp.sum(c, axis=0, keepdims=True)

    def count_gt(thr):
        def body(j, c):
            return c + fold(jnp.where(sc_ref[j] > thr, 1.0, 0.0), jnp.sum)
        return total(over_tiles(body, jnp.zeros((SUBLANES, t), F32)))

    row_max = jnp.max(mx, axis=0, keepdims=True)
    row_min = jnp.min(mn, axis=0, keepdims=True)
    c_pos, c_zero = total(cp), total(cz)
    n_valid = (i * t + 1 + lax.broadcasted_iota(jnp.int32, (1, t), 1)).astype(F32)

    big = n_valid > k_sel
    pos_q = jnp.logical_and(big, c_pos >= k_sel)
    tie_q = jnp.logical_and(jnp.logical_and(big, c_pos < k_sel), c_pos + c_zero >= k_sel)
    neg_q = jnp.logical_and(big, c_pos + c_zero < k_sel)
    need = jnp.where(tie_q, k_sel - c_pos, 0.0)
    lo0 = jnp.where(jnp.logical_or(pos_q, tie_q), 0.0, -jnp.inf)
    hi0 = jnp.where(neg_q, 0.0, row_max)
    c_lo0 = jnp.where(pos_q, c_pos, jnp.where(tie_q, k_sel, n_valid))

    searched = c_lo0 > k_sel

    def open_queries(c_lo):
        return jnp.max(jnp.where(c_lo > k_sel + END_STEPS, 1.0, 0.0))

    def next_above(thr):
        def body(j, m):
            x = sc_ref[j]
            return jnp.minimum(m, fold(jnp.where(x > thr, x, jnp.inf), jnp.min))
        return jnp.min(over_tiles(body, jnp.full((SUBLANES, t), jnp.inf, F32)), axis=0, keepdims=True)

    def bis_cond(c):
        return jnp.logical_and(c[0] < BISECT_CAP, c[1] > 0.0)

    def bisect_once(lo, hi, c_lo):
        active = c_lo > k_sel
        base = jnp.maximum(lo, row_min)
        mid = base + 0.5 * (hi - base)
        cnt = count_gt(mid)
        up = jnp.logical_and(active, cnt >= k_sel)
        dn = jnp.logical_and(active, cnt < k_sel)
        return jnp.where(up, mid, lo), jnp.where(dn, mid, hi), jnp.where(up, cnt, c_lo)

    def bis_body(c):
        it, _, lo, hi, c_lo = c
        lo, hi, c_lo = bisect_once(*bisect_once(lo, hi, c_lo))
        return it + 2, open_queries(c_lo), lo, hi, c_lo

    _, _, lo, _, c_lo = lax.while_loop(
        bis_cond, bis_body, (jnp.int32(0), open_queries(c_lo0), lo0, hi0, c_lo0))

    def step_body(_, c):
        lo, c_lo = c
        more = c_lo > k_sel
        return jnp.where(more, next_above(lo), lo), jnp.where(more, c_lo - 1.0, c_lo)

    lo, _ = lax.fori_loop(0, END_STEPS, step_body, (lo, c_lo))
    c_lo = jnp.where(searched, count_gt(lo), c_lo)

    still_open = c_lo != k_sel
    still_open = jnp.logical_and(searched, still_open)

    def exact_kth(_):
        def to_float(kk):
            return lax.bitcast_convert_type(jnp.where(kk < 0, kk ^ jnp.int32(0x7FFFFFFF), kk), F32)

        def count_ge(thr):
            def body(j, c):
                return c + fold(jnp.where(sc_ref[j] >= thr, 1.0, 0.0), jnp.sum)
            return total(over_tiles(body, jnp.zeros((SUBLANES, t), F32)))

        def body(b, kth):
            cand = kth + lax.shift_left(jnp.int32(1), 31 - b)
            return jnp.where(count_ge(to_float(cand)) >= k_sel, cand, kth)

        kth = to_float(lax.fori_loop(0, 32, body, jnp.full((1, t), -2 ** 31, jnp.int32)))
        return kth, count_gt(kth)

    kth, above = lax.cond(jnp.max(jnp.where(still_open, 1.0, 0.0)) > 0.0, exact_kth,
                          lambda _: (jnp.zeros((1, t), F32), jnp.zeros((1, t), F32)), 0)
    lo = jnp.where(still_open, kth, lo)
    need = jnp.where(still_open, k_sel - above, need)
    tie_val = jnp.where(still_open, kth, 0.0)

    def mask_plain(_):
        def body(j, c):
            sc_ref[j] = jnp.where(sc_ref[j] > lo, 0.0, NEG)
            return c
        return lax.fori_loop(0, i + 1, body, 0)

    def mask_ties(_):
        lower = jnp.where(qry <= key, 1.0, 0.0).astype(BF)

        def body(jj, seen):
            xs = [sc_ref[2 * jj + n] for n in range(2)]
            tied = [x == tie_val for x in xs]
            tfs = [jnp.where(m, 1.0, 0.0) for m in tied]
            ranks = [_dot(lower, tf.astype(BF)) for tf in tfs]
            for n in range(2):
                take = jnp.logical_and(tied[n], ranks[n] + seen <= need)
                sc_ref[2 * jj + n] = jnp.where(jnp.logical_or(xs[n] > lo, take), 0.0, NEG)
                seen = seen + total(fold(tfs[n], jnp.sum))
            return seen
        lax.fori_loop(0, (i + 2) // 2, body, jnp.zeros((1, t), F32))
        return 0

    lax.cond(jnp.max(need) > 0.0, mask_ties, mask_plain, 0)

    qh = []
    for p in range(DSA_HEADS // 2):
        qp = qd_ref[:, p * LANES:(p + 1) * LANES]
        qh.append(jnp.where(lo_half, qp, zero_b))
        qh.append(jnp.where(lo_half, zero_b, qp))
    pq = pos_ref[pl.ds(i, 1), :]
    pq_min = pmin_ref[pl.program_id(0), i]
    far_bias = [tab_ref[NUM_BUCKETS - 1, h] * LOG2E for h in range(DSA_HEADS)]
    log_ratio = math.log(MAX_DISTANCE / MAX_EXACT)

    dist = lax.broadcasted_iota(jnp.int32, (SUBLANES, LANES), 1)
    large = MAX_EXACT + (jnp.log(jnp.maximum(dist, 1).astype(F32) / MAX_EXACT) / log_ratio
                         * (NUM_BUCKETS - MAX_EXACT)).astype(jnp.int32)
    bucket = jnp.where(dist < MAX_EXACT, dist, jnp.minimum(large, NUM_BUCKETS - 1))
    by_dist = []
    for h in range(DSA_HEADS):
        bh = jnp.full((SUBLANES, LANES), tab_ref[0, h], F32)
        for jb in range(1, NUM_BUCKETS):
            bh = jnp.where(bucket >= jb, tab_ref[jb, h], bh)
        by_dist.append(jnp.concatenate([bh * LOG2E] * (t // SUBLANES), axis=0))

    def bias_of(n):
        n = jnp.clip(n, 0, LANES - 1)
        return [jnp.concatenate([jnp.take_along_axis(by_dist[h], n[:, c * LANES:(c + 1) * LANES], axis=1)
                                 for c in range(t // LANES)], axis=1) for h in range(DSA_HEADS)]

    def pair_bias(pk_row):
        pk = jnp.broadcast_to(pk_row, (SUBLANES, t)).T[:, 0:1]
        return bias_of(pq - pk)

    b_idx = pl.program_id(0)

    @pl.when(jnp.logical_and(b_idx == 0, i == 0))
    def _():
        for gap in range(2):
            for h, tile in enumerate(bias_of(gap * t + qry - key)):
                bias_ref[gap * DSA_HEADS + h] = tile

    def attend(tiles, carry, bias):
        ss = []
        for j in tiles:
            madd = sc_ref[j]
            ks = key_rows(j)
            kps = [kd_ref[ks, p * LANES:(p + 1) * LANES] for p in range(DSA_HEADS // 2)]
            ss.append([_dot_nt(kps[h // 2], qh[h]) + (madd + bias[h]) for h in range(DSA_HEADS)])
        ms = []
        for h in range(DSA_HEADS):
            m = carry[h][0]
            for s in ss:
                m = jnp.maximum(m, jnp.max(s[h], axis=0, keepdims=True))
            ms.append(m)
        out = []
        for h in range(DSA_HEADS):
            acc = jnp.exp2(carry[h][0] - ms[h]) * carry[h][1]
            for j, s in zip(tiles, ss):
                p = jnp.exp2(s[h] - ms[h]).astype(BF)
                acc = acc + _dot(vt_ref[j, h * DSA_V_ROWS:(h + 1) * DSA_V_ROWS, :], p)
            out.append((ms[h], acc))
        return tuple(out)

    def is_far(j):
        return pq_min - pmax_ref[b_idx, j] >= MAX_DISTANCE

    def p3_body(j, carry):
        gap = i - j
        consecutive = jnp.logical_and(
            jnp.logical_and(run_ref[b_idx, i] == 1, run_ref[b_idx, j] == 1),
            jnp.logical_and(gap <= 1, pq_min - pmin_ref[b_idx, j] == gap * t))

        def near(c):
            return lax.cond(
                consecutive,
                lambda c: attend([j], c, [bias_ref[gap * DSA_HEADS + h] for h in range(DSA_HEADS)]),
                lambda c: attend([j], c, pair_bias(pos_ref[pl.ds(j, 1), :])), c)

        return lax.cond(is_far(j), lambda c: attend([j], c, far_bias), near, carry)

    def p3_pair(jj, carry):
        j0 = 2 * jj
        n_here = jnp.minimum(i + 1 - j0, 2)
        both_far = jnp.logical_and(n_here == 2,
                                   jnp.logical_and(is_far(j0), is_far(jnp.minimum(j0 + 1, i))))
        return lax.cond(both_far,
                        lambda c: attend([j0, j0 + 1], c, far_bias),
                        lambda c: lax.fori_loop(0, n_here, lambda n, c: p3_body(j0 + n, c), c),
                        carry)

    init = tuple((jnp.full((1, t), NEG, F32), jnp.zeros((DSA_V_ROWS, t), F32))
                 for _ in range(DSA_HEADS))
    carry = lax.fori_loop(0, (i + 2) // 2, p3_pair, init)
    outs = [acc[:DSA_HEAD_DIM, :] * (1.0 / acc[DSA_HEAD_DIM:DSA_HEAD_DIM + 1, :])
            for (_, acc) in carry]
    o_ref[...] = jnp.concatenate(outs, axis=0).T.astype(BF)


def _dsa(gq, gk, gd, ki2, vdt, pos_tiles, tab, B, S, t=ATT_TILE):
    nq = S // t
    n_sel = min(TOPK_MAX, S // 4)
    vrows = DSA_HEADS * DSA_V_ROWS
    qrow = lambda c: (lambda b, i: (b * nq + i, c))
    return pl.pallas_call(
        functools.partial(_dsa_kernel, t=t, n_sel=n_sel),
        grid=(B, nq),
        in_specs=[pl.BlockSpec((t, W_Q), qrow(0)),
                  pl.BlockSpec((t, LANES), qrow(1)),
                  pl.BlockSpec((t, DSA_WIDTH), qrow(0)),
                  pl.BlockSpec((None, nq, t), lambda b, i: (b, 0, 0)),
                  pl.BlockSpec((S, LANES), lambda b, i: (b, 0)),
                  pl.BlockSpec((S, DSA_WIDTH), lambda b, i: (b, 1)),
                  pl.BlockSpec((nq, vrows, t), lambda b, i: (b, 0, 0)),
                  pl.BlockSpec(memory_space=pltpu.SMEM), pl.BlockSpec(memory_space=pltpu.SMEM),
                  pl.BlockSpec(memory_space=pltpu.SMEM), pl.BlockSpec(memory_space=pltpu.SMEM)],
        out_specs=pl.BlockSpec((t, DSA_WIDTH), qrow(0)),
        out_shape=jax.ShapeDtypeStruct((B * S, DSA_WIDTH), BF),
        scratch_shapes=[pltpu.VMEM((nq, t, t), F32), pltpu.VMEM((IDX_HEADS, t, LANES), BF),
                        pltpu.VMEM((2 * DSA_HEADS, t, t), F32)],
        compiler_params=_cparams(("arbitrary", "arbitrary")),
        name="dsa",
    )(gq, gk, gd, pos_tiles, ki2, gd, vdt, tab, jnp.min(pos_tiles, axis=-1), jnp.max(pos_tiles, axis=-1),
      jnp.all(pos_tiles[..., 1:] - pos_tiles[..., :-1] == 1, axis=-1).astype(jnp.int32))


def _outmlp_kernel(h_ref, yh_ref, ym_ref, yd_ref, wo_ref, nw_ref, w1_ref, w2_ref, fw_ref, o_ref,
                   *, final, ff_chunk):
    mixed = jnp.concatenate([yh_ref[...], ym_ref[...], yd_ref[...]], axis=1)
    h = h_ref[...] + _dot(mixed, wo_ref[...])
    u = _rms(h, nw_ref[...]).astype(BF)
    out = h
    for c in range(D_FF // ff_chunk):
        cs = slice(c * ff_chunk, (c + 1) * ff_chunk)
        a = jnp.maximum(_dot(u, w1_ref[:, cs]), 0.0)
        out = out + _dot((a * a).astype(BF), w2_ref[cs, :])
    if final:
        out = _rms(out, fw_ref[...])
    o_ref[...] = out


def _outmlp(h, yh, ym, yd, wo, nw, w1, w2, fw, final, tm=ROW_TILE, ff_chunk=1024):
    T = h.shape[0]
    tm = min(tm, T)
    row = lambda i: (i, 0)
    return pl.pallas_call(
        functools.partial(_outmlp_kernel, final=final, ff_chunk=ff_chunk),
        grid=(T // tm,),
        in_specs=[pl.BlockSpec((tm, D_MODEL), row), pl.BlockSpec((tm, HGRN_W), row),
                  pl.BlockSpec((tm, MLA_HEADS * MLA_V), row), pl.BlockSpec((tm, DSA_WIDTH), row),
                  _const_spec((D_MODEL, D_MODEL)), _const_spec((1, D_MODEL)),
                  _const_spec((D_MODEL, D_FF)), _const_spec((D_FF, D_MODEL)),
                  _const_spec((1, D_MODEL))],
        out_specs=pl.BlockSpec((tm, D_MODEL), row),
        out_shape=jax.ShapeDtypeStruct((T, D_MODEL), F32),
        compiler_params=_cparams(("parallel",)),
        name="outmlp",
    )(h, yh, ym, yd, wo, nw, w1, w2, fw)


def _rot_cols(w):
    half = w.shape[-1] // 2
    return jnp.concatenate([-w[..., half:], w[..., :half]], axis=-1)


def _place(w, width, off):
    pad = [(0, 0)] * (w.ndim - 1) + [(off, width - off - w.shape[-1])]
    return jnp.pad(w, pad)


def _layout_w_in(w_in):
    sizes = (HGRN_W, HGRN_W, HGRN_W, HGRN_W, MLA_Q_LORA, MLA_KV_LORA + MLA_ROPE,
             DSA_WIDTH, DSA_WIDTH, DSA_WIDTH, IDX_HEADS * IDX_DIM, IDX_DIM, IDX_HEADS)
    offs = [0]
    for s in sizes:
        offs.append(offs[-1] + s)
    hq, hf, hi, hg, mqa, mkva, dq, dk, dv, iq, ik, iw = [
        w_in[..., offs[n]:offs[n + 1]] for n in range(len(sizes))]
    ckv, kpe = mkva[..., :MLA_KV_LORA], mkva[..., MLA_KV_LORA:]
    cols = [hq, hf, hi, hg,
            _place(mqa, 256, 0), ckv, _place(kpe, LANES, MLA_NOPE), _place(_rot_cols(kpe), LANES, MLA_NOPE),
            dq * (DSA_HEAD_DIM ** -0.5 * LOG2E), dk,
            iq,
            ik, ik, _place(iw, LANES, 0)]
    w_cat = jnp.concatenate(cols, axis=-1).astype(BF)
    L = w_in.shape[0]
    dvt = jnp.swapaxes(dv, 1, 2).reshape(L, DSA_HEADS, DSA_HEAD_DIM, D_MODEL)
    dvt = jnp.pad(dvt, ((0, 0), (0, 0), (0, DSA_V_ROWS - DSA_HEAD_DIM), (0, 0)))
    return w_cat, dvt.reshape(L, DSA_HEADS * DSA_V_ROWS, D_MODEL).astype(BF)


def _layout_mla(w_qb, w_kvb):
    L = w_qb.shape[0]
    dq = MLA_NOPE + MLA_ROPE
    wq = w_qb.reshape(L, MLA_Q_LORA, MLA_HEADS, dq)
    wq_rot = jnp.concatenate([jnp.zeros_like(wq[..., :MLA_NOPE]), _rot_cols(wq[..., MLA_NOPE:])], axis=-1)
    pad_q = lambda w: jnp.pad(w, ((0, 0), (0, 256 - MLA_Q_LORA), (0, 0), (0, HEAD_PAD - dq))).reshape(
        L, 256, MLA_HEADS * HEAD_PAD).astype(BF)
    wkv = w_kvb.reshape(L, MLA_KV_LORA, MLA_HEADS, MLA_NOPE + MLA_V)
    pad_kv = lambda w: jnp.pad(w, ((0, 0), (0, 0), (0, 0), (0, HEAD_PAD - w.shape[-1]))).reshape(
        L, MLA_KV_LORA, MLA_HEADS * HEAD_PAD).astype(BF)
    wvt = jnp.swapaxes(pad_kv(wkv[..., MLA_NOPE:]), 1, 2)
    return pad_q(wq), pad_q(wq_rot), pad_kv(wkv[..., :MLA_NOPE]), wvt


def kernel(x, positions, attn_norm_w, w_in, hgrn_lb_logits, hgrn_norm_w, mla_q_norm_w, mla_w_qb,
           mla_kv_norm_w, mla_w_kvb, idx_k_norm_w, idx_k_norm_b, rel_bias_table, w_out,
           mlp_norm_w, w_mlp_in, w_mlp_out, final_norm_w):
    B, S, _ = x.shape
    T = B * S
    depth = w_in.shape[0]

    inv_freq = 1.0 / (ROPE_THETA ** (jnp.arange(0, MLA_ROPE, 2, dtype=F32) / MLA_ROPE))
    ang = positions.astype(F32)[..., None] * inv_freq
    cos, sin = jnp.cos(ang).reshape(T, -1), jnp.sin(ang).reshape(T, -1)
    cos_t = jnp.concatenate([jnp.ones((T, MLA_NOPE), F32), cos, cos,
                             jnp.zeros((T, HEAD_PAD - MLA_NOPE - MLA_ROPE), F32)], axis=1)
    sin_t = jnp.concatenate([jnp.zeros((T, MLA_NOPE), F32), sin, sin,
                             jnp.zeros((T, HEAD_PAD - MLA_NOPE - MLA_ROPE), F32)], axis=1)
    pos_tiles = positions.reshape(B, S // ATT_TILE, ATT_TILE)

    lb = jnp.cumsum(jax.nn.softmax(hgrn_lb_logits.astype(F32), axis=0), axis=0)
    lb = lb - lb[0:1]
    lb3 = jnp.stack([jnp.log(lb), jnp.log1p(-lb), 1.0 - lb], axis=1)

    w_cat, w_dvt = _layout_w_in(w_in)
    wq, wqr, wk, wvt = _layout_mla(mla_w_qb, mla_w_kvb)
    qnw = jnp.pad(mla_q_norm_w, ((0, 0), (0, 256 - MLA_Q_LORA)))
    lnw = jnp.concatenate([idx_k_norm_w, idx_k_norm_w], axis=-1)
    lnb = jnp.concatenate([idx_k_norm_b, idx_k_norm_b], axis=-1)
    wo = w_out.astype(BF)
    w1 = w_mlp_in.astype(BF)
    w2 = w_mlp_out.astype(BF)
    tab = rel_bias_table.astype(F32)

    h = x.reshape(T, D_MODEL)
    for l in range(depth):
        gh, gm, gd, gq, gk, vdt = _inproj(h, attn_norm_w[l][None], w_cat[l], w_dvt[l])
        q_m, k_m, v_m, ki2 = _prep(gm, gk, cos_t, sin_t, qnw[l][None], wq[l], wqr[l],
                                   mla_kv_norm_w[l][None], wk[l], wvt[l], lnw[l][None], lnb[l][None])
        y_h = _hgrn(gh, lb3[l], hgrn_norm_w[l][None], B, S)
        y_m = _mla_attn(q_m, k_m, v_m, B, S)
        y_d = _dsa(gq, gk, gd, ki2, vdt, pos_tiles, tab, B, S)
        h = _outmlp(h, y_h, y_m, y_d, wo[l], mlp_norm_w[l][None], w1[l], w2[l],
                    final_norm_w[None], final=(l == depth - 1))
    return h.reshape(B, S, D_MODEL)
```

```python
import functools
import math

import jax
import jax.numpy as jnp
from jax import lax
from jax.experimental import pallas as pl
from jax.experimental.pallas import tpu as pltpu

D_MODEL = 1024
HGRN_HEADS = 4
HGRN_DK = 128
HGRN_DV = 128
HGRN_W = HGRN_HEADS * HGRN_DK
HGRN_CHUNK = 32
MLA_HEADS = 4
MLA_NOPE = 64
MLA_ROPE = 32
MLA_V = 64
MLA_Q_LORA = 192
MLA_KV_LORA = 128
ROPE_THETA = 10000.0
DSA_HEADS = 4
DSA_HEAD_DIM = 64
DSA_WIDTH = DSA_HEADS * DSA_HEAD_DIM
IDX_HEADS = 8
IDX_DIM = 64
TOPK_MAX = 256
NUM_BUCKETS = 32
MAX_EXACT = NUM_BUCKETS // 2
MAX_DISTANCE = 128
D_FF = 4 * D_MODEL
EPS = 1e-6

LANES = 128
SUBLANES = 8
BF16_ROWS = 16
HEAD_PAD = 128
NEG = -1e30
ATT_TILE = 256
MLA_KEY_TILE = 512
ROW_TILE = 512
VMEM_LIMIT = 52 * 1024 * 1024
LOG2E = math.log2(math.e)

DSA_V_ROWS = DSA_HEAD_DIM + BF16_ROWS

C_H = 0
C_M = C_H + 4 * HGRN_W
W_M = 640
C_D = C_M + W_M
W_D = 2 * DSA_WIDTH
C_Q = C_D + W_D
W_Q = IDX_HEADS * IDX_DIM
C_K = C_Q + W_Q
W_K = 256
N_PAD = C_K + W_K

BF = jnp.bfloat16
F32 = jnp.float32


def _dot(a, b):
    return jnp.dot(a, b, preferred_element_type=F32)


def _dot_nt(a, b):
    return lax.dot_general(a, b, (((1,), (1,)), ((), ())), preferred_element_type=F32)


def _rms(x, w, n=None):
    n = x.shape[-1] if n is None else n
    ms = jnp.sum(x * x, axis=-1, keepdims=True) * (1.0 / n)
    return x * lax.rsqrt(ms + EPS) * w


def _cparams(sem):
    return pltpu.CompilerParams(dimension_semantics=sem, vmem_limit_bytes=VMEM_LIMIT)


def _const_spec(shape):
    nd = len(shape)
    return pl.BlockSpec(shape, lambda *_: (0,) * nd, pipeline_mode=pl.Buffered(1))


def _inproj_kernel(x_ref, nw_ref, w_ref, wvt_ref, oh_ref, om_ref, od_ref, oq_ref, ok_ref, ovt_ref):
    u = _rms(x_ref[...], nw_ref[...]).astype(BF)
    oh_ref[...] = _dot(u, w_ref[:, C_H:C_M])
    om_ref[...] = _dot(u, w_ref[:, C_M:C_D])
    od_ref[...] = _dot(u, w_ref[:, C_D:C_Q]).astype(BF)
    oq_ref[...] = _dot(u, w_ref[:, C_Q:C_K]).astype(BF)
    ok_ref[...] = _dot(u, w_ref[:, C_K:N_PAD])
    vt = _dot_nt(wvt_ref[...], u)
    row = lax.broadcasted_iota(jnp.int32, vt.shape, 0)
    vt = jnp.where(row % DSA_V_ROWS == DSA_HEAD_DIM, 1.0, vt).astype(BF)
    for n in range(ovt_ref.shape[0]):
        ovt_ref[n] = vt[:, n * ATT_TILE:(n + 1) * ATT_TILE]


def _inproj(h, nw, w, wvt, tm=ROW_TILE):
    T = h.shape[0]
    tm = min(tm, T)
    row = lambda i: (i, 0)
    vrows = DSA_HEADS * DSA_V_ROWS
    per = tm // ATT_TILE
    return pl.pallas_call(
        _inproj_kernel,
        grid=(T // tm,),
        in_specs=[pl.BlockSpec((tm, D_MODEL), row), _const_spec((1, D_MODEL)),
                  _const_spec((D_MODEL, N_PAD)), _const_spec((vrows, D_MODEL))],
        out_specs=[pl.BlockSpec((tm, 4 * HGRN_W), row), pl.BlockSpec((tm, W_M), row),
                   pl.BlockSpec((tm, W_D), row), pl.BlockSpec((tm, W_Q), row),
                   pl.BlockSpec((tm, W_K), row),
                   pl.BlockSpec((per, vrows, ATT_TILE), lambda i: (i, 0, 0))],
        out_shape=[jax.ShapeDtypeStruct((T, 4 * HGRN_W), F32), jax.ShapeDtypeStruct((T, W_M), F32),
                   jax.ShapeDtypeStruct((T, W_D), BF), jax.ShapeDtypeStruct((T, W_Q), BF),
                   jax.ShapeDtypeStruct((T, W_K), F32),
                   jax.ShapeDtypeStruct((T // ATT_TILE, vrows, ATT_TILE), BF)],
        compiler_params=_cparams(("parallel",)),
        name="inproj",
    )(h, nw, w, wvt)


def _prep_kernel(gm_ref, gk_ref, cos_ref, sin_ref, qnw_ref, wq_ref, wqr_ref, kvnw_ref, wk_ref,
                 wvt_ref, lnw_ref, lnb_ref, q_ref, k_ref, vt_ref, ki_ref):
    cs = cos_ref[...]
    sn = sin_ref[...]
    cs4 = jnp.concatenate([cs] * MLA_HEADS, axis=1)
    sn4 = jnp.concatenate([sn] * MLA_HEADS, axis=1)
    qn = _rms(gm_ref[:, 0:256], qnw_ref[...], n=MLA_Q_LORA).astype(BF)
    scale = (MLA_NOPE + MLA_ROPE) ** -0.5 * LOG2E
    q = (_dot(qn, wq_ref[...]) * cs4 + _dot(qn, wqr_ref[...]) * sn4) * scale
    q_ref[...] = q.astype(BF)
    cn = _rms(gm_ref[:, 256:384], kvnw_ref[...]).astype(BF)
    kp = gm_ref[:, 384:512] * cs + gm_ref[:, 512:640] * sn
    k = _dot(cn, wk_ref[...]) + jnp.concatenate([kp] * MLA_HEADS, axis=1)
    k_ref[...] = k.astype(BF)
    vt = _dot_nt(wvt_ref[...], cn)
    row = lax.broadcasted_iota(jnp.int32, vt.shape, 0)
    vt_ref[...] = jnp.where(row % HEAD_PAD == MLA_V, 1.0, vt).astype(BF)
    x = gk_ref[:, 0:LANES]
    first = lax.broadcasted_iota(jnp.int32, x.shape, 1) < IDX_DIM
    mu = jnp.sum(jnp.where(first, x, 0.0), axis=-1, keepdims=True) * (1.0 / IDX_DIM)
    xc = x - mu
    var = jnp.sum(jnp.where(first, xc * xc, 0.0), axis=-1, keepdims=True) * (1.0 / IDX_DIM)
    ki_ref[...] = (xc * lax.rsqrt(var + EPS) * lnw_ref[...] + lnb_ref[...]).astype(BF)


def _prep(gm, gk, cos_t, sin_t, qnw, wq, wqr, kvnw, wk, wvt, lnw, lnb, tm=MLA_KEY_TILE):
    T = gm.shape[0]
    row = lambda i: (i, 0)
    hp = MLA_HEADS * HEAD_PAD
    return pl.pallas_call(
        _prep_kernel,
        grid=(T // tm,),
        in_specs=[pl.BlockSpec((tm, W_M), row), pl.BlockSpec((tm, W_K), row),
                  pl.BlockSpec((tm, LANES), row), pl.BlockSpec((tm, LANES), row),
                  _const_spec((1, 256)), _const_spec((256, hp)), _const_spec((256, hp)),
                  _const_spec((1, MLA_KV_LORA)), _const_spec((MLA_KV_LORA, hp)),
                  _const_spec((hp, MLA_KV_LORA)), _const_spec((1, LANES)), _const_spec((1, LANES))],
        out_specs=[pl.BlockSpec((tm, hp), row), pl.BlockSpec((tm, hp), row),
                   pl.BlockSpec((None, hp, tm), lambda i: (i, 0, 0)), pl.BlockSpec((tm, LANES), row)],
        out_shape=[jax.ShapeDtypeStruct((T, hp), BF), jax.ShapeDtypeStruct((T, hp), BF),
                   jax.ShapeDtypeStruct((T // tm, hp, tm), BF), jax.ShapeDtypeStruct((T, LANES), BF)],
        compiler_params=_cparams(("parallel",)),
        name="prep",
    )(gm, gk, cos_t, sin_t, qnw, wq, wqr, kvnw, wk, wvt, lnw, lnb)


def _split3(x):
    a = x.astype(BF)
    r = x - a.astype(F32)
    b = r.astype(BF)
    c = (r - b.astype(F32)).astype(BF)
    return a, b, c


def _hgrn_kernel(q_ref, f_ref, i_ref, g_ref, lb_ref, nw_ref, o_ref, st_ref, os_ref, *, tb):
    C = HGRN_CHUNK
    nc = tb // C

    @pl.when(pl.program_id(1) == 0)
    def _():
        st_ref[...] = jnp.zeros_like(st_ref)

    fp = f_ref[...]
    log_lb = lb_ref[0:1, :]
    log1m_lb = lb_ref[1:2, :]
    one_m_lb = lb_ref[2:3, :]
    ls = jnp.minimum(fp, 0.0) - jnp.log1p(jnp.exp(-jnp.abs(fp)))
    b = log1m_lb + ls
    log_f = jnp.maximum(log_lb, b) + jnp.log1p(jnp.exp(-jnp.abs(log_lb - b)))
    k = one_m_lb * (1.0 / (1.0 + jnp.exp(fp)))

    r = lax.broadcasted_iota(jnp.int32, (tb, tb), 0)
    c = lax.broadcasted_iota(jnp.int32, (tb, tb), 1)
    same = (r // C) == (c // C)
    tri = jnp.where(same & (c <= r), 1.0, 0.0).astype(BF)
    blk = jnp.where(same, 1.0, 0.0).astype(BF)
    a0, a1, a2 = _split3(log_f)
    G = _dot(tri, a0) + _dot(tri, a1) + _dot(tri, a2)
    G_last = _dot(blk, a0) + _dot(blk, a1) + _dot(blk, a2)

    eg = jnp.exp(G)
    q_dec = (q_ref[...] * (HGRN_DK ** -0.5) * eg).astype(BF)
    k_inv = (k * jnp.exp(-G)).astype(BF)
    k_state = k * jnp.exp(G_last - G)
    decay = jnp.exp(G_last)
    v = i_ref[...]
    vb = v.astype(BF)

    rr = lax.broadcasted_iota(jnp.int32, (C, C), 0)
    cc = lax.broadcasted_iota(jnp.int32, (C, C), 1)
    causal = cc <= rr

    heads = [slice(h * HGRN_DK, (h + 1) * HGRN_DK) for h in range(HGRN_HEADS)]
    sts = [st_ref[h] for h in range(HGRN_HEADS)]
    for n in range(nc):
        rs = slice(n * C, (n + 1) * C)
        qds = [q_dec[rs, hs] for hs in heads]
        As = [_dot_nt(qds[h], k_inv[rs, heads[h]]) for h in range(HGRN_HEADS)]
        kvs = [_dot(v[rs, heads[h]].T.astype(BF), k_state[rs, heads[h]].astype(BF))
               for h in range(HGRN_HEADS)]
        inter = [_dot_nt(qds[h], sts[h].astype(BF)) for h in range(HGRN_HEADS)]
        for h in range(HGRN_HEADS):
            A = jnp.where(causal, As[h], 0.0).astype(BF)
            os_ref[rs, heads[h]] = _dot(A, vb[rs, heads[h]]) + inter[h]
            sts[h] = decay[n * C:n * C + 1, heads[h]] * sts[h] + kvs[h]
    for h in range(HGRN_HEADS):
        st_ref[h] = sts[h]

    g = g_ref[...]
    gate = g * (1.0 / (1.0 + jnp.exp(-g)))
    for h in range(HGRN_HEADS):
        hs = slice(h * HGRN_DK, (h + 1) * HGRN_DK)
        o_ref[:, hs] = (_rms(os_ref[:, hs], nw_ref[...]) * gate[:, hs]).astype(BF)


def _hgrn(gh, lb3, nw, B, S, tb=256):
    nb = S // tb
    col = lambda j: (lambda b, i: (b * nb + i, j))
    return pl.pallas_call(
        functools.partial(_hgrn_kernel, tb=tb),
        grid=(B, nb),
        in_specs=[pl.BlockSpec((tb, HGRN_W), col(0)), pl.BlockSpec((tb, HGRN_W), col(1)),
                  pl.BlockSpec((tb, HGRN_W), col(2)), pl.BlockSpec((tb, HGRN_W), col(3)),
                  pl.BlockSpec((3, HGRN_W), lambda b, i: (0, 0)),
                  pl.BlockSpec((1, HGRN_DV), lambda b, i: (0, 0))],
        out_specs=pl.BlockSpec((tb, HGRN_W), col(0)),
        out_shape=jax.ShapeDtypeStruct((B * S, HGRN_W), BF),
        scratch_shapes=[pltpu.VMEM((HGRN_HEADS, HGRN_DV, HGRN_DK), F32),
                        pltpu.VMEM((tb, HGRN_W), F32)],
        compiler_params=_cparams(("parallel", "arbitrary")),
        name="hgrn",
    )(gh, gh, gh, gh, lb3, nw)


def _mla_kernel(q_ref, k_ref, vt_ref, o_ref, *, t, tk):
    i = pl.program_id(1)
    n_full = (i * t) // tk
    key = lax.broadcasted_iota(jnp.int32, (tk, t), 0)
    qry = lax.broadcasted_iota(jnp.int32, (tk, t), 1)
    causal = n_full * tk + key <= i * t + qry

    heads = [slice(h * HEAD_PAD, (h + 1) * HEAD_PAD) for h in range(MLA_HEADS)]

    def logits(j):
        ks = pl.ds(pl.multiple_of(j * tk, tk), tk)
        return tuple(_dot_nt(k_ref[ks, hs], q_ref[:, hs]) for hs in heads)

    def step(tiles, state, mask_last):
        ss = [list(logits(j)) for j in tiles]
        if mask_last:
            ss[-1] = [jnp.where(causal, s, NEG) for s in ss[-1]]
        ms = []
        for h in range(MLA_HEADS):
            m = state[h][0]
            for s in ss:
                m = jnp.maximum(m, jnp.max(s[h], axis=0, keepdims=True))
            ms.append(m)
        out = []
        for h in range(MLA_HEADS):
            acc = jnp.exp2(state[h][0] - ms[h]) * state[h][1]
            for j, s in zip(tiles, ss):
                acc = acc + _dot(vt_ref[j, heads[h], :], jnp.exp2(s[h] - ms[h]).astype(BF))
            out.append((ms[h], acc))
        return tuple(out)

    init = tuple((jnp.full((1, t), NEG, F32), jnp.zeros((HEAD_PAD, t), F32))
                 for _ in range(MLA_HEADS))
    state = lax.fori_loop(0, n_full // 2, lambda jj, c: step([2 * jj, 2 * jj + 1], c, False), init)
    state = lax.cond(n_full % 2 == 1,
                     lambda c: step([n_full - 1, n_full], c, True),
                     lambda c: step([n_full], c, True), state)
    outs = [acc[:MLA_V, :] * (1.0 / acc[MLA_V:MLA_V + 1, :]) for (_, acc) in state]
    o_ref[...] = jnp.concatenate(outs, axis=0).T.astype(BF)


def _mla_attn(q, k, vt, B, S, t=ATT_TILE, tk=MLA_KEY_TILE):
    nq = S // t
    nk = S // tk
    hp = MLA_HEADS * HEAD_PAD
    return pl.pallas_call(
        functools.partial(_mla_kernel, t=t, tk=tk),
        grid=(B, nq),
        in_specs=[pl.BlockSpec((t, hp), lambda b, i: (b * nq + i, 0)),
                  pl.BlockSpec((S, hp), lambda b, i: (b, 0)),
                  pl.BlockSpec((nk, hp, tk), lambda b, i: (b, 0, 0))],
        out_specs=pl.BlockSpec((t, MLA_HEADS * MLA_V), lambda b, i: (b * nq + i, 0)),
        out_shape=jax.ShapeDtypeStruct((B * S, MLA_HEADS * MLA_V), BF),
        compiler_params=_cparams(("parallel", "arbitrary")),
        name="mla_attn",
    )(q, k, vt)


BISECT_CAP = 48
END_STEPS = 4
_LAST_BUCKET_FROM = next(
    n for n in range(MAX_EXACT, 1 << 20)
    if MAX_EXACT + int(math.log(n / MAX_EXACT) / math.log(MAX_DISTANCE / MAX_EXACT)
                       * (NUM_BUCKETS - MAX_EXACT)) >= NUM_BUCKETS - 1)
assert _LAST_BUCKET_FROM <= LANES - 1 and _LAST_BUCKET_FROM <= MAX_DISTANCE


def _dsa_kernel(qi_ref, w_ref, qd_ref, pos_ref, ki_ref, kd_ref, vt_ref, tab_ref, pmin_ref, pmax_ref,
                run_ref, o_ref, sc_ref, qm_ref, bias_ref, *, t, n_sel):
    i = pl.program_id(1)
    lane = lax.broadcasted_iota(jnp.int32, (t, LANES), 1)
    lo_half = lane < DSA_HEAD_DIM
    key = lax.broadcasted_iota(jnp.int32, (t, t), 0)
    qry = lax.broadcasted_iota(jnp.int32, (t, t), 1)
    causal = key <= qry
    k_sel = float(n_sel)

    def key_rows(j):
        return pl.ds(pl.multiple_of(j * t, t), t)

    @pl.when(i + 1 < sc_ref.shape[0])
    def _():
        sc_ref[i + 1] = jnp.full((t, t), -jnp.inf, F32)

    def over_tiles(body, init):
        return lax.fori_loop(0, (i + 2) // 2, lambda jj, c: body(2 * jj + 1, body(2 * jj, c)), init)

    def fold(x, op):
        return op(x.reshape(t // SUBLANES, SUBLANES, t), axis=0)

    zero_b = jnp.zeros((t, LANES), BF)
    for p in range(IDX_HEADS // 2):
        qp = qi_ref[:, p * LANES:(p + 1) * LANES]
        qm_ref[2 * p] = jnp.where(lo_half, qp, zero_b)
        qm_ref[2 * p + 1] = jnp.where(lo_half, zero_b, qp)
    wt = w_ref[...].T * ((IDX_HEADS ** -0.5) * (IDX_DIM ** -0.5))

    def score_tiles(js):
        kts = [ki_ref[key_rows(j), :] for j in js]
        accs = [None] * len(js)
        for h0 in range(0, IDX_HEADS, 4):
            ss = [[_dot_nt(kt, qm_ref[h]) for h in range(h0, h0 + 4)] for kt in kts]
            for n in range(len(js)):
                for m, s in enumerate(ss[n]):
                    term = jnp.maximum(s, 0.0) * wt[h0 + m:h0 + m + 1, :]
                    accs[n] = term if accs[n] is None else accs[n] + term
        return accs

    def add_stats(x, c):
        mx, mn, cp, cz = c
        return (jnp.maximum(mx, fold(x, jnp.max)),
                jnp.minimum(mn, fold(jnp.where(x == -jnp.inf, jnp.inf, x), jnp.min)),
                cp + fold(jnp.where(x > 0.0, 1.0, 0.0), jnp.sum),
                cz + fold(jnp.where(x == 0.0, 1.0, 0.0), jnp.sum))

    def p1_tiles(js, c, diag_last):
        xs = score_tiles(js)
        if diag_last:
            xs[-1] = jnp.where(causal, xs[-1], -jnp.inf)
        for j, x in zip(js, xs):
            sc_ref[j] = x
            c = add_stats(x, c)
        return c

    z8 = jnp.zeros((SUBLANES, t), F32)
    stats = lax.fori_loop(0, i // 2, lambda jj, c: p1_tiles([2 * jj, 2 * jj + 1], c, False),
                          (z8 - jnp.inf, z8 + jnp.inf, z8, z8))
    mx, mn, cp, cz = lax.cond(i % 2 == 1,
                              lambda c: p1_tiles([i - 1, i], c, True),
                              lambda c: p1_tiles([i], c, True), stats)

    def total(c):
        return jnp.sum(c, axis=0, keepdims=True)

    def count_gt(thr):
        def body(j, c):
            return c + fold(jnp.where(sc_ref[j] > thr, 1.0, 0.0), jnp.sum)
        return total(over_tiles(body, jnp.zeros((SUBLANES, t), F32)))

    row_max = jnp.max(mx, axis=0, keepdims=True)
    row_min = jnp.min(mn, axis=0, keepdims=True)
    c_pos, c_zero = total(cp), total(cz)
    n_valid = (i * t + 1 + lax.broadcasted_iota(jnp.int32, (1, t), 1)).astype(F32)

    big = n_valid > k_sel
    pos_q = jnp.logical_and(big, c_pos >= k_sel)
    tie_q = jnp.logical_and(jnp.logical_and(big, c_pos < k_sel), c_pos + c_zero >= k_sel)
    neg_q = jnp.logical_and(big, c_pos + c_zero < k_sel)
    need = jnp.where(tie_q, k_sel - c_pos, 0.0)
    lo0 = jnp.where(jnp.logical_or(pos_q, tie_q), 0.0, -jnp.inf)
    hi0 = jnp.where(neg_q, 0.0, row_max)
    c_lo0 = jnp.where(pos_q, c_pos, jnp.where(tie_q, k_sel, n_valid))

    searched = c_lo0 > k_sel

    def open_queries(c_lo):
        return jnp.max(jnp.where(c_lo > k_sel + END_STEPS, 1.0, 0.0))

    def next_above(thr):
        def body(j, m):
            x = sc_ref[j]
            return jnp.minimum(m, fold(jnp.where(x > thr, x, jnp.inf), jnp.min))
        return jnp.min(over_tiles(body, jnp.full((SUBLANES, t), jnp.inf, F32)), axis=0, keepdims=True)

    def bis_cond(c):
        return jnp.logical_and(c[0] < BISECT_CAP, c[1] > 0.0)

    def bisect_once(lo, hi, c_lo):
        active = c_lo > k_sel
        base = jnp.maximum(lo, row_min)
        mid = base + 0.5 * (hi - base)
        cnt = count_gt(mid)
        up = jnp.logical_and(active, cnt >= k_sel)
        dn = jnp.logical_and(active, cnt < k_sel)
        return jnp.where(up, mid, lo), jnp.where(dn, mid, hi), jnp.where(up, cnt, c_lo)

    def bis_body(c):
        it, _, lo, hi, c_lo = c
        lo, hi, c_lo = bisect_once(*bisect_once(lo, hi, c_lo))
        return it + 2, open_queries(c_lo), lo, hi, c_lo

    _, _, lo, _, c_lo = lax.while_loop(
        bis_cond, bis_body, (jnp.int32(0), open_queries(c_lo0), lo0, hi0, c_lo0))

    def step_body(_, c):
        lo, c_lo = c
        more = c_lo > k_sel
        return jnp.where(more, next_above(lo), lo), jnp.where(more, c_lo - 1.0, c_lo)

    lo, _ = lax.fori_loop(0, END_STEPS, step_body, (lo, c_lo))
    c_lo = jnp.where(searched, count_gt(lo), c_lo)

    still_open = c_lo != k_sel
    still_open = jnp.logical_and(searched, still_open)

    def exact_kth(_):
        def to_float(kk):
            return lax.bitcast_convert_type(jnp.where(kk < 0, kk ^ jnp.int32(0x7FFFFFFF), kk), F32)

        def count_ge(thr):
            def body(j, c):
                return c + fold(jnp.where(sc_ref[j] >= thr, 1.0, 0.0), jnp.sum)
            return total(over_tiles(body, jnp.zeros((SUBLANES, t), F32)))

        def body(b, kth):
            cand = kth + lax.shift_left(jnp.int32(1), 31 - b)
            return jnp.where(count_ge(to_float(cand)) >= k_sel, cand, kth)

        kth = to_float(lax.fori_loop(0, 32, body, jnp.full((1, t), -2 ** 31, jnp.int32)))
        return kth, count_gt(kth)

    kth, above = lax.cond(jnp.max(jnp.where(still_open, 1.0, 0.0)) > 0.0, exact_kth,
                          lambda _: (jnp.zeros((1, t), F32), jnp.zeros((1, t), F32)), 0)
    lo = jnp.where(still_open, kth, lo)
    need = jnp.where(still_open, k_sel - above, need)
    tie_val = jnp.where(still_open, kth, 0.0)

    def mask_plain(_):
        def body(j, c):
            sc_ref[j] = jnp.where(sc_ref[j] > lo, 0.0, NEG)
            return c
        return lax.fori_loop(0, i + 1, body, 0)

    def mask_ties(_):
        lower = jnp.where(qry <= key, 1.0, 0.0).astype(BF)

        def body(jj, seen):
            xs = [sc_ref[2 * jj + n] for n in range(2)]
            tied = [x == tie_val for x in xs]
            tfs = [jnp.where(m, 1.0, 0.0) for m in tied]
            ranks = [_dot(lower, tf.astype(BF)) for tf in tfs]
            for n in range(2):
                take = jnp.logical_and(tied[n], ranks[n] + seen <= need)
                sc_ref[2 * jj + n] = jnp.where(jnp.logical_or(xs[n] > lo, take), 0.0, NEG)
                seen = seen + total(fold(tfs[n], jnp.sum))
            return seen
        lax.fori_loop(0, (i + 2) // 2, body, jnp.zeros((1, t), F32))
        return 0

    lax.cond(jnp.max(need) > 0.0, mask_ties, mask_plain, 0)

    qh = []
    for p in range(DSA_HEADS // 2):
        qp = qd_ref[:, p * LANES:(p + 1) * LANES]
        qh.append(jnp.where(lo_half, qp, zero_b))
        qh.append(jnp.where(lo_half, zero_b, qp))
    pq = pos_ref[pl.ds(i, 1), :]
    pq_min = pmin_ref[pl.program_id(0), i]
    far_bias = [tab_ref[NUM_BUCKETS - 1, h] * LOG2E for h in range(DSA_HEADS)]
    log_ratio = math.log(MAX_DISTANCE / MAX_EXACT)

    dist = lax.broadcasted_iota(jnp.int32, (SUBLANES, LANES), 1)
    large = MAX_EXACT + (jnp.log(jnp.maximum(dist, 1).astype(F32) / MAX_EXACT) / log_ratio
                         * (NUM_BUCKETS - MAX_EXACT)).astype(jnp.int32)
    bucket = jnp.where(dist < MAX_EXACT, dist, jnp.minimum(large, NUM_BUCKETS - 1))
    by_dist = []
    for h in range(DSA_HEADS):
        bh = jnp.full((SUBLANES, LANES), tab_ref[0, h], F32)
        for jb in range(1, NUM_BUCKETS):
            bh = jnp.where(bucket >= jb, tab_ref[jb, h], bh)
        by_dist.append(jnp.concatenate([bh * LOG2E] * (t // SUBLANES), axis=0))

    def bias_of(n):
        n = jnp.clip(n, 0, LANES - 1)
        return [jnp.concatenate([jnp.take_along_axis(by_dist[h], n[:, c * LANES:(c + 1) * LANES], axis=1)
                                 for c in range(t // LANES)], axis=1) for h in range(DSA_HEADS)]

    def pair_bias(pk_row):
        pk = jnp.broadcast_to(pk_row, (SUBLANES, t)).T[:, 0:1]
        return bias_of(pq - pk)

    b_idx = pl.program_id(0)

    @pl.when(jnp.logical_and(b_idx == 0, i == 0))
    def _():
        for gap in range(2):
            for h, tile in enumerate(bias_of(gap * t + qry - key)):
                bias_ref[gap * DSA_HEADS + h] = tile

    def attend(tiles, carry, bias):
        ss = []
        for j in tiles:
            madd = sc_ref[j]
            ks = key_rows(j)
            kps = [kd_ref[ks, p * LANES:(p + 1) * LANES] for p in range(DSA_HEADS // 2)]
            ss.append([_dot_nt(kps[h // 2], qh[h]) + (madd + bias[h]) for h in range(DSA_HEADS)])
        ms = []
        for h in range(DSA_HEADS):
            m = carry[h][0]
            for s in ss:
                m = jnp.maximum(m, jnp.max(s[h], axis=0, keepdims=True))
            ms.append(m)
        out = []
        for h in range(DSA_HEADS):
            acc = jnp.exp2(carry[h][0] - ms[h]) * carry[h][1]
            for j, s in zip(tiles, ss):
                p = jnp.exp2(s[h] - ms[h]).astype(BF)
                acc = acc + _dot(vt_ref[j, h * DSA_V_ROWS:(h + 1) * DSA_V_ROWS, :], p)
            out.append((ms[h], acc))
        return tuple(out)

    def is_far(j):
        return pq_min - pmax_ref[b_idx, j] >= MAX_DISTANCE

    def p3_body(j, carry):
        gap = i - j
        consecutive = jnp.logical_and(
            jnp.logical_and(run_ref[b_idx, i] == 1, run_ref[b_idx, j] == 1),
            jnp.logical_and(gap <= 1, pq_min - pmin_ref[b_idx, j] == gap * t))

        def near(c):
            return lax.cond(
                consecutive,
                lambda c: attend([j], c, [bias_ref[gap * DSA_HEADS + h] for h in range(DSA_HEADS)]),
                lambda c: attend([j], c, pair_bias(pos_ref[pl.ds(j, 1), :])), c)

        return lax.cond(is_far(j), lambda c: attend([j], c, far_bias), near, carry)

    def p3_pair(jj, carry):
        j0 = 2 * jj
        n_here = jnp.minimum(i + 1 - j0, 2)
        both_far = jnp.logical_and(n_here == 2,
                                   jnp.logical_and(is_far(j0), is_far(jnp.minimum(j0 + 1, i))))
        return lax.cond(both_far,
                        lambda c: attend([j0, j0 + 1], c, far_bias),
                        lambda c: lax.fori_loop(0, n_here, lambda n, c: p3_body(j0 + n, c), c),
                        carry)

    init = tuple((jnp.full((1, t), NEG, F32), jnp.zeros((DSA_V_ROWS, t), F32))
                 for _ in range(DSA_HEADS))
    carry = lax.fori_loop(0, (i + 2) // 2, p3_pair, init)
    outs = [acc[:DSA_HEAD_DIM, :] * (1.0 / acc[DSA_HEAD_DIM:DSA_HEAD_DIM + 1, :])
            for (_, acc) in carry]
    o_ref[...] = jnp.concatenate(outs, axis=0).T.astype(BF)


def _dsa(gq, gk, gd, ki2, vdt, pos_tiles, tab, B, S, t=ATT_TILE):
    nq = S // t
    n_sel = min(TOPK_MAX, S // 4)
    vrows = DSA_HEADS * DSA_V_ROWS
    qrow = lambda c: (lambda b, i: (b * nq + i, c))
    return pl.pallas_call(
        functools.partial(_dsa_kernel, t=t, n_sel=n_sel),
        grid=(B, nq),
        in_specs=[pl.BlockSpec((t, W_Q), qrow(0)),
                  pl.BlockSpec((t, LANES), qrow(1)),
                  pl.BlockSpec((t, DSA_WIDTH), qrow(0)),
                  pl.BlockSpec((None, nq, t), lambda b, i: (b, 0, 0)),
                  pl.BlockSpec((S, LANES), lambda b, i: (b, 0)),
                  pl.BlockSpec((S, DSA_WIDTH), lambda b, i: (b, 1)),
                  pl.BlockSpec((nq, vrows, t), lambda b, i: (b, 0, 0)),
                  pl.BlockSpec(memory_space=pltpu.SMEM), pl.BlockSpec(memory_space=pltpu.SMEM),
                  pl.BlockSpec(memory_space=pltpu.SMEM), pl.BlockSpec(memory_space=pltpu.SMEM)],
        out_specs=pl.BlockSpec((t, DSA_WIDTH), qrow(0)),
        out_shape=jax.ShapeDtypeStruct((B * S, DSA_WIDTH), BF),
        scratch_shapes=[pltpu.VMEM((nq, t, t), F32), pltpu.VMEM((IDX_HEADS, t, LANES), BF),
                        pltpu.VMEM((2 * DSA_HEADS, t, t), F32)],
        compiler_params=_cparams(("arbitrary", "arbitrary")),
        name="dsa",
    )(gq, gk, gd, pos_tiles, ki2, gd, vdt, tab, jnp.min(pos_tiles, axis=-1), jnp.max(pos_tiles, axis=-1),
      jnp.all(pos_tiles[..., 1:] - pos_tiles[..., :-1] == 1, axis=-1).astype(jnp.int32))


def _outmlp_kernel(h_ref, yh_ref, ym_ref, yd_ref, wo_ref, nw_ref, w1_ref, w2_ref, fw_ref, o_ref,
                   *, final, ff_chunk):
    mixed = jnp.concatenate([yh_ref[...], ym_ref[...], yd_ref[...]], axis=1)
    h = h_ref[...] + _dot(mixed, wo_ref[...])
    u = _rms(h, nw_ref[...]).astype(BF)
    out = h
    for c in range(D_FF // ff_chunk):
        cs = slice(c * ff_chunk, (c + 1) * ff_chunk)
        a = jnp.maximum(_dot(u, w1_ref[:, cs]), 0.0)
        out = out + _dot((a * a).astype(BF), w2_ref[cs, :])
    if final:
        out = _rms(out, fw_ref[...])
    o_ref[...] = out


def _outmlp(h, yh, ym, yd, wo, nw, w1, w2, fw, final, tm=ROW_TILE, ff_chunk=1024):
    T = h.shape[0]
    tm = min(tm, T)
    row = lambda i: (i, 0)
    return pl.pallas_call(
        functools.partial(_outmlp_kernel, final=final, ff_chunk=ff_chunk),
        grid=(T // tm,),
        in_specs=[pl.BlockSpec((tm, D_MODEL), row), pl.BlockSpec((tm, HGRN_W), row),
                  pl.BlockSpec((tm, MLA_HEADS * MLA_V), row), pl.BlockSpec((tm, DSA_WIDTH), row),
                  _const_spec((D_MODEL, D_MODEL)), _const_spec((1, D_MODEL)),
                  _const_spec((D_MODEL, D_FF)), _const_spec((D_FF, D_MODEL)),
                  _const_spec((1, D_MODEL))],
        out_specs=pl.BlockSpec((tm, D_MODEL), row),
        out_shape=jax.ShapeDtypeStruct((T, D_MODEL), F32),
        compiler_params=_cparams(("parallel",)),
        name="outmlp",
    )(h, yh, ym, yd, wo, nw, w1, w2, fw)


def _rot_cols(w):
    half = w.shape[-1] // 2
    return jnp.concatenate([-w[..., half:], w[..., :half]], axis=-1)


def _place(w, width, off):
    pad = [(0, 0)] * (w.ndim - 1) + [(off, width - off - w.shape[-1])]
    return jnp.pad(w, pad)


def _layout_w_in(w_in):
    sizes = (HGRN_W, HGRN_W, HGRN_W, HGRN_W, MLA_Q_LORA, MLA_KV_LORA + MLA_ROPE,
             DSA_WIDTH, DSA_WIDTH, DSA_WIDTH, IDX_HEADS * IDX_DIM, IDX_DIM, IDX_HEADS)
    offs = [0]
    for s in sizes:
        offs.append(offs[-1] + s)
    hq, hf, hi, hg, mqa, mkva, dq, dk, dv, iq, ik, iw = [
        w_in[..., offs[n]:offs[n + 1]] for n in range(len(sizes))]
    ckv, kpe = mkva[..., :MLA_KV_LORA], mkva[..., MLA_KV_LORA:]
    cols = [hq, hf, hi, hg,
            _place(mqa, 256, 0), ckv, _place(kpe, LANES, MLA_NOPE), _place(_rot_cols(kpe), LANES, MLA_NOPE),
            dq * (DSA_HEAD_DIM ** -0.5 * LOG2E), dk,
            iq,
            ik, ik, _place(iw, LANES, 0)]
    w_cat = jnp.concatenate(cols, axis=-1).astype(BF)
    L = w_in.shape[0]
    dvt = jnp.swapaxes(dv, 1, 2).reshape(L, DSA_HEADS, DSA_HEAD_DIM, D_MODEL)
    dvt = jnp.pad(dvt, ((0, 0), (0, 0), (0, DSA_V_ROWS - DSA_HEAD_DIM), (0, 0)))
    return w_cat, dvt.reshape(L, DSA_HEADS * DSA_V_ROWS, D_MODEL).astype(BF)


def _layout_mla(w_qb, w_kvb):
    L = w_qb.shape[0]
    dq = MLA_NOPE + MLA_ROPE
    wq = w_qb.reshape(L, MLA_Q_LORA, MLA_HEADS, dq)
    wq_rot = jnp.concatenate([jnp.zeros_like(wq[..., :MLA_NOPE]), _rot_cols(wq[..., MLA_NOPE:])], axis=-1)
    pad_q = lambda w: jnp.pad(w, ((0, 0), (0, 256 - MLA_Q_LORA), (0, 0), (0, HEAD_PAD - dq))).reshape(
        L, 256, MLA_HEADS * HEAD_PAD).astype(BF)
    wkv = w_kvb.reshape(L, MLA_KV_LORA, MLA_HEADS, MLA_NOPE + MLA_V)
    pad_kv = lambda w: jnp.pad(w, ((0, 0), (0, 0), (0, 0), (0, HEAD_PAD - w.shape[-1]))).reshape(
        L, MLA_KV_LORA, MLA_HEADS * HEAD_PAD).astype(BF)
    wvt = jnp.swapaxes(pad_kv(wkv[..., MLA_NOPE:]), 1, 2)
    return pad_q(wq), pad_q(wq_rot), pad_kv(wkv[..., :MLA_NOPE]), wvt


def kernel(x, positions, attn_norm_w, w_in, hgrn_lb_logits, hgrn_norm_w, mla_q_norm_w, mla_w_qb,
           mla_kv_norm_w, mla_w_kvb, idx_k_norm_w, idx_k_norm_b, rel_bias_table, w_out,
           mlp_norm_w, w_mlp_in, w_mlp_out, final_norm_w):
    B, S, _ = x.shape
    T = B * S
    depth = w_in.shape[0]

    inv_freq = 1.0 / (ROPE_THETA ** (jnp.arange(0, MLA_ROPE, 2, dtype=F32) / MLA_ROPE))
    ang = positions.astype(F32)[..., None] * inv_freq
    cos, sin = jnp.cos(ang).reshape(T, -1), jnp.sin(ang).reshape(T, -1)
    cos_t = jnp.concatenate([jnp.ones((T, MLA_NOPE), F32), cos, cos,
                             jnp.zeros((T, HEAD_PAD - MLA_NOPE - MLA_ROPE), F32)], axis=1)
    sin_t = jnp.concatenate([jnp.zeros((T, MLA_NOPE), F32), sin, sin,
                             jnp.zeros((T, HEAD_PAD - MLA_NOPE - MLA_ROPE), F32)], axis=1)
    pos_tiles = positions.reshape(B, S // ATT_TILE, ATT_TILE)

    lb = jnp.cumsum(jax.nn.softmax(hgrn_lb_logits.astype(F32), axis=0), axis=0)
    lb = lb - lb[0:1]
    lb3 = jnp.stack([jnp.log(lb), jnp.log1p(-lb), 1.0 - lb], axis=1)

    w_cat, w_dvt = _layout_w_in(w_in)
    wq, wqr, wk, wvt = _layout_mla(mla_w_qb, mla_w_kvb)
    qnw = jnp.pad(mla_q_norm_w, ((0, 0), (0, 256 - MLA_Q_LORA)))
    lnw = jnp.concatenate([idx_k_norm_w, idx_k_norm_w], axis=-1)
    lnb = jnp.concatenate([idx_k_norm_b, idx_k_norm_b], axis=-1)
    wo = w_out.astype(BF)
    w1 = w_mlp_in.astype(BF)
    w2 = w_mlp_out.astype(BF)
    tab = rel_bias_table.astype(F32)

    h = x.reshape(T, D_MODEL)
    for l in range(depth):
        gh, gm, gd, gq, gk, vdt = _inproj(h, attn_norm_w[l][None], w_cat[l], w_dvt[l])
        q_m, k_m, v_m, ki2 = _prep(gm, gk, cos_t, sin_t, qnw[l][None], wq[l], wqr[l],
                                   mla_kv_norm_w[l][None], wk[l], wvt[l], lnw[l][None], lnb[l][None])
        y_h = _hgrn(gh, lb3[l], hgrn_norm_w[l][None], B, S)
        y_m = _mla_attn(q_m, k_m, v_m, B, S)
        y_d = _dsa(gq, gk, gd, ki2, vdt, pos_tiles, tab, B, S)
        h = _outmlp(h, y_h, y_m, y_d, wo[l], mlp_norm_w[l][None], w1[l], w2[l],
                    final_norm_w[None], final=(l == depth - 1))
    return h.reshape(B, S, D_MODEL)
```

```python
import functools
import math

import jax
import jax.numpy as jnp
from jax import lax
from jax.experimental import pallas as pl
from jax.experimental.pallas import tpu as pltpu

D_MODEL = 1024
HGRN_HEADS = 4
HGRN_DK = 128
HGRN_DV = 128
HGRN_W = HGRN_HEADS * HGRN_DK
HGRN_CHUNK = 32
MLA_HEADS = 4
MLA_NOPE = 64
MLA_ROPE = 32
MLA_V = 64
MLA_Q_LORA = 192
MLA_KV_LORA = 128
ROPE_THETA = 10000.0
DSA_HEADS = 4
DSA_HEAD_DIM = 64
DSA_WIDTH = DSA_HEADS * DSA_HEAD_DIM
IDX_HEADS = 8
IDX_DIM = 64
TOPK_MAX = 256
NUM_BUCKETS = 32
MAX_EXACT = NUM_BUCKETS // 2
MAX_DISTANCE = 128
D_FF = 4 * D_MODEL
EPS = 1e-6

LANES = 128
SUBLANES = 8
BF16_ROWS = 16
HEAD_PAD = 128
NEG = -1e30
ATT_TILE = 256
MLA_KEY_TILE = 512
ROW_TILE = 512
VMEM_LIMIT = 52 * 1024 * 1024
LOG2E = math.log2(math.e)

DSA_V_ROWS = DSA_HEAD_DIM + BF16_ROWS

C_H = 0
C_M = C_H + 4 * HGRN_W
W_M = 640
C_D = C_M + W_M
W_D = 2 * DSA_WIDTH
C_Q = C_D + W_D
W_Q = IDX_HEADS * IDX_DIM
C_K = C_Q + W_Q
W_K = 256
N_PAD = C_K + W_K

BF = jnp.bfloat16
F32 = jnp.float32


def _dot(a, b):
    return jnp.dot(a, b, preferred_element_type=F32)


def _dot_nt(a, b):
    return lax.dot_general(a, b, (((1,), (1,)), ((), ())), preferred_element_type=F32)


def _rms(x, w, n=None):
    n = x.shape[-1] if n is None else n
    ms = jnp.sum(x * x, axis=-1, keepdims=True) * (1.0 / n)
    return x * lax.rsqrt(ms + EPS) * w


def _cparams(sem):
    return pltpu.CompilerParams(dimension_semantics=sem, vmem_limit_bytes=VMEM_LIMIT)


def _const_spec(shape):
    nd = len(shape)
    return pl.BlockSpec(shape, lambda *_: (0,) * nd, pipeline_mode=pl.Buffered(1))


def _layer_spec(shape, layer):
    nd = len(shape)
    return pl.BlockSpec((None,) + tuple(shape), lambda *_: (layer,) + (0,) * nd,
                        pipeline_mode=pl.Buffered(1))


def _inproj_kernel(x_ref, nw_ref, w_ref, wvt_ref, oh_ref, om_ref, od_ref, oq_ref, ok_ref, ovt_ref):
    u = _rms(x_ref[...], nw_ref[...]).astype(BF)
    oh_ref[...] = _dot(u, w_ref[:, C_H:C_M])
    om_ref[...] = _dot(u, w_ref[:, C_M:C_D])
    od_ref[...] = _dot(u, w_ref[:, C_D:C_Q]).astype(BF)
    oq_ref[...] = _dot(u, w_ref[:, C_Q:C_K]).astype(BF)
    ok_ref[...] = _dot(u, w_ref[:, C_K:N_PAD])
    vt = _dot_nt(wvt_ref[...], u)
    row = lax.broadcasted_iota(jnp.int32, vt.shape, 0)
    vt = jnp.where(row % DSA_V_ROWS == DSA_HEAD_DIM, 1.0, vt).astype(BF)
    for n in range(ovt_ref.shape[0]):
        ovt_ref[n] = vt[:, n * ATT_TILE:(n + 1) * ATT_TILE]


def _inproj(h, nw, w, wvt, layer, tm=ROW_TILE):
    T = h.shape[0]
    tm = min(tm, T)
    row = lambda i: (i, 0)
    vrows = DSA_HEADS * DSA_V_ROWS
    per = tm // ATT_TILE
    return pl.pallas_call(
        _inproj_kernel,
        grid=(T // tm,),
        in_specs=[pl.BlockSpec((tm, D_MODEL), row), _const_spec((1, D_MODEL)),
                  _layer_spec((D_MODEL, N_PAD), layer), _layer_spec((vrows, D_MODEL), layer)],
        out_specs=[pl.BlockSpec((tm, 4 * HGRN_W), row), pl.BlockSpec((tm, W_M), row),
                   pl.BlockSpec((tm, W_D), row), pl.BlockSpec((tm, W_Q), row),
                   pl.BlockSpec((tm, W_K), row),
                   pl.BlockSpec((per, vrows, ATT_TILE), lambda i: (i, 0, 0))],
        out_shape=[jax.ShapeDtypeStruct((T, 4 * HGRN_W), F32), jax.ShapeDtypeStruct((T, W_M), F32),
                   jax.ShapeDtypeStruct((T, W_D), BF), jax.ShapeDtypeStruct((T, W_Q), BF),
                   jax.ShapeDtypeStruct((T, W_K), F32),
                   jax.ShapeDtypeStruct((T // ATT_TILE, vrows, ATT_TILE), BF)],
        compiler_params=_cparams(("parallel",)),
        name="inproj",
    )(h, nw, w, wvt)


def _prep_kernel(gm_ref, gk_ref, cos_ref, sin_ref, qnw_ref, wq_ref, wqr_ref, kvnw_ref, wk_ref,
                 wvt_ref, lnw_ref, lnb_ref, q_ref, k_ref, vt_ref, ki_ref):
    cs = cos_ref[...]
    sn = sin_ref[...]
    cs4 = jnp.concatenate([cs] * MLA_HEADS, axis=1)
    sn4 = jnp.concatenate([sn] * MLA_HEADS, axis=1)
    qn = _rms(gm_ref[:, 0:256], qnw_ref[...], n=MLA_Q_LORA).astype(BF)
    scale = (MLA_NOPE + MLA_ROPE) ** -0.5 * LOG2E
    q = (_dot(qn, wq_ref[...]) * cs4 + _dot(qn, wqr_ref[...]) * sn4) * scale
    q_ref[...] = q.astype(BF)
    cn = _rms(gm_ref[:, 256:384], kvnw_ref[...]).astype(BF)
    kp = gm_ref[:, 384:512] * cs + gm_ref[:, 512:640] * sn
    k = _dot(cn, wk_ref[...]) + jnp.concatenate([kp] * MLA_HEADS, axis=1)
    k_ref[...] = k.astype(BF)
    vt = _dot_nt(wvt_ref[...], cn)
    row = lax.broadcasted_iota(jnp.int32, vt.shape, 0)
    vt_ref[...] = jnp.where(row % HEAD_PAD == MLA_V, 1.0, vt).astype(BF)
    x = gk_ref[:, 0:LANES]
    first = lax.broadcasted_iota(jnp.int32, x.shape, 1) < IDX_DIM
    mu = jnp.sum(jnp.where(first, x, 0.0), axis=-1, keepdims=True) * (1.0 / IDX_DIM)
    xc = x - mu
    var = jnp.sum(jnp.where(first, xc * xc, 0.0), axis=-1, keepdims=True) * (1.0 / IDX_DIM)
    ki_ref[...] = (xc * lax.rsqrt(var + EPS) * lnw_ref[...] + lnb_ref[...]).astype(BF)


def _prep(gm, gk, cos_t, sin_t, qnw, wq, wqr, kvnw, wk, wvt, lnw, lnb, tm=MLA_KEY_TILE):
    T = gm.shape[0]
    row = lambda i: (i, 0)
    hp = MLA_HEADS * HEAD_PAD
    return pl.pallas_call(
        _prep_kernel,
        grid=(T // tm,),
        in_specs=[pl.BlockSpec((tm, W_M), row), pl.BlockSpec((tm, W_K), row),
                  pl.BlockSpec((tm, LANES), row), pl.BlockSpec((tm, LANES), row),
                  _const_spec((1, 256)), _const_spec((256, hp)), _const_spec((256, hp)),
                  _const_spec((1, MLA_KV_LORA)), _const_spec((MLA_KV_LORA, hp)),
                  _const_spec((hp, MLA_KV_LORA)), _const_spec((1, LANES)), _const_spec((1, LANES))],
        out_specs=[pl.BlockSpec((tm, hp), row), pl.BlockSpec((tm, hp), row),
                   pl.BlockSpec((None, hp, tm), lambda i: (i, 0, 0)), pl.BlockSpec((tm, LANES), row)],
        out_shape=[jax.ShapeDtypeStruct((T, hp), BF), jax.ShapeDtypeStruct((T, hp), BF),
                   jax.ShapeDtypeStruct((T // tm, hp, tm), BF), jax.ShapeDtypeStruct((T, LANES), BF)],
        compiler_params=_cparams(("parallel",)),
        name="prep",
    )(gm, gk, cos_t, sin_t, qnw, wq, wqr, kvnw, wk, wvt, lnw, lnb)


def _split3(x):
    a = x.astype(BF)
    r = x - a.astype(F32)
    b = r.astype(BF)
    c = (r - b.astype(F32)).astype(BF)
    return a, b, c


def _hgrn_kernel(q_ref, f_ref, i_ref, g_ref, lb_ref, nw_ref, o_ref, st_ref, os_ref, *, tb):
    C = HGRN_CHUNK
    nc = tb // C

    @pl.when(pl.program_id(1) == 0)
    def _():
        st_ref[...] = jnp.zeros_like(st_ref)

    fp = f_ref[...]
    log_lb = lb_ref[0:1, :]
    log1m_lb = lb_ref[1:2, :]
    one_m_lb = lb_ref[2:3, :]
    ls = jnp.minimum(fp, 0.0) - jnp.log1p(jnp.exp(-jnp.abs(fp)))
    b = log1m_lb + ls
    log_f = jnp.maximum(log_lb, b) + jnp.log1p(jnp.exp(-jnp.abs(log_lb - b)))
    k = one_m_lb * (1.0 / (1.0 + jnp.exp(fp)))

    r = lax.broadcasted_iota(jnp.int32, (tb, tb), 0)
    c = lax.broadcasted_iota(jnp.int32, (tb, tb), 1)
    same = (r // C) == (c // C)
    tri = jnp.where(same & (c <= r), 1.0, 0.0).astype(BF)
    blk = jnp.where(same, 1.0, 0.0).astype(BF)
    a0, a1, a2 = _split3(log_f)
    G = _dot(tri, a0) + _dot(tri, a1) + _dot(tri, a2)
    G_last = _dot(blk, a0) + _dot(blk, a1) + _dot(blk, a2)

    eg = jnp.exp(G)
    q_dec = (q_ref[...] * (HGRN_DK ** -0.5) * eg).astype(BF)
    k_inv = (k * jnp.exp(-G)).astype(BF)
    k_state = k * jnp.exp(G_last - G)
    decay = jnp.exp(G_last)
    v = i_ref[...]
    vb = v.astype(BF)

    rr = lax.broadcasted_iota(jnp.int32, (C, C), 0)
    cc = lax.broadcasted_iota(jnp.int32, (C, C), 1)
    causal = cc <= rr

    heads = [slice(h * HGRN_DK, (h + 1) * HGRN_DK) for h in range(HGRN_HEADS)]
    sts = [st_ref[h] for h in range(HGRN_HEADS)]
    for n in range(nc):
        rs = slice(n * C, (n + 1) * C)
        qds = [q_dec[rs, hs] for hs in heads]
        As = [_dot_nt(qds[h], k_inv[rs, heads[h]]) for h in range(HGRN_HEADS)]
        kvs = [_dot(v[rs, heads[h]].T.astype(BF), k_state[rs, heads[h]].astype(BF))
               for h in range(HGRN_HEADS)]
        inter = [_dot_nt(qds[h], sts[h].astype(BF)) for h in range(HGRN_HEADS)]
        for h in range(HGRN_HEADS):
            A = jnp.where(causal, As[h], 0.0).astype(BF)
            os_ref[rs, heads[h]] = _dot(A, vb[rs, heads[h]]) + inter[h]
            sts[h] = decay[n * C:n * C + 1, heads[h]] * sts[h] + kvs[h]
    for h in range(HGRN_HEADS):
        st_ref[h] = sts[h]

    g = g_ref[...]
    gate = g * (1.0 / (1.0 + jnp.exp(-g)))
    for h in range(HGRN_HEADS):
        hs = slice(h * HGRN_DK, (h + 1) * HGRN_DK)
        o_ref[:, hs] = (_rms(os_ref[:, hs], nw_ref[...]) * gate[:, hs]).astype(BF)


def _hgrn(gh, lb3, nw, B, S, tb=256):
    nb = S // tb
    col = lambda j: (lambda b, i: (b * nb + i, j))
    return pl.pallas_call(
        functools.partial(_hgrn_kernel, tb=tb),
        grid=(B, nb),
        in_specs=[pl.BlockSpec((tb, HGRN_W), col(0)), pl.BlockSpec((tb, HGRN_W), col(1)),
                  pl.BlockSpec((tb, HGRN_W), col(2)), pl.BlockSpec((tb, HGRN_W), col(3)),
                  pl.BlockSpec((3, HGRN_W), lambda b, i: (0, 0)),
                  pl.BlockSpec((1, HGRN_DV), lambda b, i: (0, 0))],
        out_specs=pl.BlockSpec((tb, HGRN_W), col(0)),
        out_shape=jax.ShapeDtypeStruct((B * S, HGRN_W), BF),
        scratch_shapes=[pltpu.VMEM((HGRN_HEADS, HGRN_DV, HGRN_DK), F32),
                        pltpu.VMEM((tb, HGRN_W), F32)],
        compiler_params=_cparams(("parallel", "arbitrary")),
        name="hgrn",
    )(gh, gh, gh, gh, lb3, nw)


def _mla_kernel(q_ref, k_ref, vt_ref, o_ref, *, t, tk):
    i = pl.program_id(1)
    n_full = (i * t) // tk
    key = lax.broadcasted_iota(jnp.int32, (tk, t), 0)
    qry = lax.broadcasted_iota(jnp.int32, (tk, t), 1)
    causal = n_full * tk + key <= i * t + qry

    heads = [slice(h * HEAD_PAD, (h + 1) * HEAD_PAD) for h in range(MLA_HEADS)]

    def logits(j):
        ks = pl.ds(pl.multiple_of(j * tk, tk), tk)
        return tuple(_dot_nt(k_ref[ks, hs], q_ref[:, hs]) for hs in heads)

    def step(tiles, state, mask_last):
        ss = [list(logits(j)) for j in tiles]
        if mask_last:
            ss[-1] = [jnp.where(causal, s, NEG) for s in ss[-1]]
        ms = []
        for h in range(MLA_HEADS):
            m = state[h][0]
            for s in ss:
                m = jnp.maximum(m, jnp.max(s[h], axis=0, keepdims=True))
            ms.append(m)
        out = []
        for h in range(MLA_HEADS):
            acc = jnp.exp2(state[h][0] - ms[h]) * state[h][1]
            for j, s in zip(tiles, ss):
                acc = acc + _dot(vt_ref[j, heads[h], :], jnp.exp2(s[h] - ms[h]).astype(BF))
            out.append((ms[h], acc))
        return tuple(out)

    init = tuple((jnp.full((1, t), NEG, F32), jnp.zeros((HEAD_PAD, t), F32))
                 for _ in range(MLA_HEADS))
    state = lax.fori_loop(0, n_full // 2, lambda jj, c: step([2 * jj, 2 * jj + 1], c, False), init)
    state = lax.cond(n_full % 2 == 1,
                     lambda c: step([n_full - 1, n_full], c, True),
                     lambda c: step([n_full], c, True), state)
    outs = [acc[:MLA_V, :] * (1.0 / acc[MLA_V:MLA_V + 1, :]) for (_, acc) in state]
    o_ref[...] = jnp.concatenate(outs, axis=0).T.astype(BF)


def _mla_attn(q, k, vt, B, S, t=ATT_TILE, tk=MLA_KEY_TILE):
    nq = S // t
    nk = S // tk
    hp = MLA_HEADS * HEAD_PAD
    return pl.pallas_call(
        functools.partial(_mla_kernel, t=t, tk=tk),
        grid=(B, nq),
        in_specs=[pl.BlockSpec((t, hp), lambda b, i: (b * nq + i, 0)),
                  pl.BlockSpec((S, hp), lambda b, i: (b, 0)),
                  pl.BlockSpec((nk, hp, tk), lambda b, i: (b, 0, 0))],
        out_specs=pl.BlockSpec((t, MLA_HEADS * MLA_V), lambda b, i: (b * nq + i, 0)),
        out_shape=jax.ShapeDtypeStruct((B * S, MLA_HEADS * MLA_V), BF),
        compiler_params=_cparams(("parallel", "arbitrary")),
        name="mla_attn",
    )(q, k, vt)


BISECT_CAP = 48
END_STEPS = 4
_LAST_BUCKET_FROM = next(
    n for n in range(MAX_EXACT, 1 << 20)
    if MAX_EXACT + int(math.log(n / MAX_EXACT) / math.log(MAX_DISTANCE / MAX_EXACT)
                       * (NUM_BUCKETS - MAX_EXACT)) >= NUM_BUCKETS - 1)
assert _LAST_BUCKET_FROM <= LANES - 1 and _LAST_BUCKET_FROM <= MAX_DISTANCE


def _dsa_kernel(qi_ref, w_ref, qd_ref, pos_ref, ki_ref, kd_ref, vt_ref, tab_ref, pmin_ref, pmax_ref,
                run_ref, o_ref, sc_ref, qm_ref, bias_ref, *, t, n_sel):
    i = pl.program_id(1)
    lane = lax.broadcasted_iota(jnp.int32, (t, LANES), 1)
    lo_half = lane < DSA_HEAD_DIM
    key = lax.broadcasted_iota(jnp.int32, (t, t), 0)
    qry = lax.broadcasted_iota(jnp.int32, (t, t), 1)
    causal = key <= qry
    k_sel = float(n_sel)

    def key_rows(j):
        return pl.ds(pl.multiple_of(j * t, t), t)

    @pl.when(i + 1 < sc_ref.shape[0])
    def _():
        sc_ref[i + 1] = jnp.full((t, t), -jnp.inf, F32)

    def over_tiles(body, init):
        return lax.fori_loop(0, (i + 2) // 2, lambda jj, c: body(2 * jj + 1, body(2 * jj, c)), init)

    def fold(x, op):
        return op(x.reshape(t // SUBLANES, SUBLANES, t), axis=0)

    zero_b = jnp.zeros((t, LANES), BF)
    for p in range(IDX_HEADS // 2):
        qp = qi_ref[:, p * LANES:(p + 1) * LANES]
        qm_ref[2 * p] = jnp.where(lo_half, qp, zero_b)
        qm_ref[2 * p + 1] = jnp.where(lo_half, zero_b, qp)
    wt = w_ref[...].T * ((IDX_HEADS ** -0.5) * (IDX_DIM ** -0.5))

    def score_tiles(js):
        kts = [ki_ref[key_rows(j), :] for j in js]
        accs = [None] * len(js)
        for h0 in range(0, IDX_HEADS, 4):
            ss = [[_dot_nt(kt, qm_ref[h]) for h in range(h0, h0 + 4)] for kt in kts]
            for n in range(len(js)):
                for m, s in enumerate(ss[n]):
                    term = jnp.maximum(s, 0.0) * wt[h0 + m:h0 + m + 1, :]
                    accs[n] = term if accs[n] is None else accs[n] + term
        return accs

    def add_stats(x, c):
        mx, mn, cp, cz = c
        return (jnp.maximum(mx, fold(x, jnp.max)),
                jnp.minimum(mn, fold(jnp.where(x == -jnp.inf, jnp.inf, x), jnp.min)),
                cp + fold(jnp.where(x > 0.0, 1.0, 0.0), jnp.sum),
                cz + fold(jnp.where(x == 0.0, 1.0, 0.0), jnp.sum))

    def p1_tiles(js, c, diag_last):
        xs = score_tiles(js)
        if diag_last:
            xs[-1] = jnp.where(causal, xs[-1], -jnp.inf)
        for j, x in zip(js, xs):
            sc_ref[j] = x
            c = add_stats(x, c)
        return c

    z8 = jnp.zeros((SUBLANES, t), F32)
    stats = lax.fori_loop(0, i // 2, lambda jj, c: p1_tiles([2 * jj, 2 * jj + 1], c, False),
                          (z8 - jnp.inf, z8 + jnp.inf, z8, z8))
    mx, mn, cp, cz = lax.cond(i % 2 == 1,
                              lambda c: p1_tiles([i - 1, i], c, True),
                              lambda c: p1_tiles([i], c, True), stats)

    def total(c):
        return jnp.sum(c, axis=0, keepdims=True)

    def count_gt(thr):
        def body(j, c):
            return c + fold(jnp.where(sc_ref[j] > thr, 1.0, 0.0), jnp.sum)
        return total(over_tiles(body, jnp.zeros((SUBLANES, t), F32)))

    row_max = jnp.max(mx, axis=0, keepdims=True)
    row_min = jnp.min(mn, axis=0, keepdims=True)
    c_pos, c_zero = total(cp), total(cz)
    n_valid = (i * t + 1 + lax.broadcasted_iota(jnp.int32, (1, t), 1)).astype(F32)

    big = n_valid > k_sel
    pos_q = jnp.logical_and(big, c_pos >= k_sel)
    tie_q = jnp.logical_and(jnp.logical_and(big, c_pos < k_sel), c_pos + c_zero >= k_sel)
    neg_q = jnp.logical_and(big, c_pos + c_zero < k_sel)
    need = jnp.where(tie_q, k_sel - c_pos, 0.0)
    lo0 = jnp.where(jnp.logical_or(pos_q, tie_q), 0.0, -jnp.inf)
    hi0 = jnp.where(neg_q, 0.0, row_max)
    c_lo0 = jnp.where(pos_q, c_pos, jnp.where(tie_q, k_sel, n_valid))

    searched = c_lo0 > k_sel

    def open_queries(c_lo):
        return jnp.max(jnp.where(c_lo > k_sel + END_STEPS, 1.0, 0.0))

    def next_above(thr):
        def body(j, m):
            x = sc_ref[j]
            return jnp.minimum(m, fold(jnp.where(x > thr, x, jnp.inf), jnp.min))
        return jnp.min(over_tiles(body, jnp.full((SUBLANES, t), jnp.inf, F32)), axis=0, keepdims=True)

    def bis_cond(c):
        return jnp.logical_and(c[0] < BISECT_CAP, c[1] > 0.0)

    def bisect_once(lo, hi, c_lo):
        active = c_lo > k_sel
        base = jnp.maximum(lo, row_min)
        mid = base + 0.5 * (hi - base)
        cnt = count_gt(mid)
        up = jnp.logical_and(active, cnt >= k_sel)
        dn = jnp.logical_and(active, cnt < k_sel)
        return jnp.where(up, mid, lo), jnp.where(dn, mid, hi), jnp.where(up, cnt, c_lo)

    def bis_body(c):
        it, _, lo, hi, c_lo = c
        lo, hi, c_lo = bisect_once(*bisect_once(lo, hi, c_lo))
        return it + 2, open_queries(c_lo), lo, hi, c_lo

    _, _, lo, _, c_lo = lax.while_loop(
        bis_cond, bis_body, (jnp.int32(0), open_queries(c_lo0), lo0, hi0, c_lo0))

    def step_body(_, c):
        lo, c_lo = c
        more = c_lo > k_sel
        return jnp.where(more, next_above(lo), lo), jnp.where(more, c_lo - 1.0, c_lo)

    lo, _ = lax.fori_loop(0, END_STEPS, step_body, (lo, c_lo))
    c_lo = jnp.where(searched, count_gt(lo), c_lo)

    still_open = c_lo != k_sel
    still_open = jnp.logical_and(searched, still_open)

    def exact_kth(_):
        def to_float(kk):
            return lax.bitcast_convert_type(jnp.where(kk < 0, kk ^ jnp.int32(0x7FFFFFFF), kk), F32)

        def count_ge(thr):
            def body(j, c):
                return c + fold(jnp.where(sc_ref[j] >= thr, 1.0, 0.0), jnp.sum)
            return total(over_tiles(body, jnp.zeros((SUBLANES, t), F32)))

        def body(b, kth):
            cand = kth + lax.shift_left(jnp.int32(1), 31 - b)
            return jnp.where(count_ge(to_float(cand)) >= k_sel, cand, kth)

        kth = to_float(lax.fori_loop(0, 32, body, jnp.full((1, t), -2 ** 31, jnp.int32)))
        return kth, count_gt(kth)

    kth, above = lax.cond(jnp.max(jnp.where(still_open, 1.0, 0.0)) > 0.0, exact_kth,
                          lambda _: (jnp.zeros((1, t), F32), jnp.zeros((1, t), F32)), 0)
    lo = jnp.where(still_open, kth, lo)
    need = jnp.where(still_open, k_sel - above, need)
    tie_val = jnp.where(still_open, kth, 0.0)

    def mask_plain(_):
        def body(j, c):
            sc_ref[j] = jnp.where(sc_ref[j] > lo, 0.0, NEG)
            return c
        return lax.fori_loop(0, i + 1, body, 0)

    def mask_ties(_):
        lower = jnp.where(qry <= key, 1.0, 0.0).astype(BF)

        def body(jj, seen):
            xs = [sc_ref[2 * jj + n] for n in range(2)]
            tied = [x == tie_val for x in xs]
            tfs = [jnp.where(m, 1.0, 0.0) for m in tied]
            ranks = [_dot(lower, tf.astype(BF)) for tf in tfs]
            for n in range(2):
                take = jnp.logical_and(tied[n], ranks[n] + seen <= need)
                sc_ref[2 * jj + n] = jnp.where(jnp.logical_or(xs[n] > lo, take), 0.0, NEG)
                seen = seen + total(fold(tfs[n], jnp.sum))
            return seen
        lax.fori_loop(0, (i + 2) // 2, body, jnp.zeros((1, t), F32))
        return 0

    lax.cond(jnp.max(need) > 0.0, mask_ties, mask_plain, 0)

    qh = []
    for p in range(DSA_HEADS // 2):
        qp = qd_ref[:, p * LANES:(p + 1) * LANES]
        qh.append(jnp.where(lo_half, qp, zero_b))
        qh.append(jnp.where(lo_half, zero_b, qp))
    pq = pos_ref[pl.ds(i, 1), :]
    pq_min = pmin_ref[pl.program_id(0), i]
    far_bias = [tab_ref[NUM_BUCKETS - 1, h] * LOG2E for h in range(DSA_HEADS)]
    log_ratio = math.log(MAX_DISTANCE / MAX_EXACT)

    dist = lax.broadcasted_iota(jnp.int32, (SUBLANES, LANES), 1)
    large = MAX_EXACT + (jnp.log(jnp.maximum(dist, 1).astype(F32) / MAX_EXACT) / log_ratio
                         * (NUM_BUCKETS - MAX_EXACT)).astype(jnp.int32)
    bucket = jnp.where(dist < MAX_EXACT, dist, jnp.minimum(large, NUM_BUCKETS - 1))
    by_dist = []
    for h in range(DSA_HEADS):
        bh = jnp.full((SUBLANES, LANES), tab_ref[0, h], F32)
        for jb in range(1, NUM_BUCKETS):
            bh = jnp.where(bucket >= jb, tab_ref[jb, h], bh)
        by_dist.append(jnp.concatenate([bh * LOG2E] * (t // SUBLANES), axis=0))

    def bias_of(n):
        n = jnp.clip(n, 0, LANES - 1)
        return [jnp.concatenate([jnp.take_along_axis(by_dist[h], n[:, c * LANES:(c + 1) * LANES], axis=1)
                                 for c in range(t // LANES)], axis=1) for h in range(DSA_HEADS)]

    def pair_bias(pk_row):
        pk = jnp.broadcast_to(pk_row, (SUBLANES, t)).T[:, 0:1]
        return bias_of(pq - pk)

    b_idx = pl.program_id(0)

    @pl.when(jnp.logical_and(b_idx == 0, i == 0))
    def _():
        for gap in range(2):
            for h, tile in enumerate(bias_of(gap * t + qry - key)):
                bias_ref[gap * DSA_HEADS + h] = tile

    def attend(tiles, carry, bias):
        ss = []
        for j in tiles:
            madd = sc_ref[j]
            ks = key_rows(j)
            kps = [kd_ref[ks, p * LANES:(p + 1) * LANES] for p in range(DSA_HEADS // 2)]
            ss.append([_dot_nt(kps[h // 2], qh[h]) + (madd + bias[h]) for h in range(DSA_HEADS)])
        ms = []
        for h in range(DSA_HEADS):
            m = carry[h][0]
            for s in ss:
                m = jnp.maximum(m, jnp.max(s[h], axis=0, keepdims=True))
            ms.append(m)
        out = []
        for h in range(DSA_HEADS):
            acc = jnp.exp2(carry[h][0] - ms[h]) * carry[h][1]
            for j, s in zip(tiles, ss):
                p = jnp.exp2(s[h] - ms[h]).astype(BF)
                acc = acc + _dot(vt_ref[j, h * DSA_V_ROWS:(h + 1) * DSA_V_ROWS, :], p)
            out.append((ms[h], acc))
        return tuple(out)

    def is_far(j):
        return pq_min - pmax_ref[b_idx, j] >= MAX_DISTANCE

    def p3_body(j, carry):
        gap = i - j
        consecutive = jnp.logical_and(
            jnp.logical_and(run_ref[b_idx, i] == 1, run_ref[b_idx, j] == 1),
            jnp.logical_and(gap <= 1, pq_min - pmin_ref[b_idx, j] == gap * t))

        def near(c):
            return lax.cond(
                consecutive,
                lambda c: attend([j], c, [bias_ref[gap * DSA_HEADS + h] for h in range(DSA_HEADS)]),
                lambda c: attend([j], c, pair_bias(pos_ref[pl.ds(j, 1), :])), c)

        return lax.cond(is_far(j), lambda c: attend([j], c, far_bias), near, carry)

    def p3_pair(jj, carry):
        j0 = 2 * jj
        n_here = jnp.minimum(i + 1 - j0, 2)
        both_far = jnp.logical_and(n_here == 2,
                                   jnp.logical_and(is_far(j0), is_far(jnp.minimum(j0 + 1, i))))
        return lax.cond(both_far,
                        lambda c: attend([j0, j0 + 1], c, far_bias),
                        lambda c: lax.fori_loop(0, n_here, lambda n, c: p3_body(j0 + n, c), c),
                        carry)

    init = tuple((jnp.full((1, t), NEG, F32), jnp.zeros((DSA_V_ROWS, t), F32))
                 for _ in range(DSA_HEADS))
    carry = lax.fori_loop(0, (i + 2) // 2, p3_pair, init)
    outs = [acc[:DSA_HEAD_DIM, :] * (1.0 / acc[DSA_HEAD_DIM:DSA_HEAD_DIM + 1, :])
            for (_, acc) in carry]
    o_ref[...] = jnp.concatenate(outs, axis=0).T.astype(BF)


def _dsa(gq, gk, gd, ki2, vdt, pos_tiles, tab, B, S, t=ATT_TILE):
    nq = S // t
    n_sel = min(TOPK_MAX, S // 4)
    vrows = DSA_HEADS * DSA_V_ROWS
    qrow = lambda c: (lambda b, i: (b * nq + i, c))
    return pl.pallas_call(
        functools.partial(_dsa_kernel, t=t, n_sel=n_sel),
        grid=(B, nq),
        in_specs=[pl.BlockSpec((t, W_Q), qrow(0)),
                  pl.BlockSpec((t, LANES), qrow(1)),
                  pl.BlockSpec((t, DSA_WIDTH), qrow(0)),
                  pl.BlockSpec((None, nq, t), lambda b, i: (b, 0, 0)),
                  pl.BlockSpec((S, LANES), lambda b, i: (b, 0)),
                  pl.BlockSpec((S, DSA_WIDTH), lambda b, i: (b, 1)),
                  pl.BlockSpec((nq, vrows, t), lambda b, i: (b, 0, 0)),
                  pl.BlockSpec(memory_space=pltpu.SMEM), pl.BlockSpec(memory_space=pltpu.SMEM),
                  pl.BlockSpec(memory_space=pltpu.SMEM), pl.BlockSpec(memory_space=pltpu.SMEM)],
        out_specs=pl.BlockSpec((t, DSA_WIDTH), qrow(0)),
        out_shape=jax.ShapeDtypeStruct((B * S, DSA_WIDTH), BF),
        scratch_shapes=[pltpu.VMEM((nq, t, t), F32), pltpu.VMEM((IDX_HEADS, t, LANES), BF),
                        pltpu.VMEM((2 * DSA_HEADS, t, t), F32)],
        compiler_params=_cparams(("arbitrary", "arbitrary")),
        name="dsa",
    )(gq, gk, gd, pos_tiles, ki2, gd, vdt, tab, jnp.min(pos_tiles, axis=-1), jnp.max(pos_tiles, axis=-1),
      jnp.all(pos_tiles[..., 1:] - pos_tiles[..., :-1] == 1, axis=-1).astype(jnp.int32))


def _outmlp_kernel(h_ref, yh_ref, ym_ref, yd_ref, wo_ref, nw_ref, w1_ref, w2_ref, fw_ref, o_ref,
                   *, final, ff_chunk):
    mixed = jnp.concatenate([yh_ref[...], ym_ref[...], yd_ref[...]], axis=1)
    h = h_ref[...] + _dot(mixed, wo_ref[...])
    u = _rms(h, nw_ref[...]).astype(BF)
    out = h
    for c in range(D_FF // ff_chunk):
        cs = slice(c * ff_chunk, (c + 1) * ff_chunk)
        a = jnp.maximum(_dot(u, w1_ref[:, cs]), 0.0)
        out = out + _dot((a * a).astype(BF), w2_ref[cs, :])
    if final:
        out = _rms(out, fw_ref[...])
    o_ref[...] = out


def _outmlp(h, yh, ym, yd, wo, nw, w1, w2, fw, layer, final, tm=ROW_TILE, ff_chunk=1024):
    T = h.shape[0]
    tm = min(tm, T)
    row = lambda i: (i, 0)
    return pl.pallas_call(
        functools.partial(_outmlp_kernel, final=final, ff_chunk=ff_chunk),
        grid=(T // tm,),
        in_specs=[pl.BlockSpec((tm, D_MODEL), row), pl.BlockSpec((tm, HGRN_W), row),
                  pl.BlockSpec((tm, MLA_HEADS * MLA_V), row), pl.BlockSpec((tm, DSA_WIDTH), row),
                  _layer_spec((D_MODEL, D_MODEL), layer), _const_spec((1, D_MODEL)),
                  _layer_spec((D_MODEL, D_FF), layer), _layer_spec((D_FF, D_MODEL), layer),
                  _const_spec((1, D_MODEL))],
        out_specs=pl.BlockSpec((tm, D_MODEL), row),
        out_shape=jax.ShapeDtypeStruct((T, D_MODEL), F32),
        compiler_params=_cparams(("parallel",)),
        name="outmlp",
    )(h, yh, ym, yd, wo, nw, w1, w2, fw)


def _rot_cols(w):
    half = w.shape[-1] // 2
    return jnp.concatenate([-w[..., half:], w[..., :half]], axis=-1)


def _place(w, width, off):
    pad = [(0, 0)] * (w.ndim - 1) + [(off, width - off - w.shape[-1])]
    return jnp.pad(w, pad)


def _layout_w_in(w_in):
    sizes = (HGRN_W, HGRN_W, HGRN_W, HGRN_W, MLA_Q_LORA, MLA_KV_LORA + MLA_ROPE,
             DSA_WIDTH, DSA_WIDTH, DSA_WIDTH, IDX_HEADS * IDX_DIM, IDX_DIM, IDX_HEADS)
    offs = [0]
    for s in sizes:
        offs.append(offs[-1] + s)
    hq, hf, hi, hg, mqa, mkva, dq, dk, dv, iq, ik, iw = [
        w_in[..., offs[n]:offs[n + 1]] for n in range(len(sizes))]
    ckv, kpe = mkva[..., :MLA_KV_LORA], mkva[..., MLA_KV_LORA:]
    cols = [hq, hf, hi, hg,
            _place(mqa, 256, 0), ckv, _place(kpe, LANES, MLA_NOPE), _place(_rot_cols(kpe), LANES, MLA_NOPE),
            dq * (DSA_HEAD_DIM ** -0.5 * LOG2E), dk,
            iq,
            ik, ik, _place(iw, LANES, 0)]
    w_cat = jnp.concatenate(cols, axis=-1).astype(BF)
    L = w_in.shape[0]
    dvt = jnp.swapaxes(dv, 1, 2).reshape(L, DSA_HEADS, DSA_HEAD_DIM, D_MODEL)
    dvt = jnp.pad(dvt, ((0, 0), (0, 0), (0, DSA_V_ROWS - DSA_HEAD_DIM), (0, 0)))
    return w_cat, dvt.reshape(L, DSA_HEADS * DSA_V_ROWS, D_MODEL).astype(BF)


def _layout_mla(w_qb, w_kvb):
    L = w_qb.shape[0]
    dq = MLA_NOPE + MLA_ROPE
    wq = w_qb.reshape(L, MLA_Q_LORA, MLA_HEADS, dq)
    wq_rot = jnp.concatenate([jnp.zeros_like(wq[..., :MLA_NOPE]), _rot_cols(wq[..., MLA_NOPE:])], axis=-1)
    pad_q = lambda w: jnp.pad(w, ((0, 0), (0, 256 - MLA_Q_LORA), (0, 0), (0, HEAD_PAD - dq))).reshape(
        L, 256, MLA_HEADS * HEAD_PAD).astype(BF)
    wkv = w_kvb.reshape(L, MLA_KV_LORA, MLA_HEADS, MLA_NOPE + MLA_V)
    pad_kv = lambda w: jnp.pad(w, ((0, 0), (0, 0), (0, 0), (0, HEAD_PAD - w.shape[-1]))).reshape(
        L, MLA_KV_LORA, MLA_HEADS * HEAD_PAD).astype(BF)
    wvt = jnp.swapaxes(pad_kv(wkv[..., MLA_NOPE:]), 1, 2)
    return pad_q(wq), pad_q(wq_rot), pad_kv(wkv[..., :MLA_NOPE]), wvt


def kernel(x, positions, attn_norm_w, w_in, hgrn_lb_logits, hgrn_norm_w, mla_q_norm_w, mla_w_qb,
           mla_kv_norm_w, mla_w_kvb, idx_k_norm_w, idx_k_norm_b, rel_bias_table, w_out,
           mlp_norm_w, w_mlp_in, w_mlp_out, final_norm_w):
    B, S, _ = x.shape
    T = B * S
    depth = w_in.shape[0]

    inv_freq = 1.0 / (ROPE_THETA ** (jnp.arange(0, MLA_ROPE, 2, dtype=F32) / MLA_ROPE))
    ang = positions.astype(F32)[..., None] * inv_freq
    cos, sin = jnp.cos(ang).reshape(T, -1), jnp.sin(ang).reshape(T, -1)
    cos_t = jnp.concatenate([jnp.ones((T, MLA_NOPE), F32), cos, cos,
                             jnp.zeros((T, HEAD_PAD - MLA_NOPE - MLA_ROPE), F32)], axis=1)
    sin_t = jnp.concatenate([jnp.zeros((T, MLA_NOPE), F32), sin, sin,
                             jnp.zeros((T, HEAD_PAD - MLA_NOPE - MLA_ROPE), F32)], axis=1)
    pos_tiles = positions.reshape(B, S // ATT_TILE, ATT_TILE)

    lb = jnp.cumsum(jax.nn.softmax(hgrn_lb_logits.astype(F32), axis=0), axis=0)
    lb = lb - lb[0:1]
    lb3 = jnp.stack([jnp.log(lb), jnp.log1p(-lb), 1.0 - lb], axis=1)

    w_cat, w_dvt = _layout_w_in(w_in)
    wq, wqr, wk, wvt = _layout_mla(mla_w_qb, mla_w_kvb)
    qnw = jnp.pad(mla_q_norm_w, ((0, 0), (0, 256 - MLA_Q_LORA)))
    lnw = jnp.concatenate([idx_k_norm_w, idx_k_norm_w], axis=-1)
    lnb = jnp.concatenate([idx_k_norm_b, idx_k_norm_b], axis=-1)
    wo = w_out.astype(BF)
    w1 = w_mlp_in.astype(BF)
    w2 = w_mlp_out.astype(BF)
    tab = rel_bias_table.astype(F32)

    h = x.reshape(T, D_MODEL)
    for l in range(depth):
        gh, gm, gd, gq, gk, vdt = _inproj(h, attn_norm_w[l][None], w_cat, w_dvt, l)
        q_m, k_m, v_m, ki2 = _prep(gm, gk, cos_t, sin_t, qnw[l][None], wq[l], wqr[l],
                                   mla_kv_norm_w[l][None], wk[l], wvt[l], lnw[l][None], lnb[l][None])
        y_h = _hgrn(gh, lb3[l], hgrn_norm_w[l][None], B, S)
        y_m = _mla_attn(q_m, k_m, v_m, B, S)
        y_d = _dsa(gq, gk, gd, ki2, vdt, pos_tiles, tab, B, S)
        h = _outmlp(h, y_h, y_m, y_d, wo, mlp_norm_w[l][None], w1, w2,
                    final_norm_w[None], l, final=(l == depth - 1))
    return h.reshape(B, S, D_MODEL)
```

```python
import functools
import math

import jax
import jax.numpy as jnp
from jax import lax
from jax.experimental import pallas as pl
from jax.experimental.pallas import tpu as pltpu

D_MODEL = 1024
HGRN_HEADS = 4
HGRN_DK = 128
HGRN_DV = 128
HGRN_W = HGRN_HEADS * HGRN_DK
HGRN_CHUNK = 32
MLA_HEADS = 4
MLA_NOPE = 64
MLA_ROPE = 32
MLA_V = 64
MLA_Q_LORA = 192
MLA_KV_LORA = 128
ROPE_THETA = 10000.0
DSA_HEADS = 4
DSA_HEAD_DIM = 64
DSA_WIDTH = DSA_HEADS * DSA_HEAD_DIM
IDX_HEADS = 8
IDX_DIM = 64
TOPK_MAX = 256
NUM_BUCKETS = 32
MAX_EXACT = NUM_BUCKETS // 2
MAX_DISTANCE = 128
D_FF = 4 * D_MODEL
EPS = 1e-6

LANES = 128
SUBLANES = 8
BF16_ROWS = 16
HEAD_PAD = 128
NEG = -1e30
ATT_TILE = 256
MLA_KEY_TILE = 512
ROW_TILE = 512
HGRN_BLOCK = 256
VMEM_LIMIT = 52 * 1024 * 1024
LOG2E = math.log2(math.e)

DSA_V_ROWS = DSA_HEAD_DIM + BF16_ROWS

C_H = 0
C_M = C_H + 4 * HGRN_W
W_M = 640
C_D = C_M + W_M
W_D = 2 * DSA_WIDTH
C_Q = C_D + W_D
W_Q = IDX_HEADS * IDX_DIM
C_K = C_Q + W_Q
W_K = 256
N_PAD = C_K + W_K

BF = jnp.bfloat16
F32 = jnp.float32


def _dot(a, b):
    return jnp.dot(a, b, preferred_element_type=F32)


def _dot_nt(a, b):
    return lax.dot_general(a, b, (((1,), (1,)), ((), ())), preferred_element_type=F32)


def _rms(x, w, n=None):
    n = x.shape[-1] if n is None else n
    ms = jnp.sum(x * x, axis=-1, keepdims=True) * (1.0 / n)
    return x * lax.rsqrt(ms + EPS) * w


def _cparams(sem):
    return pltpu.CompilerParams(dimension_semantics=sem, vmem_limit_bytes=VMEM_LIMIT)


def _const_spec(shape):
    nd = len(shape)
    return pl.BlockSpec(shape, lambda *_: (0,) * nd, pipeline_mode=pl.Buffered(1))


def _layer_spec(shape, layer):
    nd = len(shape)
    return pl.BlockSpec((None,) + tuple(shape), lambda *_: (layer,) + (0,) * nd,
                        pipeline_mode=pl.Buffered(1))


def _inproj_kernel(x_ref, nw_ref, w_ref, wvt_ref, oh_ref, om_ref, od_ref, oq_ref, ok_ref, ovt_ref):
    u = _rms(x_ref[...], nw_ref[...]).astype(BF)
    oh_ref[...] = _dot(u, w_ref[:, C_H:C_M])
    om_ref[...] = _dot(u, w_ref[:, C_M:C_D])
    od_ref[...] = _dot(u, w_ref[:, C_D:C_Q]).astype(BF)
    oq_ref[...] = _dot(u, w_ref[:, C_Q:C_K]).astype(BF)
    ok_ref[...] = _dot(u, w_ref[:, C_K:N_PAD])
    vt = _dot_nt(wvt_ref[...], u)
    row = lax.broadcasted_iota(jnp.int32, vt.shape, 0)
    vt = jnp.where(row % DSA_V_ROWS == DSA_HEAD_DIM, 1.0, vt).astype(BF)
    for n in range(ovt_ref.shape[0]):
        ovt_ref[n] = vt[:, n * ATT_TILE:(n + 1) * ATT_TILE]


def _inproj(h, nw, w, wvt, layer, tm=ROW_TILE):
    T = h.shape[0]
    tm = min(tm, T)
    row = lambda i: (i, 0)
    vrows = DSA_HEADS * DSA_V_ROWS
    per = tm // ATT_TILE
    return pl.pallas_call(
        _inproj_kernel,
        grid=(T // tm,),
        in_specs=[pl.BlockSpec((tm, D_MODEL), row), _const_spec((1, D_MODEL)),
                  _layer_spec((D_MODEL, N_PAD), layer), _layer_spec((vrows, D_MODEL), layer)],
        out_specs=[pl.BlockSpec((tm, 4 * HGRN_W), row), pl.BlockSpec((tm, W_M), row),
                   pl.BlockSpec((tm, W_D), row), pl.BlockSpec((tm, W_Q), row),
                   pl.BlockSpec((tm, W_K), row),
                   pl.BlockSpec((per, vrows, ATT_TILE), lambda i: (i, 0, 0))],
        out_shape=[jax.ShapeDtypeStruct((T, 4 * HGRN_W), F32), jax.ShapeDtypeStruct((T, W_M), F32),
                   jax.ShapeDtypeStruct((T, W_D), BF), jax.ShapeDtypeStruct((T, W_Q), BF),
                   jax.ShapeDtypeStruct((T, W_K), F32),
                   jax.ShapeDtypeStruct((T // ATT_TILE, vrows, ATT_TILE), BF)],
        compiler_params=_cparams(("parallel",)),
        name="inproj",
    )(h, nw, w, wvt)


def _prep_kernel(gm_ref, gk_ref, cos_ref, sin_ref, qnw_ref, wq_ref, wqr_ref, kvnw_ref, wk_ref,
                 wvt_ref, lnw_ref, lnb_ref, q_ref, k_ref, vt_ref, ki_ref):
    cs = cos_ref[...]
    sn = sin_ref[...]
    cs4 = jnp.concatenate([cs] * MLA_HEADS, axis=1)
    sn4 = jnp.concatenate([sn] * MLA_HEADS, axis=1)
    qn = _rms(gm_ref[:, 0:256], qnw_ref[...], n=MLA_Q_LORA).astype(BF)
    scale = (MLA_NOPE + MLA_ROPE) ** -0.5 * LOG2E
    q = (_dot(qn, wq_ref[...]) * cs4 + _dot(qn, wqr_ref[...]) * sn4) * scale
    q_ref[...] = q.astype(BF)
    cn = _rms(gm_ref[:, 256:384], kvnw_ref[...]).astype(BF)
    kp = gm_ref[:, 384:512] * cs + gm_ref[:, 512:640] * sn
    k = _dot(cn, wk_ref[...]) + jnp.concatenate([kp] * MLA_HEADS, axis=1)
    k_ref[...] = k.astype(BF)
    vt = _dot_nt(wvt_ref[...], cn)
    row = lax.broadcasted_iota(jnp.int32, vt.shape, 0)
    vt_ref[...] = jnp.where(row % HEAD_PAD == MLA_V, 1.0, vt).astype(BF)
    x = gk_ref[:, 0:LANES]
    first = lax.broadcasted_iota(jnp.int32, x.shape, 1) < IDX_DIM
    mu = jnp.sum(jnp.where(first, x, 0.0), axis=-1, keepdims=True) * (1.0 / IDX_DIM)
    xc = x - mu
    var = jnp.sum(jnp.where(first, xc * xc, 0.0), axis=-1, keepdims=True) * (1.0 / IDX_DIM)
    ki_ref[...] = (xc * lax.rsqrt(var + EPS) * lnw_ref[...] + lnb_ref[...]).astype(BF)


def _prep(gm, gk, cos_t, sin_t, qnw, wq, wqr, kvnw, wk, wvt, lnw, lnb, tm=MLA_KEY_TILE):
    T = gm.shape[0]
    row = lambda i: (i, 0)
    hp = MLA_HEADS * HEAD_PAD
    return pl.pallas_call(
        _prep_kernel,
        grid=(T // tm,),
        in_specs=[pl.BlockSpec((tm, W_M), row), pl.BlockSpec((tm, W_K), row),
                  pl.BlockSpec((tm, LANES), row), pl.BlockSpec((tm, LANES), row),
                  _const_spec((1, 256)), _const_spec((256, hp)), _const_spec((256, hp)),
                  _const_spec((1, MLA_KV_LORA)), _const_spec((MLA_KV_LORA, hp)),
                  _const_spec((hp, MLA_KV_LORA)), _const_spec((1, LANES)), _const_spec((1, LANES))],
        out_specs=[pl.BlockSpec((tm, hp), row), pl.BlockSpec((tm, hp), row),
                   pl.BlockSpec((None, hp, tm), lambda i: (i, 0, 0)), pl.BlockSpec((tm, LANES), row)],
        out_shape=[jax.ShapeDtypeStruct((T, hp), BF), jax.ShapeDtypeStruct((T, hp), BF),
                   jax.ShapeDtypeStruct((T // tm, hp, tm), BF), jax.ShapeDtypeStruct((T, LANES), BF)],
        compiler_params=_cparams(("parallel",)),
        name="prep",
    )(gm, gk, cos_t, sin_t, qnw, wq, wqr, kvnw, wk, wvt, lnw, lnb)


def _split3(x):
    a = x.astype(BF)
    r = x - a.astype(F32)
    b = r.astype(BF)
    c = (r - b.astype(F32)).astype(BF)
    return a, b, c


def _hgrn_kernel(q_ref, f_ref, i_ref, g_ref, lb_ref, nw_ref, o_ref, st_ref, os_ref, *, tb):
    C = HGRN_CHUNK
    nc = tb // C

    @pl.when(pl.program_id(1) == 0)
    def _():
        st_ref[...] = jnp.zeros_like(st_ref)

    fp = f_ref[...]
    log_lb = lb_ref[0:1, :]
    log1m_lb = lb_ref[1:2, :]
    one_m_lb = lb_ref[2:3, :]
    ls = jnp.minimum(fp, 0.0) - jnp.log1p(jnp.exp(-jnp.abs(fp)))
    b = log1m_lb + ls
    log_f = jnp.maximum(log_lb, b) + jnp.log1p(jnp.exp(-jnp.abs(log_lb - b)))
    k = one_m_lb * (1.0 / (1.0 + jnp.exp(fp)))

    r = lax.broadcasted_iota(jnp.int32, (tb, tb), 0)
    c = lax.broadcasted_iota(jnp.int32, (tb, tb), 1)
    same = (r // C) == (c // C)
    tri = jnp.where(same & (c <= r), 1.0, 0.0).astype(BF)
    blk = jnp.where(same, 1.0, 0.0).astype(BF)
    a0, a1, a2 = _split3(log_f)
    G = _dot(tri, a0) + _dot(tri, a1) + _dot(tri, a2)
    G_last = _dot(blk, a0) + _dot(blk, a1) + _dot(blk, a2)

    eg = jnp.exp(G)
    q_dec = (q_ref[...] * (HGRN_DK ** -0.5) * eg).astype(BF)
    k_inv = (k * jnp.exp(-G)).astype(BF)
    k_state = k * jnp.exp(G_last - G)
    decay = jnp.exp(G_last)
    v = i_ref[...]
    vb = v.astype(BF)

    rr = lax.broadcasted_iota(jnp.int32, (C, C), 0)
    cc = lax.broadcasted_iota(jnp.int32, (C, C), 1)
    causal = cc <= rr

    heads = [slice(h * HGRN_DK, (h + 1) * HGRN_DK) for h in range(HGRN_HEADS)]
    sts = [st_ref[h] for h in range(HGRN_HEADS)]
    for n in range(nc):
        rs = slice(n * C, (n + 1) * C)
        qds = [q_dec[rs, hs] for hs in heads]
        As = [_dot_nt(qds[h], k_inv[rs, heads[h]]) for h in range(HGRN_HEADS)]
        kvs = [_dot(v[rs, heads[h]].T.astype(BF), k_state[rs, heads[h]].astype(BF))
               for h in range(HGRN_HEADS)]
        inter = [_dot_nt(qds[h], sts[h].astype(BF)) for h in range(HGRN_HEADS)]
        for h in range(HGRN_HEADS):
            A = jnp.where(causal, As[h], 0.0).astype(BF)
            os_ref[rs, heads[h]] = _dot(A, vb[rs, heads[h]]) + inter[h]
            sts[h] = decay[n * C:n * C + 1, heads[h]] * sts[h] + kvs[h]
    for h in range(HGRN_HEADS):
        st_ref[h] = sts[h]

    g = g_ref[...]
    gate = g * (1.0 / (1.0 + jnp.exp(-g)))
    for h in range(HGRN_HEADS):
        hs = slice(h * HGRN_DK, (h + 1) * HGRN_DK)
        o_ref[:, hs] = (_rms(os_ref[:, hs], nw_ref[...]) * gate[:, hs]).astype(BF)


def _hgrn(gh, lb3, nw, B, S, tb=HGRN_BLOCK):
    nb = S // tb
    col = lambda j: (lambda b, i: (b * nb + i, j))
    return pl.pallas_call(
        functools.partial(_hgrn_kernel, tb=tb),
        grid=(B, nb),
        in_specs=[pl.BlockSpec((tb, HGRN_W), col(0)), pl.BlockSpec((tb, HGRN_W), col(1)),
                  pl.BlockSpec((tb, HGRN_W), col(2)), pl.BlockSpec((tb, HGRN_W), col(3)),
                  pl.BlockSpec((3, HGRN_W), lambda b, i: (0, 0)),
                  pl.BlockSpec((1, HGRN_DV), lambda b, i: (0, 0))],
        out_specs=pl.BlockSpec((tb, HGRN_W), col(0)),
        out_shape=jax.ShapeDtypeStruct((B * S, HGRN_W), BF),
        scratch_shapes=[pltpu.VMEM((HGRN_HEADS, HGRN_DV, HGRN_DK), F32),
                        pltpu.VMEM((tb, HGRN_W), F32)],
        compiler_params=_cparams(("parallel", "arbitrary")),
        name="hgrn",
    )(gh, gh, gh, gh, lb3, nw)


def _mla_kernel(q_ref, k_ref, vt_ref, o_ref, *, t, tk):
    i = pl.program_id(1)
    n_full = (i * t) // tk
    key = lax.broadcasted_iota(jnp.int32, (tk, t), 0)
    qry = lax.broadcasted_iota(jnp.int32, (tk, t), 1)
    causal = n_full * tk + key <= i * t + qry

    heads = [slice(h * HEAD_PAD, (h + 1) * HEAD_PAD) for h in range(MLA_HEADS)]

    def logits(j):
        ks = pl.ds(pl.multiple_of(j * tk, tk), tk)
        return tuple(_dot_nt(k_ref[ks, hs], q_ref[:, hs]) for hs in heads)

    def step(tiles, state, mask_last):
        ss = [list(logits(j)) for j in tiles]
        if mask_last:
            ss[-1] = [jnp.where(causal, s, NEG) for s in ss[-1]]
        ms = []
        for h in range(MLA_HEADS):
            m = state[h][0]
            for s in ss:
                m = jnp.maximum(m, jnp.max(s[h], axis=0, keepdims=True))
            ms.append(m)
        out = []
        for h in range(MLA_HEADS):
            acc = jnp.exp2(state[h][0] - ms[h]) * state[h][1]
            for j, s in zip(tiles, ss):
                acc = acc + _dot(vt_ref[j, heads[h], :], jnp.exp2(s[h] - ms[h]).astype(BF))
            out.append((ms[h], acc))
        return tuple(out)

    init = tuple((jnp.full((1, t), NEG, F32), jnp.zeros((HEAD_PAD, t), F32))
                 for _ in range(MLA_HEADS))
    state = lax.fori_loop(0, n_full // 2, lambda jj, c: step([2 * jj, 2 * jj + 1], c, False), init)
    state = lax.cond(n_full % 2 == 1,
                     lambda c: step([n_full - 1, n_full], c, True),
                     lambda c: step([n_full], c, True), state)
    outs = [acc[:MLA_V, :] * (1.0 / acc[MLA_V:MLA_V + 1, :]) for (_, acc) in state]
    o_ref[...] = jnp.concatenate(outs, axis=0).T.astype(BF)


def _mla_attn(q, k, vt, B, S, t=ATT_TILE, tk=MLA_KEY_TILE):
    nq = S // t
    nk = S // tk
    hp = MLA_HEADS * HEAD_PAD
    return pl.pallas_call(
        functools.partial(_mla_kernel, t=t, tk=tk),
        grid=(B, nq),
        in_specs=[pl.BlockSpec((t, hp), lambda b, i: (b * nq + i, 0)),
                  pl.BlockSpec((S, hp), lambda b, i: (b, 0)),
                  pl.BlockSpec((nk, hp, tk), lambda b, i: (b, 0, 0))],
        out_specs=pl.BlockSpec((t, MLA_HEADS * MLA_V), lambda b, i: (b * nq + i, 0)),
        out_shape=jax.ShapeDtypeStruct((B * S, MLA_HEADS * MLA_V), BF),
        compiler_params=_cparams(("parallel", "arbitrary")),
        name="mla_attn",
    )(q, k, vt)


BISECT_CAP = 48
END_STEPS = 4
_LAST_BUCKET_FROM = next(
    n for n in range(MAX_EXACT, 1 << 20)
    if MAX_EXACT + int(math.log(n / MAX_EXACT) / math.log(MAX_DISTANCE / MAX_EXACT)
                       * (NUM_BUCKETS - MAX_EXACT)) >= NUM_BUCKETS - 1)
assert _LAST_BUCKET_FROM <= LANES - 1 and _LAST_BUCKET_FROM <= MAX_DISTANCE


def _dsa_kernel(qi_ref, w_ref, qd_ref, pos_ref, ki_ref, kd_ref, vt_ref, tab_ref, pmin_ref, pmax_ref,
                run_ref, o_ref, sc_ref, qm_ref, bias_ref, *, t, n_sel):
    i = pl.program_id(1)
    lane = lax.broadcasted_iota(jnp.int32, (t, LANES), 1)
    lo_half = lane < DSA_HEAD_DIM
    key = lax.broadcasted_iota(jnp.int32, (t, t), 0)
    qry = lax.broadcasted_iota(jnp.int32, (t, t), 1)
    causal = key <= qry
    k_sel = float(n_sel)

    def key_rows(j):
        return pl.ds(pl.multiple_of(j * t, t), t)

    @pl.when(i + 1 < sc_ref.shape[0])
    def _():
        sc_ref[i + 1] = jnp.full((t, t), -jnp.inf, F32)

    def over_tiles(body, init):
        return lax.fori_loop(0, (i + 2) // 2, lambda jj, c: body(2 * jj + 1, body(2 * jj, c)), init)

    def fold(x, op):
        return op(x.reshape(t // SUBLANES, SUBLANES, t), axis=0)

    zero_b = jnp.zeros((t, LANES), BF)
    for p in range(IDX_HEADS // 2):
        qp = qi_ref[:, p * LANES:(p + 1) * LANES]
        qm_ref[2 * p] = jnp.where(lo_half, qp, zero_b)
        qm_ref[2 * p + 1] = jnp.where(lo_half, zero_b, qp)
    wt = w_ref[...].T * ((IDX_HEADS ** -0.5) * (IDX_DIM ** -0.5))

    def score_tiles(js):
        kts = [ki_ref[key_rows(j), :] for j in js]
        accs = [None] * len(js)
        for h0 in range(0, IDX_HEADS, 4):
            ss = [[_dot_nt(kt, qm_ref[h]) for h in range(h0, h0 + 4)] for kt in kts]
            for n in range(len(js)):
                for m, s in enumerate(ss[n]):
                    term = jnp.maximum(s, 0.0) * wt[h0 + m:h0 + m + 1, :]
                    accs[n] = term if accs[n] is None else accs[n] + term
        return accs

    def add_stats(x, c):
        mx, mn, cp, cz = c
        return (jnp.maximum(mx, fold(x, jnp.max)),
                jnp.minimum(mn, fold(jnp.where(x == -jnp.inf, jnp.inf, x), jnp.min)),
                cp + fold(jnp.where(x > 0.0, 1.0, 0.0), jnp.sum),
                cz + fold(jnp.where(x == 0.0, 1.0, 0.0), jnp.sum))

    def p1_tiles(js, c, diag_last):
        xs = score_tiles(js)
        if diag_last:
            xs[-1] = jnp.where(causal, xs[-1], -jnp.inf)
        for j, x in zip(js, xs):
            sc_ref[j] = x
            c = add_stats(x, c)
        return c

    z8 = jnp.zeros((SUBLANES, t), F32)
    stats = lax.fori_loop(0, i // 2, lambda jj, c: p1_tiles([2 * jj, 2 * jj + 1], c, False),
                          (z8 - jnp.inf, z8 + jnp.inf, z8, z8))
    mx, mn, cp, cz = lax.cond(i % 2 == 1,
                              lambda c: p1_tiles([i - 1, i], c, True),
                              lambda c: p1_tiles([i], c, True), stats)

    def total(c):
        return jnp.sum(c, axis=0, keepdims=True)

    def count_gt(thr):
        def body(j, c):
            return c + fold(jnp.where(sc_ref[j] > thr, 1.0, 0.0), jnp.sum)
        return total(over_tiles(body, jnp.zeros((SUBLANES, t), F32)))

    row_max = jnp.max(mx, axis=0, keepdims=True)
    row_min = jnp.min(mn, axis=0, keepdims=True)
    c_pos, c_zero = total(cp), total(cz)
    n_valid = (i * t + 1 + lax.broadcasted_iota(jnp.int32, (1, t), 1)).astype(F32)

    big = n_valid > k_sel
    pos_q = jnp.logical_and(big, c_pos >= k_sel)
    tie_q = jnp.logical_and(jnp.logical_and(big, c_pos < k_sel), c_pos + c_zero >= k_sel)
    neg_q = jnp.logical_and(big, c_pos + c_zero < k_sel)
    need = jnp.where(tie_q, k_sel - c_pos, 0.0)
    lo0 = jnp.where(jnp.logical_or(pos_q, tie_q), 0.0, -jnp.inf)
    hi0 = jnp.where(neg_q, 0.0, row_max)
    c_lo0 = jnp.where(pos_q, c_pos, jnp.where(tie_q, k_sel, n_valid))

    searched = c_lo0 > k_sel

    def open_queries(c_lo):
        return jnp.max(jnp.where(c_lo > k_sel + END_STEPS, 1.0, 0.0))

    def next_above(thr):
        def body(j, m):
            x = sc_ref[j]
            return jnp.minimum(m, fold(jnp.where(x > thr, x, jnp.inf), jnp.min))
        return jnp.min(over_tiles(body, jnp.full((SUBLANES, t), jnp.inf, F32)), axis=0, keepdims=True)

    def bis_cond(c):
        return jnp.logical_and(c[0] < BISECT_CAP, c[1] > 0.0)

    def bisect_once(lo, hi, c_lo):
        active = c_lo > k_sel
        base = jnp.maximum(lo, row_min)
        mid = base + 0.5 * (hi - base)
        cnt = count_gt(mid)
        up = jnp.logical_and(active, cnt >= k_sel)
        dn = jnp.logical_and(active, cnt < k_sel)
        return jnp.where(up, mid, lo), jnp.where(dn, mid, hi), jnp.where(up, cnt, c_lo)

    def bis_body(c):
        it, _, lo, hi, c_lo = c
        lo, hi, c_lo = bisect_once(*bisect_once(lo, hi, c_lo))
        return it + 2, open_queries(c_lo), lo, hi, c_lo

    _, _, lo, _, c_lo = lax.while_loop(
        bis_cond, bis_body, (jnp.int32(0), open_queries(c_lo0), lo0, hi0, c_lo0))

    def step_body(_, c):
        lo, c_lo = c
        more = c_lo > k_sel
        return jnp.where(more, next_above(lo), lo), jnp.where(more, c_lo - 1.0, c_lo)

    lo, _ = lax.fori_loop(0, END_STEPS, step_body, (lo, c_lo))
    c_lo = jnp.where(searched, count_gt(lo), c_lo)

    still_open = c_lo != k_sel
    still_open = jnp.logical_and(searched, still_open)

    def exact_kth(_):
        def to_float(kk):
            return lax.bitcast_convert_type(jnp.where(kk < 0, kk ^ jnp.int32(0x7FFFFFFF), kk), F32)

        def count_ge(thr):
            def body(j, c):
                return c + fold(jnp.where(sc_ref[j] >= thr, 1.0, 0.0), jnp.sum)
            return total(over_tiles(body, jnp.zeros((SUBLANES, t), F32)))

        def body(b, kth):
            cand = kth + lax.shift_left(jnp.int32(1), 31 - b)
            return jnp.where(count_ge(to_float(cand)) >= k_sel, cand, kth)

        kth = to_float(lax.fori_loop(0, 32, body, jnp.full((1, t), -2 ** 31, jnp.int32)))
        return kth, count_gt(kth)

    kth, above = lax.cond(jnp.max(jnp.where(still_open, 1.0, 0.0)) > 0.0, exact_kth,
                          lambda _: (jnp.zeros((1, t), F32), jnp.zeros((1, t), F32)), 0)
    lo = jnp.where(still_open, kth, lo)
    need = jnp.where(still_open, k_sel - above, need)
    tie_val = jnp.where(still_open, kth, 0.0)

    def mask_plain(_):
        def body(j, c):
            sc_ref[j] = jnp.where(sc_ref[j] > lo, 0.0, NEG)
            return c
        return lax.fori_loop(0, i + 1, body, 0)

    def mask_ties(_):
        lower = jnp.where(qry <= key, 1.0, 0.0).astype(BF)

        def body(jj, seen):
            xs = [sc_ref[2 * jj + n] for n in range(2)]
            tied = [x == tie_val for x in xs]
            tfs = [jnp.where(m, 1.0, 0.0) for m in tied]
            ranks = [_dot(lower, tf.astype(BF)) for tf in tfs]
            for n in range(2):
                take = jnp.logical_and(tied[n], ranks[n] + seen <= need)
                sc_ref[2 * jj + n] = jnp.where(jnp.logical_or(xs[n] > lo, take), 0.0, NEG)
                seen = seen + total(fold(tfs[n], jnp.sum))
            return seen
        lax.fori_loop(0, (i + 2) // 2, body, jnp.zeros((1, t), F32))
        return 0

    lax.cond(jnp.max(need) > 0.0, mask_ties, mask_plain, 0)

    qh = []
    for p in range(DSA_HEADS // 2):
        qp = qd_ref[:, p * LANES:(p + 1) * LANES]
        qh.append(jnp.where(lo_half, qp, zero_b))
        qh.append(jnp.where(lo_half, zero_b, qp))
    pq = pos_ref[pl.ds(i, 1), :]
    pq_min = pmin_ref[pl.program_id(0), i]
    far_bias = [tab_ref[NUM_BUCKETS - 1, h] * LOG2E for h in range(DSA_HEADS)]
    log_ratio = math.log(MAX_DISTANCE / MAX_EXACT)

    dist = lax.broadcasted_iota(jnp.int32, (SUBLANES, LANES), 1)
    large = MAX_EXACT + (jnp.log(jnp.maximum(dist, 1).astype(F32) / MAX_EXACT) / log_ratio
                         * (NUM_BUCKETS - MAX_EXACT)).astype(jnp.int32)
    bucket = jnp.where(dist < MAX_EXACT, dist, jnp.minimum(large, NUM_BUCKETS - 1))
    by_dist = []
    for h in range(DSA_HEADS):
        bh = jnp.full((SUBLANES, LANES), tab_ref[0, h], F32)
        for jb in range(1, NUM_BUCKETS):
            bh = jnp.where(bucket >= jb, tab_ref[jb, h], bh)
        by_dist.append(jnp.concatenate([bh * LOG2E] * (t // SUBLANES), axis=0))

    def bias_of(n):
        n = jnp.clip(n, 0, LANES - 1)
        return [jnp.concatenate([jnp.take_along_axis(by_dist[h], n[:, c * LANES:(c + 1) * LANES], axis=1)
                                 for c in range(t // LANES)], axis=1) for h in range(DSA_HEADS)]

    def pair_bias(pk_row):
        pk = jnp.broadcast_to(pk_row, (SUBLANES, t)).T[:, 0:1]
        return bias_of(pq - pk)

    b_idx = pl.program_id(0)

    @pl.when(jnp.logical_and(b_idx == 0, i == 0))
    def _():
        for gap in range(2):
            for h, tile in enumerate(bias_of(gap * t + qry - key)):
                bias_ref[gap * DSA_HEADS + h] = tile

    def attend(tiles, carry, bias):
        ss = []
        for j in tiles:
            madd = sc_ref[j]
            ks = key_rows(j)
            kps = [kd_ref[ks, p * LANES:(p + 1) * LANES] for p in range(DSA_HEADS // 2)]
            ss.append([_dot_nt(kps[h // 2], qh[h]) + (madd + bias[h]) for h in range(DSA_HEADS)])
        ms = []
        for h in range(DSA_HEADS):
            m = carry[h][0]
            for s in ss:
                m = jnp.maximum(m, jnp.max(s[h], axis=0, keepdims=True))
            ms.append(m)
        out = []
        for h in range(DSA_HEADS):
            acc = jnp.exp2(carry[h][0] - ms[h]) * carry[h][1]
            for j, s in zip(tiles, ss):
                p = jnp.exp2(s[h] - ms[h]).astype(BF)
                acc = acc + _dot(vt_ref[j, h * DSA_V_ROWS:(h + 1) * DSA_V_ROWS, :], p)
            out.append((ms[h], acc))
        return tuple(out)

    def is_far(j):
        return pq_min - pmax_ref[b_idx, j] >= MAX_DISTANCE

    def p3_body(j, carry):
        gap = i - j
        consecutive = jnp.logical_and(
            jnp.logical_and(run_ref[b_idx, i] == 1, run_ref[b_idx, j] == 1),
            jnp.logical_and(gap <= 1, pq_min - pmin_ref[b_idx, j] == gap * t))

        def near(c):
            return lax.cond(
                consecutive,
                lambda c: attend([j], c, [bias_ref[gap * DSA_HEADS + h] for h in range(DSA_HEADS)]),
                lambda c: attend([j], c, pair_bias(pos_ref[pl.ds(j, 1), :])), c)

        return lax.cond(is_far(j), lambda c: attend([j], c, far_bias), near, carry)

    def p3_pair(jj, carry):
        j0 = 2 * jj
        n_here = jnp.minimum(i + 1 - j0, 2)
        both_far = jnp.logical_and(n_here == 2,
                                   jnp.logical_and(is_far(j0), is_far(jnp.minimum(j0 + 1, i))))
        return lax.cond(both_far,
                        lambda c: attend([j0, j0 + 1], c, far_bias),
                        lambda c: lax.fori_loop(0, n_here, lambda n, c: p3_body(j0 + n, c), c),
                        carry)

    init = tuple((jnp.full((1, t), NEG, F32), jnp.zeros((DSA_V_ROWS, t), F32))
                 for _ in range(DSA_HEADS))
    carry = lax.fori_loop(0, (i + 2) // 2, p3_pair, init)
    outs = [acc[:DSA_HEAD_DIM, :] * (1.0 / acc[DSA_HEAD_DIM:DSA_HEAD_DIM + 1, :])
            for (_, acc) in carry]
    o_ref[...] = jnp.concatenate(outs, axis=0).T.astype(BF)


def _dsa(gq, gk, gd, ki2, vdt, pos_tiles, tab, B, S, t=ATT_TILE):
    nq = S // t
    n_sel = min(TOPK_MAX, S // 4)
    vrows = DSA_HEADS * DSA_V_ROWS
    qrow = lambda c: (lambda b, i: (b * nq + i, c))
    return pl.pallas_call(
        functools.partial(_dsa_kernel, t=t, n_sel=n_sel),
        grid=(B, nq),
        in_specs=[pl.BlockSpec((t, W_Q), qrow(0)),
                  pl.BlockSpec((t, LANES), qrow(1)),
                  pl.BlockSpec((t, DSA_WIDTH), qrow(0)),
                  pl.BlockSpec((None, nq, t), lambda b, i: (b, 0, 0)),
                  pl.BlockSpec((S, LANES), lambda b, i: (b, 0)),
                  pl.BlockSpec((S, DSA_WIDTH), lambda b, i: (b, 1)),
                  pl.BlockSpec((nq, vrows, t), lambda b, i: (b, 0, 0)),
                  pl.BlockSpec(memory_space=pltpu.SMEM), pl.BlockSpec(memory_space=pltpu.SMEM),
                  pl.BlockSpec(memory_space=pltpu.SMEM), pl.BlockSpec(memory_space=pltpu.SMEM)],
        out_specs=pl.BlockSpec((t, DSA_WIDTH), qrow(0)),
        out_shape=jax.ShapeDtypeStruct((B * S, DSA_WIDTH), BF),
        scratch_shapes=[pltpu.VMEM((nq, t, t), F32), pltpu.VMEM((IDX_HEADS, t, LANES), BF),
                        pltpu.VMEM((2 * DSA_HEADS, t, t), F32)],
        compiler_params=_cparams(("arbitrary", "arbitrary")),
        name="dsa",
    )(gq, gk, gd, pos_tiles, ki2, gd, vdt, tab, jnp.min(pos_tiles, axis=-1), jnp.max(pos_tiles, axis=-1),
      jnp.all(pos_tiles[..., 1:] - pos_tiles[..., :-1] == 1, axis=-1).astype(jnp.int32))


def _outmlp_kernel(h_ref, yh_ref, ym_ref, yd_ref, wo_ref, nw_ref, w1_ref, w2_ref, fw_ref, o_ref,
                   *, final, ff_chunk):
    mixed = jnp.concatenate([yh_ref[...], ym_ref[...], yd_ref[...]], axis=1)
    h = h_ref[...] + _dot(mixed, wo_ref[...])
    u = _rms(h, nw_ref[...]).astype(BF)
    out = h
    for c in range(D_FF // ff_chunk):
        cs = slice(c * ff_chunk, (c + 1) * ff_chunk)
        a = jnp.maximum(_dot(u, w1_ref[:, cs]), 0.0)
        out = out + _dot((a * a).astype(BF), w2_ref[cs, :])
    if final:
        out = _rms(out, fw_ref[...])
    o_ref[...] = out


def _outmlp(h, yh, ym, yd, wo, nw, w1, w2, fw, layer, final, tm=ROW_TILE, ff_chunk=1024):
    T = h.shape[0]
    tm = min(tm, T)
    row = lambda i: (i, 0)
    return pl.pallas_call(
        functools.partial(_outmlp_kernel, final=final, ff_chunk=ff_chunk),
        grid=(T // tm,),
        in_specs=[pl.BlockSpec((tm, D_MODEL), row), pl.BlockSpec((tm, HGRN_W), row),
                  pl.BlockSpec((tm, MLA_HEADS * MLA_V), row), pl.BlockSpec((tm, DSA_WIDTH), row),
                  _layer_spec((D_MODEL, D_MODEL), layer), _const_spec((1, D_MODEL)),
                  _layer_spec((D_MODEL, D_FF), layer), _layer_spec((D_FF, D_MODEL), layer),
                  _const_spec((1, D_MODEL))],
        out_specs=pl.BlockSpec((tm, D_MODEL), row),
        out_shape=jax.ShapeDtypeStruct((T, D_MODEL), F32),
        compiler_params=_cparams(("parallel",)),
        name="outmlp",
    )(h, yh, ym, yd, wo, nw, w1, w2, fw)


def _rot_cols(w):
    half = w.shape[-1] // 2
    return jnp.concatenate([-w[..., half:], w[..., :half]], axis=-1)


def _place(w, width, off):
    pad = [(0, 0)] * (w.ndim - 1) + [(off, width - off - w.shape[-1])]
    return jnp.pad(w, pad)


def _layout_w_in(w_in):
    sizes = (HGRN_W, HGRN_W, HGRN_W, HGRN_W, MLA_Q_LORA, MLA_KV_LORA + MLA_ROPE,
             DSA_WIDTH, DSA_WIDTH, DSA_WIDTH, IDX_HEADS * IDX_DIM, IDX_DIM, IDX_HEADS)
    offs = [0]
    for s in sizes:
        offs.append(offs[-1] + s)
    hq, hf, hi, hg, mqa, mkva, dq, dk, dv, iq, ik, iw = [
        w_in[..., offs[n]:offs[n + 1]] for n in range(len(sizes))]
    ckv, kpe = mkva[..., :MLA_KV_LORA], mkva[..., MLA_KV_LORA:]
    cols = [hq, hf, hi, hg,
            _place(mqa, 256, 0), ckv, _place(kpe, LANES, MLA_NOPE), _place(_rot_cols(kpe), LANES, MLA_NOPE),
            dq * (DSA_HEAD_DIM ** -0.5 * LOG2E), dk,
            iq,
            ik, ik, _place(iw, LANES, 0)]
    w_cat = jnp.concatenate(cols, axis=-1).astype(BF)
    L = w_in.shape[0]
    dvt = jnp.swapaxes(dv, 1, 2).reshape(L, DSA_HEADS, DSA_HEAD_DIM, D_MODEL)
    dvt = jnp.pad(dvt, ((0, 0), (0, 0), (0, DSA_V_ROWS - DSA_HEAD_DIM), (0, 0)))
    return w_cat, dvt.reshape(L, DSA_HEADS * DSA_V_ROWS, D_MODEL).astype(BF)


def _layout_mla(w_qb, w_kvb):
    L = w_qb.shape[0]
    dq = MLA_NOPE + MLA_ROPE
    wq = w_qb.reshape(L, MLA_Q_LORA, MLA_HEADS, dq)
    wq_rot = jnp.concatenate([jnp.zeros_like(wq[..., :MLA_NOPE]), _rot_cols(wq[..., MLA_NOPE:])], axis=-1)
    pad_q = lambda w: jnp.pad(w, ((0, 0), (0, 256 - MLA_Q_LORA), (0, 0), (0, HEAD_PAD - dq))).reshape(
        L, 256, MLA_HEADS * HEAD_PAD).astype(BF)
    wkv = w_kvb.reshape(L, MLA_KV_LORA, MLA_HEADS, MLA_NOPE + MLA_V)
    pad_kv = lambda w: jnp.pad(w, ((0, 0), (0, 0), (0, 0), (0, HEAD_PAD - w.shape[-1]))).reshape(
        L, MLA_KV_LORA, MLA_HEADS * HEAD_PAD).astype(BF)
    wvt = jnp.swapaxes(pad_kv(wkv[..., MLA_NOPE:]), 1, 2)
    return pad_q(wq), pad_q(wq_rot), pad_kv(wkv[..., :MLA_NOPE]), wvt


def kernel(x, positions, attn_norm_w, w_in, hgrn_lb_logits, hgrn_norm_w, mla_q_norm_w, mla_w_qb,
           mla_kv_norm_w, mla_w_kvb, idx_k_norm_w, idx_k_norm_b, rel_bias_table, w_out,
           mlp_norm_w, w_mlp_in, w_mlp_out, final_norm_w):
    B, S, _ = x.shape
    T = B * S
    depth = w_in.shape[0]

    inv_freq = 1.0 / (ROPE_THETA ** (jnp.arange(0, MLA_ROPE, 2, dtype=F32) / MLA_ROPE))
    ang = positions.astype(F32)[..., None] * inv_freq
    cos, sin = jnp.cos(ang).reshape(T, -1), jnp.sin(ang).reshape(T, -1)
    cos_t = jnp.concatenate([jnp.ones((T, MLA_NOPE), F32), cos, cos,
                             jnp.zeros((T, HEAD_PAD - MLA_NOPE - MLA_ROPE), F32)], axis=1)
    sin_t = jnp.concatenate([jnp.zeros((T, MLA_NOPE), F32), sin, sin,
                             jnp.zeros((T, HEAD_PAD - MLA_NOPE - MLA_ROPE), F32)], axis=1)
    pos_tiles = positions.reshape(B, S // ATT_TILE, ATT_TILE)

    lb = jnp.cumsum(jax.nn.softmax(hgrn_lb_logits.astype(F32), axis=0), axis=0)
    lb = lb - lb[0:1]
    lb3 = jnp.stack([jnp.log(lb), jnp.log1p(-lb), 1.0 - lb], axis=1)

    w_cat, w_dvt = _layout_w_in(w_in)
    wq, wqr, wk, wvt = _layout_mla(mla_w_qb, mla_w_kvb)
    qnw = jnp.pad(mla_q_norm_w, ((0, 0), (0, 256 - MLA_Q_LORA)))
    lnw = jnp.concatenate([idx_k_norm_w, idx_k_norm_w], axis=-1)
    lnb = jnp.concatenate([idx_k_norm_b, idx_k_norm_b], axis=-1)
    wo = w_out.astype(BF)
    w1 = w_mlp_in.astype(BF)
    w2 = w_mlp_out.astype(BF)
    tab = rel_bias_table.astype(F32)

    h = x.reshape(T, D_MODEL)
    for l in range(depth):
        gh, gm, gd, gq, gk, vdt = _inproj(h, attn_norm_w[l][None], w_cat, w_dvt, l)
        q_m, k_m, v_m, ki2 = _prep(gm, gk, cos_t, sin_t, qnw[l][None], wq[l], wqr[l],
                                   mla_kv_norm_w[l][None], wk[l], wvt[l], lnw[l][None], lnb[l][None])
        y_h = _hgrn(gh, lb3[l], hgrn_norm_w[l][None], B, S)
        y_m = _mla_attn(q_m, k_m, v_m, B, S)
        y_d = _dsa(gq, gk, gd, ki2, vdt, pos_tiles, tab, B, S)
        h = _outmlp(h, y_h, y_m, y_d, wo, mlp_norm_w[l][None], w1, w2,
                    final_norm_w[None], l, final=(l == depth - 1))
    return h.reshape(B, S, D_MODEL)
```

```python
import functools
import math

import jax
import jax.numpy as jnp
from jax import lax
from jax.experimental import pallas as pl
from jax.experimental.pallas import tpu as pltpu

D_MODEL = 1024
HGRN_HEADS = 4
HGRN_DK = 128
HGRN_DV = 128
HGRN_W = HGRN_HEADS * HGRN_DK
HGRN_CHUNK = 32
MLA_HEADS = 4
MLA_NOPE = 64
MLA_ROPE = 32
MLA_V = 64
MLA_Q_LORA = 192
MLA_KV_LORA = 128
ROPE_THETA = 10000.0
DSA_HEADS = 4
DSA_HEAD_DIM = 64
DSA_WIDTH = DSA_HEADS * DSA_HEAD_DIM
IDX_HEADS = 8
IDX_DIM = 64
TOPK_MAX = 256
NUM_BUCKETS = 32
MAX_EXACT = NUM_BUCKETS // 2
MAX_DISTANCE = 128
D_FF = 4 * D_MODEL
EPS = 1e-6

LANES = 128
SUBLANES = 8
BF16_ROWS = 16
HEAD_PAD = 128
NEG = -1e30
ATT_TILE = 256
MLA_KEY_TILE = 512
ROW_TILE = 512
HGRN_BLOCK = 256
VMEM_LIMIT = 52 * 1024 * 1024
LOG2E = math.log2(math.e)

DSA_V_ROWS = DSA_HEAD_DIM + BF16_ROWS

C_H = 0
C_M = C_H + 4 * HGRN_W
W_M = 640
C_D = C_M + W_M
W_D = 2 * DSA_WIDTH
C_Q = C_D + W_D
W_Q = IDX_HEADS * IDX_DIM
C_K = C_Q + W_Q
W_K = 256
N_PAD = C_K + W_K

BF = jnp.bfloat16
F32 = jnp.float32


def _dot(a, b):
    return jnp.dot(a, b, preferred_element_type=F32)


def _dot_nt(a, b):
    return lax.dot_general(a, b, (((1,), (1,)), ((), ())), preferred_element_type=F32)


def _rms(x, w, n=None):
    n = x.shape[-1] if n is None else n
    ms = jnp.sum(x * x, axis=-1, keepdims=True) * (1.0 / n)
    return x * lax.rsqrt(ms + EPS) * w


def _cparams(sem):
    return pltpu.CompilerParams(dimension_semantics=sem, vmem_limit_bytes=VMEM_LIMIT)


def _const_spec(shape):
    nd = len(shape)
    return pl.BlockSpec(shape, lambda *_: (0,) * nd, pipeline_mode=pl.Buffered(1))


def _layer_spec(shape, layer):
    nd = len(shape)
    return pl.BlockSpec((None,) + tuple(shape), lambda *_: (layer,) + (0,) * nd,
                        pipeline_mode=pl.Buffered(1))


def _inproj_kernel(x_ref, nw_ref, w_ref, wvt_ref, oh_ref, om_ref, od_ref, oq_ref, ok_ref, ovt_ref):
    u = _rms(x_ref[...], nw_ref[...]).astype(BF)
    oh_ref[...] = _dot(u, w_ref[:, C_H:C_M])
    om_ref[...] = _dot(u, w_ref[:, C_M:C_D])
    od_ref[...] = _dot(u, w_ref[:, C_D:C_Q]).astype(BF)
    oq_ref[...] = _dot(u, w_ref[:, C_Q:C_K]).astype(BF)
    ok_ref[...] = _dot(u, w_ref[:, C_K:N_PAD])
    vt = _dot_nt(wvt_ref[...], u)
    row = lax.broadcasted_iota(jnp.int32, vt.shape, 0)
    vt = jnp.where(row % DSA_V_ROWS == DSA_HEAD_DIM, 1.0, vt).astype(BF)
    for n in range(ovt_ref.shape[0]):
        ovt_ref[n] = vt[:, n * ATT_TILE:(n + 1) * ATT_TILE]


def _inproj(h, nw, w, wvt, layer, tm=ROW_TILE):
    T = h.shape[0]
    tm = min(tm, T)
    row = lambda i: (i, 0)
    vrows = DSA_HEADS * DSA_V_ROWS
    per = tm // ATT_TILE
    return pl.pallas_call(
        _inproj_kernel,
        grid=(T // tm,),
        in_specs=[pl.BlockSpec((tm, D_MODEL), row), _const_spec((1, D_MODEL)),
                  _layer_spec((D_MODEL, N_PAD), layer), _layer_spec((vrows, D_MODEL), layer)],
        out_specs=[pl.BlockSpec((tm, 4 * HGRN_W), row), pl.BlockSpec((tm, W_M), row),
                   pl.BlockSpec((tm, W_D), row), pl.BlockSpec((tm, W_Q), row),
                   pl.BlockSpec((tm, W_K), row),
                   pl.BlockSpec((per, vrows, ATT_TILE), lambda i: (i, 0, 0))],
        out_shape=[jax.ShapeDtypeStruct((T, 4 * HGRN_W), F32), jax.ShapeDtypeStruct((T, W_M), F32),
                   jax.ShapeDtypeStruct((T, W_D), BF), jax.ShapeDtypeStruct((T, W_Q), BF),
                   jax.ShapeDtypeStruct((T, W_K), F32),
                   jax.ShapeDtypeStruct((T // ATT_TILE, vrows, ATT_TILE), BF)],
        compiler_params=_cparams(("parallel",)),
        name="inproj",
    )(h, nw, w, wvt)


def _prep_kernel(gm_ref, gk_ref, cos_ref, sin_ref, qnw_ref, wq_ref, wqr_ref, kvnw_ref, wk_ref,
                 wvt_ref, lnw_ref, lnb_ref, q_ref, k_ref, vt_ref, ki_ref):
    cs = cos_ref[...]
    sn = sin_ref[...]
    cs4 = jnp.concatenate([cs] * MLA_HEADS, axis=1)
    sn4 = jnp.concatenate([sn] * MLA_HEADS, axis=1)
    qn = _rms(gm_ref[:, 0:256], qnw_ref[...], n=MLA_Q_LORA).astype(BF)
    scale = (MLA_NOPE + MLA_ROPE) ** -0.5 * LOG2E
    q = (_dot(qn, wq_ref[...]) * cs4 + _dot(qn, wqr_ref[...]) * sn4) * scale
    q_ref[...] = q.astype(BF)
    cn = _rms(gm_ref[:, 256:384], kvnw_ref[...]).astype(BF)
    kp = gm_ref[:, 384:512] * cs + gm_ref[:, 512:640] * sn
    k = _dot(cn, wk_ref[...]) + jnp.concatenate([kp] * MLA_HEADS, axis=1)
    k_ref[...] = k.astype(BF)
    vt = _dot_nt(wvt_ref[...], cn)
    row = lax.broadcasted_iota(jnp.int32, vt.shape, 0)
    vt_ref[...] = jnp.where(row % HEAD_PAD == MLA_V, 1.0, vt).astype(BF)
    x = gk_ref[:, 0:LANES]
    first = lax.broadcasted_iota(jnp.int32, x.shape, 1) < IDX_DIM
    mu = jnp.sum(jnp.where(first, x, 0.0), axis=-1, keepdims=True) * (1.0 / IDX_DIM)
    xc = x - mu
    var = jnp.sum(jnp.where(first, xc * xc, 0.0), axis=-1, keepdims=True) * (1.0 / IDX_DIM)
    ki_ref[...] = (xc * lax.rsqrt(var + EPS) * lnw_ref[...] + lnb_ref[...]).astype(BF)


def _prep(gm, gk, cos_t, sin_t, qnw, wq, wqr, kvnw, wk, wvt, lnw, lnb, tm=MLA_KEY_TILE):
    T = gm.shape[0]
    row = lambda i: (i, 0)
    hp = MLA_HEADS * HEAD_PAD
    return pl.pallas_call(
        _prep_kernel,
        grid=(T // tm,),
        in_specs=[pl.BlockSpec((tm, W_M), row), pl.BlockSpec((tm, W_K), row),
                  pl.BlockSpec((tm, LANES), row), pl.BlockSpec((tm, LANES), row),
                  _const_spec((1, 256)), _const_spec((256, hp)), _const_spec((256, hp)),
                  _const_spec((1, MLA_KV_LORA)), _const_spec((MLA_KV_LORA, hp)),
                  _const_spec((hp, MLA_KV_LORA)), _const_spec((1, LANES)), _const_spec((1, LANES))],
        out_specs=[pl.BlockSpec((tm, hp), row), pl.BlockSpec((tm, hp), row),
                   pl.BlockSpec((None, hp, tm), lambda i: (i, 0, 0)), pl.BlockSpec((tm, LANES), row)],
        out_shape=[jax.ShapeDtypeStruct((T, hp), BF), jax.ShapeDtypeStruct((T, hp), BF),
                   jax.ShapeDtypeStruct((T // tm, hp, tm), BF), jax.ShapeDtypeStruct((T, LANES), BF)],
        compiler_params=_cparams(("parallel",)),
        name="prep",
    )(gm, gk, cos_t, sin_t, qnw, wq, wqr, kvnw, wk, wvt, lnw, lnb)


def _split3(x):
    a = x.astype(BF)
    r = x - a.astype(F32)
    b = r.astype(BF)
    c = (r - b.astype(F32)).astype(BF)
    return a, b, c


def _hgrn_kernel(q_ref, f_ref, i_ref, g_ref, lb_ref, nw_ref, o_ref, st_ref, os_ref, *, tb):
    C = HGRN_CHUNK
    nc = tb // C

    @pl.when(pl.program_id(1) == 0)
    def _():
        st_ref[...] = jnp.zeros_like(st_ref)

    fp = f_ref[...]
    log_lb = lb_ref[0:1, :]
    log1m_lb = lb_ref[1:2, :]
    one_m_lb = lb_ref[2:3, :]
    ls = jnp.minimum(fp, 0.0) - jnp.log1p(jnp.exp(-jnp.abs(fp)))
    b = log1m_lb + ls
    log_f = jnp.maximum(log_lb, b) + jnp.log1p(jnp.exp(-jnp.abs(log_lb - b)))
    k = one_m_lb * (1.0 / (1.0 + jnp.exp(fp)))

    r = lax.broadcasted_iota(jnp.int32, (tb, tb), 0)
    c = lax.broadcasted_iota(jnp.int32, (tb, tb), 1)
    same = (r // C) == (c // C)
    tri = jnp.where(same & (c <= r), 1.0, 0.0).astype(BF)
    blk = jnp.where(same, 1.0, 0.0).astype(BF)
    a0, a1, a2 = _split3(log_f)
    G = _dot(tri, a0) + _dot(tri, a1) + _dot(tri, a2)
    G_last = _dot(blk, a0) + _dot(blk, a1) + _dot(blk, a2)

    eg = jnp.exp(G)
    q_dec = (q_ref[...] * (HGRN_DK ** -0.5) * eg).astype(BF)
    k_inv = (k * jnp.exp(-G)).astype(BF)
    k_state = k * jnp.exp(G_last - G)
    decay = jnp.exp(G_last)
    v = i_ref[...]
    vb = v.astype(BF)

    rr = lax.broadcasted_iota(jnp.int32, (C, C), 0)
    cc = lax.broadcasted_iota(jnp.int32, (C, C), 1)
    causal = cc <= rr

    heads = [slice(h * HGRN_DK, (h + 1) * HGRN_DK) for h in range(HGRN_HEADS)]
    sts = [st_ref[h] for h in range(HGRN_HEADS)]
    for n in range(nc):
        rs = slice(n * C, (n + 1) * C)
        qds = [q_dec[rs, hs] for hs in heads]
        As = [_dot_nt(qds[h], k_inv[rs, heads[h]]) for h in range(HGRN_HEADS)]
        kvs = [_dot(v[rs, heads[h]].T.astype(BF), k_state[rs, heads[h]].astype(BF))
               for h in range(HGRN_HEADS)]
        inter = [_dot_nt(qds[h], sts[h].astype(BF)) for h in range(HGRN_HEADS)]
        for h in range(HGRN_HEADS):
            A = jnp.where(causal, As[h], 0.0).astype(BF)
            os_ref[rs, heads[h]] = _dot(A, vb[rs, heads[h]]) + inter[h]
            sts[h] = decay[n * C:n * C + 1, heads[h]] * sts[h] + kvs[h]
    for h in range(HGRN_HEADS):
        st_ref[h] = sts[h]

    g = g_ref[...]
    gate = g * (1.0 / (1.0 + jnp.exp(-g)))
    for h in range(HGRN_HEADS):
        hs = slice(h * HGRN_DK, (h + 1) * HGRN_DK)
        o_ref[:, hs] = (_rms(os_ref[:, hs], nw_ref[...]) * gate[:, hs]).astype(BF)


def _hgrn(gh, lb3, nw, B, S, tb=HGRN_BLOCK):
    nb = S // tb
    col = lambda j: (lambda b, i: (b * nb + i, j))
    return pl.pallas_call(
        functools.partial(_hgrn_kernel, tb=tb),
        grid=(B, nb),
        in_specs=[pl.BlockSpec((tb, HGRN_W), col(0)), pl.BlockSpec((tb, HGRN_W), col(1)),
                  pl.BlockSpec((tb, HGRN_W), col(2)), pl.BlockSpec((tb, HGRN_W), col(3)),
                  pl.BlockSpec((3, HGRN_W), lambda b, i: (0, 0)),
                  pl.BlockSpec((1, HGRN_DV), lambda b, i: (0, 0))],
        out_specs=pl.BlockSpec((tb, HGRN_W), col(0)),
        out_shape=jax.ShapeDtypeStruct((B * S, HGRN_W), BF),
        scratch_shapes=[pltpu.VMEM((HGRN_HEADS, HGRN_DV, HGRN_DK), F32),
                        pltpu.VMEM((tb, HGRN_W), F32)],
        compiler_params=_cparams(("parallel", "arbitrary")),
        name="hgrn",
    )(gh, gh, gh, gh, lb3, nw)


def _mla_kernel(q_ref, k_ref, vt_ref, o_ref, *, t, tk):
    i = pl.program_id(1)
    n_full = (i * t) // tk
    key = lax.broadcasted_iota(jnp.int32, (tk, t), 0)
    qry = lax.broadcasted_iota(jnp.int32, (tk, t), 1)
    causal = n_full * tk + key <= i * t + qry

    heads = [slice(h * HEAD_PAD, (h + 1) * HEAD_PAD) for h in range(MLA_HEADS)]

    def logits(j):
        ks = pl.ds(pl.multiple_of(j * tk, tk), tk)
        return tuple(_dot_nt(k_ref[ks, hs], q_ref[:, hs]) for hs in heads)

    def step(tiles, state, mask_last):
        ss = [list(logits(j)) for j in tiles]
        if mask_last:
            ss[-1] = [jnp.where(causal, s, NEG) for s in ss[-1]]
        ms = []
        for h in range(MLA_HEADS):
            m = state[h][0]
            for s in ss:
                m = jnp.maximum(m, jnp.max(s[h], axis=0, keepdims=True))
            ms.append(m)
        out = []
        for h in range(MLA_HEADS):
            acc = jnp.exp2(state[h][0] - ms[h]) * state[h][1]
            for j, s in zip(tiles, ss):
                acc = acc + _dot(vt_ref[j, heads[h], :], jnp.exp2(s[h] - ms[h]).astype(BF))
            out.append((ms[h], acc))
        return tuple(out)

    init = tuple((jnp.full((1, t), NEG, F32), jnp.zeros((HEAD_PAD, t), F32))
                 for _ in range(MLA_HEADS))
    state = lax.fori_loop(0, n_full // 2, lambda jj, c: step([2 * jj, 2 * jj + 1], c, False), init)
    state = lax.cond(n_full % 2 == 1,
                     lambda c: step([n_full - 1, n_full], c, True),
                     lambda c: step([n_full], c, True), state)
    outs = [acc[:MLA_V, :] * (1.0 / acc[MLA_V:MLA_V + 1, :]) for (_, acc) in state]
    o_ref[...] = jnp.concatenate(outs, axis=0).T.astype(BF)


def _mla_attn(q, k, vt, B, S, t=ATT_TILE, tk=MLA_KEY_TILE):
    nq = S // t
    nk = S // tk
    hp = MLA_HEADS * HEAD_PAD
    return pl.pallas_call(
        functools.partial(_mla_kernel, t=t, tk=tk),
        grid=(B, nq),
        in_specs=[pl.BlockSpec((t, hp), lambda b, i: (b * nq + i, 0)),
                  pl.BlockSpec((S, hp), lambda b, i: (b, 0)),
                  pl.BlockSpec((nk, hp, tk), lambda b, i: (b, 0, 0))],
        out_specs=pl.BlockSpec((t, MLA_HEADS * MLA_V), lambda b, i: (b * nq + i, 0)),
        out_shape=jax.ShapeDtypeStruct((B * S, MLA_HEADS * MLA_V), BF),
        compiler_params=_cparams(("parallel", "arbitrary")),
        name="mla_attn",
    )(q, k, vt)


BISECT_CAP = 48
END_STEPS = 2
_LAST_BUCKET_FROM = next(
    n for n in range(MAX_EXACT, 1 << 20)
    if MAX_EXACT + int(math.log(n / MAX_EXACT) / math.log(MAX_DISTANCE / MAX_EXACT)
                       * (NUM_BUCKETS - MAX_EXACT)) >= NUM_BUCKETS - 1)
assert _LAST_BUCKET_FROM <= LANES - 1 and _LAST_BUCKET_FROM <= MAX_DISTANCE


def _dsa_kernel(qi_ref, w_ref, qd_ref, pos_ref, ki_ref, kd_ref, vt_ref, tab_ref, pmin_ref, pmax_ref,
                run_ref, o_ref, sc_ref, qm_ref, bias_ref, *, t, n_sel):
    i = pl.program_id(1)
    lane = lax.broadcasted_iota(jnp.int32, (t, LANES), 1)
    lo_half = lane < DSA_HEAD_DIM
    key = lax.broadcasted_iota(jnp.int32, (t, t), 0)
    qry = lax.broadcasted_iota(jnp.int32, (t, t), 1)
    causal = key <= qry
    k_sel = float(n_sel)

    def key_rows(j):
        return pl.ds(pl.multiple_of(j * t, t), t)

    @pl.when(i + 1 < sc_ref.shape[0])
    def _():
        sc_ref[i + 1] = jnp.full((t, t), -jnp.inf, F32)

    def over_tiles(body, init):
        return lax.fori_loop(0, (i + 2) // 2, lambda jj, c: body(2 * jj + 1, body(2 * jj, c)), init)

    def fold(x, op):
        return op(x.reshape(t // SUBLANES, SUBLANES, t), axis=0)

    zero_b = jnp.zeros((t, LANES), BF)
    for p in range(IDX_HEADS // 2):
        qp = qi_ref[:, p * LANES:(p + 1) * LANES]
        qm_ref[2 * p] = jnp.where(lo_half, qp, zero_b)
        qm_ref[2 * p + 1] = jnp.where(lo_half, zero_b, qp)
    wt = w_ref[...].T * ((IDX_HEADS ** -0.5) * (IDX_DIM ** -0.5))

    def score_tiles(js):
        kts = [ki_ref[key_rows(j), :] for j in js]
        accs = [None] * len(js)
        for h0 in range(0, IDX_HEADS, 4):
            ss = [[_dot_nt(kt, qm_ref[h]) for h in range(h0, h0 + 4)] for kt in kts]
            for n in range(len(js)):
                for m, s in enumerate(ss[n]):
                    term = jnp.maximum(s, 0.0) * wt[h0 + m:h0 + m + 1, :]
                    accs[n] = term if accs[n] is None else accs[n] + term
        return accs

    def add_stats(x, c):
        mx, mn, cp, cz = c
        return (jnp.maximum(mx, fold(x, jnp.max)),
                jnp.minimum(mn, fold(jnp.where(x == -jnp.inf, jnp.inf, x), jnp.min)),
                cp + fold(jnp.where(x > 0.0, 1.0, 0.0), jnp.sum),
                cz + fold(jnp.where(x == 0.0, 1.0, 0.0), jnp.sum))

    def p1_tiles(js, c, diag_last):
        xs = score_tiles(js)
        if diag_last:
            xs[-1] = jnp.where(causal, xs[-1], -jnp.inf)
        for j, x in zip(js, xs):
            sc_ref[j] = x
            c = add_stats(x, c)
        return c

    z8 = jnp.zeros((SUBLANES, t), F32)
    stats = lax.fori_loop(0, i // 2, lambda jj, c: p1_tiles([2 * jj, 2 * jj + 1], c, False),
                          (z8 - jnp.inf, z8 + jnp.inf, z8, z8))
    mx, mn, cp, cz = lax.cond(i % 2 == 1,
                              lambda c: p1_tiles([i - 1, i], c, True),
                              lambda c: p1_tiles([i], c, True), stats)

    def total(c):
        return jnp.sum(c, axis=0, keepdims=True)

    def count_gt(thr):
        def body(j, c):
            return c + fold(jnp.where(sc_ref[j] > thr, 1.0, 0.0), jnp.sum)
        return total(over_tiles(body, jnp.zeros((SUBLANES, t), F32)))

    row_max = jnp.max(mx, axis=0, keepdims=True)
    row_min = jnp.min(mn, axis=0, keepdims=True)
    c_pos, c_zero = total(cp), total(cz)
    n_valid = (i * t + 1 + lax.broadcasted_iota(jnp.int32, (1, t), 1)).astype(F32)

    big = n_valid > k_sel
    pos_q = jnp.logical_and(big, c_pos >= k_sel)
    tie_q = jnp.logical_and(jnp.logical_and(big, c_pos < k_sel), c_pos + c_zero >= k_sel)
    neg_q = jnp.logical_and(big, c_pos + c_zero < k_sel)
    need = jnp.where(tie_q, k_sel - c_pos, 0.0)
    lo0 = jnp.where(jnp.logical_or(pos_q, tie_q), 0.0, -jnp.inf)
    hi0 = jnp.where(neg_q, 0.0, row_max)
    c_lo0 = jnp.where(pos_q, c_pos, jnp.where(tie_q, k_sel, n_valid))

    searched = c_lo0 > k_sel

    def open_queries(c_lo):
        return jnp.max(jnp.where(c_lo > k_sel + END_STEPS, 1.0, 0.0))

    def next_above(thr):
        def body(j, m):
            x = sc_ref[j]
            return jnp.minimum(m, fold(jnp.where(x > thr, x, jnp.inf), jnp.min))
        return jnp.min(over_tiles(body, jnp.full((SUBLANES, t), jnp.inf, F32)), axis=0, keepdims=True)

    def bis_cond(c):
        return jnp.logical_and(c[0] < BISECT_CAP, c[1] > 0.0)

    def bisect_once(lo, hi, c_lo):
        active = c_lo > k_sel
        base = jnp.maximum(lo, row_min)
        mid = base + 0.5 * (hi - base)
        cnt = count_gt(mid)
        up = jnp.logical_and(active, cnt >= k_sel)
        dn = jnp.logical_and(active, cnt < k_sel)
        return jnp.where(up, mid, lo), jnp.where(dn, mid, hi), jnp.where(up, cnt, c_lo)

    def bis_body(c):
        it, _, lo, hi, c_lo = c
        lo, hi, c_lo = bisect_once(*bisect_once(lo, hi, c_lo))
        return it + 2, open_queries(c_lo), lo, hi, c_lo

    _, _, lo, _, c_lo = lax.while_loop(
        bis_cond, bis_body, (jnp.int32(0), open_queries(c_lo0), lo0, hi0, c_lo0))

    def step_body(_, c):
        lo, c_lo = c
        more = c_lo > k_sel
        return jnp.where(more, next_above(lo), lo), jnp.where(more, c_lo - 1.0, c_lo)

    lo, _ = lax.fori_loop(0, END_STEPS, step_body, (lo, c_lo))
    c_lo = jnp.where(searched, count_gt(lo), c_lo)

    still_open = c_lo != k_sel
    still_open = jnp.logical_and(searched, still_open)

    def exact_kth(_):
        def to_float(kk):
            return lax.bitcast_convert_type(jnp.where(kk < 0, kk ^ jnp.int32(0x7FFFFFFF), kk), F32)

        def count_ge(thr):
            def body(j, c):
                return c + fold(jnp.where(sc_ref[j] >= thr, 1.0, 0.0), jnp.sum)
            return total(over_tiles(body, jnp.zeros((SUBLANES, t), F32)))

        def body(b, kth):
            cand = kth + lax.shift_left(jnp.int32(1), 31 - b)
            return jnp.where(count_ge(to_float(cand)) >= k_sel, cand, kth)

        kth = to_float(lax.fori_loop(0, 32, body, jnp.full((1, t), -2 ** 31, jnp.int32)))
        return kth, count_gt(kth)

    kth, above = lax.cond(jnp.max(jnp.where(still_open, 1.0, 0.0)) > 0.0, exact_kth,
                          lambda _: (jnp.zeros((1, t), F32), jnp.zeros((1, t), F32)), 0)
    lo = jnp.where(still_open, kth, lo)
    need = jnp.where(still_open, k_sel - above, need)
    tie_val = jnp.where(still_open, kth, 0.0)

    def mask_plain(_):
        def body(j, c):
            sc_ref[j] = jnp.where(sc_ref[j] > lo, 0.0, NEG)
            return c
        return lax.fori_loop(0, i + 1, body, 0)

    def mask_ties(_):
        lower = jnp.where(qry <= key, 1.0, 0.0).astype(BF)

        def body(jj, seen):
            xs = [sc_ref[2 * jj + n] for n in range(2)]
            tied = [x == tie_val for x in xs]
            tfs = [jnp.where(m, 1.0, 0.0) for m in tied]
            ranks = [_dot(lower, tf.astype(BF)) for tf in tfs]
            for n in range(2):
                take = jnp.logical_and(tied[n], ranks[n] + seen <= need)
                sc_ref[2 * jj + n] = jnp.where(jnp.logical_or(xs[n] > lo, take), 0.0, NEG)
                seen = seen + total(fold(tfs[n], jnp.sum))
            return seen
        lax.fori_loop(0, (i + 2) // 2, body, jnp.zeros((1, t), F32))
        return 0

    lax.cond(jnp.max(need) > 0.0, mask_ties, mask_plain, 0)

    qh = []
    for p in range(DSA_HEADS // 2):
        qp = qd_ref[:, p * LANES:(p + 1) * LANES]
        qh.append(jnp.where(lo_half, qp, zero_b))
        qh.append(jnp.where(lo_half, zero_b, qp))
    pq = pos_ref[pl.ds(i, 1), :]
    pq_min = pmin_ref[pl.program_id(0), i]
    far_bias = [tab_ref[NUM_BUCKETS - 1, h] * LOG2E for h in range(DSA_HEADS)]
    log_ratio = math.log(MAX_DISTANCE / MAX_EXACT)

    dist = lax.broadcasted_iota(jnp.int32, (SUBLANES, LANES), 1)
    large = MAX_EXACT + (jnp.log(jnp.maximum(dist, 1).astype(F32) / MAX_EXACT) / log_ratio
                         * (NUM_BUCKETS - MAX_EXACT)).astype(jnp.int32)
    bucket = jnp.where(dist < MAX_EXACT, dist, jnp.minimum(large, NUM_BUCKETS - 1))
    by_dist = []
    for h in range(DSA_HEADS):
        bh = jnp.full((SUBLANES, LANES), tab_ref[0, h], F32)
        for jb in range(1, NUM_BUCKETS):
            bh = jnp.where(bucket >= jb, tab_ref[jb, h], bh)
        by_dist.append(jnp.concatenate([bh * LOG2E] * (t // SUBLANES), axis=0))

    def bias_of(n):
        n = jnp.clip(n, 0, LANES - 1)
        return [jnp.concatenate([jnp.take_along_axis(by_dist[h], n[:, c * LANES:(c + 1) * LANES], axis=1)
                                 for c in range(t // LANES)], axis=1) for h in range(DSA_HEADS)]

    def pair_bias(pk_row):
        pk = jnp.broadcast_to(pk_row, (SUBLANES, t)).T[:, 0:1]
        return bias_of(pq - pk)

    b_idx = pl.program_id(0)

    @pl.when(jnp.logical_and(b_idx == 0, i == 0))
    def _():
        for gap in range(2):
            for h, tile in enumerate(bias_of(gap * t + qry - key)):
                bias_ref[gap * DSA_HEADS + h] = tile

    def attend(tiles, carry, bias):
        ss = []
        for j in tiles:
            madd = sc_ref[j]
            ks = key_rows(j)
            kps = [kd_ref[ks, p * LANES:(p + 1) * LANES] for p in range(DSA_HEADS // 2)]
            ss.append([_dot_nt(kps[h // 2], qh[h]) + (madd + bias[h]) for h in range(DSA_HEADS)])
        ms = []
        for h in range(DSA_HEADS):
            m = carry[h][0]
            for s in ss:
                m = jnp.maximum(m, jnp.max(s[h], axis=0, keepdims=True))
            ms.append(m)
        out = []
        for h in range(DSA_HEADS):
            acc = jnp.exp2(carry[h][0] - ms[h]) * carry[h][1]
            for j, s in zip(tiles, ss):
                p = jnp.exp2(s[h] - ms[h]).astype(BF)
                acc = acc + _dot(vt_ref[j, h * DSA_V_ROWS:(h + 1) * DSA_V_ROWS, :], p)
            out.append((ms[h], acc))
        return tuple(out)

    def is_far(j):
        return pq_min - pmax_ref[b_idx, j] >= MAX_DISTANCE

    def p3_body(j, carry):
        gap = i - j
        consecutive = jnp.logical_and(
            jnp.logical_and(run_ref[b_idx, i] == 1, run_ref[b_idx, j] == 1),
            jnp.logical_and(gap <= 1, pq_min - pmin_ref[b_idx, j] == gap * t))

        def near(c):
            return lax.cond(
                consecutive,
                lambda c: attend([j], c, [bias_ref[gap * DSA_HEADS + h] for h in range(DSA_HEADS)]),
                lambda c: attend([j], c, pair_bias(pos_ref[pl.ds(j, 1), :])), c)

        return lax.cond(is_far(j), lambda c: attend([j], c, far_bias), near, carry)

    def p3_pair(jj, carry):
        j0 = 2 * jj
        n_here = jnp.minimum(i + 1 - j0, 2)
        both_far = jnp.logical_and(n_here == 2,
                                   jnp.logical_and(is_far(j0), is_far(jnp.minimum(j0 + 1, i))))
        return lax.cond(both_far,
                        lambda c: attend([j0, j0 + 1], c, far_bias),
                        lambda c: lax.fori_loop(0, n_here, lambda n, c: p3_body(j0 + n, c), c),
                        carry)

    init = tuple((jnp.full((1, t), NEG, F32), jnp.zeros((DSA_V_ROWS, t), F32))
                 for _ in range(DSA_HEADS))
    carry = lax.fori_loop(0, (i + 2) // 2, p3_pair, init)
    outs = [acc[:DSA_HEAD_DIM, :] * (1.0 / acc[DSA_HEAD_DIM:DSA_HEAD_DIM + 1, :])
            for (_, acc) in carry]
    o_ref[...] = jnp.concatenate(outs, axis=0).T.astype(BF)


def _dsa(gq, gk, gd, ki2, vdt, pos_tiles, tab, B, S, t=ATT_TILE):
    nq = S // t
    n_sel = min(TOPK_MAX, S // 4)
    vrows = DSA_HEADS * DSA_V_ROWS
    qrow = lambda c: (lambda b, i: (b * nq + i, c))
    return pl.pallas_call(
        functools.partial(_dsa_kernel, t=t, n_sel=n_sel),
        grid=(B, nq),
        in_specs=[pl.BlockSpec((t, W_Q), qrow(0)),
                  pl.BlockSpec((t, LANES), qrow(1)),
                  pl.BlockSpec((t, DSA_WIDTH), qrow(0)),
                  pl.BlockSpec((None, nq, t), lambda b, i: (b, 0, 0)),
                  pl.BlockSpec((S, LANES), lambda b, i: (b, 0)),
                  pl.BlockSpec((S, DSA_WIDTH), lambda b, i: (b, 1)),
                  pl.BlockSpec((nq, vrows, t), lambda b, i: (b, 0, 0)),
                  pl.BlockSpec(memory_space=pltpu.SMEM), pl.BlockSpec(memory_space=pltpu.SMEM),
                  pl.BlockSpec(memory_space=pltpu.SMEM), pl.BlockSpec(memory_space=pltpu.SMEM)],
        out_specs=pl.BlockSpec((t, DSA_WIDTH), qrow(0)),
        out_shape=jax.ShapeDtypeStruct((B * S, DSA_WIDTH), BF),
        scratch_shapes=[pltpu.VMEM((nq, t, t), F32), pltpu.VMEM((IDX_HEADS, t, LANES), BF),
                        pltpu.VMEM((2 * DSA_HEADS, t, t), F32)],
        compiler_params=_cparams(("arbitrary", "arbitrary")),
        name="dsa",
    )(gq, gk, gd, pos_tiles, ki2, gd, vdt, tab, jnp.min(pos_tiles, axis=-1), jnp.max(pos_tiles, axis=-1),
      jnp.all(pos_tiles[..., 1:] - pos_tiles[..., :-1] == 1, axis=-1).astype(jnp.int32))


def _outmlp_kernel(h_ref, yh_ref, ym_ref, yd_ref, wo_ref, nw_ref, w1_ref, w2_ref, fw_ref, o_ref,
                   *, final, ff_chunk):
    mixed = jnp.concatenate([yh_ref[...], ym_ref[...], yd_ref[...]], axis=1)
    h = h_ref[...] + _dot(mixed, wo_ref[...])
    u = _rms(h, nw_ref[...]).astype(BF)
    out = h
    for c in range(D_FF // ff_chunk):
        cs = slice(c * ff_chunk, (c + 1) * ff_chunk)
        a = jnp.maximum(_dot(u, w1_ref[:, cs]), 0.0)
        out = out + _dot((a * a).astype(BF), w2_ref[cs, :])
    if final:
        out = _rms(out, fw_ref[...])
    o_ref[...] = out


def _outmlp(h, yh, ym, yd, wo, nw, w1, w2, fw, layer, final, tm=ROW_TILE, ff_chunk=1024):
    T = h.shape[0]
    tm = min(tm, T)
    row = lambda i: (i, 0)
    return pl.pallas_call(
        functools.partial(_outmlp_kernel, final=final, ff_chunk=ff_chunk),
        grid=(T // tm,),
        in_specs=[pl.BlockSpec((tm, D_MODEL), row), pl.BlockSpec((tm, HGRN_W), row),
                  pl.BlockSpec((tm, MLA_HEADS * MLA_V), row), pl.BlockSpec((tm, DSA_WIDTH), row),
                  _layer_spec((D_MODEL, D_MODEL), layer), _const_spec((1, D_MODEL)),
                  _layer_spec((D_MODEL, D_FF), layer), _layer_spec((D_FF, D_MODEL), layer),
                  _const_spec((1, D_MODEL))],
        out_specs=pl.BlockSpec((tm, D_MODEL), row),
        out_shape=jax.ShapeDtypeStruct((T, D_MODEL), F32),
        compiler_params=_cparams(("parallel",)),
        name="outmlp",
    )(h, yh, ym, yd, wo, nw, w1, w2, fw)


def _rot_cols(w):
    half = w.shape[-1] // 2
    return jnp.concatenate([-w[..., half:], w[..., :half]], axis=-1)


def _place(w, width, off):
    pad = [(0, 0)] * (w.ndim - 1) + [(off, width - off - w.shape[-1])]
    return jnp.pad(w, pad)


def _layout_w_in(w_in):
    sizes = (HGRN_W, HGRN_W, HGRN_W, HGRN_W, MLA_Q_LORA, MLA_KV_LORA + MLA_ROPE,
             DSA_WIDTH, DSA_WIDTH, DSA_WIDTH, IDX_HEADS * IDX_DIM, IDX_DIM, IDX_HEADS)
    offs = [0]
    for s in sizes:
        offs.append(offs[-1] + s)
    hq, hf, hi, hg, mqa, mkva, dq, dk, dv, iq, ik, iw = [
        w_in[..., offs[n]:offs[n + 1]] for n in range(len(sizes))]
    ckv, kpe = mkva[..., :MLA_KV_LORA], mkva[..., MLA_KV_LORA:]
    cols = [hq, hf, hi, hg,
            _place(mqa, 256, 0), ckv, _place(kpe, LANES, MLA_NOPE), _place(_rot_cols(kpe), LANES, MLA_NOPE),
            dq * (DSA_HEAD_DIM ** -0.5 * LOG2E), dk,
            iq,
            ik, ik, _place(iw, LANES, 0)]
    w_cat = jnp.concatenate(cols, axis=-1).astype(BF)
    L = w_in.shape[0]
    dvt = jnp.swapaxes(dv, 1, 2).reshape(L, DSA_HEADS, DSA_HEAD_DIM, D_MODEL)
    dvt = jnp.pad(dvt, ((0, 0), (0, 0), (0, DSA_V_ROWS - DSA_HEAD_DIM), (0, 0)))
    return w_cat, dvt.reshape(L, DSA_HEADS * DSA_V_ROWS, D_MODEL).astype(BF)


def _layout_mla(w_qb, w_kvb):
    L = w_qb.shape[0]
    dq = MLA_NOPE + MLA_ROPE
    wq = w_qb.reshape(L, MLA_Q_LORA, MLA_HEADS, dq)
    wq_rot = jnp.concatenate([jnp.zeros_like(wq[..., :MLA_NOPE]), _rot_cols(wq[..., MLA_NOPE:])], axis=-1)
    pad_q = lambda w: jnp.pad(w, ((0, 0), (0, 256 - MLA_Q_LORA), (0, 0), (0, HEAD_PAD - dq))).reshape(
        L, 256, MLA_HEADS * HEAD_PAD).astype(BF)
    wkv = w_kvb.reshape(L, MLA_KV_LORA, MLA_HEADS, MLA_NOPE + MLA_V)
    pad_kv = lambda w: jnp.pad(w, ((0, 0), (0, 0), (0, 0), (0, HEAD_PAD - w.shape[-1]))).reshape(
        L, MLA_KV_LORA, MLA_HEADS * HEAD_PAD).astype(BF)
    wvt = jnp.swapaxes(pad_kv(wkv[..., MLA_NOPE:]), 1, 2)
    return pad_q(wq), pad_q(wq_rot), pad_kv(wkv[..., :MLA_NOPE]), wvt


def kernel(x, positions, attn_norm_w, w_in, hgrn_lb_logits, hgrn_norm_w, mla_q_norm_w, mla_w_qb,
           mla_kv_norm_w, mla_w_kvb, idx_k_norm_w, idx_k_norm_b, rel_bias_table, w_out,
           mlp_norm_w, w_mlp_in, w_mlp_out, final_norm_w):
    B, S, _ = x.shape
    T = B * S
    depth = w_in.shape[0]

    inv_freq = 1.0 / (ROPE_THETA ** (jnp.arange(0, MLA_ROPE, 2, dtype=F32) / MLA_ROPE))
    ang = positions.astype(F32)[..., None] * inv_freq
    cos, sin = jnp.cos(ang).reshape(T, -1), jnp.sin(ang).reshape(T, -1)
    cos_t = jnp.concatenate([jnp.ones((T, MLA_NOPE), F32), cos, cos,
                             jnp.zeros((T, HEAD_PAD - MLA_NOPE - MLA_ROPE), F32)], axis=1)
    sin_t = jnp.concatenate([jnp.zeros((T, MLA_NOPE), F32), sin, sin,
                             jnp.zeros((T, HEAD_PAD - MLA_NOPE - MLA_ROPE), F32)], axis=1)
    pos_tiles = positions.reshape(B, S // ATT_TILE, ATT_TILE)

    lb = jnp.cumsum(jax.nn.softmax(hgrn_lb_logits.astype(F32), axis=0), axis=0)
    lb = lb - lb[0:1]
    lb3 = jnp.stack([jnp.log(lb), jnp.log1p(-lb), 1.0 - lb], axis=1)

    w_cat, w_dvt = _layout_w_in(w_in)
    wq, wqr, wk, wvt = _layout_mla(mla_w_qb, mla_w_kvb)
    qnw = jnp.pad(mla_q_norm_w, ((0, 0), (0, 256 - MLA_Q_LORA)))
    lnw = jnp.concatenate([idx_k_norm_w, idx_k_norm_w], axis=-1)
    lnb = jnp.concatenate([idx_k_norm_b, idx_k_norm_b], axis=-1)
    wo = w_out.astype(BF)
    w1 = w_mlp_in.astype(BF)
    w2 = w_mlp_out.astype(BF)
    tab = rel_bias_table.astype(F32)

    h = x.reshape(T, D_MODEL)
    for l in range(depth):
        gh, gm, gd, gq, gk, vdt = _inproj(h, attn_norm_w[l][None], w_cat, w_dvt, l)
        q_m, k_m, v_m, ki2 = _prep(gm, gk, cos_t, sin_t, qnw[l][None], wq[l], wqr[l],
                                   mla_kv_norm_w[l][None], wk[l], wvt[l], lnw[l][None], lnb[l][None])
        y_h = _hgrn(gh, lb3[l], hgrn_norm_w[l][None], B, S)
        y_m = _mla_attn(q_m, k_m, v_m, B, S)
        y_d = _dsa(gq, gk, gd, ki2, vdt, pos_tiles, tab, B, S)
        h = _outmlp(h, y_h, y_m, y_d, wo, mlp_norm_w[l][None], w1, w2,
                    final_norm_w[None], l, final=(l == depth - 1))
    return h.reshape(B, S, D_MODEL)
```

```python
import functools
import math

import jax
import jax.numpy as jnp
from jax import lax
from jax.experimental import pallas as pl
from jax.experimental.pallas import tpu as pltpu

D_MODEL = 1024
HGRN_HEADS = 4
HGRN_DK = 128
HGRN_DV = 128
HGRN_W = HGRN_HEADS * HGRN_DK
HGRN_CHUNK = 32
MLA_HEADS = 4
MLA_NOPE = 64
MLA_ROPE = 32
MLA_V = 64
MLA_Q_LORA = 192
MLA_KV_LORA = 128
ROPE_THETA = 10000.0
DSA_HEADS = 4
DSA_HEAD_DIM = 64
DSA_WIDTH = DSA_HEADS * DSA_HEAD_DIM
IDX_HEADS = 8
IDX_DIM = 64
TOPK_MAX = 256
NUM_BUCKETS = 32
MAX_EXACT = NUM_BUCKETS // 2
MAX_DISTANCE = 128
D_FF = 4 * D_MODEL
EPS = 1e-6

LANES = 128
SUBLANES = 8
BF16_ROWS = 16
HEAD_PAD = 128
NEG = -1e30
ATT_TILE = 256
MLA_KEY_TILE = 512
ROW_TILE = 512
HGRN_BLOCK = 256
VMEM_LIMIT = 52 * 1024 * 1024
LOG2E = math.log2(math.e)

DSA_V_ROWS = DSA_HEAD_DIM + BF16_ROWS
MLA_V_ROWS = MLA_V + BF16_ROWS

C_H = 0
C_M = C_H + 4 * HGRN_W
W_M = 640
C_D = C_M + W_M
W_D = 2 * DSA_WIDTH
C_Q = C_D + W_D
W_Q = IDX_HEADS * IDX_DIM
C_K = C_Q + W_Q
W_K = 256
N_PAD = C_K + W_K

BF = jnp.bfloat16
F32 = jnp.float32


def _dot(a, b):
    return jnp.dot(a, b, preferred_element_type=F32)


def _dot_nt(a, b):
    return lax.dot_general(a, b, (((1,), (1,)), ((), ())), preferred_element_type=F32)


def _rms(x, w, n=None):
    n = x.shape[-1] if n is None else n
    ms = jnp.sum(x * x, axis=-1, keepdims=True) * (1.0 / n)
    return x * lax.rsqrt(ms + EPS) * w


def _cparams(sem):
    return pltpu.CompilerParams(dimension_semantics=sem, vmem_limit_bytes=VMEM_LIMIT)


def _const_spec(shape):
    nd = len(shape)
    return pl.BlockSpec(shape, lambda *_: (0,) * nd, pipeline_mode=pl.Buffered(1))


def _layer_spec(shape, layer):
    nd = len(shape)
    return pl.BlockSpec((None,) + tuple(shape), lambda *_: (layer,) + (0,) * nd,
                        pipeline_mode=pl.Buffered(1))


def _inproj_kernel(x_ref, nw_ref, w_ref, wvt_ref, oh_ref, om_ref, od_ref, oq_ref, ok_ref, ovt_ref):
    u = _rms(x_ref[...], nw_ref[...]).astype(BF)
    oh_ref[...] = _dot(u, w_ref[:, C_H:C_M])
    om_ref[...] = _dot(u, w_ref[:, C_M:C_D])
    od_ref[...] = _dot(u, w_ref[:, C_D:C_Q]).astype(BF)
    oq_ref[...] = _dot(u, w_ref[:, C_Q:C_K]).astype(BF)
    ok_ref[...] = _dot(u, w_ref[:, C_K:N_PAD])
    vt = _dot_nt(wvt_ref[...], u)
    row = lax.broadcasted_iota(jnp.int32, vt.shape, 0)
    vt = jnp.where(row % DSA_V_ROWS == DSA_HEAD_DIM, 1.0, vt).astype(BF)
    for n in range(ovt_ref.shape[0]):
        ovt_ref[n] = vt[:, n * ATT_TILE:(n + 1) * ATT_TILE]


def _inproj(h, nw, w, wvt, layer, tm=ROW_TILE):
    T = h.shape[0]
    tm = min(tm, T)
    row = lambda i: (i, 0)
    vrows = DSA_HEADS * DSA_V_ROWS
    per = tm // ATT_TILE
    return pl.pallas_call(
        _inproj_kernel,
        grid=(T // tm,),
        in_specs=[pl.BlockSpec((tm, D_MODEL), row), _const_spec((1, D_MODEL)),
                  _layer_spec((D_MODEL, N_PAD), layer), _layer_spec((vrows, D_MODEL), layer)],
        out_specs=[pl.BlockSpec((tm, 4 * HGRN_W), row), pl.BlockSpec((tm, W_M), row),
                   pl.BlockSpec((tm, W_D), row), pl.BlockSpec((tm, W_Q), row),
                   pl.BlockSpec((tm, W_K), row),
                   pl.BlockSpec((per, vrows, ATT_TILE), lambda i: (i, 0, 0))],
        out_shape=[jax.ShapeDtypeStruct((T, 4 * HGRN_W), F32), jax.ShapeDtypeStruct((T, W_M), F32),
                   jax.ShapeDtypeStruct((T, W_D), BF), jax.ShapeDtypeStruct((T, W_Q), BF),
                   jax.ShapeDtypeStruct((T, W_K), F32),
                   jax.ShapeDtypeStruct((T // ATT_TILE, vrows, ATT_TILE), BF)],
        compiler_params=_cparams(("parallel",)),
        name="inproj",
    )(h, nw, w, wvt)


def _prep_kernel(gm_ref, gk_ref, cos_ref, sin_ref, qnw_ref, wq_ref, wqr_ref, kvnw_ref, wk_ref,
                 wvt_ref, lnw_ref, lnb_ref, q_ref, k_ref, vt_ref, ki_ref):
    cs = cos_ref[...]
    sn = sin_ref[...]
    cs4 = jnp.concatenate([cs] * MLA_HEADS, axis=1)
    sn4 = jnp.concatenate([sn] * MLA_HEADS, axis=1)
    qn = _rms(gm_ref[:, 0:256], qnw_ref[...], n=MLA_Q_LORA).astype(BF)
    scale = (MLA_NOPE + MLA_ROPE) ** -0.5 * LOG2E
    q = (_dot(qn, wq_ref[...]) * cs4 + _dot(qn, wqr_ref[...]) * sn4) * scale
    q_ref[...] = q.astype(BF)
    cn = _rms(gm_ref[:, 256:384], kvnw_ref[...]).astype(BF)
    kp = gm_ref[:, 384:512] * cs + gm_ref[:, 512:640] * sn
    k = _dot(cn, wk_ref[...]) + jnp.concatenate([kp] * MLA_HEADS, axis=1)
    k_ref[...] = k.astype(BF)
    vt = _dot_nt(wvt_ref[...], cn)
    row = lax.broadcasted_iota(jnp.int32, vt.shape, 0)
    vt_ref[...] = jnp.where(row % MLA_V_ROWS == MLA_V, 1.0, vt).astype(BF)
    x = gk_ref[:, 0:LANES]
    first = lax.broadcasted_iota(jnp.int32, x.shape, 1) < IDX_DIM
    mu = jnp.sum(jnp.where(first, x, 0.0), axis=-1, keepdims=True) * (1.0 / IDX_DIM)
    xc = x - mu
    var = jnp.sum(jnp.where(first, xc * xc, 0.0), axis=-1, keepdims=True) * (1.0 / IDX_DIM)
    ki_ref[...] = (xc * lax.rsqrt(var + EPS) * lnw_ref[...] + lnb_ref[...]).astype(BF)


def _prep(gm, gk, cos_t, sin_t, qnw, wq, wqr, kvnw, wk, wvt, lnw, lnb, tm=MLA_KEY_TILE):
    T = gm.shape[0]
    row = lambda i: (i, 0)
    hp = MLA_HEADS * HEAD_PAD
    vrows = MLA_HEADS * MLA_V_ROWS
    return pl.pallas_call(
        _prep_kernel,
        grid=(T // tm,),
        in_specs=[pl.BlockSpec((tm, W_M), row), pl.BlockSpec((tm, W_K), row),
                  pl.BlockSpec((tm, LANES), row), pl.BlockSpec((tm, LANES), row),
                  _const_spec((1, 256)), _const_spec((256, hp)), _const_spec((256, hp)),
                  _const_spec((1, MLA_KV_LORA)), _const_spec((MLA_KV_LORA, hp)),
                  _const_spec((vrows, MLA_KV_LORA)), _const_spec((1, LANES)), _const_spec((1, LANES))],
        out_specs=[pl.BlockSpec((tm, hp), row), pl.BlockSpec((tm, hp), row),
                   pl.BlockSpec((None, vrows, tm), lambda i: (i, 0, 0)), pl.BlockSpec((tm, LANES), row)],
        out_shape=[jax.ShapeDtypeStruct((T, hp), BF), jax.ShapeDtypeStruct((T, hp), BF),
                   jax.ShapeDtypeStruct((T // tm, vrows, tm), BF), jax.ShapeDtypeStruct((T, LANES), BF)],
        compiler_params=_cparams(("parallel",)),
        name="prep",
    )(gm, gk, cos_t, sin_t, qnw, wq, wqr, kvnw, wk, wvt, lnw, lnb)


def _split3(x):
    a = x.astype(BF)
    r = x - a.astype(F32)
    b = r.astype(BF)
    c = (r - b.astype(F32)).astype(BF)
    return a, b, c


def _hgrn_kernel(q_ref, f_ref, i_ref, g_ref, lb_ref, nw_ref, o_ref, st_ref, os_ref, *, tb):
    C = HGRN_CHUNK
    nc = tb // C

    @pl.when(pl.program_id(1) == 0)
    def _():
        st_ref[...] = jnp.zeros_like(st_ref)

    fp = f_ref[...]
    log_lb = lb_ref[0:1, :]
    log1m_lb = lb_ref[1:2, :]
    one_m_lb = lb_ref[2:3, :]
    ls = jnp.minimum(fp, 0.0) - jnp.log1p(jnp.exp(-jnp.abs(fp)))
    b = log1m_lb + ls
    log_f = jnp.maximum(log_lb, b) + jnp.log1p(jnp.exp(-jnp.abs(log_lb - b)))
    k = one_m_lb * (1.0 / (1.0 + jnp.exp(fp)))

    r = lax.broadcasted_iota(jnp.int32, (tb, tb), 0)
    c = lax.broadcasted_iota(jnp.int32, (tb, tb), 1)
    same = (r // C) == (c // C)
    tri = jnp.where(same & (c <= r), 1.0, 0.0).astype(BF)
    blk = jnp.where(same, 1.0, 0.0).astype(BF)
    a0, a1, a2 = _split3(log_f)
    G = _dot(tri, a0) + _dot(tri, a1) + _dot(tri, a2)
    G_last = _dot(blk, a0) + _dot(blk, a1) + _dot(blk, a2)

    eg = jnp.exp(G)
    q_dec = (q_ref[...] * (HGRN_DK ** -0.5) * eg).astype(BF)
    k_inv = (k * jnp.exp(-G)).astype(BF)
    k_state = k * jnp.exp(G_last - G)
    decay = jnp.exp(G_last)
    v = i_ref[...]
    vb = v.astype(BF)

    rr = lax.broadcasted_iota(jnp.int32, (C, C), 0)
    cc = lax.broadcasted_iota(jnp.int32, (C, C), 1)
    causal = cc <= rr

    heads = [slice(h * HGRN_DK, (h + 1) * HGRN_DK) for h in range(HGRN_HEADS)]
    sts = [st_ref[h] for h in range(HGRN_HEADS)]
    for n in range(nc):
        rs = slice(n * C, (n + 1) * C)
        qds = [q_dec[rs, hs] for hs in heads]
        As = [_dot_nt(qds[h], k_inv[rs, heads[h]]) for h in range(HGRN_HEADS)]
        kvs = [_dot(v[rs, heads[h]].T.astype(BF), k_state[rs, heads[h]].astype(BF))
               for h in range(HGRN_HEADS)]
        inter = [_dot_nt(qds[h], sts[h].astype(BF)) for h in range(HGRN_HEADS)]
        for h in range(HGRN_HEADS):
            A = jnp.where(causal, As[h], 0.0).astype(BF)
            os_ref[rs, heads[h]] = _dot(A, vb[rs, heads[h]]) + inter[h]
            sts[h] = decay[n * C:n * C + 1, heads[h]] * sts[h] + kvs[h]
    for h in range(HGRN_HEADS):
        st_ref[h] = sts[h]

    g = g_ref[...]
    gate = g * (1.0 / (1.0 + jnp.exp(-g)))
    for h in range(HGRN_HEADS):
        hs = slice(h * HGRN_DK, (h + 1) * HGRN_DK)
        o_ref[:, hs] = (_rms(os_ref[:, hs], nw_ref[...]) * gate[:, hs]).astype(BF)


def _hgrn(gh, lb3, nw, B, S, tb=HGRN_BLOCK):
    nb = S // tb
    col = lambda j: (lambda b, i: (b * nb + i, j))
    return pl.pallas_call(
        functools.partial(_hgrn_kernel, tb=tb),
        grid=(B, nb),
        in_specs=[pl.BlockSpec((tb, HGRN_W), col(0)), pl.BlockSpec((tb, HGRN_W), col(1)),
                  pl.BlockSpec((tb, HGRN_W), col(2)), pl.BlockSpec((tb, HGRN_W), col(3)),
                  pl.BlockSpec((3, HGRN_W), lambda b, i: (0, 0)),
                  pl.BlockSpec((1, HGRN_DV), lambda b, i: (0, 0))],
        out_specs=pl.BlockSpec((tb, HGRN_W), col(0)),
        out_shape=jax.ShapeDtypeStruct((B * S, HGRN_W), BF),
        scratch_shapes=[pltpu.VMEM((HGRN_HEADS, HGRN_DV, HGRN_DK), F32),
                        pltpu.VMEM((tb, HGRN_W), F32)],
        compiler_params=_cparams(("parallel", "arbitrary")),
        name="hgrn",
    )(gh, gh, gh, gh, lb3, nw)


def _mla_kernel(q_ref, k_ref, vt_ref, o_ref, *, t, tk):
    i = pl.program_id(1)
    n_full = (i * t) // tk
    key = lax.broadcasted_iota(jnp.int32, (tk, t), 0)
    qry = lax.broadcasted_iota(jnp.int32, (tk, t), 1)
    causal = n_full * tk + key <= i * t + qry

    heads = [slice(h * HEAD_PAD, (h + 1) * HEAD_PAD) for h in range(MLA_HEADS)]

    def logits(j):
        ks = pl.ds(pl.multiple_of(j * tk, tk), tk)
        return tuple(_dot_nt(k_ref[ks, hs], q_ref[:, hs]) for hs in heads)

    def step(tiles, state, mask_last):
        ss = [list(logits(j)) for j in tiles]
        if mask_last:
            ss[-1] = [jnp.where(causal, s, NEG) for s in ss[-1]]
        ms = []
        for h in range(MLA_HEADS):
            m = state[h][0]
            for s in ss:
                m = jnp.maximum(m, jnp.max(s[h], axis=0, keepdims=True))
            ms.append(m)
        out = []
        for h in range(MLA_HEADS):
            acc = jnp.exp2(state[h][0] - ms[h]) * state[h][1]
            for j, s in zip(tiles, ss):
                acc = acc + _dot(vt_ref[j, h * MLA_V_ROWS:(h + 1) * MLA_V_ROWS, :],
                                 jnp.exp2(s[h] - ms[h]).astype(BF))
            out.append((ms[h], acc))
        return tuple(out)

    init = tuple((jnp.full((1, t), NEG, F32), jnp.zeros((MLA_V_ROWS, t), F32))
                 for _ in range(MLA_HEADS))
    state = lax.fori_loop(0, n_full // 2, lambda jj, c: step([2 * jj, 2 * jj + 1], c, False), init)
    state = lax.cond(n_full % 2 == 1,
                     lambda c: step([n_full - 1, n_full], c, True),
                     lambda c: step([n_full], c, True), state)
    outs = [acc[:MLA_V, :] * (1.0 / acc[MLA_V:MLA_V + 1, :]) for (_, acc) in state]
    o_ref[...] = jnp.concatenate(outs, axis=0).T.astype(BF)


def _mla_attn(q, k, vt, B, S, t=ATT_TILE, tk=MLA_KEY_TILE):
    nq = S // t
    nk = S // tk
    hp = MLA_HEADS * HEAD_PAD
    return pl.pallas_call(
        functools.partial(_mla_kernel, t=t, tk=tk),
        grid=(B, nq),
        in_specs=[pl.BlockSpec((t, hp), lambda b, i: (b * nq + i, 0)),
                  pl.BlockSpec((S, hp), lambda b, i: (b, 0)),
                  pl.BlockSpec((nk, MLA_HEADS * MLA_V_ROWS, tk), lambda b, i: (b, 0, 0))],
        out_specs=pl.BlockSpec((t, MLA_HEADS * MLA_V), lambda b, i: (b * nq + i, 0)),
        out_shape=jax.ShapeDtypeStruct((B * S, MLA_HEADS * MLA_V), BF),
        compiler_params=_cparams(("parallel", "arbitrary")),
        name="mla_attn",
    )(q, k, vt)


BISECT_CAP = 48
END_STEPS = 2
_LAST_BUCKET_FROM = next(
    n for n in range(MAX_EXACT, 1 << 20)
    if MAX_EXACT + int(math.log(n / MAX_EXACT) / math.log(MAX_DISTANCE / MAX_EXACT)
                       * (NUM_BUCKETS - MAX_EXACT)) >= NUM_BUCKETS - 1)
assert _LAST_BUCKET_FROM <= LANES - 1 and _LAST_BUCKET_FROM <= MAX_DISTANCE


def _dsa_kernel(qi_ref, w_ref, qd_ref, pos_ref, ki_ref, kd_ref, vt_ref, tab_ref, pmin_ref, pmax_ref,
                run_ref, o_ref, sc_ref, qm_ref, bias_ref, *, t, n_sel):
    i = pl.program_id(1)
    lane = lax.broadcasted_iota(jnp.int32, (t, LANES), 1)
    lo_half = lane < DSA_HEAD_DIM
    key = lax.broadcasted_iota(jnp.int32, (t, t), 0)
    qry = lax.broadcasted_iota(jnp.int32, (t, t), 1)
    causal = key <= qry
    k_sel = float(n_sel)

    def key_rows(j):
        return pl.ds(pl.multiple_of(j * t, t), t)

    @pl.when(i + 1 < sc_ref.shape[0])
    def _():
        sc_ref[i + 1] = jnp.full((t, t), -jnp.inf, F32)

    def over_tiles(body, init):
        return lax.fori_loop(0, (i + 2) // 2, lambda jj, c: body(2 * jj + 1, body(2 * jj, c)), init)

    def fold(x, op):
        return op(x.reshape(t // SUBLANES, SUBLANES, t), axis=0)

    zero_b = jnp.zeros((t, LANES), BF)
    for p in range(IDX_HEADS // 2):
        qp = qi_ref[:, p * LANES:(p + 1) * LANES]
        qm_ref[2 * p] = jnp.where(lo_half, qp, zero_b)
        qm_ref[2 * p + 1] = jnp.where(lo_half, zero_b, qp)
    wt = w_ref[...].T * ((IDX_HEADS ** -0.5) * (IDX_DIM ** -0.5))

    def score_tiles(js):
        kts = [ki_ref[key_rows(j), :] for j in js]
        accs = [None] * len(js)
        for h0 in range(0, IDX_HEADS, IDX_HEADS):
            ss = [[_dot_nt(kt, qm_ref[h]) for h in range(h0, h0 + IDX_HEADS)] for kt in kts]
            for n in range(len(js)):
                for m, s in enumerate(ss[n]):
                    term = jnp.maximum(s, 0.0) * wt[h0 + m:h0 + m + 1, :]
                    accs[n] = term if accs[n] is None else accs[n] + term
        return accs

    def add_stats(x, c):
        mx, mn, cp, cz = c
        return (jnp.maximum(mx, fold(x, jnp.max)),
                jnp.minimum(mn, fold(jnp.where(x == -jnp.inf, jnp.inf, x), jnp.min)),
                cp + fold(jnp.where(x > 0.0, 1.0, 0.0), jnp.sum),
                cz + fold(jnp.where(x == 0.0, 1.0, 0.0), jnp.sum))

    def p1_tiles(js, c, diag_last):
        xs = score_tiles(js)
        if diag_last:
            xs[-1] = jnp.where(causal, xs[-1], -jnp.inf)
        for j, x in zip(js, xs):
            sc_ref[j] = x
            c = add_stats(x, c)
        return c

    z8 = jnp.zeros((SUBLANES, t), F32)
    stats = lax.fori_loop(0, i // 2, lambda jj, c: p1_tiles([2 * jj, 2 * jj + 1], c, False),
                          (z8 - jnp.inf, z8 + jnp.inf, z8, z8))
    mx, mn, cp, cz = lax.cond(i % 2 == 1,
                              lambda c: p1_tiles([i - 1, i], c, True),
                              lambda c: p1_tiles([i], c, True), stats)

    def total(c):
        return jnp.sum(c, axis=0, keepdims=True)

    def count_gt(thr):
        def body(j, c):
            return c + fold(jnp.where(sc_ref[j] > thr, 1.0, 0.0), jnp.sum)
        return total(over_tiles(body, jnp.zeros((SUBLANES, t), F32)))

    row_max = jnp.max(mx, axis=0, keepdims=True)
    row_min = jnp.min(mn, axis=0, keepdims=True)
    c_pos, c_zero = total(cp), total(cz)
    n_valid = (i * t + 1 + lax.broadcasted_iota(jnp.int32, (1, t), 1)).astype(F32)

    big = n_valid > k_sel
    pos_q = jnp.logical_and(big, c_pos >= k_sel)
    tie_q = jnp.logical_and(jnp.logical_and(big, c_pos < k_sel), c_pos + c_zero >= k_sel)
    neg_q = jnp.logical_and(big, c_pos + c_zero < k_sel)
    need = jnp.where(tie_q, k_sel - c_pos, 0.0)
    lo0 = jnp.where(jnp.logical_or(pos_q, tie_q), 0.0, -jnp.inf)
    hi0 = jnp.where(neg_q, 0.0, row_max)
    c_lo0 = jnp.where(pos_q, c_pos, jnp.where(tie_q, k_sel, n_valid))

    searched = c_lo0 > k_sel

    def open_queries(c_lo):
        return jnp.max(jnp.where(c_lo > k_sel + END_STEPS, 1.0, 0.0))

    def next_above(thr):
        def body(j, m):
            x = sc_ref[j]
            return jnp.minimum(m, fold(jnp.where(x > thr, x, jnp.inf), jnp.min))
        return jnp.min(over_tiles(body, jnp.full((SUBLANES, t), jnp.inf, F32)), axis=0, keepdims=True)

    def bis_cond(c):
        return jnp.logical_and(c[0] < BISECT_CAP, c[1] > 0.0)

    def bisect_once(lo, hi, c_lo):
        active = c_lo > k_sel
        base = jnp.maximum(lo, row_min)
        mid = base + 0.5 * (hi - base)
        cnt = count_gt(mid)
        up = jnp.logical_and(active, cnt >= k_sel)
        dn = jnp.logical_and(active, cnt < k_sel)
        return jnp.where(up, mid, lo), jnp.where(dn, mid, hi), jnp.where(up, cnt, c_lo)

    def bis_body(c):
        it, _, lo, hi, c_lo = c
        lo, hi, c_lo = bisect_once(*bisect_once(lo, hi, c_lo))
        return it + 2, open_queries(c_lo), lo, hi, c_lo

    _, _, lo, _, c_lo = lax.while_loop(
        bis_cond, bis_body, (jnp.int32(0), open_queries(c_lo0), lo0, hi0, c_lo0))

    def step_body(_, c):
        lo, c_lo = c
        more = c_lo > k_sel
        return jnp.where(more, next_above(lo), lo), jnp.where(more, c_lo - 1.0, c_lo)

    lo, _ = lax.fori_loop(0, END_STEPS, step_body, (lo, c_lo))
    c_lo = jnp.where(searched, count_gt(lo), c_lo)

    still_open = c_lo != k_sel
    still_open = jnp.logical_and(searched, still_open)

    def exact_kth(_):
        def to_float(kk):
            return lax.bitcast_convert_type(jnp.where(kk < 0, kk ^ jnp.int32(0x7FFFFFFF), kk), F32)

        def count_ge(thr):
            def body(j, c):
                return c + fold(jnp.where(sc_ref[j] >= thr, 1.0, 0.0), jnp.sum)
            return total(over_tiles(body, jnp.zeros((SUBLANES, t), F32)))

        def body(b, kth):
            cand = kth + lax.shift_left(jnp.int32(1), 31 - b)
            return jnp.where(count_ge(to_float(cand)) >= k_sel, cand, kth)

        kth = to_float(lax.fori_loop(0, 32, body, jnp.full((1, t), -2 ** 31, jnp.int32)))
        return kth, count_gt(kth)

    kth, above = lax.cond(jnp.max(jnp.where(still_open, 1.0, 0.0)) > 0.0, exact_kth,
                          lambda _: (jnp.zeros((1, t), F32), jnp.zeros((1, t), F32)), 0)
    lo = jnp.where(still_open, kth, lo)
    need = jnp.where(still_open, k_sel - above, need)
    tie_val = jnp.where(still_open, kth, 0.0)

    def mask_plain(_):
        def body(j, c):
            sc_ref[j] = jnp.where(sc_ref[j] > lo, 0.0, NEG)
            return c
        return lax.fori_loop(0, i + 1, body, 0)

    def mask_ties(_):
        lower = jnp.where(qry <= key, 1.0, 0.0).astype(BF)

        def body(jj, seen):
            xs = [sc_ref[2 * jj + n] for n in range(2)]
            tied = [x == tie_val for x in xs]
            tfs = [jnp.where(m, 1.0, 0.0) for m in tied]
            ranks = [_dot(lower, tf.astype(BF)) for tf in tfs]
            for n in range(2):
                take = jnp.logical_and(tied[n], ranks[n] + seen <= need)
                sc_ref[2 * jj + n] = jnp.where(jnp.logical_or(xs[n] > lo, take), 0.0, NEG)
                seen = seen + total(fold(tfs[n], jnp.sum))
            return seen
        lax.fori_loop(0, (i + 2) // 2, body, jnp.zeros((1, t), F32))
        return 0

    lax.cond(jnp.max(need) > 0.0, mask_ties, mask_plain, 0)

    qh = []
    for p in range(DSA_HEADS // 2):
        qp = qd_ref[:, p * LANES:(p + 1) * LANES]
        qh.append(jnp.where(lo_half, qp, zero_b))
        qh.append(jnp.where(lo_half, zero_b, qp))
    pq = pos_ref[pl.ds(i, 1), :]
    pq_min = pmin_ref[pl.program_id(0), i]
    far_bias = [tab_ref[NUM_BUCKETS - 1, h] * LOG2E for h in range(DSA_HEADS)]
    log_ratio = math.log(MAX_DISTANCE / MAX_EXACT)

    dist = lax.broadcasted_iota(jnp.int32, (SUBLANES, LANES), 1)
    large = MAX_EXACT + (jnp.log(jnp.maximum(dist, 1).astype(F32) / MAX_EXACT) / log_ratio
                         * (NUM_BUCKETS - MAX_EXACT)).astype(jnp.int32)
    bucket = jnp.where(dist < MAX_EXACT, dist, jnp.minimum(large, NUM_BUCKETS - 1))
    by_dist = []
    for h in range(DSA_HEADS):
        bh = jnp.full((SUBLANES, LANES), tab_ref[0, h], F32)
        for jb in range(1, NUM_BUCKETS):
            bh = jnp.where(bucket >= jb, tab_ref[jb, h], bh)
        by_dist.append(jnp.concatenate([bh * LOG2E] * (t // SUBLANES), axis=0))

    def bias_of(n):
        n = jnp.clip(n, 0, LANES - 1)
        return [jnp.concatenate([jnp.take_along_axis(by_dist[h], n[:, c * LANES:(c + 1) * LANES], axis=1)
                                 for c in range(t // LANES)], axis=1) for h in range(DSA_HEADS)]

    def pair_bias(pk_row):
        pk = jnp.broadcast_to(pk_row, (SUBLANES, t)).T[:, 0:1]
        return bias_of(pq - pk)

    b_idx = pl.program_id(0)

    @pl.when(jnp.logical_and(b_idx == 0, i == 0))
    def _():
        for gap in range(2):
            for h, tile in enumerate(bias_of(gap * t + qry - key)):
                bias_ref[gap * DSA_HEADS + h] = tile

    def attend(tiles, carry, biases):
        ss = []
        for j, bias in zip(tiles, biases):
            madd = sc_ref[j]
            ks = key_rows(j)
            kps = [kd_ref[ks, p * LANES:(p + 1) * LANES] for p in range(DSA_HEADS // 2)]
            ss.append([_dot_nt(kps[h // 2], qh[h]) + (madd + bias[h]) for h in range(DSA_HEADS)])
        ms = []
        for h in range(DSA_HEADS):
            m = carry[h][0]
            for s in ss:
                m = jnp.maximum(m, jnp.max(s[h], axis=0, keepdims=True))
            ms.append(m)
        out = []
        for h in range(DSA_HEADS):
            acc = jnp.exp2(carry[h][0] - ms[h]) * carry[h][1]
            for j, s in zip(tiles, ss):
                p = jnp.exp2(s[h] - ms[h]).astype(BF)
                acc = acc + _dot(vt_ref[j, h * DSA_V_ROWS:(h + 1) * DSA_V_ROWS, :], p)
            out.append((ms[h], acc))
        return tuple(out)

    def is_far(j):
        return pq_min - pmax_ref[b_idx, j] >= MAX_DISTANCE

    def is_consecutive(j):
        gap = i - j
        return jnp.logical_and(
            jnp.logical_and(run_ref[b_idx, i] == 1, run_ref[b_idx, j] == 1),
            jnp.logical_and(gap <= 1, pq_min - pmin_ref[b_idx, j] == gap * t))

    def cached(gap):
        return [bias_ref[gap * DSA_HEADS + h] for h in range(DSA_HEADS)]

    def p3_body(j, carry):
        def near(c):
            return lax.cond(
                is_consecutive(j),
                lambda c: attend([j], c, [cached(i - j)]),
                lambda c: attend([j], c, [pair_bias(pos_ref[pl.ds(j, 1), :])]), c)

        return lax.cond(is_far(j), lambda c: attend([j], c, [far_bias]), near, carry)

    def p3_pair(jj, carry):
        j1 = i - 2 * jj
        j0 = jnp.maximum(j1 - 1, 0)
        two = j1 >= 1
        both_far = jnp.logical_and(two, jnp.logical_and(is_far(j0), is_far(j1)))
        near_pair = jnp.logical_and(jnp.logical_and(two, jj == 0),
                                    jnp.logical_and(is_consecutive(j0), is_consecutive(j1)))

        def others(c):
            return lax.cond(
                near_pair,
                lambda c: attend([j0, j1], c, [cached(1), cached(0)]),
                lambda c: lax.fori_loop(0, jnp.where(two, 2, 1), lambda n, c: p3_body(j1 - n, c), c), c)

        return lax.cond(both_far, lambda c: attend([j0, j1], c, [far_bias, far_bias]), others, carry)

    init = tuple((jnp.full((1, t), NEG, F32), jnp.zeros((DSA_V_ROWS, t), F32))
                 for _ in range(DSA_HEADS))
    carry = lax.fori_loop(0, (i + 2) // 2, p3_pair, init)
    outs = [acc[:DSA_HEAD_DIM, :] * (1.0 / acc[DSA_HEAD_DIM:DSA_HEAD_DIM + 1, :])
            for (_, acc) in carry]
    o_ref[...] = jnp.concatenate(outs, axis=0).T.astype(BF)


def _dsa(gq, gk, gd, ki2, vdt, pos_tiles, tab, B, S, t=ATT_TILE):
    nq = S // t
    n_sel = min(TOPK_MAX, S // 4)
    vrows = DSA_HEADS * DSA_V_ROWS
    qrow = lambda c: (lambda b, i: (b * nq + i, c))
    return pl.pallas_call(
        functools.partial(_dsa_kernel, t=t, n_sel=n_sel),
        grid=(B, nq),
        in_specs=[pl.BlockSpec((t, W_Q), qrow(0)),
                  pl.BlockSpec((t, LANES), qrow(1)),
                  pl.BlockSpec((t, DSA_WIDTH), qrow(0)),
                  pl.BlockSpec((None, nq, t), lambda b, i: (b, 0, 0)),
                  pl.BlockSpec((S, LANES), lambda b, i: (b, 0)),
                  pl.BlockSpec((S, DSA_WIDTH), lambda b, i: (b, 1)),
                  pl.BlockSpec((nq, vrows, t), lambda b, i: (b, 0, 0)),
                  pl.BlockSpec(memory_space=pltpu.SMEM), pl.BlockSpec(memory_space=pltpu.SMEM),
                  pl.BlockSpec(memory_space=pltpu.SMEM), pl.BlockSpec(memory_space=pltpu.SMEM)],
        out_specs=pl.BlockSpec((t, DSA_WIDTH), qrow(0)),
        out_shape=jax.ShapeDtypeStruct((B * S, DSA_WIDTH), BF),
        scratch_shapes=[pltpu.VMEM((nq, t, t), F32), pltpu.VMEM((IDX_HEADS, t, LANES), BF),
                        pltpu.VMEM((2 * DSA_HEADS, t, t), F32)],
        compiler_params=_cparams(("arbitrary", "arbitrary")),
        name="dsa",
    )(gq, gk, gd, pos_tiles, ki2, gd, vdt, tab, jnp.min(pos_tiles, axis=-1), jnp.max(pos_tiles, axis=-1),
      jnp.all(pos_tiles[..., 1:] - pos_tiles[..., :-1] == 1, axis=-1).astype(jnp.int32))


def _outmlp_kernel(h_ref, yh_ref, ym_ref, yd_ref, wo_ref, nw_ref, w1_ref, w2_ref, fw_ref, o_ref,
                   *, final, ff_chunk):
    mixed = jnp.concatenate([yh_ref[...], ym_ref[...], yd_ref[...]], axis=1)
    h = h_ref[...] + _dot(mixed, wo_ref[...])
    u = _rms(h, nw_ref[...]).astype(BF)
    out = h
    for c in range(D_FF // ff_chunk):
        cs = slice(c * ff_chunk, (c + 1) * ff_chunk)
        a = jnp.maximum(_dot(u, w1_ref[:, cs]), 0.0)
        out = out + _dot((a * a).astype(BF), w2_ref[cs, :])
    if final:
        out = _rms(out, fw_ref[...])
    o_ref[...] = out


def _outmlp(h, yh, ym, yd, wo, nw, w1, w2, fw, layer, final, tm=ROW_TILE, ff_chunk=1024):
    T = h.shape[0]
    tm = min(tm, T)
    row = lambda i: (i, 0)
    return pl.pallas_call(
        functools.partial(_outmlp_kernel, final=final, ff_chunk=ff_chunk),
        grid=(T // tm,),
        in_specs=[pl.BlockSpec((tm, D_MODEL), row), pl.BlockSpec((tm, HGRN_W), row),
                  pl.BlockSpec((tm, MLA_HEADS * MLA_V), row), pl.BlockSpec((tm, DSA_WIDTH), row),
                  _layer_spec((D_MODEL, D_MODEL), layer), _const_spec((1, D_MODEL)),
                  _layer_spec((D_MODEL, D_FF), layer), _layer_spec((D_FF, D_MODEL), layer),
                  _const_spec((1, D_MODEL))],
        out_specs=pl.BlockSpec((tm, D_MODEL), row),
        out_shape=jax.ShapeDtypeStruct((T, D_MODEL), F32),
        compiler_params=_cparams(("parallel",)),
        name="outmlp",
    )(h, yh, ym, yd, wo, nw, w1, w2, fw)


def _rot_cols(w):
    half = w.shape[-1] // 2
    return jnp.concatenate([-w[..., half:], w[..., :half]], axis=-1)


def _place(w, width, off):
    pad = [(0, 0)] * (w.ndim - 1) + [(off, width - off - w.shape[-1])]
    return jnp.pad(w, pad)


def _layout_w_in(w_in):
    sizes = (HGRN_W, HGRN_W, HGRN_W, HGRN_W, MLA_Q_LORA, MLA_KV_LORA + MLA_ROPE,
             DSA_WIDTH, DSA_WIDTH, DSA_WIDTH, IDX_HEADS * IDX_DIM, IDX_DIM, IDX_HEADS)
    offs = [0]
    for s in sizes:
        offs.append(offs[-1] + s)
    hq, hf, hi, hg, mqa, mkva, dq, dk, dv, iq, ik, iw = [
        w_in[..., offs[n]:offs[n + 1]] for n in range(len(sizes))]
    ckv, kpe = mkva[..., :MLA_KV_LORA], mkva[..., MLA_KV_LORA:]
    cols = [hq, hf, hi, hg,
            _place(mqa, 256, 0), ckv, _place(kpe, LANES, MLA_NOPE), _place(_rot_cols(kpe), LANES, MLA_NOPE),
            dq * (DSA_HEAD_DIM ** -0.5 * LOG2E), dk,
            iq,
            ik, ik, _place(iw, LANES, 0)]
    w_cat = jnp.concatenate(cols, axis=-1).astype(BF)
    L = w_in.shape[0]
    dvt = jnp.swapaxes(dv, 1, 2).reshape(L, DSA_HEADS, DSA_HEAD_DIM, D_MODEL)
    dvt = jnp.pad(dvt, ((0, 0), (0, 0), (0, DSA_V_ROWS - DSA_HEAD_DIM), (0, 0)))
    return w_cat, dvt.reshape(L, DSA_HEADS * DSA_V_ROWS, D_MODEL).astype(BF)


def _layout_mla(w_qb, w_kvb):
    L = w_qb.shape[0]
    dq = MLA_NOPE + MLA_ROPE
    wq = w_qb.reshape(L, MLA_Q_LORA, MLA_HEADS, dq)
    wq_rot = jnp.concatenate([jnp.zeros_like(wq[..., :MLA_NOPE]), _rot_cols(wq[..., MLA_NOPE:])], axis=-1)
    pad_q = lambda w: jnp.pad(w, ((0, 0), (0, 256 - MLA_Q_LORA), (0, 0), (0, HEAD_PAD - dq))).reshape(
        L, 256, MLA_HEADS * HEAD_PAD).astype(BF)
    wkv = w_kvb.reshape(L, MLA_KV_LORA, MLA_HEADS, MLA_NOPE + MLA_V)
    pad_kv = lambda w: jnp.pad(w, ((0, 0), (0, 0), (0, 0), (0, HEAD_PAD - w.shape[-1]))).reshape(
        L, MLA_KV_LORA, MLA_HEADS * HEAD_PAD).astype(BF)
    wvt = jnp.pad(wkv[..., MLA_NOPE:], ((0, 0), (0, 0), (0, 0), (0, MLA_V_ROWS - MLA_V)))
    wvt = jnp.swapaxes(wvt.reshape(L, MLA_KV_LORA, MLA_HEADS * MLA_V_ROWS), 1, 2).astype(BF)
    return pad_q(wq), pad_q(wq_rot), pad_kv(wkv[..., :MLA_NOPE]), wvt


def kernel(x, positions, attn_norm_w, w_in, hgrn_lb_logits, hgrn_norm_w, mla_q_norm_w, mla_w_qb,
           mla_kv_norm_w, mla_w_kvb, idx_k_norm_w, idx_k_norm_b, rel_bias_table, w_out,
           mlp_norm_w, w_mlp_in, w_mlp_out, final_norm_w):
    B, S, _ = x.shape
    T = B * S
    depth = w_in.shape[0]

    inv_freq = 1.0 / (ROPE_THETA ** (jnp.arange(0, MLA_ROPE, 2, dtype=F32) / MLA_ROPE))
    ang = positions.astype(F32)[..., None] * inv_freq
    cos, sin = jnp.cos(ang).reshape(T, -1), jnp.sin(ang).reshape(T, -1)
    cos_t = jnp.concatenate([jnp.ones((T, MLA_NOPE), F32), cos, cos,
                             jnp.zeros((T, HEAD_PAD - MLA_NOPE - MLA_ROPE), F32)], axis=1)
    sin_t = jnp.concatenate([jnp.zeros((T, MLA_NOPE), F32), sin, sin,
                             jnp.zeros((T, HEAD_PAD - MLA_NOPE - MLA_ROPE), F32)], axis=1)
    pos_tiles = positions.reshape(B, S // ATT_TILE, ATT_TILE)

    lb = jnp.cumsum(jax.nn.softmax(hgrn_lb_logits.astype(F32), axis=0), axis=0)
    lb = lb - lb[0:1]
    lb3 = jnp.stack([jnp.log(lb), jnp.log1p(-lb), 1.0 - lb], axis=1)

    w_cat, w_dvt = _layout_w_in(w_in)
    wq, wqr, wk, wvt = _layout_mla(mla_w_qb, mla_w_kvb)
    qnw = jnp.pad(mla_q_norm_w, ((0, 0), (0, 256 - MLA_Q_LORA)))
    lnw = jnp.concatenate([idx_k_norm_w, idx_k_norm_w], axis=-1)
    lnb = jnp.concatenate([idx_k_norm_b, idx_k_norm_b], axis=-1)
    wo = w_out.astype(BF)
    w1 = w_mlp_in.astype(BF)
    w2 = w_mlp_out.astype(BF)
    tab = rel_bias_table.astype(F32)

    h = x.reshape(T, D_MODEL)
    for l in range(depth):
        gh, gm, gd, gq, gk, vdt = _inproj(h, attn_norm_w[l][None], w_cat, w_dvt, l)
        q_m, k_m, v_m, ki2 = _prep(gm, gk, cos_t, sin_t, qnw[l][None], wq[l], wqr[l],
                                   mla_kv_norm_w[l][None], wk[l], wvt[l], lnw[l][None], lnb[l][None])
        y_h = _hgrn(gh, lb3[l], hgrn_norm_w[l][None], B, S)
        y_m = _mla_attn(q_m, k_m, v_m, B, S)
        y_d = _dsa(gq, gk, gd, ki2, vdt, pos_tiles, tab, B, S)
        h = _outmlp(h, y_h, y_m, y_d, wo, mlp_norm_w[l][None], w1, w2,
                    final_norm_w[None], l, final=(l == depth - 1))
    return h.reshape(B, S, D_MODEL)
```
